```python
import jax, jax.numpy as jnp
from jax import lax
import numpy as np

D_MODEL = 1024
BATCH = 8
SEQ = 4096
DEPTH = 1

N_META = 16
D_MIX = D_MODEL
D_ATTN = D_MIX // 2
D_CONV = D_MIX - D_ATTN
N_HEADS = 8
HEAD_DIM = D_ATTN // N_HEADS
CONV_WIDTH = 31
Q_BLOCK = 128
LN_EPS = 1e-5
ALPHA = (2.0 * DEPTH) ** 0.25
BETA = (8.0 * DEPTH) ** -0.25

OFF_K = D_ATTN
OFF_V = 2 * D_ATTN
OFF_F = 3 * D_ATTN
OFF_GA = OFF_F + N_HEADS
OFF_U = OFF_GA + D_ATTN
OFF_UG = OFF_U + D_CONV
OFF_GC = OFF_UG + D_CONV
D_IN = OFF_GC + D_CONV
SPLITS = (OFF_K, OFF_V, OFF_F, OFF_GA, OFF_U, OFF_UG, OFF_GC)

kernel_name = "fox_conformer_hymba_deepnorm"


def layer_norm(x, g, b):
    xf = x.astype(jnp.float32)
    mu = jnp.mean(xf, axis=-1, keepdims=True)
    var = jnp.mean(jnp.square(xf - mu), axis=-1, keepdims=True)
    return ((xf - mu) * lax.rsqrt(var + LN_EPS) * g + b).astype(x.dtype)


def forgetting_attention(q, k, v, log_f):
    L = q.shape[1]
    scale = HEAD_DIM ** -0.5
    c = jnp.cumsum(log_f, axis=1).transpose(0, 2, 1)
    starts = [0] + list(range(N_META, L, Q_BLOCK))
    ends = starts[1:] + [L]
    outs = []
    for qs, qe in zip(starts, ends):
        qb, kb, vb = q[:, qs:qe], k[:, :qe], v[:, :qe]
        s = jnp.einsum('bqhd,bkhd->bhqk', qb, kb, preferred_element_type=jnp.float32) * scale
        bias = c[:, :, qs:qe, None] - c[:, :, None, :qe]
        qpos = jnp.arange(qs, qe)[:, None]
        kpos = jnp.arange(qe)[None, :]
        s = jnp.where(kpos <= qpos, s + bias, -jnp.inf)
        p = jax.nn.softmax(s, axis=-1)
        outs.append(jnp.einsum('bhqk,bkhd->bqhd', p.astype(vb.dtype), vb))
    return jnp.concatenate(outs, axis=1)


def conformer_conv(u, u_gate, conv_w, conv_b, g, b, w_pw):
    h = u * jax.nn.sigmoid(u_gate)
    h = lax.conv_general_dilated(
        h, conv_w[:, None, :].astype(h.dtype), window_strides=(1,),
        padding=[(CONV_WIDTH - 1, 0)], dimension_numbers=('NWC', 'WIO', 'NWC'),
        feature_group_count=D_CONV) + conv_b
    h = jax.nn.silu(layer_norm(h, g, b))
    return h @ w_pw


def hybrid_layer(h, w_in, b_f, conv_w, conv_b, ln_conv_g, ln_conv_b, w_pw, w_out, ln_out_g, ln_out_b):
    B, L, _ = h.shape
    proj = h @ w_in
    q, k, v, f_logit, g_attn, u, u_gate, g_conv = jnp.split(proj, SPLITS, axis=-1)
    q = q.reshape(B, L, N_HEADS, HEAD_DIM)
    k = k.reshape(B, L, N_HEADS, HEAD_DIM)
    v = v.reshape(B, L, N_HEADS, HEAD_DIM)
    log_f = jax.nn.log_sigmoid((f_logit + b_f).astype(jnp.float32))
    y_attn = forgetting_attention(q, k, v, log_f).reshape(B, L, D_ATTN) * jax.nn.silu(g_attn)
    y_conv = conformer_conv(u, u_gate, conv_w, conv_b, ln_conv_g, ln_conv_b, w_pw) * jax.nn.silu(g_conv)
    y = jnp.concatenate([y_attn, y_conv], axis=-1) @ w_out
    return layer_norm(ALPHA * h + y, ln_out_g, ln_out_b)


def _fwd_setup_inputs(seed: int = 0) -> dict:
    key = jax.random.key(seed)
    ks = jax.random.split(key, 16)
    f32 = jnp.float32
    x = jax.random.normal(ks[0], (BATCH, SEQ, D_MODEL), f32)
    meta = jax.random.normal(ks[1], (N_META, D_MODEL), f32)
    ln_in_g = 1.0 + 0.02 * jax.random.normal(ks[2], (D_MODEL,), f32)
    ln_in_b = 0.02 * jax.random.normal(ks[3], (D_MODEL,), f32)
    col_scale = jnp.ones((D_IN,), f32).at[OFF_V:OFF_F].set(BETA)
    w_in = jax.random.normal(ks[4], (DEPTH, D_MODEL, D_IN), f32) * (D_MODEL ** -0.5) * col_scale
    b_f = jnp.linspace(1.0, 6.0, N_HEADS, dtype=f32)[None, :] + 0.1 * jax.random.normal(ks[5], (DEPTH, N_HEADS), f32)
    conv_w = jax.random.normal(ks[6], (DEPTH, CONV_WIDTH, D_CONV), f32) * (CONV_WIDTH ** -0.5)
    conv_b = 0.02 * jax.random.normal(ks[7], (DEPTH, D_CONV), f32)
    ln_conv_g = 1.0 + 0.02 * jax.random.normal(ks[8], (DEPTH, D_CONV), f32)
    ln_conv_b = 0.02 * jax.random.normal(ks[9], (DEPTH, D_CONV), f32)
    w_pw = jax.random.normal(ks[10], (DEPTH, D_CONV, D_CONV), f32) * (D_CONV ** -0.5) * BETA
    w_out = jax.random.normal(ks[11], (DEPTH, D_MIX, D_MODEL), f32) * (D_MIX ** -0.5) * BETA
    ln_out_g = 1.0 + 0.02 * jax.random.normal(ks[12], (DEPTH, D_MODEL), f32)
    ln_out_b = 0.02 * jax.random.normal(ks[13], (DEPTH, D_MODEL), f32)
    return {"x": x, "meta": meta, "ln_in_g": ln_in_g, "ln_in_b": ln_in_b, "w_in": w_in,
            "b_f": b_f, "conv_w": conv_w, "conv_b": conv_b, "ln_conv_g": ln_conv_g,
            "ln_conv_b": ln_conv_b, "w_pw": w_pw, "w_out": w_out,
            "ln_out_g": ln_out_g, "ln_out_b": ln_out_b}


def _fwd_reference(x, meta, ln_in_g, ln_in_b, w_in, b_f, conv_w, conv_b, ln_conv_g, ln_conv_b,
              w_pw, w_out, ln_out_g, ln_out_b):
    B = x.shape[0]
    meta_b = jnp.broadcast_to(meta[None].astype(x.dtype), (B, N_META, D_MODEL))
    h = jnp.concatenate([meta_b, x], axis=1)
    h = layer_norm(h, ln_in_g, ln_in_b)
    for l in range(DEPTH):
        h = hybrid_layer(h, w_in[l], b_f[l], conv_w[l], conv_b[l], ln_conv_g[l], ln_conv_b[l],
                         w_pw[l], w_out[l], ln_out_g[l], ln_out_b[l])
    return h[:, N_META:]


import jax as _jax
import jax.numpy as _jnp

TWIN_FORMAT = 'train_step'
FWD_PARAMS = ['x', 'meta', 'ln_in_g', 'ln_in_b', 'w_in', 'b_f', 'conv_w', 'conv_b', 'ln_conv_g', 'ln_conv_b', 'w_pw', 'w_out', 'ln_out_g', 'ln_out_b']
TWIN_WEIGHTS = ['meta', 'ln_in_g', 'ln_in_b', 'w_in', 'b_f', 'conv_w', 'conv_b', 'ln_conv_g', 'ln_conv_b', 'w_pw', 'w_out', 'ln_out_g', 'ln_out_b']
TWIN_DIFF_INPUT = 'x'
TWIN_INPUTS = ['x', 'meta', 'ln_in_g', 'ln_in_b', 'w_in', 'b_f', 'conv_w', 'conv_b', 'ln_conv_g', 'ln_conv_b', 'w_pw', 'w_out', 'ln_out_g', 'ln_out_b', 'loss_target', 'm_meta', 'm_ln_in_g', 'm_ln_in_b', 'm_w_in', 'm_b_f', 'm_conv_w', 'm_conv_b', 'm_ln_conv_g', 'm_ln_conv_b', 'm_w_pw', 'm_w_out', 'm_ln_out_g', 'm_ln_out_b', 'v_meta', 'v_ln_in_g', 'v_ln_in_b', 'v_w_in', 'v_b_f', 'v_conv_w', 'v_conv_b', 'v_ln_conv_g', 'v_ln_conv_b', 'v_w_pw', 'v_w_out', 'v_ln_out_g', 'v_ln_out_b']
TWIN_OUTPUTS = ['loss', 'grad_x', 'grad_meta', 'grad_ln_in_g', 'grad_ln_in_b', 'grad_w_in', 'grad_b_f', 'grad_conv_w', 'grad_conv_b', 'grad_ln_conv_g', 'grad_ln_conv_b', 'grad_w_pw', 'grad_w_out', 'grad_ln_out_g', 'grad_ln_out_b', 'delta_meta', 'delta_ln_in_g', 'delta_ln_in_b', 'delta_w_in', 'delta_b_f', 'delta_conv_w', 'delta_conv_b', 'delta_ln_conv_g', 'delta_ln_conv_b', 'delta_w_pw', 'delta_w_out', 'delta_ln_out_g', 'delta_ln_out_b', 'new_m_meta', 'new_m_ln_in_g', 'new_m_ln_in_b', 'new_m_w_in', 'new_m_b_f', 'new_m_conv_w', 'new_m_conv_b', 'new_m_ln_conv_g', 'new_m_ln_conv_b', 'new_m_w_pw', 'new_m_w_out', 'new_m_ln_out_g', 'new_m_ln_out_b', 'new_v_meta', 'new_v_ln_in_g', 'new_v_ln_in_b', 'new_v_w_in', 'new_v_b_f', 'new_v_conv_w', 'new_v_conv_b', 'new_v_ln_conv_g', 'new_v_ln_conv_b', 'new_v_w_pw', 'new_v_w_out', 'new_v_ln_out_g', 'new_v_ln_out_b']
TWIN_LEAF_KINDS = {'loss': 'loss', 'grad_x': 'grad_x', 'grad_meta': 'grad_w', 'grad_ln_in_g': 'grad_w', 'grad_ln_in_b': 'grad_w', 'grad_w_in': 'grad_w', 'grad_b_f': 'grad_w', 'grad_conv_w': 'grad_w', 'grad_conv_b': 'grad_w', 'grad_ln_conv_g': 'grad_w', 'grad_ln_conv_b': 'grad_w', 'grad_w_pw': 'grad_w', 'grad_w_out': 'grad_w', 'grad_ln_out_g': 'grad_w', 'grad_ln_out_b': 'grad_w', 'delta_meta': 'delta_w', 'delta_ln_in_g': 'delta_w', 'delta_ln_in_b': 'delta_w', 'delta_w_in': 'delta_w', 'delta_b_f': 'delta_w', 'delta_conv_w': 'delta_w', 'delta_conv_b': 'delta_w', 'delta_ln_conv_g': 'delta_w', 'delta_ln_conv_b': 'delta_w', 'delta_w_pw': 'delta_w', 'delta_w_out': 'delta_w', 'delta_ln_out_g': 'delta_w', 'delta_ln_out_b': 'delta_w', 'new_m_meta': 'new_m', 'new_m_ln_in_g': 'new_m', 'new_m_ln_in_b': 'new_m', 'new_m_w_in': 'new_m', 'new_m_b_f': 'new_m', 'new_m_conv_w': 'new_m', 'new_m_conv_b': 'new_m', 'new_m_ln_conv_g': 'new_m', 'new_m_ln_conv_b': 'new_m', 'new_m_w_pw': 'new_m', 'new_m_w_out': 'new_m', 'new_m_ln_out_g': 'new_m', 'new_m_ln_out_b': 'new_m', 'new_v_meta': 'new_v', 'new_v_ln_in_g': 'new_v', 'new_v_ln_in_b': 'new_v', 'new_v_w_in': 'new_v', 'new_v_b_f': 'new_v', 'new_v_conv_w': 'new_v', 'new_v_conv_b': 'new_v', 'new_v_ln_conv_g': 'new_v', 'new_v_ln_conv_b': 'new_v', 'new_v_w_pw': 'new_v', 'new_v_w_out': 'new_v', 'new_v_ln_out_g': 'new_v', 'new_v_ln_out_b': 'new_v'}


def _forward(args):
    return _fwd_reference(*[args[k] for k in FWD_PARAMS])


def _output_shape():
    out = _jax.eval_shape(lambda: _forward(_fwd_setup_inputs(0)))
    return out.shape, out.dtype

N_MICROBATCH = 1
ADAM_LR = 0.001
ADAM_B1 = 0.9
ADAM_B2 = 0.999
ADAM_EPS = 1e-08
ADAM_WD = 0.01
ADAM_STEP = 10
PER_EXAMPLE_BATCH_AXIS = {'x': 0, 'loss_target': 0}
SHARED_INPUTS = []
_WEIGHT_DTYPES = {'meta': _jnp.float32, 'ln_in_g': _jnp.float32, 'ln_in_b': _jnp.float32, 'w_in': _jnp.float32, 'b_f': _jnp.float32, 'conv_w': _jnp.float32, 'conv_b': _jnp.float32, 'ln_conv_g': _jnp.float32, 'ln_conv_b': _jnp.float32, 'w_pw': _jnp.float32, 'w_out': _jnp.float32, 'ln_out_g': _jnp.float32, 'ln_out_b': _jnp.float32}
MOMENT_SCALE = {'meta': 1.001668e-03, 'ln_in_g': 1.020959e+00, 'ln_in_b': 4.721524e-01, 'w_in': 1.379161e-02, 'b_f': 8.577902e-02, 'conv_w': 1.987310e-02, 'conv_b': 3.964470e-02, 'ln_conv_g': 2.764030e-02, 'ln_conv_b': 2.173082e-02, 'w_pw': 3.278439e-02, 'w_out': 2.549433e-02, 'ln_out_g': 3.203166e+01, 'ln_out_b': 6.664466e-01}


def _to_microbatches(a, axis):
    t = _jnp.moveaxis(a, axis, 0)
    t = t.reshape((N_MICROBATCH, t.shape[0] // N_MICROBATCH) + t.shape[1:])
    return _jnp.moveaxis(t, 1, axis + 1)


def setup_inputs(seed: int = 0) -> dict:
    inp = _fwd_setup_inputs(seed)
    key = _jax.random.fold_in(_jax.random.key(seed), 7919)
    shape, _ = _output_shape()
    out = dict(inp)
    out["loss_target"] = _jax.random.normal(_jax.random.fold_in(key, 0), shape, _jnp.float32)
    for i, name in enumerate(TWIN_WEIGHTS):
        w = inp[name].astype(_jnp.float32)
        if MOMENT_SCALE is None:
            s = _jnp.sqrt(_jnp.mean(_jnp.square(w)) + 1e-30)
        else:
            s = MOMENT_SCALE[name]
        km, kv = _jax.random.split(_jax.random.fold_in(key, i + 1))
        out[name] = w
        out["m_" + name] = s * _jax.random.normal(km, w.shape, _jnp.float32)
        out["v_" + name] = (s * s) * _jax.random.uniform(kv, w.shape, _jnp.float32, 0.5, 1.5)
    if N_MICROBATCH > 1:
        for name, axis in PER_EXAMPLE_BATCH_AXIS.items():
            out[name] = _to_microbatches(out[name], axis)
    return {'x': out['x'], 'meta': out['meta'], 'ln_in_g': out['ln_in_g'], 'ln_in_b': out['ln_in_b'], 'w_in': out['w_in'], 'b_f': out['b_f'], 'conv_w': out['conv_w'], 'conv_b': out['conv_b'], 'ln_conv_g': out['ln_conv_g'], 'ln_conv_b': out['ln_conv_b'], 'w_pw': out['w_pw'], 'w_out': out['w_out'], 'ln_out_g': out['ln_out_g'], 'ln_out_b': out['ln_out_b'], 'loss_target': out['loss_target'], 'm_meta': out['m_meta'], 'm_ln_in_g': out['m_ln_in_g'], 'm_ln_in_b': out['m_ln_in_b'], 'm_w_in': out['m_w_in'], 'm_b_f': out['m_b_f'], 'm_conv_w': out['m_conv_w'], 'm_conv_b': out['m_conv_b'], 'm_ln_conv_g': out['m_ln_conv_g'], 'm_ln_conv_b': out['m_ln_conv_b'], 'm_w_pw': out['m_w_pw'], 'm_w_out': out['m_w_out'], 'm_ln_out_g': out['m_ln_out_g'], 'm_ln_out_b': out['m_ln_out_b'], 'v_meta': out['v_meta'], 'v_ln_in_g': out['v_ln_in_g'], 'v_ln_in_b': out['v_ln_in_b'], 'v_w_in': out['v_w_in'], 'v_b_f': out['v_b_f'], 'v_conv_w': out['v_conv_w'], 'v_conv_b': out['v_conv_b'], 'v_ln_conv_g': out['v_ln_conv_g'], 'v_ln_conv_b': out['v_ln_conv_b'], 'v_w_pw': out['v_w_pw'], 'v_w_out': out['v_w_out'], 'v_ln_out_g': out['v_ln_out_g'], 'v_ln_out_b': out['v_ln_out_b']}


def _loss(weights, diff, rest, loss_target):
    with _jax.named_scope("forward"):
        args = {**rest, TWIN_DIFF_INPUT: diff, **{k: w.astype(_WEIGHT_DTYPES[k]) for k, w in weights.items()}}
        y = _forward(args)
    with _jax.named_scope("loss_head"):
        err = _jnp.square(y.astype(_jnp.float32) - loss_target)
        return 0.5 * _jnp.sum(_jnp.mean(err, axis=-1)) if err.ndim else 0.5 * err


def _adamw(w, g, m, v):
    m = ADAM_B1 * m + (1.0 - ADAM_B1) * g
    v = ADAM_B2 * v + (1.0 - ADAM_B2) * _jnp.square(g)
    m_hat = m / (1.0 - ADAM_B1 ** ADAM_STEP)
    v_hat = v / (1.0 - ADAM_B2 ** ADAM_STEP)
    delta = -ADAM_LR * (m_hat / (_jnp.sqrt(v_hat) + ADAM_EPS) + ADAM_WD * w)
    return delta, m, v


def reference(x, meta, ln_in_g, ln_in_b, w_in, b_f, conv_w, conv_b, ln_conv_g, ln_conv_b, w_pw, w_out, ln_out_g, ln_out_b, loss_target, m_meta, m_ln_in_g, m_ln_in_b, m_w_in, m_b_f, m_conv_w, m_conv_b, m_ln_conv_g, m_ln_conv_b, m_w_pw, m_w_out, m_ln_out_g, m_ln_out_b, v_meta, v_ln_in_g, v_ln_in_b, v_w_in, v_b_f, v_conv_w, v_conv_b, v_ln_conv_g, v_ln_conv_b, v_w_pw, v_w_out, v_ln_out_g, v_ln_out_b):
    given = dict(x=x, meta=meta, ln_in_g=ln_in_g, ln_in_b=ln_in_b, w_in=w_in, b_f=b_f, conv_w=conv_w, conv_b=conv_b, ln_conv_g=ln_conv_g, ln_conv_b=ln_conv_b, w_pw=w_pw, w_out=w_out, ln_out_g=ln_out_g, ln_out_b=ln_out_b, loss_target=loss_target, m_meta=m_meta, m_ln_in_g=m_ln_in_g, m_ln_in_b=m_ln_in_b, m_w_in=m_w_in, m_b_f=m_b_f, m_conv_w=m_conv_w, m_conv_b=m_conv_b, m_ln_conv_g=m_ln_conv_g, m_ln_conv_b=m_ln_conv_b, m_w_pw=m_w_pw, m_w_out=m_w_out, m_ln_out_g=m_ln_out_g, m_ln_out_b=m_ln_out_b, v_meta=v_meta, v_ln_in_g=v_ln_in_g, v_ln_in_b=v_ln_in_b, v_w_in=v_w_in, v_b_f=v_b_f, v_conv_w=v_conv_w, v_conv_b=v_conv_b, v_ln_conv_g=v_ln_conv_g, v_ln_conv_b=v_ln_conv_b, v_w_pw=v_w_pw, v_w_out=v_w_out, v_ln_out_g=v_ln_out_g, v_ln_out_b=v_ln_out_b)
    weights = {n: given[n] for n in TWIN_WEIGHTS}
    shared = {n: given[n] for n in SHARED_INPUTS}
    per_example = {n: given[n] for n in ['x']}
    grad_fn = _jax.value_and_grad(_loss, argnums=(0, 1))

    def one_microbatch(ex, loss_target):
        ex = dict(ex)
        diff = ex.pop(TWIN_DIFF_INPUT)
        return grad_fn(weights, diff, {**shared, **ex}, loss_target)

    if N_MICROBATCH == 1:
        loss, (grad_w, grad_x) = one_microbatch(per_example, given["loss_target"])
    else:
        def body(carry, xs):
            loss_sum, grad_sum = carry
            l_k, (gw_k, gx_k) = one_microbatch(xs[0], xs[1])
            with _jax.named_scope("update"):
                return (loss_sum + l_k, _jax.tree.map(_jnp.add, grad_sum, gw_k)), gx_k

        init = (_jnp.zeros((), _jnp.float32), _jax.tree.map(_jnp.zeros_like, weights))
        (loss, grad_w), grad_x = _jax.lax.scan(body, init, (per_example, given["loss_target"]))
    with _jax.named_scope("update"):
        delta_w, new_m, new_v = {}, {}, {}
        for n in TWIN_WEIGHTS:
            delta_w[n], new_m[n], new_v[n] = _adamw(weights[n], grad_w[n], given["m_" + n], given["v_" + n])
    return (loss, grad_x, *[grad_w[n] for n in TWIN_WEIGHTS], *[delta_w[n] for n in TWIN_WEIGHTS],
            *[new_m[n] for n in TWIN_WEIGHTS], *[new_v[n] for n in TWIN_WEIGHTS])
```

```python
import functools

import jax
import jax.numpy as jnp
from jax import lax
from jax.experimental import pallas as pl
from jax.experimental.pallas import tpu as pltpu

D_MODEL = 1024
N_META = 16
D_ATTN = 512
D_CONV = 512
N_HEADS = 8
HEAD_DIM = 64
N_PAIRS = 4
CONV_WIDTH = 31
LN_EPS = 1e-5
ALPHA = 2.0 ** 0.25
ATTN_SCALE = 0.125
D_IN = 3592
OFF_F = 3 * D_ATTN
D_IN_PAD = 3712
N_SHARD = 4
W_IN_SHARD = D_IN // N_SHARD

ROW_PAD = 112
ROW_X = ROW_PAD + N_META
HALO = 32
MASKED = 1e30

ADAM_LR = 0.001
ADAM_B1 = 0.9
ADAM_B2 = 0.999
ADAM_EPS = 1e-08
ADAM_WD = 0.01
ADAM_STEP = 10

VMEM_LIMIT = 56 * 1024 * 1024

_MM = jnp.bfloat16
_F32 = jnp.float32
MESH = pl.DeviceIdType.MESH


def _params(sem=None):
    return pltpu.CompilerParams(dimension_semantics=sem, vmem_limit_bytes=VMEM_LIMIT)


def _row_tile(n_rows):
    for t in (384, 256, 128):
        if n_rows % t == 0:
            return t
    raise ValueError(f"padded sequence length {n_rows} is not a multiple of 128")


def _sigmoid(x):
    return 1.0 / (1.0 + jnp.exp(-x))


def _ln_stats(x):
    mu = jnp.mean(x, axis=-1, keepdims=True)
    xc = x - mu
    var = jnp.mean(xc * xc, axis=-1, keepdims=True)
    rstd = lax.rsqrt(var + LN_EPS)
    return xc * rstd, rstd


def _ln_bwd(dy, xhat, rstd, g):
    dxh = dy * g
    m1 = jnp.mean(dxh, axis=-1, keepdims=True)
    m2 = jnp.mean(dxh * xhat, axis=-1, keepdims=True)
    return rstd * (dxh - m1 - xhat * m2)


def _dot(a, b):
    return jnp.dot(a, b, preferred_element_type=_F32)


def _dot_t0(a, b):
    return lax.dot_general(a, b, (((0,), (0,)), ((), ())), preferred_element_type=_F32)


def _dot_t1(a, b):
    return lax.dot_general(a, b, (((1,), (1,)), ((), ())), preferred_element_type=_F32)


def _dot_exact(a, b):
    return jnp.dot(a, b, preferred_element_type=_F32, precision=lax.Precision.HIGHEST)


def _full(shape):
    return pl.BlockSpec(shape, lambda *_: (0,) * len(shape))


def _ln_proj(h0, g, b, wp):
    lp = h0.shape[0]
    t = _row_tile(lp)

    def body(h0_ref, g_ref, b_ref, w_ref, h_ref, q_ref, k_ref, v_ref, ga_ref, u_ref, ug_ref, gc_ref, f_ref):
        i = pl.program_id(0)
        xhat, _ = _ln_stats(h0_ref[...])
        rows = i * t + lax.broadcasted_iota(jnp.int32, (t, 1), 0)
        h = jnp.where(rows >= ROW_PAD, xhat * g_ref[...] + b_ref[...], 0.0)
        h_ref[...] = h
        hb = h.astype(_MM)
        q_ref[...] = (_dot(hb, w_ref[:, 0:512]) * ATTN_SCALE).astype(q_ref.dtype)
        k_ref[...] = _dot(hb, w_ref[:, 512:1024]).astype(k_ref.dtype)
        v_ref[...] = _dot(hb, w_ref[:, 1024:1536]).astype(v_ref.dtype)
        ga_ref[...] = _dot(hb, w_ref[:, 1536:2048])
        u_ref[...] = _dot(hb, w_ref[:, 2048:2560])
        ug_ref[...] = _dot(hb, w_ref[:, 2560:3072])
        gc_ref[...] = _dot(hb, w_ref[:, 3072:3584])
        f_ref[...] = _dot(hb, w_ref[:, 3584:3712])

    row = lambda w: pl.BlockSpec((t, w), lambda i: (i, 0))
    f32 = lambda w: jax.ShapeDtypeStruct((lp, w), _F32)
    mm = lambda w: jax.ShapeDtypeStruct((lp, w), _MM)
    return pl.pallas_call(
        body, name="ln_proj", grid=(lp // t,),
        in_specs=[row(D_MODEL), _full((1, D_MODEL)), _full((1, D_MODEL)), _full((D_MODEL, D_IN_PAD))],
        out_specs=[row(D_MODEL)] + [row(512)] * 7 + [row(128)],
        out_shape=[f32(D_MODEL), mm(512), mm(512), mm(512), f32(512), f32(512), f32(512), f32(512), f32(128)],
        compiler_params=_params(("arbitrary",)),
    )(h0, g, b, wp)


def _log_sigmoid(z):
    return jnp.minimum(z, 0.0) - jnp.log(1.0 + jnp.exp(-jnp.abs(z)))


def _decay_cumsum(f, bf):
    lp = f.shape[0]
    nb = lp // 128

    def body(f_ref, bf_ref, ccol_ref, crow_ref):
        r = lax.broadcasted_iota(jnp.int32, (128, 128), 0)
        c = lax.broadcasted_iota(jnp.int32, (128, 128), 1)
        tril = (c <= r).astype(_F32)
        pos0 = lax.broadcasted_iota(jnp.int32, (8, 128), 1)

        def step(bi, carry):
            rows = pl.ds(pl.multiple_of(bi * 128, 128), 128)
            lf = _log_sigmoid(f_ref[rows, :] + bf_ref[...])
            cb = _dot_exact(tril, lf) + carry
            for p in range(N_PAIRS):
                shifted = cb if p == 0 else pltpu.roll(cb, 128 - 2 * p, axis=1)
                ccol_ref[p, rows, :] = shifted
                ct = shifted.T[0:8, :]
                crow_ref[p, :, rows] = jnp.where(bi * 128 + pos0 < ROW_PAD, MASKED, ct)
            return cb[127:128, :]

        lax.fori_loop(0, nb, step, jnp.zeros((1, 128), _F32))

    return pl.pallas_call(
        body, name="decay_cumsum",
        out_shape=[jax.ShapeDtypeStruct((N_PAIRS, lp, 128), _F32), jax.ShapeDtypeStruct((N_PAIRS, 8, lp), _F32)],
        compiler_params=_params(),
    )(f, bf)


def _head_masks():
    lane = lax.broadcasted_iota(jnp.int32, (1, 128), 1)
    return lane < HEAD_DIM, lane >= HEAD_DIM


def _attn_fwd(q, k, v, ccol, crow):
    lp = q.shape[0]
    t = _row_tile(lp)

    def body(q_ref, k_ref, v_ref, ccol_ref, crow_ref, o_ref, lse_ref):
        i = pl.program_id(1)
        qb = q_ref[...]
        cc = ccol_ref[0]
        masks = _head_masks()
        tri = lax.broadcasted_iota(jnp.int32, (t, t), 1) <= lax.broadcasted_iota(jnp.int32, (t, t), 0)
        outs, lses = [], []
        for hd in range(2):
            qh = jnp.where(masks[hd], qb, jnp.zeros_like(qb))
            cq = cc[:, hd:hd + 1]

            def block(j, carry, diag):
                m, l, acc = carry
                cols = pl.ds(pl.multiple_of(j * t, 128), t)
                s = _dot_t1(qh, k_ref[cols, :]) + (cq - crow_ref[0, hd:hd + 1, cols])
                if diag:
                    s = jnp.where(tri, s, -MASKED)
                m_new = jnp.maximum(m, jnp.max(s, axis=-1, keepdims=True))
                a = jnp.exp(m - m_new)
                p = jnp.exp(s - m_new)
                l = a * l + jnp.sum(p, axis=-1, keepdims=True)
                acc = a * acc + _dot(p.astype(_MM), v_ref[cols, :])
                return m_new, l, acc

            init = (jnp.full((t, 1), -MASKED, _F32), jnp.zeros((t, 1), _F32), jnp.zeros((t, 128), _F32))
            carry = lax.fori_loop(0, i, functools.partial(block, diag=False), init)
            m, l, acc = block(i, carry, True)
            outs.append(acc / l)
            lses.append(m + jnp.log(l))
        o_ref[...] = jnp.where(masks[0], outs[0], outs[1])
        lane = lax.broadcasted_iota(jnp.int32, (t, 128), 1)
        lse_ref[0] = jnp.where(lane == 0, lses[0], jnp.where(lane == 1, lses[1], 0.0))

    return pl.pallas_call(
        body, name="attn_fwd", grid=(N_PAIRS, lp // t),
        in_specs=[pl.BlockSpec((t, 128), lambda p, i: (i, p)),
                  pl.BlockSpec((lp, 128), lambda p, i: (0, p)),
                  pl.BlockSpec((lp, 128), lambda p, i: (0, p)),
                  pl.BlockSpec((1, t, 128), lambda p, i: (p, i, 0)),
                  pl.BlockSpec((1, 8, lp), lambda p, i: (p, 0, 0))],
        out_specs=[pl.BlockSpec((t, 128), lambda p, i: (i, p)),
                   pl.BlockSpec((1, t, 128), lambda p, i: (p, i, 0))],
        out_shape=[jax.ShapeDtypeStruct((lp, D_ATTN), _F32), jax.ShapeDtypeStruct((N_PAIRS, lp, 128), _F32)],
        compiler_params=_params(("arbitrary", "arbitrary")),
    )(q, k, v, ccol, crow)


def _conv_fwd(u, ug, cw, cb, g, b, wpw):
    lp = u.shape[0]
    t = _row_tile(lp)

    def body(u_ref, ug_ref, cw_ref, cb_ref, g_ref, b_ref, wpw_ref, z_ref, pw_ref, win):
        i = pl.program_id(0)

        @pl.when(i == 0)
        def _():
            win[0:HALO, :] = jnp.zeros((HALO, D_CONV), _F32)

        win[HALO:HALO + t, :] = u_ref[...] * _sigmoid(ug_ref[...])
        acc = jnp.broadcast_to(cb_ref[...], (t, D_CONV))
        for j in range(CONV_WIDTH):
            acc = acc + cw_ref[j:j + 1, :] * win[HALO - (CONV_WIDTH - 1) + j:HALO - (CONV_WIDTH - 1) + j + t, :]
        z_ref[...] = acc
        xhat, _ = _ln_stats(acc)
        zn = xhat * g_ref[...] + b_ref[...]
        a = zn * _sigmoid(zn)
        pw_ref[...] = _dot(a.astype(_MM), wpw_ref[...])
        win[0:HALO, :] = win[t:t + HALO, :]

    row = pl.BlockSpec((t, D_CONV), lambda i: (i, 0))
    vec = _full((1, D_CONV))
    return pl.pallas_call(
        body, name="conv_fwd", grid=(lp // t,),
        in_specs=[row, row, _full((32, D_CONV)), vec, vec, vec, _full((D_CONV, D_CONV))],
        out_specs=[row, row],
        out_shape=[jax.ShapeDtypeStruct((lp, D_CONV), _F32)] * 2,
        scratch_shapes=[pltpu.VMEM((t + HALO, D_CONV), _F32)],
        compiler_params=_params(("arbitrary",)),
    )(u, ug, cw, cb, g, b, wpw)


def _out_loss_bwd(o, ga, pw, gc, h, tgt, w_out, w_out_t, g, b):
    lp = o.shape[0]
    t = _row_tile(lp)

    def body(o_ref, ga_ref, pw_ref, gc_ref, h_ref, tgt_ref, w_ref, wt_ref, g_ref, b_ref,
             dr_ref, do_ref, dga_ref, dpw_ref, dgc_ref, delta_ref, dw_ref, dg_ref, db_ref, loss_ref):
        i = pl.program_id(0)

        @pl.when(i == 0)
        def _():
            dw_ref[...] = jnp.zeros_like(dw_ref)
            dg_ref[...] = jnp.zeros_like(dg_ref)
            db_ref[...] = jnp.zeros_like(db_ref)
            loss_ref[...] = jnp.zeros_like(loss_ref)

        o, ga, pw, gc = o_ref[...], ga_ref[...], pw_ref[...], gc_ref[...]
        sa, sc = _sigmoid(ga), _sigmoid(gc)
        silu_a, silu_c = ga * sa, gc * sc
        ycat = jnp.concatenate([o * silu_a, pw * silu_c], axis=-1).astype(_MM)
        r = ALPHA * h_ref[...] + _dot(ycat, w_ref[...])
        xhat, rstd = _ln_stats(r)
        out = xhat * g_ref[...] + b_ref[...]
        rows = i * t + lax.broadcasted_iota(jnp.int32, (t, 1), 0)
        diff = jnp.where(rows >= ROW_X, out - tgt_ref[...], 0.0)
        loss_ref[...] += (0.5 / D_MODEL) * jnp.sum(diff * diff, keepdims=True)
        dout = diff * (1.0 / D_MODEL)
        dg_ref[...] += jnp.sum(dout * xhat, axis=0, keepdims=True)
        db_ref[...] += jnp.sum(dout, axis=0, keepdims=True)
        dr = _ln_bwd(dout, xhat, rstd, g_ref[...])
        dr_ref[...] = dr
        drb = dr.astype(_MM)
        dw_ref[...] += _dot_t0(ycat, drb)
        dy = _dot(drb, wt_ref[...])
        dya, dyc = dy[:, :D_ATTN], dy[:, D_ATTN:]
        do = dya * silu_a
        do_ref[...] = do.astype(do_ref.dtype)
        dga_ref[...] = (dya * o * (sa * (1.0 + ga * (1.0 - sa)))).astype(dga_ref.dtype)
        dpw_ref[...] = (dyc * silu_c).astype(dpw_ref.dtype)
        dgc_ref[...] = (dyc * pw * (sc * (1.0 + gc * (1.0 - sc)))).astype(dgc_ref.dtype)
        sel = (lax.broadcasted_iota(jnp.int32, (128, 128), 0) // HEAD_DIM
               == lax.broadcasted_iota(jnp.int32, (128, 128), 1)).astype(_F32)
        prod = do * o
        for p in range(N_PAIRS):
            delta_ref[p] = _dot_exact(prod[:, 128 * p:128 * (p + 1)], sel)

    row = lambda w: pl.BlockSpec((t, w), lambda i: (i, 0))
    return pl.pallas_call(
        body, name="out_loss_bwd", grid=(lp // t,),
        in_specs=[row(512)] * 4 + [row(D_MODEL)] * 2 + [_full((D_MODEL, D_MODEL))] * 2 + [_full((1, D_MODEL))] * 2,
        out_specs=[row(D_MODEL)] + [row(512)] * 4 + [pl.BlockSpec((N_PAIRS, t, 128), lambda i: (0, i, 0)),
                   _full((D_MODEL, D_MODEL)), _full((1, D_MODEL)), _full((1, D_MODEL)), _full((1, 1))],
        out_shape=[jax.ShapeDtypeStruct((lp, D_MODEL), _F32)] + [jax.ShapeDtypeStruct((lp, 512), _MM)] * 4
                  + [jax.ShapeDtypeStruct((N_PAIRS, lp, 128), _F32), jax.ShapeDtypeStruct((D_MODEL, D_MODEL), _F32),
                     jax.ShapeDtypeStruct((1, D_MODEL), _F32), jax.ShapeDtypeStruct((1, D_MODEL), _F32),
                     jax.ShapeDtypeStruct((1, 1), _F32)],
        compiler_params=_params(("arbitrary",)),
    )(o, ga, pw, gc, h, tgt, w_out, w_out_t, g, b)


def _attn_bwd(q, k, v, do, lse, delta, ccol, crow):
    lp = q.shape[0]
    t = _row_tile(lp)
    nb = lp // t

    def body(q_ref, k_ref, v_ref, do_ref, lse_ref, delta_ref, ccol_ref, crow_ref,
             dq_ref, dk_ref, dv_ref, dc_ref, dcq_ref):
        j = pl.program_id(1)

        @pl.when(j == 0)
        def _():
            dq_ref[...] = jnp.zeros_like(dq_ref)
            dcq_ref[...] = jnp.zeros_like(dcq_ref)

        kb, vb = k_ref[...], v_ref[...]
        masks = _head_masks()
        tri = lax.broadcasted_iota(jnp.int32, (t, t), 1) <= lax.broadcasted_iota(jnp.int32, (t, t), 0)
        lane = lax.broadcasted_iota(jnp.int32, (1, 128), 1)
        dk = jnp.zeros((t, 128), _F32)
        dv = jnp.zeros((t, 128), _F32)
        dcs = []
        for hd in range(2):
            kh = jnp.where(masks[hd], kb, jnp.zeros_like(kb))
            ck = crow_ref[0, hd:hd + 1, :]

            def block(i, carry, diag):
                dk, dv, dc = carry
                rows = pl.ds(pl.multiple_of(i * t, 128), t)
                qb, dob = q_ref[rows, :], do_ref[rows, :]
                qh = jnp.where(masks[hd], qb, jnp.zeros_like(qb))
                doh = jnp.where(masks[hd], dob, jnp.zeros_like(dob))
                s = _dot_t1(qh, kb) + (ccol_ref[0, rows, hd:hd + 1] - ck)
                if diag:
                    s = jnp.where(tri, s, -MASKED)
                p = jnp.exp(s - lse_ref[0, rows, hd:hd + 1])
                ds = p * (_dot_t1(doh, vb) - delta_ref[0, rows, hd:hd + 1])
                dsb = ds.astype(_MM)
                dv = dv + _dot_t0(p.astype(_MM), doh)
                dk = dk + _dot_t0(dsb, qh)
                dq_ref[rows, :] += _dot(dsb, kh) * ATTN_SCALE
                dcq_ref[0, rows, :] += jnp.where(lane == hd, jnp.sum(ds, axis=-1, keepdims=True), 0.0)
                return dk, dv, dc + jnp.sum(ds, axis=0, keepdims=True)

            carry = block(j, (dk, dv, jnp.zeros((1, t), _F32)), True)
            dk, dv, dc = lax.fori_loop(j + 1, nb, functools.partial(block, diag=False), carry)
            dcs.append(dc)
        dk_ref[...] = dk.astype(dk_ref.dtype)
        dv_ref[...] = dv.astype(dv_ref.dtype)
        sub = lax.broadcasted_iota(jnp.int32, (8, t), 0)
        dc_ref[0] = jnp.where(sub == 0, dcs[0], jnp.where(sub == 1, dcs[1], 0.0))

    blk = pl.BlockSpec((t, 128), lambda p, j: (j, p))
    res = pl.BlockSpec((lp, 128), lambda p, j: (0, p))
    stat = pl.BlockSpec((1, lp, 128), lambda p, j: (p, 0, 0))
    return pl.pallas_call(
        body, name="attn_bwd", grid=(N_PAIRS, nb),
        in_specs=[res, blk, blk, res, stat, stat, stat, pl.BlockSpec((1, 8, t), lambda p, j: (p, 0, j))],
        out_specs=[res, blk, blk, pl.BlockSpec((1, 8, t), lambda p, j: (p, 0, j)), stat],
        out_shape=[jax.ShapeDtypeStruct((lp, D_ATTN), _F32), jax.ShapeDtypeStruct((lp, D_ATTN), _MM),
                   jax.ShapeDtypeStruct((lp, D_ATTN), _MM), jax.ShapeDtypeStruct((N_PAIRS, 8, lp), _F32),
                   jax.ShapeDtypeStruct((N_PAIRS, lp, 128), _F32)],
        compiler_params=_params(("arbitrary", "arbitrary")),
    )(q, k, v, do, lse, delta, ccol, crow)


def _decay_bwd(dc, dcq, f, bf):
    lp = f.shape[0]
    nb = lp // 128

    def body(dc_ref, dcq_ref, f_ref, bf_ref, df_ref, dbf_ref):
        r = lax.broadcasted_iota(jnp.int32, (128, 128), 0)
        c = lax.broadcasted_iota(jnp.int32, (128, 128), 1)
        triu = (c >= r).astype(_F32)

        def step(n, carry):
            tail, dbf = carry
            bi = nb - 1 - n
            rows = pl.ds(pl.multiple_of(bi * 128, 128), 128)
            cols = jnp.zeros((128, 128), _F32)
            for p in range(N_PAIRS):
                dck = jnp.concatenate([dc_ref[p, :, rows], jnp.zeros((120, 128), _F32)], axis=0).T
                dcp = dcq_ref[p, rows, :] - dck
                cols = cols + (dcp if p == 0 else pltpu.roll(dcp, 2 * p, axis=1))
            lane = lax.broadcasted_iota(jnp.int32, (128, 128), 1)
            cols = jnp.where(lane < N_HEADS, cols, 0.0)
            rev = _dot_exact(triu, cols) + tail
            pos = bi * 128 + lax.broadcasted_iota(jnp.int32, (128, 1), 0)
            df = jnp.where(pos >= ROW_PAD, rev * (1.0 - _sigmoid(f_ref[rows, :] + bf_ref[...])), 0.0)
            df_ref[rows, :] = df.astype(df_ref.dtype)
            return rev[0:1, :], dbf + jnp.sum(df, axis=0, keepdims=True)

        zero = jnp.zeros((1, 128), _F32)
        _, dbf = lax.fori_loop(0, nb, step, (zero, zero), unroll=True)
        dbf_ref[...] = dbf

    return pl.pallas_call(
        body, name="decay_bwd",
        out_shape=[jax.ShapeDtypeStruct((lp, 128), _MM), jax.ShapeDtypeStruct((1, 128), _F32)],
        compiler_params=_params(),
    )(dc, dcq, f, bf)


def _conv_bwd_rows(dpw, z, g, b, wpw_t):
    lp = z.shape[0]
    t = _row_tile(lp)

    def body(dpw_ref, z_ref, g_ref, b_ref, wt_ref, dz_ref, dw_ref, dg_ref, db_ref, dcb_ref):
        i = pl.program_id(0)

        @pl.when(i == 0)
        def _():
            dw_ref[...] = jnp.zeros_like(dw_ref)
            dg_ref[...] = jnp.zeros_like(dg_ref)
            db_ref[...] = jnp.zeros_like(db_ref)
            dcb_ref[...] = jnp.zeros_like(dcb_ref)

        xhat, rstd = _ln_stats(z_ref[...])
        zn = xhat * g_ref[...] + b_ref[...]
        sg = _sigmoid(zn)
        dpw = dpw_ref[...]
        dw_ref[...] += _dot_t0((zn * sg).astype(_MM), dpw)
        dzn = _dot(dpw, wt_ref[...]) * (sg * (1.0 + zn * (1.0 - sg)))
        dg_ref[...] += jnp.sum(dzn * xhat, axis=0, keepdims=True)
        db_ref[...] += jnp.sum(dzn, axis=0, keepdims=True)
        dz = _ln_bwd(dzn, xhat, rstd, g_ref[...])
        dz_ref[...] = dz
        dcb_ref[...] += jnp.sum(dz, axis=0, keepdims=True)

    row = pl.BlockSpec((t, D_CONV), lambda i: (i, 0))
    vec = _full((1, D_CONV))
    vshape = jax.ShapeDtypeStruct((1, D_CONV), _F32)
    return pl.pallas_call(
        body, name="conv_bwd_rows", grid=(lp // t,),
        in_specs=[row, row, vec, vec, _full((D_CONV, D_CONV))],
        out_specs=[row, _full((D_CONV, D_CONV)), vec, vec, vec],
        out_shape=[jax.ShapeDtypeStruct((lp, D_CONV), _F32), jax.ShapeDtypeStruct((D_CONV, D_CONV), _F32),
                   vshape, vshape, vshape],
        compiler_params=_params(("arbitrary",)),
    )(dpw, z, g, b, wpw_t)


def _conv_bwd_taps(dz, u, ug, cw):
    lp = u.shape[0]
    t = _row_tile(lp)
    nt = lp // t

    def body(dz_ref, u_ref, ug_ref, cw_ref, du_ref, dug_ref, dcw_ref, win):
        i = pl.program_id(0)

        @pl.when(i == 0)
        def _():
            win[t:t + HALO, :] = jnp.zeros((HALO, D_CONV), _F32)
            dcw_ref[...] = jnp.zeros_like(dcw_ref)

        win[0:t, :] = dz_ref[...]
        u = u_ref[...]
        sg = _sigmoid(ug_ref[...])
        hh = u * sg
        dhh = jnp.zeros((t, D_CONV), _F32)
        for j in range(CONV_WIDTH):
            sh = win[CONV_WIDTH - 1 - j:CONV_WIDTH - 1 - j + t, :]
            dhh = dhh + cw_ref[j:j + 1, :] * sh
            dcw_ref[j:j + 1, :] += jnp.sum(hh * sh, axis=0, keepdims=True)
        du_ref[...] = (dhh * sg).astype(du_ref.dtype)
        dug_ref[...] = (dhh * u * sg * (1.0 - sg)).astype(dug_ref.dtype)
        win[t:t + HALO, :] = win[0:HALO, :]

    row = pl.BlockSpec((t, D_CONV), lambda i: (nt - 1 - i, 0))
    return pl.pallas_call(
        body, name="conv_bwd_taps", grid=(nt,),
        in_specs=[row, row, row, _full((32, D_CONV))],
        out_specs=[row, row, _full((32, D_CONV))],
        out_shape=[jax.ShapeDtypeStruct((lp, D_CONV), _MM)] * 2 + [jax.ShapeDtypeStruct((32, D_CONV), _F32)],
        scratch_shapes=[pltpu.VMEM((t + HALO, D_CONV), _F32)],
        compiler_params=_params(("arbitrary",)),
    )(dz, u, ug, cw)


def _proj_bwd_x(pieces, df, wpt, dr, h0, g):
    lp = h0.shape[0]
    t = _row_tile(lp)

    def body(dq_ref, dk_ref, dv_ref, dga_ref, du_ref, dug_ref, dgc_ref, df_ref, wt_ref, dr_ref, h0_ref, g_ref,
             dx_ref, dg_ref, db_ref):
        i = pl.program_id(0)

        @pl.when(i == 0)
        def _():
            dg_ref[...] = jnp.zeros_like(dg_ref)
            db_ref[...] = jnp.zeros_like(db_ref)

        dh = ALPHA * dr_ref[...]
        for n, ref in enumerate((dq_ref, dk_ref, dv_ref, dga_ref, du_ref, dug_ref, dgc_ref)):
            dh = dh + _dot(ref[...].astype(_MM), wt_ref[512 * n:512 * (n + 1), :])
        dh = dh + _dot(df_ref[...], wt_ref[3584:3712, :])
        rows = i * t + lax.broadcasted_iota(jnp.int32, (t, 1), 0)
        dh = jnp.where(rows >= ROW_PAD, dh, 0.0)
        xhat, rstd = _ln_stats(h0_ref[...])
        dg_ref[...] += jnp.sum(dh * xhat, axis=0, keepdims=True)
        db_ref[...] += jnp.sum(dh, axis=0, keepdims=True)
        dx_ref[...] = _ln_bwd(dh, xhat, rstd, g_ref[...])

    row = lambda w: pl.BlockSpec((t, w), lambda i: (i, 0))
    vec = _full((1, D_MODEL))
    return pl.pallas_call(
        body, name="proj_bwd_x", grid=(lp // t,),
        in_specs=[row(512)] * 7 + [row(128), _full((D_IN_PAD, D_MODEL)), row(D_MODEL), row(D_MODEL), vec],
        out_specs=[row(D_MODEL), vec, vec],
        out_shape=[jax.ShapeDtypeStruct((lp, D_MODEL), _F32), jax.ShapeDtypeStruct((1, D_MODEL), _F32),
                   jax.ShapeDtypeStruct((1, D_MODEL), _F32)],
        compiler_params=_params(("arbitrary",)),
    )(*pieces, df, wpt, dr, h0, g)


def _proj_bwd_w(h, pieces, name):
    lp = h.shape[0]
    t = _row_tile(lp)
    n = len(pieces)
    widths = [p.shape[1] for p in pieces]

    def body(*refs):
        h_ref, d_refs, w_refs = refs[0], refs[1:1 + n], refs[1 + n:]
        i = pl.program_id(0)

        @pl.when(i == 0)
        def _():
            for w_ref in w_refs:
                w_ref[...] = jnp.zeros_like(w_ref)

        hb = h_ref[...].astype(_MM)
        for d_ref, w_ref in zip(d_refs, w_refs):
            w_ref[...] += _dot_t0(hb, d_ref[...].astype(_MM))

    row = lambda w: pl.BlockSpec((t, w), lambda i: (i, 0))
    return pl.pallas_call(
        body, name=name, grid=(lp // t,),
        in_specs=[row(D_MODEL)] + [row(w) for w in widths],
        out_specs=[_full((D_MODEL, w)) for w in widths],
        out_shape=[jax.ShapeDtypeStruct((D_MODEL, w), _F32) for w in widths],
        compiler_params=_params(("arbitrary",)),
    )(h, *pieces)


def _adamw_math(w, g, m, v):
    m = ADAM_B1 * m + (1.0 - ADAM_B1) * g
    v = ADAM_B2 * v + (1.0 - ADAM_B2) * (g * g)
    m_hat = m / (1.0 - ADAM_B1 ** ADAM_STEP)
    v_hat = v / (1.0 - ADAM_B2 ** ADAM_STEP)
    delta = -ADAM_LR * (m_hat / (jnp.sqrt(v_hat) + ADAM_EPS) + ADAM_WD * w)
    return delta, m, v


def _adamw_tiled(w, g, m, v, rows, name):
    r, c = w.shape

    def body(w_ref, g_ref, m_ref, v_ref, d_ref, nm_ref, nv_ref):
        d_ref[...], nm_ref[...], nv_ref[...] = _adamw_math(w_ref[...], g_ref[...], m_ref[...], v_ref[...])

    spec = pl.BlockSpec((rows, c), lambda i: (i, 0))
    return pl.pallas_call(
        body, name=name, grid=(r // rows,), in_specs=[spec] * 4, out_specs=[spec] * 3,
        out_shape=[jax.ShapeDtypeStruct((r, c), _F32)] * 3,
        compiler_params=_params(("arbitrary",)),
    )(w, g, m, v)


def _adamw_small(ws, gs, ms, vs):
    n = len(ws)

    def body(*refs):
        ins, outs = refs[:4 * n], refs[4 * n:]
        for k in range(n):
            d, m, v = _adamw_math(ins[k][...], ins[n + k][...], ins[2 * n + k][...], ins[3 * n + k][...])
            outs[k][...], outs[n + k][...], outs[2 * n + k][...] = d, m, v

    shapes = [jax.ShapeDtypeStruct(w.shape, _F32) for w in ws]
    res = pl.pallas_call(body, name="adamw_small", out_shape=shapes * 3, compiler_params=_params())(*ws, *gs, *ms, *vs)
    return res[:n], res[n:2 * n], res[2 * n:]


def _place():
    x, y, c = lax.axis_index("x"), lax.axis_index("y"), lax.axis_index("c")
    chips = [(1 - x, y), (x, 1 - y), (1 - x, 1 - y)]
    return x, y, c, chips


def _half(ref, slot, c, rows):
    hr = rows // 2
    return ref.at[slot, pl.ds(pl.multiple_of(c * hr, 8), hr), :]


def _all_gather(shards):
    n = len(shards)

    def body(*refs):
        ins, outs = refs[:n], refs[n:2 * n]
        send, recv, local = refs[2 * n:]
        x, y, c, chips = _place()
        me = 2 * x + y
        sibling = (x, y, 1 - c)

        def copy(k, t, slot, half, to, src=None):
            rows = ins[t].shape[0]
            dst = _half(outs[t], slot, half, rows)
            return pltpu.make_async_remote_copy(
                src_ref=dst if src is None else src, dst_ref=dst,
                send_sem=send.at[k], recv_sem=recv.at[k], device_id=to, device_id_type=MESH)

        own = [pltpu.make_async_copy(ins[t], outs[t].at[me], local.at[t]) for t in range(n)]
        for cp in own:
            cp.start()
        first, passed = [], []
        for t in range(n):
            hr = ins[t].shape[0] // 2
            mine = ins[t].at[pl.ds(pl.multiple_of(c * hr, 8), hr), :]
            for j, chip in enumerate(chips):
                first.append(copy(6 * t + j, t, me, c, (*chip, c), src=mine))
        for cp in first:
            cp.start()
        for t in range(n):
            for j, (cx, cy) in enumerate(chips):
                copy(6 * t + j, t, 2 * cx + cy, c, sibling).wait_recv()
                fwd = copy(6 * t + 3 + j, t, 2 * cx + cy, c, sibling)
                fwd.start()
                passed.append(fwd)
        for t in range(n):
            for j, (cx, cy) in enumerate(chips):
                copy(6 * t + 3 + j, t, 2 * cx + cy, 1 - c, sibling).wait_recv()
        for cp in first + passed:
            cp.wait_send()
        for cp in own:
            cp.wait()

    any_spec = pl.BlockSpec(memory_space=pl.ANY)
    return pl.pallas_call(
        body, name="all_gather",
        in_specs=[any_spec] * n, out_specs=[any_spec] * n,
        out_shape=[jax.ShapeDtypeStruct((N_SHARD,) + s.shape, s.dtype) for s in shards],
        scratch_shapes=[pltpu.SemaphoreType.DMA((6 * n,)), pltpu.SemaphoreType.DMA((6 * n,)),
                        pltpu.SemaphoreType.DMA((n,))],
    )(*shards)


def _sibling_exchange(bufs):
    n = len(bufs)

    def body(*refs):
        ins, outs = refs[:n], refs[n:2 * n]
        send, recv = refs[2 * n:]
        x, y, c, _ = _place()
        cps = []
        for t in range(n):
            hr = ins[t].shape[1] // 2
            src = ins[t].at[:, pl.ds(pl.multiple_of((1 - c) * hr, 8), hr), :]
            cps.append(pltpu.make_async_remote_copy(
                src_ref=src, dst_ref=outs[t], send_sem=send.at[t], recv_sem=recv.at[t],
                device_id=(x, y, 1 - c), device_id_type=MESH))
        for cp in cps:
            cp.start()
        for cp in cps:
            cp.wait()

    any_spec = pl.BlockSpec(memory_space=pl.ANY)
    return pl.pallas_call(
        body, name="sibling_exchange",
        in_specs=[any_spec] * n, out_specs=[any_spec] * n,
        out_shape=[jax.ShapeDtypeStruct((N_SHARD, b.shape[1] // 2, b.shape[2]), b.dtype) for b in bufs],
        scratch_shapes=[pltpu.SemaphoreType.DMA((n,)), pltpu.SemaphoreType.DMA((n,))],
    )(*bufs)


def _half_sum(bufs, got):
    n = len(bufs)

    def body(*refs):
        ins, gots, outs = refs[:n], refs[n:2 * n], refs[2 * n:]
        c = lax.axis_index("c")
        for t in range(n):
            hr = ins[t].shape[1] // 2
            outs[t][0] = ins[t][0, pl.ds(pl.multiple_of(c * hr, 8), hr), :] + gots[t][0]

    slot = lambda shape: pl.BlockSpec((1,) + tuple(shape[1:]), lambda s: (s, 0, 0))
    return pl.pallas_call(
        body, name="half_sum", grid=(N_SHARD,),
        in_specs=[slot(b.shape) for b in bufs] + [slot(g.shape) for g in got],
        out_specs=[slot(g.shape) for g in got],
        out_shape=[jax.ShapeDtypeStruct(g.shape, g.dtype) for g in got],
        compiler_params=_params(("arbitrary",)),
    )(*bufs, *got)


def _chip_exchange(parts):
    n = len(parts)

    def body(*refs):
        ins, outs = refs[:n], refs[n:2 * n]
        send, recv, local = refs[2 * n:]
        x, y, c, chips = _place()
        me = 2 * x + y
        own = [pltpu.make_async_copy(ins[t].at[me], outs[t].at[me], local.at[t]) for t in range(n)]
        cps = []
        for t in range(n):
            for j, (cx, cy) in enumerate(chips):
                cps.append(pltpu.make_async_remote_copy(
                    src_ref=ins[t].at[2 * cx + cy], dst_ref=outs[t].at[me],
                    send_sem=send.at[3 * t + j], recv_sem=recv.at[3 * t + j],
                    device_id=(cx, cy, c), device_id_type=MESH))
        for cp in own + cps:
            cp.start()
        for cp in cps + own:
            cp.wait()

    any_spec = pl.BlockSpec(memory_space=pl.ANY)
    return pl.pallas_call(
        body, name="chip_exchange",
        in_specs=[any_spec] * n, out_specs=[any_spec] * n,
        out_shape=[jax.ShapeDtypeStruct(p.shape, p.dtype) for p in parts],
        scratch_shapes=[pltpu.SemaphoreType.DMA((3 * n,)), pltpu.SemaphoreType.DMA((3 * n,)),
                        pltpu.SemaphoreType.DMA((n,))],
    )(*parts)


def _chip_sum(got):
    n = len(got)

    def body(*refs):
        gots, outs = refs[:n], refs[n:]
        for t in range(n):
            outs[t][...] = ((gots[t][0] + gots[t][1]) + gots[t][2]) + gots[t][3]

    return pl.pallas_call(
        body, name="chip_sum", out_shape=[jax.ShapeDtypeStruct(g.shape[1:], g.dtype) for g in got],
        compiler_params=_params(),
    )(*got)


def _sibling_share(halves):
    n = len(halves)

    def body(*refs):
        ins, outs = refs[:n], refs[n:2 * n]
        send, recv, local = refs[2 * n:]
        x, y, c, _ = _place()
        cps, own = [], []
        for t in range(n):
            hr = ins[t].shape[0]
            dst = outs[t].at[pl.ds(pl.multiple_of(c * hr, 8), hr), :]
            own.append(pltpu.make_async_copy(ins[t], dst, local.at[t]))
            cps.append(pltpu.make_async_remote_copy(
                src_ref=ins[t], dst_ref=dst, send_sem=send.at[t], recv_sem=recv.at[t],
                device_id=(x, y, 1 - c), device_id_type=MESH))
        for cp in own + cps:
            cp.start()
        for cp in cps + own:
            cp.wait()

    any_spec = pl.BlockSpec(memory_space=pl.ANY)
    return pl.pallas_call(
        body, name="sibling_share",
        in_specs=[any_spec] * n, out_specs=[any_spec] * n,
        out_shape=[jax.ShapeDtypeStruct((2 * h.shape[0], h.shape[1]), h.dtype) for h in halves],
        scratch_shapes=[pltpu.SemaphoreType.DMA((n,)), pltpu.SemaphoreType.DMA((n,)), pltpu.SemaphoreType.DMA((n,))],
    )(*halves)


def _reduce_scatter(bufs):
    chip_part = _half_sum(bufs, _sibling_exchange(bufs))
    half = _chip_sum(_chip_exchange(chip_part))
    return _sibling_share(half)


REPL_SIZES = (D_MODEL, D_MODEL, N_HEADS, D_CONV, D_CONV, D_CONV, D_MODEL, D_MODEL)
REPL_ROWS = 48


def _pack_repl(parts):
    flat = jnp.concatenate([p.reshape(-1).astype(_F32) for p in parts])
    return jnp.pad(flat, (0, REPL_ROWS * 128 - flat.shape[0])).reshape(REPL_ROWS, 128)


def _unpack_repl(packed, shapes):
    flat = packed.reshape(-1)
    out, off = [], 0
    for shape, size in zip(shapes, REPL_SIZES):
        out.append(flat[off:off + size].reshape(shape))
        off += size
    return out


def _local_grads(x, target, meta, ln_in_g, ln_in_b, w_in, b_f, conv_w, conv_b, ln_conv_g, ln_conv_b,
                 w_pw, w_out, ln_out_g, ln_out_b):
    seq = x.shape[0]
    lp = ROW_X + seq
    vec = lambda a: a.reshape(1, -1).astype(_F32)
    h0 = jnp.concatenate([jnp.zeros((ROW_PAD, D_MODEL), _F32), meta, x], axis=0)
    tgt = jnp.concatenate([jnp.zeros((ROW_X, D_MODEL), _F32), target], axis=0)
    zcol = jnp.zeros((D_MODEL, 128 - N_HEADS), w_in.dtype)
    wp = jnp.concatenate([w_in[:, :OFF_F], w_in[:, OFF_F + N_HEADS:], w_in[:, OFF_F:OFF_F + N_HEADS], zcol], axis=1)
    bf = jnp.pad(vec(b_f), ((0, 0), (0, 128 - N_HEADS)))
    cw = jnp.pad(conv_w, ((0, 32 - CONV_WIDTH), (0, 0)))

    h, q, k, v, ga, u, ug, gc, f = _ln_proj(h0, vec(ln_in_g), vec(ln_in_b), wp)
    ccol, crow = _decay_cumsum(f, bf)
    o, lse = _attn_fwd(q, k, v, ccol, crow)
    z, pw = _conv_fwd(u, ug, cw, vec(conv_b), vec(ln_conv_g), vec(ln_conv_b), w_pw)
    dr, do, dga, dpw, dgc, delta, dw_out, dg_out, db_out, loss = _out_loss_bwd(
        o, ga, pw, gc, h, tgt, w_out, w_out.T, vec(ln_out_g), vec(ln_out_b))
    dq, dk, dv, dc, dcq = _attn_bwd(q, k, v, do, lse, delta, ccol, crow)
    df, dbf = _decay_bwd(dc, dcq, f, bf)
    dz, dw_pw, dg_conv, db_conv, dconv_b = _conv_bwd_rows(dpw, z, vec(ln_conv_g), vec(ln_conv_b), w_pw.T)
    du, dug, dcw = _conv_bwd_taps(dz, u, ug, cw)
    pieces = (dq, dk, dv, dga, du, dug, dgc)
    dx, dg_in, db_in = _proj_bwd_x(pieces, df, wp.T, dr, h0, vec(ln_in_g))
    dwq, dwk, dwv, dwga = _proj_bwd_w(h, pieces[:4], "proj_bwd_w_attn")
    dwu, dwug, dwgc, dwf = _proj_bwd_w(h, pieces[4:] + (df,), "proj_bwd_w_conv")
    dw_in = jnp.concatenate([dwq, dwk, dwv, dwf[:, :N_HEADS], dwga, dwu, dwug, dwgc], axis=1)
    grads = dict(meta=dx[ROW_PAD:ROW_X], ln_in_g=dg_in, ln_in_b=db_in, w_in=dw_in, b_f=dbf[:, :N_HEADS],
                 conv_w=dcw[:CONV_WIDTH], conv_b=dconv_b, ln_conv_g=dg_conv, ln_conv_b=db_conv,
                 w_pw=dw_pw, w_out=dw_out, ln_out_g=dg_out, ln_out_b=db_out)
    return loss, dx, grads


def kernel(x, meta, ln_in_g, ln_in_b, w_in, b_f, conv_w, conv_b, ln_conv_g, ln_conv_b, w_pw, w_out, ln_out_g, ln_out_b, loss_target, m_meta, m_ln_in_g, m_ln_in_b, m_w_in, m_b_f, m_conv_w, m_conv_b, m_ln_conv_g, m_ln_conv_b, m_w_pw, m_w_out, m_ln_out_g, m_ln_out_b, v_meta, v_ln_in_g, v_ln_in_b, v_w_in, v_b_f, v_conv_w, v_conv_b, v_ln_conv_g, v_ln_conv_b, v_w_pw, v_w_out, v_ln_out_g, v_ln_out_b):
    w_in2, conv_w2, w_pw2, w_out2 = w_in[0], conv_w[0], w_pw[0], w_out[0]

    small = jnp.concatenate([jnp.pad(conv_w2, ((0, 1), (0, 0))), meta.reshape(32, 128)], axis=0)
    g_in, g_pw, g_out, g_small = _all_gather(
        [w_in2.astype(_MM), w_pw2.astype(_MM), w_out2.astype(_MM), small])
    w_in_full = jnp.transpose(g_in, (1, 0, 2)).reshape(D_MODEL, D_IN)
    w_pw_full = g_pw.reshape(D_CONV, D_CONV)
    w_out_full = g_out.reshape(D_MODEL, D_MODEL)
    conv_w_full = jnp.transpose(g_small[:, :CONV_WIDTH, :], (1, 0, 2)).reshape(CONV_WIDTH, D_CONV)
    meta_full = jnp.transpose(g_small[:, 32:64, :].reshape(N_SHARD, N_META, 256), (1, 0, 2)).reshape(N_META, D_MODEL)

    loss, dx, gr = _local_grads(x[0], loss_target[0], meta_full, ln_in_g, ln_in_b, w_in_full, b_f, conv_w_full,
                                conv_b, ln_conv_g, ln_conv_b, w_pw_full, w_out_full, ln_out_g, ln_out_b)
    grad_x = dx[ROW_X:][None]

    repl_names = ("ln_in_g", "ln_in_b", "b_f", "conv_b", "ln_conv_g", "ln_conv_b", "ln_out_g", "ln_out_b")
    repl = _pack_repl([gr[n] for n in repl_names] + [loss])
    gcw = jnp.transpose(jnp.pad(gr["conv_w"], ((0, 1), (0, 0))).reshape(32, N_SHARD, 128), (1, 0, 2))
    gmeta = jnp.transpose(gr["meta"].reshape(N_META, N_SHARD, 256), (1, 0, 2)).reshape(N_SHARD, 32, 128)
    small_g = jnp.concatenate([gcw, gmeta, jnp.broadcast_to(repl[None], (N_SHARD, REPL_ROWS, 128))], axis=1)
    bufs = [jnp.transpose(gr["w_in"].reshape(D_MODEL, N_SHARD, W_IN_SHARD), (1, 0, 2)),
            gr["w_pw"].reshape(N_SHARD, D_CONV // N_SHARD, D_CONV),
            gr["w_out"].reshape(N_SHARD, D_MODEL // N_SHARD, D_MODEL),
            small_g]
    gw_in, gw_pw, gw_out, gsmall = _reduce_scatter(bufs)
    g_conv_w = gsmall[:CONV_WIDTH]
    g_meta = gsmall[32:64].reshape(N_META, 256)
    g_repl = gsmall[64:64 + REPL_ROWS]

    repl_w = (ln_in_g, ln_in_b, b_f, conv_b, ln_conv_g, ln_conv_b, ln_out_g, ln_out_b)
    repl_m = (m_ln_in_g, m_ln_in_b, m_b_f, m_conv_b, m_ln_conv_g, m_ln_conv_b, m_ln_out_g, m_ln_out_b)
    repl_v = (v_ln_in_g, v_ln_in_b, v_b_f, v_conv_b, v_ln_conv_g, v_ln_conv_b, v_ln_out_g, v_ln_out_b)
    zero = jnp.zeros((1,), _F32)
    d_in, nm_in, nv_in = _adamw_tiled(w_in2, gw_in, m_w_in[0], v_w_in[0], 256, "adamw_w_in")
    ws = [w_pw2, w_out2, conv_w2, meta, _pack_repl(repl_w + (zero,))]
    gs = [gw_pw, gw_out, g_conv_w, g_meta, g_repl]
    ms = [m_w_pw[0], m_w_out[0], m_conv_w[0], m_meta, _pack_repl(repl_m + (zero,))]
    vs = [v_w_pw[0], v_w_out[0], v_conv_w[0], v_meta, _pack_repl(repl_v + (zero + 1.0,))]
    ds, nms, nvs = _adamw_small(ws, gs, ms, vs)

    shapes = [w.shape for w in repl_w]
    total_loss = g_repl.reshape(-1)[sum(REPL_SIZES)]
    names = ("meta", "ln_in_g", "ln_in_b", "w_in", "b_f", "conv_w", "conv_b", "ln_conv_g", "ln_conv_b",
             "w_pw", "w_out", "ln_out_g", "ln_out_b")

    def assemble(t_in, smalls):
        pw_, out_, cw_, meta_, packed = smalls
        r = dict(zip(repl_names, _unpack_repl(packed, shapes)))
        r.update(meta=meta_, w_in=t_in[None], conv_w=cw_[None], w_pw=pw_[None], w_out=out_[None])
        return [r[n] for n in names]

    return (total_loss, grad_x, *assemble(gw_in, gs), *assemble(d_in, ds), *assemble(nm_in, nms),
            *assemble(nv_in, nvs))
```

```python
import functools

import jax
import jax.numpy as jnp
from jax import lax
from jax.experimental import pallas as pl
from jax.experimental.pallas import tpu as pltpu

D_MODEL = 1024
N_META = 16
D_ATTN = 512
D_CONV = 512
N_HEADS = 8
HEAD_DIM = 64
N_PAIRS = 4
CONV_WIDTH = 31
LN_EPS = 1e-5
ALPHA = 2.0 ** 0.25
ATTN_SCALE = 0.125
D_IN = 3592
OFF_F = 3 * D_ATTN
D_IN_PAD = 3712
N_SHARD = 4
W_IN_SHARD = D_IN // N_SHARD

ROW_PAD = 112
ROW_X = ROW_PAD + N_META
HALO = 32
MASKED = 1e30

ADAM_LR = 0.001
ADAM_B1 = 0.9
ADAM_B2 = 0.999
ADAM_EPS = 1e-08
ADAM_WD = 0.01
ADAM_STEP = 10

VMEM_LIMIT = 56 * 1024 * 1024

_MM = jnp.bfloat16
_F32 = jnp.float32
MESH = pl.DeviceIdType.MESH


def _params(sem=None):
    return pltpu.CompilerParams(dimension_semantics=sem, vmem_limit_bytes=VMEM_LIMIT)


def _row_tile(n_rows):
    for t in (384, 256, 128):
        if n_rows % t == 0:
            return t
    raise ValueError(f"padded sequence length {n_rows} is not a multiple of 128")


def _sigmoid(x):
    return 1.0 / (1.0 + jnp.exp(-x))


def _ln_stats(x):
    mu = jnp.mean(x, axis=-1, keepdims=True)
    xc = x - mu
    var = jnp.mean(xc * xc, axis=-1, keepdims=True)
    rstd = lax.rsqrt(var + LN_EPS)
    return xc * rstd, rstd


def _ln_bwd(dy, xhat, rstd, g):
    dxh = dy * g
    m1 = jnp.mean(dxh, axis=-1, keepdims=True)
    m2 = jnp.mean(dxh * xhat, axis=-1, keepdims=True)
    return rstd * (dxh - m1 - xhat * m2)


def _dot(a, b):
    return jnp.dot(a, b, preferred_element_type=_F32)


def _dot_t0(a, b):
    return lax.dot_general(a, b, (((0,), (0,)), ((), ())), preferred_element_type=_F32)


def _dot_t1(a, b):
    return lax.dot_general(a, b, (((1,), (1,)), ((), ())), preferred_element_type=_F32)


def _dot_exact(a, b):
    return jnp.dot(a, b, preferred_element_type=_F32, precision=lax.Precision.HIGHEST)


def _full(shape):
    return pl.BlockSpec(shape, lambda *_: (0,) * len(shape))


def _ln_proj(h0, g, b, wp):
    lp = h0.shape[0]
    t = _row_tile(lp)

    def body(h0_ref, g_ref, b_ref, w_ref, h_ref, q_ref, k_ref, v_ref, ga_ref, u_ref, ug_ref, gc_ref, f_ref, vt_ref):
        i = pl.program_id(0)
        xhat, _ = _ln_stats(h0_ref[...])
        rows = i * t + lax.broadcasted_iota(jnp.int32, (t, 1), 0)
        h = jnp.where(rows >= ROW_PAD, xhat * g_ref[...] + b_ref[...], 0.0)
        h_ref[...] = h
        hb = h.astype(_MM)
        q_ref[...] = (_dot(hb, w_ref[:, 0:512]) * ATTN_SCALE).astype(q_ref.dtype)
        k_ref[...] = _dot(hb, w_ref[:, 512:1024]).astype(k_ref.dtype)
        v = _dot(hb, w_ref[:, 1024:1536])
        v_ref[...] = v.astype(v_ref.dtype)
        vt_ref[...] = v.T.astype(vt_ref.dtype)
        ga_ref[...] = _dot(hb, w_ref[:, 1536:2048])
        u_ref[...] = _dot(hb, w_ref[:, 2048:2560])
        ug_ref[...] = _dot(hb, w_ref[:, 2560:3072])
        gc_ref[...] = _dot(hb, w_ref[:, 3072:3584])
        f_ref[...] = _dot(hb, w_ref[:, 3584:3712])

    row = lambda w: pl.BlockSpec((t, w), lambda i: (i, 0))
    f32 = lambda w: jax.ShapeDtypeStruct((lp, w), _F32)
    mm = lambda w: jax.ShapeDtypeStruct((lp, w), _MM)
    return pl.pallas_call(
        body, name="ln_proj", grid=(lp // t,),
        in_specs=[row(D_MODEL), _full((1, D_MODEL)), _full((1, D_MODEL)), _full((D_MODEL, D_IN_PAD))],
        out_specs=[row(D_MODEL)] + [row(512)] * 7 + [row(128), pl.BlockSpec((512, t), lambda i: (0, i))],
        out_shape=[f32(D_MODEL), mm(512), mm(512), mm(512), f32(512), f32(512), f32(512), f32(512), f32(128),
                   jax.ShapeDtypeStruct((512, lp), _MM)],
        compiler_params=_params(("arbitrary",)),
    )(h0, g, b, wp)


def _log_sigmoid(z):
    return jnp.minimum(z, 0.0) - jnp.log(1.0 + jnp.exp(-jnp.abs(z)))


def _decay_cumsum(f, bf):
    lp = f.shape[0]
    nb = lp // 128

    def body(f_ref, bf_ref, ccol_ref):
        r = lax.broadcasted_iota(jnp.int32, (128, 128), 0)
        c = lax.broadcasted_iota(jnp.int32, (128, 128), 1)
        tril = (c <= r).astype(_F32)

        def step(bi, carry):
            rows = pl.ds(pl.multiple_of(bi * 128, 128), 128)
            lf = _log_sigmoid(f_ref[rows, :] + bf_ref[...])
            cb = _dot_exact(tril, lf) + carry
            for p in range(N_PAIRS):
                ccol_ref[p, rows, :] = cb if p == 0 else pltpu.roll(cb, 128 - 2 * p, axis=1)
            return cb[127:128, :]

        lax.fori_loop(0, nb, step, jnp.zeros((1, 128), _F32))

    return pl.pallas_call(
        body, name="decay_cumsum",
        out_shape=jax.ShapeDtypeStruct((N_PAIRS, lp, 128), _F32),
        compiler_params=_params(),
    )(f, bf)


def _split3(x):
    hi = x.astype(_MM)
    r1 = x - hi.astype(_F32)
    mid = r1.astype(_MM)
    return hi, mid, (r1 - mid.astype(_F32)).astype(_MM)


def _augment(x, hd, first, second):
    lane = lax.broadcasted_iota(jnp.int32, (1, 128), 1)
    base = HEAD_DIM * (1 - hd)
    out = jnp.where(lane // HEAD_DIM == hd, x, jnp.zeros_like(x))
    for slot, val in enumerate((first, second)):
        lo = base + 3 * slot
        if val is None:
            continue
        if isinstance(val, float):
            out = jnp.where((lane >= lo) & (lane < lo + 3), jnp.asarray(val, x.dtype), out)
        else:
            for n, piece in enumerate(_split3(val)):
                out = jnp.where(lane == lo + n, piece, out)
    return out


def _transpose_mm(x):
    return x.astype(_F32).T.astype(_MM)


def _diag_valid(i, t):
    sub = lax.broadcasted_iota(jnp.int32, (t, t), 0)
    return (sub <= lax.broadcasted_iota(jnp.int32, (t, t), 1)) & (i * t + sub >= ROW_PAD)


def _key_operand(k, c, hd, first_row):
    pos = first_row + lax.broadcasted_iota(jnp.int32, (k.shape[0], 1), 0)
    return _augment(k, hd, jnp.where(pos >= ROW_PAD, -c, -MASKED), 1.0)


def _attn_fwd(q, k, vt, ccol):
    lp = q.shape[0]
    t = _row_tile(lp)
    nb = lp // t

    def body(q_ref, k_ref, vt_ref, ccol_ref, o_ref, lse_ref, ka_ref):
        i = pl.program_id(1)

        @pl.when(i == 0)
        def _():
            for hd in range(2):
                for n in range(nb):
                    rows = slice(n * t, (n + 1) * t)
                    ka_ref[hd, rows, :] = _key_operand(k_ref[rows, :], ccol_ref[0, rows, hd:hd + 1], hd, n * t)

        qats = [_transpose_mm(_augment(q_ref[...], hd, 1.0, None)) for hd in range(2)]

        def scores(j):
            keys = pl.ds(pl.multiple_of(j * t, 128), t)
            return tuple(_dot(ka_ref[hd, keys, :], qats[hd]) for hd in range(2))

        def update(j, sts, states, diag):
            vt = vt_ref[:, pl.ds(pl.multiple_of(j * t, 128), t)]
            new = []
            for st, (m, l, acc) in zip(sts, states):
                if diag:
                    st = jnp.where(_diag_valid(i, t), st, -MASKED)
                m_new = jnp.maximum(m, jnp.max(st, axis=0, keepdims=True))
                a = jnp.exp(m - m_new)
                pt = jnp.exp(st - m_new)
                l = a * l + jnp.sum(pt, axis=0, keepdims=True)
                new.append((m_new, l, a * acc + _dot(vt, pt.astype(_MM))))
            return tuple(new)

        def step(j, carry):
            sts, states = carry
            return scores(j + 1), update(j, sts, states, False)

        init = (jnp.full((1, t), -MASKED, _F32), jnp.zeros((1, t), _F32), jnp.zeros((128, t), _F32))
        sts, states = lax.fori_loop(0, i, step, (scores(0), (init, init)))
        (m0, l0, acc0), (m1, l1, acc1) = update(i, sts, states, True)
        row = lax.broadcasted_iota(jnp.int32, (128, t), 0)
        o_ref[...] = jnp.where(row < HEAD_DIM, acc0 / l0, acc1 / l1).T
        lse_ref[0] = jnp.where(row == 0, m0 + jnp.log(l0), jnp.where(row == 1, m1 + jnp.log(l1), 0.0)).T

    return pl.pallas_call(
        body, name="attn_fwd", grid=(N_PAIRS, nb),
        in_specs=[pl.BlockSpec((t, 128), lambda p, i: (i, p)),
                  pl.BlockSpec((lp, 128), lambda p, i: (0, p)),
                  pl.BlockSpec((128, lp), lambda p, i: (p, 0)),
                  pl.BlockSpec((1, lp, 128), lambda p, i: (p, 0, 0))],
        out_specs=[pl.BlockSpec((t, 128), lambda p, i: (i, p)),
                   pl.BlockSpec((1, t, 128), lambda p, i: (p, i, 0))],
        out_shape=[jax.ShapeDtypeStruct((lp, D_ATTN), _F32), jax.ShapeDtypeStruct((N_PAIRS, lp, 128), _F32)],
        scratch_shapes=[pltpu.VMEM((2, lp, 128), _MM)],
        compiler_params=_params(("arbitrary", "arbitrary")),
    )(q, k, vt, ccol)


def _conv_fwd(u, ug, cw, cb, g, b, wpw):
    lp = u.shape[0]
    t = _row_tile(lp)

    def body(u_ref, ug_ref, cw_ref, cb_ref, g_ref, b_ref, wpw_ref, z_ref, pw_ref, win):
        i = pl.program_id(0)

        @pl.when(i == 0)
        def _():
            win[0:HALO, :] = jnp.zeros((HALO, D_CONV), _F32)

        win[HALO:HALO + t, :] = u_ref[...] * _sigmoid(ug_ref[...])
        acc = jnp.broadcast_to(cb_ref[...], (t, D_CONV))
        for j in range(CONV_WIDTH):
            acc = acc + cw_ref[j:j + 1, :] * win[HALO - (CONV_WIDTH - 1) + j:HALO - (CONV_WIDTH - 1) + j + t, :]
        z_ref[...] = acc
        xhat, _ = _ln_stats(acc)
        zn = xhat * g_ref[...] + b_ref[...]
        a = zn * _sigmoid(zn)
        pw_ref[...] = _dot(a.astype(_MM), wpw_ref[...])
        win[0:HALO, :] = win[t:t + HALO, :]

    row = pl.BlockSpec((t, D_CONV), lambda i: (i, 0))
    vec = _full((1, D_CONV))
    return pl.pallas_call(
        body, name="conv_fwd", grid=(lp // t,),
        in_specs=[row, row, _full((32, D_CONV)), vec, vec, vec, _full((D_CONV, D_CONV))],
        out_specs=[row, row],
        out_shape=[jax.ShapeDtypeStruct((lp, D_CONV), _F32)] * 2,
        scratch_shapes=[pltpu.VMEM((t + HALO, D_CONV), _F32)],
        compiler_params=_params(("arbitrary",)),
    )(u, ug, cw, cb, g, b, wpw)


def _out_loss_bwd(o, ga, pw, gc, h, tgt, w_out, w_out_t, g, b):
    lp = o.shape[0]
    t = _row_tile(lp)

    def body(o_ref, ga_ref, pw_ref, gc_ref, h_ref, tgt_ref, w_ref, wt_ref, g_ref, b_ref,
             dr_ref, do_ref, dga_ref, dpw_ref, dgc_ref, delta_ref, dw_ref, dg_ref, db_ref, loss_ref):
        i = pl.program_id(0)

        @pl.when(i == 0)
        def _():
            dw_ref[...] = jnp.zeros_like(dw_ref)
            dg_ref[...] = jnp.zeros_like(dg_ref)
            db_ref[...] = jnp.zeros_like(db_ref)
            loss_ref[...] = jnp.zeros_like(loss_ref)

        o, ga, pw, gc = o_ref[...], ga_ref[...], pw_ref[...], gc_ref[...]
        sa, sc = _sigmoid(ga), _sigmoid(gc)
        silu_a, silu_c = ga * sa, gc * sc
        ycat = jnp.concatenate([o * silu_a, pw * silu_c], axis=-1).astype(_MM)
        r = ALPHA * h_ref[...] + _dot(ycat, w_ref[...])
        xhat, rstd = _ln_stats(r)
        out = xhat * g_ref[...] + b_ref[...]
        rows = i * t + lax.broadcasted_iota(jnp.int32, (t, 1), 0)
        diff = jnp.where(rows >= ROW_X, out - tgt_ref[...], 0.0)
        loss_ref[...] += (0.5 / D_MODEL) * jnp.sum(diff * diff, keepdims=True)
        dout = diff * (1.0 / D_MODEL)
        dg_ref[...] += jnp.sum(dout * xhat, axis=0, keepdims=True)
        db_ref[...] += jnp.sum(dout, axis=0, keepdims=True)
        dr = _ln_bwd(dout, xhat, rstd, g_ref[...])
        dr_ref[...] = dr
        drb = dr.astype(_MM)
        dw_ref[...] += _dot_t0(ycat, drb)
        dy = _dot(drb, wt_ref[...])
        dya, dyc = dy[:, :D_ATTN], dy[:, D_ATTN:]
        do = dya * silu_a
        do_ref[...] = do.astype(do_ref.dtype)
        dga_ref[...] = (dya * o * (sa * (1.0 + ga * (1.0 - sa)))).astype(dga_ref.dtype)
        dpw_ref[...] = (dyc * silu_c).astype(dpw_ref.dtype)
        dgc_ref[...] = (dyc * pw * (sc * (1.0 + gc * (1.0 - sc)))).astype(dgc_ref.dtype)
        sel = (lax.broadcasted_iota(jnp.int32, (128, 128), 0) // HEAD_DIM
               == lax.broadcasted_iota(jnp.int32, (128, 128), 1)).astype(_F32)
        prod = do * o
        for p in range(N_PAIRS):
            delta_ref[p] = _dot_exact(prod[:, 128 * p:128 * (p + 1)], sel)

    row = lambda w: pl.BlockSpec((t, w), lambda i: (i, 0))
    return pl.pallas_call(
        body, name="out_loss_bwd", grid=(lp // t,),
        in_specs=[row(512)] * 4 + [row(D_MODEL)] * 2 + [_full((D_MODEL, D_MODEL))] * 2 + [_full((1, D_MODEL))] * 2,
        out_specs=[row(D_MODEL)] + [row(512)] * 4 + [pl.BlockSpec((N_PAIRS, t, 128), lambda i: (0, i, 0)),
                   _full((D_MODEL, D_MODEL)), _full((1, D_MODEL)), _full((1, D_MODEL)), _full((1, 1))],
        out_shape=[jax.ShapeDtypeStruct((lp, D_MODEL), _F32)] + [jax.ShapeDtypeStruct((lp, 512), _MM)] * 4
                  + [jax.ShapeDtypeStruct((N_PAIRS, lp, 128), _F32), jax.ShapeDtypeStruct((D_MODEL, D_MODEL), _F32),
                     jax.ShapeDtypeStruct((1, D_MODEL), _F32), jax.ShapeDtypeStruct((1, D_MODEL), _F32),
                     jax.ShapeDtypeStruct((1, 1), _F32)],
        compiler_params=_params(("arbitrary",)),
    )(o, ga, pw, gc, h, tgt, w_out, w_out_t, g, b)


def _attn_bwd(q, k, v, do, lse, delta, ccol):
    lp = q.shape[0]
    t = _row_tile(lp)
    nb = lp // t

    def body(q_ref, k_ref, v_ref, do_ref, lse_ref, delta_ref, ccol_ref,
             dq_ref, dk_ref, dv_ref, dck_ref, dcq_ref,
             qa_ref, qat_ref, doa_ref, doat_ref, dqt_ref):
        j = pl.program_id(1)

        @pl.when(j == 0)
        def _():
            dqt_ref[...] = jnp.zeros_like(dqt_ref)
            for hd in range(2):
                for n in range(nb):
                    rows = slice(n * t, (n + 1) * t)
                    qa = _augment(q_ref[rows, :], hd, 1.0, -lse_ref[0, rows, hd:hd + 1])
                    doa = _augment(do_ref[rows, :], hd, -delta_ref[0, rows, hd:hd + 1], None)
                    qa_ref[hd, rows, :] = qa
                    qat_ref[hd, :, rows] = _transpose_mm(qa)
                    doa_ref[hd, rows, :] = doa
                    doat_ref[hd, :, rows] = _transpose_mm(doa)

        kas = [_key_operand(k_ref[...], ccol_ref[0, :, hd:hd + 1], hd, j * t) for hd in range(2)]
        kats = [_transpose_mm(ka) for ka in kas]
        vas = [_augment(v_ref[...], hd, 1.0, None) for hd in range(2)]

        def block(i, carry, diag):
            rows = pl.ds(pl.multiple_of(i * t, 128), t)
            new = []
            for hd in range(2):
                dk, dv = carry[hd]
                st = _dot(kas[hd], qat_ref[hd, :, rows])
                if diag:
                    st = jnp.where(_diag_valid(j, t), st, -MASKED)
                pt = jnp.exp(st)
                dsb = (pt * _dot(vas[hd], doat_ref[hd, :, rows])).astype(_MM)
                dv = dv + _dot(pt.astype(_MM), doa_ref[hd, rows, :])
                dk = dk + _dot(dsb, qa_ref[hd, rows, :])
                dqt_ref[hd, :, rows] += _dot(kats[hd], dsb)
                new.append((dk, dv))
            return tuple(new)

        zero = jnp.zeros((t, 128), _F32)
        carry = block(j, ((zero, zero), (zero, zero)), True)
        (dk0, dv0), (dk1, dv1) = lax.fori_loop(j + 1, nb, functools.partial(block, diag=False), carry)
        lane = lax.broadcasted_iota(jnp.int32, (1, 128), 1)
        dk_ref[...] = jnp.where(lane < HEAD_DIM, dk0, dk1).astype(dk_ref.dtype)
        dv_ref[...] = jnp.where(lane < HEAD_DIM, dv0, dv1).astype(dv_ref.dtype)
        dck_ref[0] = jnp.where(lane == 0, dk0[:, HEAD_DIM:HEAD_DIM + 1], jnp.where(lane == 1, dk1[:, 0:1], 0.0))

        @pl.when(j == nb - 1)
        def _():
            row = lax.broadcasted_iota(jnp.int32, (128, t), 0)
            sub = lax.broadcasted_iota(jnp.int32, (8, t), 0)
            for n in range(nb):
                cols = slice(n * t, (n + 1) * t)
                t0, t1 = dqt_ref[0, :, cols], dqt_ref[1, :, cols]
                dq_ref[cols, :] = jnp.where(row < HEAD_DIM, t0, t1).T * ATTN_SCALE
                dcq_ref[0, :, cols] = jnp.where(sub == 0, t0[HEAD_DIM + 3:HEAD_DIM + 4, :],
                                                jnp.where(sub == 1, t1[3:4, :], 0.0))

    blk = pl.BlockSpec((t, 128), lambda p, j: (j, p))
    res = pl.BlockSpec((lp, 128), lambda p, j: (0, p))
    stat = pl.BlockSpec((1, lp, 128), lambda p, j: (p, 0, 0))
    stat_blk = pl.BlockSpec((1, t, 128), lambda p, j: (p, j, 0))
    return pl.pallas_call(
        body, name="attn_bwd", grid=(N_PAIRS, nb),
        in_specs=[res, blk, blk, res, stat, stat, stat_blk],
        out_specs=[res, blk, blk, stat_blk, pl.BlockSpec((1, 8, lp), lambda p, j: (p, 0, 0))],
        out_shape=[jax.ShapeDtypeStruct((lp, D_ATTN), _F32), jax.ShapeDtypeStruct((lp, D_ATTN), _MM),
                   jax.ShapeDtypeStruct((lp, D_ATTN), _MM), jax.ShapeDtypeStruct((N_PAIRS, lp, 128), _F32),
                   jax.ShapeDtypeStruct((N_PAIRS, 8, lp), _F32)],
        scratch_shapes=[pltpu.VMEM((2, lp, 128), _MM), pltpu.VMEM((2, 128, lp), _MM),
                        pltpu.VMEM((2, lp, 128), _MM), pltpu.VMEM((2, 128, lp), _MM),
                        pltpu.VMEM((2, 128, lp), _F32)],
        compiler_params=_params(("arbitrary", "arbitrary")),
    )(q, k, v, do, lse, delta, ccol)


def _decay_bwd(dck, dcq, f, bf):
    lp = f.shape[0]
    nb = lp // 128

    def body(dck_ref, dcq_ref, f_ref, bf_ref, df_ref, dbf_ref):
        r = lax.broadcasted_iota(jnp.int32, (128, 128), 0)
        c = lax.broadcasted_iota(jnp.int32, (128, 128), 1)
        triu = (c >= r).astype(_F32)

        def step(n, carry):
            tail, dbf = carry
            bi = nb - 1 - n
            rows = pl.ds(pl.multiple_of(bi * 128, 128), 128)
            cols = jnp.zeros((128, 128), _F32)
            for p in range(N_PAIRS):
                dcq = jnp.concatenate([dcq_ref[p, :, rows], jnp.zeros((120, 128), _F32)], axis=0).T
                dcp = dcq - dck_ref[p, rows, :]
                cols = cols + (dcp if p == 0 else pltpu.roll(dcp, 2 * p, axis=1))
            lane = lax.broadcasted_iota(jnp.int32, (128, 128), 1)
            cols = jnp.where(lane < N_HEADS, cols, 0.0)
            rev = _dot_exact(triu, cols) + tail
            pos = bi * 128 + lax.broadcasted_iota(jnp.int32, (128, 1), 0)
            df = jnp.where(pos >= ROW_PAD, rev * (1.0 - _sigmoid(f_ref[rows, :] + bf_ref[...])), 0.0)
            df_ref[rows, :] = df.astype(df_ref.dtype)
            return rev[0:1, :], dbf + jnp.sum(df, axis=0, keepdims=True)

        zero = jnp.zeros((1, 128), _F32)
        _, dbf = lax.fori_loop(0, nb, step, (zero, zero), unroll=True)
        dbf_ref[...] = dbf

    return pl.pallas_call(
        body, name="decay_bwd",
        out_shape=[jax.ShapeDtypeStruct((lp, 128), _MM), jax.ShapeDtypeStruct((1, 128), _F32)],
        compiler_params=_params(),
    )(dck, dcq, f, bf)


def _conv_bwd_rows(dpw, z, g, b, wpw_t):
    lp = z.shape[0]
    t = _row_tile(lp)

    def body(dpw_ref, z_ref, g_ref, b_ref, wt_ref, dz_ref, dw_ref, dg_ref, db_ref, dcb_ref):
        i = pl.program_id(0)

        @pl.when(i == 0)
        def _():
            dw_ref[...] = jnp.zeros_like(dw_ref)
            dg_ref[...] = jnp.zeros_like(dg_ref)
            db_ref[...] = jnp.zeros_like(db_ref)
            dcb_ref[...] = jnp.zeros_like(dcb_ref)

        xhat, rstd = _ln_stats(z_ref[...])
        zn = xhat * g_ref[...] + b_ref[...]
        sg = _sigmoid(zn)
        dpw = dpw_ref[...]
        dw_ref[...] += _dot_t0((zn * sg).astype(_MM), dpw)
        dzn = _dot(dpw, wt_ref[...]) * (sg * (1.0 + zn * (1.0 - sg)))
        dg_ref[...] += jnp.sum(dzn * xhat, axis=0, keepdims=True)
        db_ref[...] += jnp.sum(dzn, axis=0, keepdims=True)
        dz = _ln_bwd(dzn, xhat, rstd, g_ref[...])
        dz_ref[...] = dz
        dcb_ref[...] += jnp.sum(dz, axis=0, keepdims=True)

    row = pl.BlockSpec((t, D_CONV), lambda i: (i, 0))
    vec = _full((1, D_CONV))
    vshape = jax.ShapeDtypeStruct((1, D_CONV), _F32)
    return pl.pallas_call(
        body, name="conv_bwd_rows", grid=(lp // t,),
        in_specs=[row, row, vec, vec, _full((D_CONV, D_CONV))],
        out_specs=[row, _full((D_CONV, D_CONV)), vec, vec, vec],
        out_shape=[jax.ShapeDtypeStruct((lp, D_CONV), _F32), jax.ShapeDtypeStruct((D_CONV, D_CONV), _F32),
                   vshape, vshape, vshape],
        compiler_params=_params(("arbitrary",)),
    )(dpw, z, g, b, wpw_t)


def _conv_bwd_taps(dz, u, ug, cw):
    lp = u.shape[0]
    t = _row_tile(lp)
    nt = lp // t

    def body(dz_ref, u_ref, ug_ref, cw_ref, du_ref, dug_ref, dcw_ref, win):
        i = pl.program_id(0)

        @pl.when(i == 0)
        def _():
            win[t:t + HALO, :] = jnp.zeros((HALO, D_CONV), _F32)
            dcw_ref[...] = jnp.zeros_like(dcw_ref)

        win[0:t, :] = dz_ref[...]
        u = u_ref[...]
        sg = _sigmoid(ug_ref[...])
        hh = u * sg
        dhh = jnp.zeros((t, D_CONV), _F32)
        for j in range(CONV_WIDTH):
            sh = win[CONV_WIDTH - 1 - j:CONV_WIDTH - 1 - j + t, :]
            dhh = dhh + cw_ref[j:j + 1, :] * sh
            dcw_ref[j:j + 1, :] += jnp.sum(hh * sh, axis=0, keepdims=True)
        du_ref[...] = (dhh * sg).astype(du_ref.dtype)
        dug_ref[...] = (dhh * u * sg * (1.0 - sg)).astype(dug_ref.dtype)
        win[t:t + HALO, :] = win[0:HALO, :]

    row = pl.BlockSpec((t, D_CONV), lambda i: (nt - 1 - i, 0))
    return pl.pallas_call(
        body, name="conv_bwd_taps", grid=(nt,),
        in_specs=[row, row, row, _full((32, D_CONV))],
        out_specs=[row, row, _full((32, D_CONV))],
        out_shape=[jax.ShapeDtypeStruct((lp, D_CONV), _MM)] * 2 + [jax.ShapeDtypeStruct((32, D_CONV), _F32)],
        scratch_shapes=[pltpu.VMEM((t + HALO, D_CONV), _F32)],
        compiler_params=_params(("arbitrary",)),
    )(dz, u, ug, cw)


def _proj_bwd_x(pieces, df, wpt, dr, h0, g):
    lp = h0.shape[0]
    t = _row_tile(lp)

    def body(dq_ref, dk_ref, dv_ref, dga_ref, du_ref, dug_ref, dgc_ref, df_ref, wt_ref, dr_ref, h0_ref, g_ref,
             dx_ref, dg_ref, db_ref):
        i = pl.program_id(0)

        @pl.when(i == 0)
        def _():
            dg_ref[...] = jnp.zeros_like(dg_ref)
            db_ref[...] = jnp.zeros_like(db_ref)

        dh = ALPHA * dr_ref[...]
        for n, ref in enumerate((dq_ref, dk_ref, dv_ref, dga_ref, du_ref, dug_ref, dgc_ref)):
            dh = dh + _dot(ref[...].astype(_MM), wt_ref[512 * n:512 * (n + 1), :])
        dh = dh + _dot(df_ref[...], wt_ref[3584:3712, :])
        rows = i * t + lax.broadcasted_iota(jnp.int32, (t, 1), 0)
        dh = jnp.where(rows >= ROW_PAD, dh, 0.0)
        xhat, rstd = _ln_stats(h0_ref[...])
        dg_ref[...] += jnp.sum(dh * xhat, axis=0, keepdims=True)
        db_ref[...] += jnp.sum(dh, axis=0, keepdims=True)
        dx_ref[...] = _ln_bwd(dh, xhat, rstd, g_ref[...])

    row = lambda w: pl.BlockSpec((t, w), lambda i: (i, 0))
    vec = _full((1, D_MODEL))
    return pl.pallas_call(
        body, name="proj_bwd_x", grid=(lp // t,),
        in_specs=[row(512)] * 7 + [row(128), _full((D_IN_PAD, D_MODEL)), row(D_MODEL), row(D_MODEL), vec],
        out_specs=[row(D_MODEL), vec, vec],
        out_shape=[jax.ShapeDtypeStruct((lp, D_MODEL), _F32), jax.ShapeDtypeStruct((1, D_MODEL), _F32),
                   jax.ShapeDtypeStruct((1, D_MODEL), _F32)],
        compiler_params=_params(("arbitrary",)),
    )(*pieces, df, wpt, dr, h0, g)


def _proj_bwd_w(h, pieces, name):
    lp = h.shape[0]
    t = _row_tile(lp)
    n = len(pieces)
    widths = [p.shape[1] for p in pieces]

    def body(*refs):
        h_ref, d_refs, w_refs = refs[0], refs[1:1 + n], refs[1 + n:]
        i = pl.program_id(0)

        @pl.when(i == 0)
        def _():
            for w_ref in w_refs:
                w_ref[...] = jnp.zeros_like(w_ref)

        hb = h_ref[...].astype(_MM)
        for d_ref, w_ref in zip(d_refs, w_refs):
            w_ref[...] += _dot_t0(hb, d_ref[...].astype(_MM))

    row = lambda w: pl.BlockSpec((t, w), lambda i: (i, 0))
    return pl.pallas_call(
        body, name=name, grid=(lp // t,),
        in_specs=[row(D_MODEL)] + [row(w) for w in widths],
        out_specs=[_full((D_MODEL, w)) for w in widths],
        out_shape=[jax.ShapeDtypeStruct((D_MODEL, w), _F32) for w in widths],
        compiler_params=_params(("arbitrary",)),
    )(h, *pieces)


def _adamw_math(w, g, m, v):
    m = ADAM_B1 * m + (1.0 - ADAM_B1) * g
    v = ADAM_B2 * v + (1.0 - ADAM_B2) * (g * g)
    m_hat = m / (1.0 - ADAM_B1 ** ADAM_STEP)
    v_hat = v / (1.0 - ADAM_B2 ** ADAM_STEP)
    delta = -ADAM_LR * (m_hat / (jnp.sqrt(v_hat) + ADAM_EPS) + ADAM_WD * w)
    return delta, m, v


def _adamw_tiled(w, g, m, v, rows, name):
    r, c = w.shape

    def body(w_ref, g_ref, m_ref, v_ref, d_ref, nm_ref, nv_ref):
        d_ref[...], nm_ref[...], nv_ref[...] = _adamw_math(w_ref[...], g_ref[...], m_ref[...], v_ref[...])

    spec = pl.BlockSpec((rows, c), lambda i: (i, 0))
    return pl.pallas_call(
        body, name=name, grid=(r // rows,), in_specs=[spec] * 4, out_specs=[spec] * 3,
        out_shape=[jax.ShapeDtypeStruct((r, c), _F32)] * 3,
        compiler_params=_params(("arbitrary",)),
    )(w, g, m, v)


def _adamw_small(ws, gs, ms, vs):
    n = len(ws)

    def body(*refs):
        ins, outs = refs[:4 * n], refs[4 * n:]
        for k in range(n):
            d, m, v = _adamw_math(ins[k][...], ins[n + k][...], ins[2 * n + k][...], ins[3 * n + k][...])
            outs[k][...], outs[n + k][...], outs[2 * n + k][...] = d, m, v

    shapes = [jax.ShapeDtypeStruct(w.shape, _F32) for w in ws]
    res = pl.pallas_call(body, name="adamw_small", out_shape=shapes * 3, compiler_params=_params())(*ws, *gs, *ms, *vs)
    return res[:n], res[n:2 * n], res[2 * n:]


def _place():
    x, y, c = lax.axis_index("x"), lax.axis_index("y"), lax.axis_index("c")
    chips = [(1 - x, y), (x, 1 - y), (1 - x, 1 - y)]
    return x, y, c, chips


def _half(ref, slot, c, rows):
    hr = rows // 2
    return ref.at[slot, pl.ds(pl.multiple_of(c * hr, 8), hr), :]


def _all_gather(shards):
    n = len(shards)

    def body(*refs):
        ins, outs = refs[:n], refs[n:2 * n]
        send, recv, local = refs[2 * n:]
        x, y, c, chips = _place()
        me = 2 * x + y
        sibling = (x, y, 1 - c)

        def copy(k, t, slot, half, to, src=None):
            rows = ins[t].shape[0]
            dst = _half(outs[t], slot, half, rows)
            return pltpu.make_async_remote_copy(
                src_ref=dst if src is None else src, dst_ref=dst,
                send_sem=send.at[k], recv_sem=recv.at[k], device_id=to, device_id_type=MESH)

        own = [pltpu.make_async_copy(ins[t], outs[t].at[me], local.at[t]) for t in range(n)]
        for cp in own:
            cp.start()
        first, passed = [], []
        for t in range(n):
            hr = ins[t].shape[0] // 2
            mine = ins[t].at[pl.ds(pl.multiple_of(c * hr, 8), hr), :]
            for j, chip in enumerate(chips):
                first.append(copy(6 * t + j, t, me, c, (*chip, c), src=mine))
        for cp in first:
            cp.start()
        for t in range(n):
            for j, (cx, cy) in enumerate(chips):
                copy(6 * t + j, t, 2 * cx + cy, c, sibling).wait_recv()
                fwd = copy(6 * t + 3 + j, t, 2 * cx + cy, c, sibling)
                fwd.start()
                passed.append(fwd)
        for t in range(n):
            for j, (cx, cy) in enumerate(chips):
                copy(6 * t + 3 + j, t, 2 * cx + cy, 1 - c, sibling).wait_recv()
        for cp in first + passed:
            cp.wait_send()
        for cp in own:
            cp.wait()

    any_spec = pl.BlockSpec(memory_space=pl.ANY)
    return pl.pallas_call(
        body, name="all_gather",
        in_specs=[any_spec] * n, out_specs=[any_spec] * n,
        out_shape=[jax.ShapeDtypeStruct((N_SHARD,) + s.shape, s.dtype) for s in shards],
        scratch_shapes=[pltpu.SemaphoreType.DMA((6 * n,)), pltpu.SemaphoreType.DMA((6 * n,)),
                        pltpu.SemaphoreType.DMA((n,))],
    )(*shards)


def _sibling_exchange(bufs):
    n = len(bufs)

    def body(*refs):
        ins, outs = refs[:n], refs[n:2 * n]
        send, recv = refs[2 * n:]
        x, y, c, _ = _place()
        cps = []
        for t in range(n):
            hr = ins[t].shape[1] // 2
            src = ins[t].at[:, pl.ds(pl.multiple_of((1 - c) * hr, 8), hr), :]
            cps.append(pltpu.make_async_remote_copy(
                src_ref=src, dst_ref=outs[t], send_sem=send.at[t], recv_sem=recv.at[t],
                device_id=(x, y, 1 - c), device_id_type=MESH))
        for cp in cps:
            cp.start()
        for cp in cps:
            cp.wait()

    any_spec = pl.BlockSpec(memory_space=pl.ANY)
    return pl.pallas_call(
        body, name="sibling_exchange",
        in_specs=[any_spec] * n, out_specs=[any_spec] * n,
        out_shape=[jax.ShapeDtypeStruct((N_SHARD, b.shape[1] // 2, b.shape[2]), b.dtype) for b in bufs],
        scratch_shapes=[pltpu.SemaphoreType.DMA((n,)), pltpu.SemaphoreType.DMA((n,))],
    )(*bufs)


def _half_sum(bufs, got, wire):
    n = len(bufs)

    def body(*refs):
        ins, gots, outs = refs[:n], refs[n:2 * n], refs[2 * n:]
        c = lax.axis_index("c")
        for t in range(n):
            hr = ins[t].shape[1] // 2
            outs[t][0] = (ins[t][0, pl.ds(pl.multiple_of(c * hr, 8), hr), :] + gots[t][0]).astype(outs[t].dtype)

    slot = lambda shape: pl.BlockSpec((1,) + tuple(shape[1:]), lambda s: (s, 0, 0))
    return pl.pallas_call(
        body, name="half_sum", grid=(N_SHARD,),
        in_specs=[slot(b.shape) for b in bufs] + [slot(g.shape) for g in got],
        out_specs=[slot(g.shape) for g in got],
        out_shape=[jax.ShapeDtypeStruct(g.shape, d) for g, d in zip(got, wire)],
        compiler_params=_params(("arbitrary",)),
    )(*bufs, *got)


def _chip_exchange(parts):
    n = len(parts)

    def body(*refs):
        ins, outs = refs[:n], refs[n:2 * n]
        send, recv, local = refs[2 * n:]
        x, y, c, chips = _place()
        me = 2 * x + y
        own = [pltpu.make_async_copy(ins[t].at[me], outs[t].at[me], local.at[t]) for t in range(n)]
        cps = []
        for t in range(n):
            for j, (cx, cy) in enumerate(chips):
                cps.append(pltpu.make_async_remote_copy(
                    src_ref=ins[t].at[2 * cx + cy], dst_ref=outs[t].at[me],
                    send_sem=send.at[3 * t + j], recv_sem=recv.at[3 * t + j],
                    device_id=(cx, cy, c), device_id_type=MESH))
        for cp in own + cps:
            cp.start()
        for cp in cps + own:
            cp.wait()

    any_spec = pl.BlockSpec(memory_space=pl.ANY)
    return pl.pallas_call(
        body, name="chip_exchange",
        in_specs=[any_spec] * n, out_specs=[any_spec] * n,
        out_shape=[jax.ShapeDtypeStruct(p.shape, p.dtype) for p in parts],
        scratch_shapes=[pltpu.SemaphoreType.DMA((3 * n,)), pltpu.SemaphoreType.DMA((3 * n,)),
                        pltpu.SemaphoreType.DMA((n,))],
    )(*parts)


def _chip_sum(got):
    n = len(got)

    def body(*refs):
        gots, outs = refs[:n], refs[n:]
        for t in range(n):
            terms = [gots[t][s].astype(_F32) for s in range(N_SHARD)]
            outs[t][...] = ((terms[0] + terms[1]) + terms[2]) + terms[3]

    return pl.pallas_call(
        body, name="chip_sum", out_shape=[jax.ShapeDtypeStruct(g.shape[1:], _F32) for g in got],
        compiler_params=_params(),
    )(*got)


def _sibling_share(halves):
    n = len(halves)

    def body(*refs):
        ins, outs = refs[:n], refs[n:2 * n]
        send, recv, local = refs[2 * n:]
        x, y, c, _ = _place()
        cps, own = [], []
        for t in range(n):
            hr = ins[t].shape[0]
            dst = outs[t].at[pl.ds(pl.multiple_of(c * hr, 8), hr), :]
            own.append(pltpu.make_async_copy(ins[t], dst, local.at[t]))
            cps.append(pltpu.make_async_remote_copy(
                src_ref=ins[t], dst_ref=dst, send_sem=send.at[t], recv_sem=recv.at[t],
                device_id=(x, y, 1 - c), device_id_type=MESH))
        for cp in own + cps:
            cp.start()
        for cp in cps + own:
            cp.wait()

    any_spec = pl.BlockSpec(memory_space=pl.ANY)
    return pl.pallas_call(
        body, name="sibling_share",
        in_specs=[any_spec] * n, out_specs=[any_spec] * n,
        out_shape=[jax.ShapeDtypeStruct((2 * h.shape[0], h.shape[1]), h.dtype) for h in halves],
        scratch_shapes=[pltpu.SemaphoreType.DMA((n,)), pltpu.SemaphoreType.DMA((n,)), pltpu.SemaphoreType.DMA((n,))],
    )(*halves)


def _reduce_scatter(bufs, wire):
    chip_part = _half_sum(bufs, _sibling_exchange(bufs), wire)
    half = _chip_sum(_chip_exchange(chip_part))
    return _sibling_share(half)


REPL_SIZES = (D_MODEL, D_MODEL, N_HEADS, D_CONV, D_CONV, D_CONV, D_MODEL, D_MODEL)
REPL_ROWS = 48


def _pack_repl(parts):
    flat = jnp.concatenate([p.reshape(-1).astype(_F32) for p in parts])
    return jnp.pad(flat, (0, REPL_ROWS * 128 - flat.shape[0])).reshape(REPL_ROWS, 128)


def _unpack_repl(packed, shapes):
    flat = packed.reshape(-1)
    out, off = [], 0
    for shape, size in zip(shapes, REPL_SIZES):
        out.append(flat[off:off + size].reshape(shape))
        off += size
    return out


def _local_grads(x, target, meta, ln_in_g, ln_in_b, w_in, b_f, conv_w, conv_b, ln_conv_g, ln_conv_b,
                 w_pw, w_out, ln_out_g, ln_out_b):
    seq = x.shape[0]
    lp = ROW_X + seq
    vec = lambda a: a.reshape(1, -1).astype(_F32)
    h0 = jnp.concatenate([jnp.zeros((ROW_PAD, D_MODEL), _F32), meta, x], axis=0)
    tgt = jnp.concatenate([jnp.zeros((ROW_X, D_MODEL), _F32), target], axis=0)
    zcol = jnp.zeros((D_MODEL, 128 - N_HEADS), w_in.dtype)
    wp = jnp.concatenate([w_in[:, :OFF_F], w_in[:, OFF_F + N_HEADS:], w_in[:, OFF_F:OFF_F + N_HEADS], zcol], axis=1)
    bf = jnp.pad(vec(b_f), ((0, 0), (0, 128 - N_HEADS)))
    cw = jnp.pad(conv_w, ((0, 32 - CONV_WIDTH), (0, 0)))

    h, q, k, v, ga, u, ug, gc, f, vt = _ln_proj(h0, vec(ln_in_g), vec(ln_in_b), wp)
    ccol = _decay_cumsum(f, bf)
    o, lse = _attn_fwd(q, k, vt, ccol)
    z, pw = _conv_fwd(u, ug, cw, vec(conv_b), vec(ln_conv_g), vec(ln_conv_b), w_pw)
    dr, do, dga, dpw, dgc, delta, dw_out, dg_out, db_out, loss = _out_loss_bwd(
        o, ga, pw, gc, h, tgt, w_out, w_out.T, vec(ln_out_g), vec(ln_out_b))
    dq, dk, dv, dck, dcq = _attn_bwd(q, k, v, do, lse, delta, ccol)
    df, dbf = _decay_bwd(dck, dcq, f, bf)
    dz, dw_pw, dg_conv, db_conv, dconv_b = _conv_bwd_rows(dpw, z, vec(ln_conv_g), vec(ln_conv_b), w_pw.T)
    du, dug, dcw = _conv_bwd_taps(dz, u, ug, cw)
    pieces = (dq, dk, dv, dga, du, dug, dgc)
    dx, dg_in, db_in = _proj_bwd_x(pieces, df, wp.T, dr, h0, vec(ln_in_g))
    dwq, dwk, dwv, dwga = _proj_bwd_w(h, pieces[:4], "proj_bwd_w_attn")
    dwu, dwug, dwgc, dwf = _proj_bwd_w(h, pieces[4:] + (df,), "proj_bwd_w_conv")
    dw_in = jnp.concatenate([dwq, dwk, dwv, dwf[:, :N_HEADS], dwga, dwu, dwug, dwgc], axis=1)
    grads = dict(meta=dx[ROW_PAD:ROW_X], ln_in_g=dg_in, ln_in_b=db_in, w_in=dw_in, b_f=dbf[:, :N_HEADS],
                 conv_w=dcw[:CONV_WIDTH], conv_b=dconv_b, ln_conv_g=dg_conv, ln_conv_b=db_conv,
                 w_pw=dw_pw, w_out=dw_out, ln_out_g=dg_out, ln_out_b=db_out)
    return loss, dx, grads


def kernel(x, meta, ln_in_g, ln_in_b, w_in, b_f, conv_w, conv_b, ln_conv_g, ln_conv_b, w_pw, w_out, ln_out_g, ln_out_b, loss_target, m_meta, m_ln_in_g, m_ln_in_b, m_w_in, m_b_f, m_conv_w, m_conv_b, m_ln_conv_g, m_ln_conv_b, m_w_pw, m_w_out, m_ln_out_g, m_ln_out_b, v_meta, v_ln_in_g, v_ln_in_b, v_w_in, v_b_f, v_conv_w, v_conv_b, v_ln_conv_g, v_ln_conv_b, v_w_pw, v_w_out, v_ln_out_g, v_ln_out_b):
    w_in2, conv_w2, w_pw2, w_out2 = w_in[0], conv_w[0], w_pw[0], w_out[0]

    small = jnp.concatenate([jnp.pad(conv_w2, ((0, 1), (0, 0))), meta.reshape(32, 128)], axis=0)
    g_in, g_pw, g_out, g_small = _all_gather(
        [w_in2.astype(_MM), w_pw2.astype(_MM), w_out2.astype(_MM), small])
    w_in_full = jnp.transpose(g_in, (1, 0, 2)).reshape(D_MODEL, D_IN)
    w_pw_full = g_pw.reshape(D_CONV, D_CONV)
    w_out_full = g_out.reshape(D_MODEL, D_MODEL)
    conv_w_full = jnp.transpose(g_small[:, :CONV_WIDTH, :], (1, 0, 2)).reshape(CONV_WIDTH, D_CONV)
    meta_full = jnp.transpose(g_small[:, 32:64, :].reshape(N_SHARD, N_META, 256), (1, 0, 2)).reshape(N_META, D_MODEL)

    loss, dx, gr = _local_grads(x[0], loss_target[0], meta_full, ln_in_g, ln_in_b, w_in_full, b_f, conv_w_full,
                                conv_b, ln_conv_g, ln_conv_b, w_pw_full, w_out_full, ln_out_g, ln_out_b)
    grad_x = dx[ROW_X:][None]

    repl_names = ("ln_in_g", "ln_in_b", "b_f", "conv_b", "ln_conv_g", "ln_conv_b", "ln_out_g", "ln_out_b")
    repl = _pack_repl([gr[n] for n in repl_names] + [loss])
    gcw = jnp.transpose(jnp.pad(gr["conv_w"], ((0, 1), (0, 0))).reshape(32, N_SHARD, 128), (1, 0, 2))
    gmeta = jnp.transpose(gr["meta"].reshape(N_META, N_SHARD, 256), (1, 0, 2)).reshape(N_SHARD, 32, 128)
    small_g = jnp.concatenate([gcw, gmeta, jnp.broadcast_to(repl[None], (N_SHARD, REPL_ROWS, 128))], axis=1)
    bufs = [jnp.transpose(gr["w_in"].reshape(D_MODEL, N_SHARD, W_IN_SHARD), (1, 0, 2)),
            gr["w_pw"].reshape(N_SHARD, D_CONV // N_SHARD, D_CONV),
            gr["w_out"].reshape(N_SHARD, D_MODEL // N_SHARD, D_MODEL),
            small_g]
    gw_in, gw_pw, gw_out, gsmall = _reduce_scatter(bufs, [_MM, _MM, _MM, _F32])
    g_conv_w = gsmall[:CONV_WIDTH]
    g_meta = gsmall[32:64].reshape(N_META, 256)
    g_repl = gsmall[64:64 + REPL_ROWS]

    repl_w = (ln_in_g, ln_in_b, b_f, conv_b, ln_conv_g, ln_conv_b, ln_out_g, ln_out_b)
    repl_m = (m_ln_in_g, m_ln_in_b, m_b_f, m_conv_b, m_ln_conv_g, m_ln_conv_b, m_ln_out_g, m_ln_out_b)
    repl_v = (v_ln_in_g, v_ln_in_b, v_b_f, v_conv_b, v_ln_conv_g, v_ln_conv_b, v_ln_out_g, v_ln_out_b)
    zero = jnp.zeros((1,), _F32)
    d_in, nm_in, nv_in = _adamw_tiled(w_in2, gw_in, m_w_in[0], v_w_in[0], 256, "adamw_w_in")
    ws = [w_pw2, w_out2, conv_w2, meta, _pack_repl(repl_w + (zero,))]
    gs = [gw_pw, gw_out, g_conv_w, g_meta, g_repl]
    ms = [m_w_pw[0], m_w_out[0], m_conv_w[0], m_meta, _pack_repl(repl_m + (zero,))]
    vs = [v_w_pw[0], v_w_out[0], v_conv_w[0], v_meta, _pack_repl(repl_v + (zero + 1.0,))]
    ds, nms, nvs = _adamw_small(ws, gs, ms, vs)

    shapes = [w.shape for w in repl_w]
    total_loss = g_repl.reshape(-1)[sum(REPL_SIZES)]
    names = ("meta", "ln_in_g", "ln_in_b", "w_in", "b_f", "conv_w", "conv_b", "ln_conv_g", "ln_conv_b",
             "w_pw", "w_out", "ln_out_g", "ln_out_b")

    def assemble(t_in, smalls):
        pw_, out_, cw_, meta_, packed = smalls
        r = dict(zip(repl_names, _unpack_repl(packed, shapes)))
        r.update(meta=meta_, w_in=t_in[None], conv_w=cw_[None], w_pw=pw_[None], w_out=out_[None])
        return [r[n] for n in names]

    return (total_loss, grad_x, *assemble(gw_in, gs), *assemble(d_in, ds), *assemble(nm_in, nms),
            *assemble(nv_in, nvs))
```

```python
import functools

import jax
import jax.numpy as jnp
from jax import lax
from jax.experimental import pallas as pl
from jax.experimental.pallas import tpu as pltpu

D_MODEL = 1024
N_META = 16
D_ATTN = 512
D_CONV = 512
N_HEADS = 8
HEAD_DIM = 64
N_PAIRS = 4
CONV_WIDTH = 31
LN_EPS = 1e-5
ALPHA = 2.0 ** 0.25
ATTN_SCALE = 0.125
D_IN = 3592
OFF_F = 3 * D_ATTN
D_IN_PAD = 3712
N_SHARD = 4
W_IN_SHARD = D_IN // N_SHARD
W_IN_SLOT = 928

ROW_PAD = 112
ROW_X = ROW_PAD + N_META
HALO = 32
MASKED = 1e30

ADAM_LR = 0.001
ADAM_B1 = 0.9
ADAM_B2 = 0.999
ADAM_EPS = 1e-08
ADAM_WD = 0.01
ADAM_STEP = 10

VMEM_LIMIT = 56 * 1024 * 1024

_MM = jnp.bfloat16
_F32 = jnp.float32
MESH = pl.DeviceIdType.MESH


def _params(sem=None):
    return pltpu.CompilerParams(dimension_semantics=sem, vmem_limit_bytes=VMEM_LIMIT)


def _row_tile(n_rows):
    for t in (384, 256, 128):
        if n_rows % t == 0:
            return t
    raise ValueError(f"padded sequence length {n_rows} is not a multiple of 128")


def _sigmoid(x):
    return 1.0 / (1.0 + jnp.exp(-x))


def _ln_stats(x):
    mu = jnp.mean(x, axis=-1, keepdims=True)
    xc = x - mu
    var = jnp.mean(xc * xc, axis=-1, keepdims=True)
    rstd = lax.rsqrt(var + LN_EPS)
    return xc * rstd, rstd


def _ln_bwd(dy, xhat, rstd, g):
    dxh = dy * g
    m1 = jnp.mean(dxh, axis=-1, keepdims=True)
    m2 = jnp.mean(dxh * xhat, axis=-1, keepdims=True)
    return rstd * (dxh - m1 - xhat * m2)


def _dot(a, b):
    return jnp.dot(a, b, preferred_element_type=_F32)


def _dot_t0(a, b):
    return lax.dot_general(a, b, (((0,), (0,)), ((), ())), preferred_element_type=_F32)


def _dot_t1(a, b):
    return lax.dot_general(a, b, (((1,), (1,)), ((), ())), preferred_element_type=_F32)


def _dot_exact(a, b):
    return jnp.dot(a, b, preferred_element_type=_F32, precision=lax.Precision.HIGHEST)


def _full(shape):
    return pl.BlockSpec(shape, lambda *_: (0,) * len(shape))


def _x_specs(t):
    n = t // ROW_X
    return [pl.BlockSpec((ROW_X, D_MODEL), lambda i, s=s: (jnp.maximum(i * n + s - 1, 0), 0)) for s in range(n)]


def _x_tile(i, refs, head_ref=None):
    first = refs[0][...]
    if head_ref is not None:
        first = jnp.where(i == 0, head_ref[...], first)
    return jnp.concatenate([first] + [r[...] for r in refs[1:]], axis=0)


def _ln_proj(x, head, g, b, wpt):
    lp = ROW_X + x.shape[0]
    t = _row_tile(lp)
    nx = t // ROW_X

    def body(*refs):
        x_refs, (head_ref, g_ref, b_ref, wt_ref) = refs[:nx], refs[nx:nx + 4]
        h_ref, q_ref, k_ref, v_ref, ga_ref, u_ref, ug_ref, gc_ref, f_ref, vt_ref = refs[nx + 4:]
        i = pl.program_id(0)
        xhat, _ = _ln_stats(_x_tile(i, x_refs, head_ref))
        rows = i * t + lax.broadcasted_iota(jnp.int32, (t, 1), 0)
        h = jnp.where(rows >= ROW_PAD, xhat * g_ref[...] + b_ref[...], 0.0)
        h_ref[...] = h
        hb = h.astype(_MM)
        q_ref[...] = (_dot_t1(hb, wt_ref[0:512, :]) * ATTN_SCALE).astype(q_ref.dtype)
        k_ref[...] = _dot_t1(hb, wt_ref[512:1024, :]).astype(k_ref.dtype)
        v = _dot_t1(hb, wt_ref[1024:1536, :])
        v_ref[...] = v.astype(v_ref.dtype)
        vt_ref[...] = v.T.astype(vt_ref.dtype)
        ga_ref[...] = _dot_t1(hb, wt_ref[1536:2048, :])
        u_ref[...] = _dot_t1(hb, wt_ref[2048:2560, :])
        ug_ref[...] = _dot_t1(hb, wt_ref[2560:3072, :])
        gc_ref[...] = _dot_t1(hb, wt_ref[3072:3584, :])
        f_ref[...] = _dot_t1(hb, wt_ref[3584:3712, :])

    row = lambda w: pl.BlockSpec((t, w), lambda i: (i, 0))
    f32 = lambda w: jax.ShapeDtypeStruct((lp, w), _F32)
    mm = lambda w: jax.ShapeDtypeStruct((lp, w), _MM)
    return pl.pallas_call(
        body, name="ln_proj", grid=(lp // t,),
        in_specs=_x_specs(t) + [_full((ROW_X, D_MODEL)), _full((1, D_MODEL)), _full((1, D_MODEL)),
                                _full((D_IN_PAD, D_MODEL))],
        out_specs=[row(D_MODEL)] + [row(512)] * 7 + [row(128), pl.BlockSpec((512, t), lambda i: (0, i))],
        out_shape=[f32(D_MODEL), mm(512), mm(512), mm(512), f32(512), f32(512), f32(512), f32(512), f32(128),
                   jax.ShapeDtypeStruct((512, lp), _MM)],
        compiler_params=_params(("arbitrary",)),
    )(*([x] * nx), head, g, b, wpt)


def _log_sigmoid(z):
    return jnp.minimum(z, 0.0) - jnp.log(1.0 + jnp.exp(-jnp.abs(z)))


def _decay_cumsum(f, bf):
    lp = f.shape[0]
    nb = lp // 128

    def body(f_ref, bf_ref, ccol_ref):
        r = lax.broadcasted_iota(jnp.int32, (128, 128), 0)
        c = lax.broadcasted_iota(jnp.int32, (128, 128), 1)
        tril = (c <= r).astype(_F32)

        def step(bi, carry):
            rows = pl.ds(pl.multiple_of(bi * 128, 128), 128)
            lf = _log_sigmoid(f_ref[rows, :] + bf_ref[...])
            cb = _dot_exact(tril, lf) + carry
            for p in range(N_PAIRS):
                ccol_ref[p, rows, :] = cb if p == 0 else pltpu.roll(cb, 128 - 2 * p, axis=1)
            return cb[127:128, :]

        lax.fori_loop(0, nb, step, jnp.zeros((1, 128), _F32))

    return pl.pallas_call(
        body, name="decay_cumsum",
        out_shape=jax.ShapeDtypeStruct((N_PAIRS, lp, 128), _F32),
        compiler_params=_params(),
    )(f, bf)


def _split3(x):
    hi = x.astype(_MM)
    r1 = x - hi.astype(_F32)
    mid = r1.astype(_MM)
    return hi, mid, (r1 - mid.astype(_F32)).astype(_MM)


def _augment(x, hd, first, second):
    lane = lax.broadcasted_iota(jnp.int32, (1, 128), 1)
    base = HEAD_DIM * (1 - hd)
    out = jnp.where(lane // HEAD_DIM == hd, x, jnp.zeros_like(x))
    for slot, val in enumerate((first, second)):
        lo = base + 3 * slot
        if val is None:
            continue
        if isinstance(val, float):
            out = jnp.where((lane >= lo) & (lane < lo + 3), jnp.asarray(val, x.dtype), out)
        else:
            for n, piece in enumerate(_split3(val)):
                out = jnp.where(lane == lo + n, piece, out)
    return out


def _transpose_mm(x):
    return x.astype(_F32).T.astype(_MM)


def _diag_valid(i, t):
    sub = lax.broadcasted_iota(jnp.int32, (t, t), 0)
    return (sub <= lax.broadcasted_iota(jnp.int32, (t, t), 1)) & (i * t + sub >= ROW_PAD)


def _key_operand(k, c, hd, first_row):
    pos = first_row + lax.broadcasted_iota(jnp.int32, (k.shape[0], 1), 0)
    return _augment(k, hd, jnp.where(pos >= ROW_PAD, -c, -MASKED), 1.0)


def _attn_fwd(q, k, vt, ccol):
    lp = q.shape[0]
    t = _row_tile(lp)
    nb = lp // t

    def body(q_ref, k_ref, vt_ref, ccol_ref, o_ref, lse_ref, ka_ref):
        i = pl.program_id(1)

        @pl.when(i == 0)
        def _():
            for hd in range(2):
                for n in range(nb):
                    rows = slice(n * t, (n + 1) * t)
                    ka_ref[hd, rows, :] = _key_operand(k_ref[rows, :], ccol_ref[0, rows, hd:hd + 1], hd, n * t)

        qats = [_transpose_mm(_augment(q_ref[...], hd, 1.0, None)) for hd in range(2)]

        def scores(j):
            keys = pl.ds(pl.multiple_of(j * t, 128), t)
            return tuple(_dot(ka_ref[hd, keys, :], qats[hd]) for hd in range(2))

        def update(j, sts, states, diag):
            vt = vt_ref[:, pl.ds(pl.multiple_of(j * t, 128), t)]
            new = []
            for st, (m, l, acc) in zip(sts, states):
                if diag:
                    st = jnp.where(_diag_valid(i, t), st, -MASKED)
                m_new = jnp.maximum(m, jnp.max(st, axis=0, keepdims=True))
                a = jnp.exp(m - m_new)
                pt = jnp.exp(st - m_new)
                l = a * l + jnp.sum(pt, axis=0, keepdims=True)
                new.append((m_new, l, a * acc + _dot(vt, pt.astype(_MM))))
            return tuple(new)

        def step(j, carry):
            sts, states = carry
            return scores(j + 1), update(j, sts, states, False)

        init = (jnp.full((1, t), -MASKED, _F32), jnp.zeros((1, t), _F32), jnp.zeros((128, t), _F32))
        sts, states = lax.fori_loop(0, i, step, (scores(0), (init, init)))
        (m0, l0, acc0), (m1, l1, acc1) = update(i, sts, states, True)
        row = lax.broadcasted_iota(jnp.int32, (128, t), 0)
        o_ref[...] = jnp.where(row < HEAD_DIM, acc0 / l0, acc1 / l1).T
        lse_ref[0] = jnp.where(row == 0, m0 + jnp.log(l0), jnp.where(row == 1, m1 + jnp.log(l1), 0.0)).T

    return pl.pallas_call(
        body, name="attn_fwd", grid=(N_PAIRS, nb),
        in_specs=[pl.BlockSpec((t, 128), lambda p, i: (i, p)),
                  pl.BlockSpec((lp, 128), lambda p, i: (0, p)),
                  pl.BlockSpec((128, lp), lambda p, i: (p, 0)),
                  pl.BlockSpec((1, lp, 128), lambda p, i: (p, 0, 0))],
        out_specs=[pl.BlockSpec((t, 128), lambda p, i: (i, p)),
                   pl.BlockSpec((1, t, 128), lambda p, i: (p, i, 0))],
        out_shape=[jax.ShapeDtypeStruct((lp, D_ATTN), _F32), jax.ShapeDtypeStruct((N_PAIRS, lp, 128), _F32)],
        scratch_shapes=[pltpu.VMEM((2, lp, 128), _MM)],
        compiler_params=_params(("arbitrary", "arbitrary")),
    )(q, k, vt, ccol)


def _conv_fwd(u, ug, cw, cb, g, b, wpw):
    lp = u.shape[0]
    t = _row_tile(lp)

    def body(u_ref, ug_ref, cw_ref, cb_ref, g_ref, b_ref, wpw_ref, z_ref, pw_ref, win):
        i = pl.program_id(0)

        @pl.when(i == 0)
        def _():
            win[0:HALO, :] = jnp.zeros((HALO, D_CONV), _F32)

        win[HALO:HALO + t, :] = u_ref[...] * _sigmoid(ug_ref[...])
        acc = jnp.broadcast_to(cb_ref[...], (t, D_CONV))
        for j in range(CONV_WIDTH):
            acc = acc + cw_ref[j:j + 1, :] * win[HALO - (CONV_WIDTH - 1) + j:HALO - (CONV_WIDTH - 1) + j + t, :]
        z_ref[...] = acc
        xhat, _ = _ln_stats(acc)
        zn = xhat * g_ref[...] + b_ref[...]
        a = zn * _sigmoid(zn)
        pw_ref[...] = _dot(a.astype(_MM), wpw_ref[...])
        win[0:HALO, :] = win[t:t + HALO, :]

    row = pl.BlockSpec((t, D_CONV), lambda i: (i, 0))
    vec = _full((1, D_CONV))
    return pl.pallas_call(
        body, name="conv_fwd", grid=(lp // t,),
        in_specs=[row, row, _full((32, D_CONV)), vec, vec, vec, _full((D_CONV, D_CONV))],
        out_specs=[row, row],
        out_shape=[jax.ShapeDtypeStruct((lp, D_CONV), _F32)] * 2,
        scratch_shapes=[pltpu.VMEM((t + HALO, D_CONV), _F32)],
        compiler_params=_params(("arbitrary",)),
    )(u, ug, cw, cb, g, b, wpw)


def _out_loss_bwd(o, ga, pw, gc, h, target, w_out, g, b):
    lp = o.shape[0]
    t = _row_tile(lp)
    nx = t // ROW_X

    def body(*refs):
        o_ref, ga_ref, pw_ref, gc_ref, h_ref = refs[:5]
        tgt_refs, (w_ref, g_ref, b_ref) = refs[5:5 + nx], refs[5 + nx:8 + nx]
        dr_ref, do_ref, dga_ref, dpw_ref, dgc_ref, delta_ref, dw_ref, dg_ref, db_ref, loss_ref = refs[8 + nx:]
        i = pl.program_id(0)

        @pl.when(i == 0)
        def _():
            dw_ref[...] = jnp.zeros_like(dw_ref)
            dg_ref[...] = jnp.zeros_like(dg_ref)
            db_ref[...] = jnp.zeros_like(db_ref)
            loss_ref[...] = jnp.zeros_like(loss_ref)

        o, ga, pw, gc = o_ref[...], ga_ref[...], pw_ref[...], gc_ref[...]
        sa, sc = _sigmoid(ga), _sigmoid(gc)
        silu_a, silu_c = ga * sa, gc * sc
        ycat = jnp.concatenate([o * silu_a, pw * silu_c], axis=-1).astype(_MM)
        r = ALPHA * h_ref[...] + _dot(ycat, w_ref[...])
        xhat, rstd = _ln_stats(r)
        out = xhat * g_ref[...] + b_ref[...]
        rows = i * t + lax.broadcasted_iota(jnp.int32, (t, 1), 0)
        diff = jnp.where(rows >= ROW_X, out - _x_tile(i, tgt_refs), 0.0)
        loss_ref[...] += (0.5 / D_MODEL) * jnp.sum(diff * diff, keepdims=True)
        dout = diff * (1.0 / D_MODEL)
        dg_ref[...] += jnp.sum(dout * xhat, axis=0, keepdims=True)
        db_ref[...] += jnp.sum(dout, axis=0, keepdims=True)
        dr = _ln_bwd(dout, xhat, rstd, g_ref[...])
        dr_ref[...] = dr
        drb = dr.astype(_MM)
        dw_ref[...] += _dot_t0(ycat, drb)
        dy = _dot_t1(drb, w_ref[...])
        dya, dyc = dy[:, :D_ATTN], dy[:, D_ATTN:]
        do = dya * silu_a
        do_ref[...] = do.astype(do_ref.dtype)
        dga_ref[...] = (dya * o * (sa * (1.0 + ga * (1.0 - sa)))).astype(dga_ref.dtype)
        dpw_ref[...] = (dyc * silu_c).astype(dpw_ref.dtype)
        dgc_ref[...] = (dyc * pw * (sc * (1.0 + gc * (1.0 - sc)))).astype(dgc_ref.dtype)
        sel = (lax.broadcasted_iota(jnp.int32, (128, 128), 0) // HEAD_DIM
               == lax.broadcasted_iota(jnp.int32, (128, 128), 1)).astype(_F32)
        prod = do * o
        for p in range(N_PAIRS):
            delta_ref[p] = _dot_exact(prod[:, 128 * p:128 * (p + 1)], sel)

    row = lambda w: pl.BlockSpec((t, w), lambda i: (i, 0))
    return pl.pallas_call(
        body, name="out_loss_bwd", grid=(lp // t,),
        in_specs=[row(512)] * 4 + [row(D_MODEL)] + _x_specs(t) + [_full((D_MODEL, D_MODEL))] + [_full((1, D_MODEL))] * 2,
        out_specs=[row(D_MODEL)] + [row(512)] * 4 + [pl.BlockSpec((N_PAIRS, t, 128), lambda i: (0, i, 0)),
                   _full((D_MODEL, D_MODEL)), _full((1, D_MODEL)), _full((1, D_MODEL)), _full((1, 1))],
        out_shape=[jax.ShapeDtypeStruct((lp, D_MODEL), _F32)] + [jax.ShapeDtypeStruct((lp, 512), _MM)] * 4
                  + [jax.ShapeDtypeStruct((N_PAIRS, lp, 128), _F32), jax.ShapeDtypeStruct((D_MODEL, D_MODEL), _F32),
                     jax.ShapeDtypeStruct((1, D_MODEL), _F32), jax.ShapeDtypeStruct((1, D_MODEL), _F32),
                     jax.ShapeDtypeStruct((1, 1), _F32)],
        compiler_params=_params(("arbitrary",)),
    )(o, ga, pw, gc, h, *([target] * nx), w_out, g, b)


def _attn_bwd(q, k, v, do, lse, delta, ccol):
    lp = q.shape[0]
    t = _row_tile(lp)
    nb = lp // t

    def body(q_ref, k_ref, v_ref, do_ref, lse_ref, delta_ref, ccol_ref,
             dq_ref, dk_ref, dv_ref, dck_ref, dcq_ref,
             qa_ref, qat_ref, doa_ref, doat_ref, dqt_ref):
        j = pl.program_id(1)

        @pl.when(j == 0)
        def _():
            dqt_ref[...] = jnp.zeros_like(dqt_ref)
            for hd in range(2):
                for n in range(nb):
                    rows = slice(n * t, (n + 1) * t)
                    qa = _augment(q_ref[rows, :], hd, 1.0, -lse_ref[0, rows, hd:hd + 1])
                    doa = _augment(do_ref[rows, :], hd, -delta_ref[0, rows, hd:hd + 1], None)
                    qa_ref[hd, rows, :] = qa
                    qat_ref[hd, :, rows] = _transpose_mm(qa)
                    doa_ref[hd, rows, :] = doa
                    doat_ref[hd, :, rows] = _transpose_mm(doa)

        kas = [_key_operand(k_ref[...], ccol_ref[0, :, hd:hd + 1], hd, j * t) for hd in range(2)]
        kats = [_transpose_mm(ka) for ka in kas]
        vas = [_augment(v_ref[...], hd, 1.0, None) for hd in range(2)]

        def block(i, carry, diag):
            rows = pl.ds(pl.multiple_of(i * t, 128), t)
            new = []
            for hd in range(2):
                dk, dv = carry[hd]
                st = _dot(kas[hd], qat_ref[hd, :, rows])
                if diag:
                    st = jnp.where(_diag_valid(j, t), st, -MASKED)
                pt = jnp.exp(st)
                dsb = (pt * _dot(vas[hd], doat_ref[hd, :, rows])).astype(_MM)
                dv = dv + _dot(pt.astype(_MM), doa_ref[hd, rows, :])
                dk = dk + _dot(dsb, qa_ref[hd, rows, :])
                dqt_ref[hd, :, rows] += _dot(kats[hd], dsb)
                new.append((dk, dv))
            return tuple(new)

        zero = jnp.zeros((t, 128), _F32)
        carry = block(j, ((zero, zero), (zero, zero)), True)
        (dk0, dv0), (dk1, dv1) = lax.fori_loop(j + 1, nb, functools.partial(block, diag=False), carry)
        lane = lax.broadcasted_iota(jnp.int32, (1, 128), 1)
        dk_ref[...] = jnp.where(lane < HEAD_DIM, dk0, dk1).astype(dk_ref.dtype)
        dv_ref[...] = jnp.where(lane < HEAD_DIM, dv0, dv1).astype(dv_ref.dtype)
        dck_ref[0] = jnp.where(lane == 0, dk0[:, HEAD_DIM:HEAD_DIM + 1], jnp.where(lane == 1, dk1[:, 0:1], 0.0))

        @pl.when(j == nb - 1)
        def _():
            row = lax.broadcasted_iota(jnp.int32, (128, t), 0)
            sub = lax.broadcasted_iota(jnp.int32, (8, t), 0)
            for n in range(nb):
                cols = slice(n * t, (n + 1) * t)
                t0, t1 = dqt_ref[0, :, cols], dqt_ref[1, :, cols]
                dq_ref[cols, :] = jnp.where(row < HEAD_DIM, t0, t1).T * ATTN_SCALE
                dcq_ref[0, :, cols] = jnp.where(sub == 0, t0[HEAD_DIM + 3:HEAD_DIM + 4, :],
                                                jnp.where(sub == 1, t1[3:4, :], 0.0))

    blk = pl.BlockSpec((t, 128), lambda p, j: (j, p))
    res = pl.BlockSpec((lp, 128), lambda p, j: (0, p))
    stat = pl.BlockSpec((1, lp, 128), lambda p, j: (p, 0, 0))
    stat_blk = pl.BlockSpec((1, t, 128), lambda p, j: (p, j, 0))
    return pl.pallas_call(
        body, name="attn_bwd", grid=(N_PAIRS, nb),
        in_specs=[res, blk, blk, res, stat, stat, stat_blk],
        out_specs=[res, blk, blk, stat_blk, pl.BlockSpec((1, 8, lp), lambda p, j: (p, 0, 0))],
        out_shape=[jax.ShapeDtypeStruct((lp, D_ATTN), _F32), jax.ShapeDtypeStruct((lp, D_ATTN), _MM),
                   jax.ShapeDtypeStruct((lp, D_ATTN), _MM), jax.ShapeDtypeStruct((N_PAIRS, lp, 128), _F32),
                   jax.ShapeDtypeStruct((N_PAIRS, 8, lp), _F32)],
        scratch_shapes=[pltpu.VMEM((2, lp, 128), _MM), pltpu.VMEM((2, 128, lp), _MM),
                        pltpu.VMEM((2, lp, 128), _MM), pltpu.VMEM((2, 128, lp), _MM),
                        pltpu.VMEM((2, 128, lp), _F32)],
        compiler_params=_params(("arbitrary", "arbitrary")),
    )(q, k, v, do, lse, delta, ccol)


def _decay_bwd(dck, dcq, f, bf):
    lp = f.shape[0]
    nb = lp // 128

    def body(dck_ref, dcq_ref, f_ref, bf_ref, df_ref, dbf_ref):
        r = lax.broadcasted_iota(jnp.int32, (128, 128), 0)
        c = lax.broadcasted_iota(jnp.int32, (128, 128), 1)
        triu = (c >= r).astype(_F32)

        def step(n, carry):
            tail, dbf = carry
            bi = nb - 1 - n
            rows = pl.ds(pl.multiple_of(bi * 128, 128), 128)
            cols = jnp.zeros((128, 128), _F32)
            for p in range(N_PAIRS):
                dcq = jnp.concatenate([dcq_ref[p, :, rows], jnp.zeros((120, 128), _F32)], axis=0).T
                dcp = dcq - dck_ref[p, rows, :]
                cols = cols + (dcp if p == 0 else pltpu.roll(dcp, 2 * p, axis=1))
            lane = lax.broadcasted_iota(jnp.int32, (128, 128), 1)
            cols = jnp.where(lane < N_HEADS, cols, 0.0)
            rev = _dot_exact(triu, cols) + tail
            pos = bi * 128 + lax.broadcasted_iota(jnp.int32, (128, 1), 0)
            df = jnp.where(pos >= ROW_PAD, rev * (1.0 - _sigmoid(f_ref[rows, :] + bf_ref[...])), 0.0)
            df_ref[rows, :] = df.astype(df_ref.dtype)
            return rev[0:1, :], dbf + jnp.sum(df, axis=0, keepdims=True)

        zero = jnp.zeros((1, 128), _F32)
        _, dbf = lax.fori_loop(0, nb, step, (zero, zero), unroll=True)
        dbf_ref[...] = dbf

    return pl.pallas_call(
        body, name="decay_bwd",
        out_shape=[jax.ShapeDtypeStruct((lp, 128), _MM), jax.ShapeDtypeStruct((1, 128), _F32)],
        compiler_params=_params(),
    )(dck, dcq, f, bf)


def _conv_bwd_rows(dpw, z, g, b, wpw):
    lp = z.shape[0]
    t = _row_tile(lp)

    def body(dpw_ref, z_ref, g_ref, b_ref, w_ref, dz_ref, dw_ref, dg_ref, db_ref, dcb_ref):
        i = pl.program_id(0)

        @pl.when(i == 0)
        def _():
            dw_ref[...] = jnp.zeros_like(dw_ref)
            dg_ref[...] = jnp.zeros_like(dg_ref)
            db_ref[...] = jnp.zeros_like(db_ref)
            dcb_ref[...] = jnp.zeros_like(dcb_ref)

        xhat, rstd = _ln_stats(z_ref[...])
        zn = xhat * g_ref[...] + b_ref[...]
        sg = _sigmoid(zn)
        dpw = dpw_ref[...]
        dw_ref[...] += _dot_t0((zn * sg).astype(_MM), dpw)
        dzn = _dot_t1(dpw, w_ref[...]) * (sg * (1.0 + zn * (1.0 - sg)))
        dg_ref[...] += jnp.sum(dzn * xhat, axis=0, keepdims=True)
        db_ref[...] += jnp.sum(dzn, axis=0, keepdims=True)
        dz = _ln_bwd(dzn, xhat, rstd, g_ref[...])
        dz_ref[...] = dz
        dcb_ref[...] += jnp.sum(dz, axis=0, keepdims=True)

    row = pl.BlockSpec((t, D_CONV), lambda i: (i, 0))
    vec = _full((1, D_CONV))
    vshape = jax.ShapeDtypeStruct((1, D_CONV), _F32)
    return pl.pallas_call(
        body, name="conv_bwd_rows", grid=(lp // t,),
        in_specs=[row, row, vec, vec, _full((D_CONV, D_CONV))],
        out_specs=[row, _full((D_CONV, D_CONV)), vec, vec, vec],
        out_shape=[jax.ShapeDtypeStruct((lp, D_CONV), _F32), jax.ShapeDtypeStruct((D_CONV, D_CONV), _F32),
                   vshape, vshape, vshape],
        compiler_params=_params(("arbitrary",)),
    )(dpw, z, g, b, wpw)


def _conv_bwd_taps(dz, u, ug, cw):
    lp = u.shape[0]
    t = _row_tile(lp)
    nt = lp // t

    def body(dz_ref, u_ref, ug_ref, cw_ref, du_ref, dug_ref, dcw_ref, win):
        i = pl.program_id(0)

        @pl.when(i == 0)
        def _():
            win[t:t + HALO, :] = jnp.zeros((HALO, D_CONV), _F32)
            dcw_ref[...] = jnp.zeros_like(dcw_ref)

        win[0:t, :] = dz_ref[...]
        u = u_ref[...]
        sg = _sigmoid(ug_ref[...])
        hh = u * sg
        dhh = jnp.zeros((t, D_CONV), _F32)
        for j in range(CONV_WIDTH):
            sh = win[CONV_WIDTH - 1 - j:CONV_WIDTH - 1 - j + t, :]
            dhh = dhh + cw_ref[j:j + 1, :] * sh
            dcw_ref[j:j + 1, :] += jnp.sum(hh * sh, axis=0, keepdims=True)
        du_ref[...] = (dhh * sg).astype(du_ref.dtype)
        dug_ref[...] = (dhh * u * sg * (1.0 - sg)).astype(dug_ref.dtype)
        win[t:t + HALO, :] = win[0:HALO, :]

    row = pl.BlockSpec((t, D_CONV), lambda i: (nt - 1 - i, 0))
    return pl.pallas_call(
        body, name="conv_bwd_taps", grid=(nt,),
        in_specs=[row, row, row, _full((32, D_CONV))],
        out_specs=[row, row, _full((32, D_CONV))],
        out_shape=[jax.ShapeDtypeStruct((lp, D_CONV), _MM)] * 2 + [jax.ShapeDtypeStruct((32, D_CONV), _F32)],
        scratch_shapes=[pltpu.VMEM((t + HALO, D_CONV), _F32)],
        compiler_params=_params(("arbitrary",)),
    )(dz, u, ug, cw)


def _proj_bwd_x(pieces, df, wpt, dr, x, head, g):
    seq = x.shape[0]
    lp = ROW_X + seq
    t = _row_tile(lp)
    nt = lp // t
    nx = t // ROW_X

    def body(*refs):
        dq_ref, dk_ref, dv_ref, dga_ref, du_ref, dug_ref, dgc_ref, df_ref, wt_ref, dr_ref = refs[:10]
        x_refs, (head_ref, g_ref) = refs[10:10 + nx], refs[10 + nx:12 + nx]
        gx_ref, dmeta_ref, dg_ref, db_ref, buf, sem = refs[12 + nx:]
        i = pl.program_id(0)

        @pl.when(i == 0)
        def _():
            dg_ref[...] = jnp.zeros_like(dg_ref)
            db_ref[...] = jnp.zeros_like(db_ref)

        dh = ALPHA * dr_ref[...]
        for n, ref in enumerate((dq_ref, dk_ref, dv_ref, dga_ref, du_ref, dug_ref, dgc_ref)):
            dh = dh + _dot(ref[...].astype(_MM), wt_ref[512 * n:512 * (n + 1), :])
        dh = dh + _dot(df_ref[...], wt_ref[3584:3712, :])
        rows = i * t + lax.broadcasted_iota(jnp.int32, (t, 1), 0)
        dh = jnp.where(rows >= ROW_PAD, dh, 0.0)
        xhat, rstd = _ln_stats(_x_tile(i, x_refs, head_ref))
        dg_ref[...] += jnp.sum(dh * xhat, axis=0, keepdims=True)
        db_ref[...] += jnp.sum(dh, axis=0, keepdims=True)
        dx = _ln_bwd(dh, xhat, rstd, g_ref[...])

        def first_copy():
            return pltpu.make_async_copy(buf.at[0, pl.ds(ROW_X, t - ROW_X), :], gx_ref.at[pl.ds(0, t - ROW_X), :], sem)

        def tile_copy(n):
            return pltpu.make_async_copy(buf.at[n % 2], gx_ref.at[pl.ds(pl.multiple_of(n * t - ROW_X, ROW_X), t), :], sem)

        pl.when(i == 1)(lambda: first_copy().wait())
        pl.when(i > 1)(lambda: tile_copy(i - 1).wait())
        buf[i % 2] = dx

        @pl.when(i == 0)
        def _():
            dmeta_ref[...] = dx[ROW_PAD:ROW_X, :]
            first_copy().start()

        pl.when(i > 0)(lambda: tile_copy(i).start())
        if nt == 1:
            first_copy().wait()
        else:
            pl.when(i == nt - 1)(lambda: tile_copy(i).wait())

    row = lambda w: pl.BlockSpec((t, w), lambda i: (i, 0))
    vec = _full((1, D_MODEL))
    return pl.pallas_call(
        body, name="proj_bwd_x", grid=(nt,),
        in_specs=[row(512)] * 7 + [row(128), _full((D_IN_PAD, D_MODEL)), row(D_MODEL)] + _x_specs(t)
                 + [_full((ROW_X, D_MODEL)), vec],
        out_specs=[pl.BlockSpec(memory_space=pl.ANY), _full((N_META, D_MODEL)), vec, vec],
        out_shape=[jax.ShapeDtypeStruct((seq, D_MODEL), _F32), jax.ShapeDtypeStruct((N_META, D_MODEL), _F32),
                   jax.ShapeDtypeStruct((1, D_MODEL), _F32), jax.ShapeDtypeStruct((1, D_MODEL), _F32)],
        scratch_shapes=[pltpu.VMEM((2, t, D_MODEL), _F32), pltpu.SemaphoreType.DMA(())],
        compiler_params=_params(("arbitrary",)),
    )(*pieces, df, wpt, dr, *([x] * nx), head, g)


def _proj_bwd_w(h, pieces, name):
    lp = h.shape[0]
    t = _row_tile(lp)
    n = len(pieces)
    widths = [p.shape[1] for p in pieces]

    def body(*refs):
        h_ref, d_refs, w_refs = refs[0], refs[1:1 + n], refs[1 + n:]
        i = pl.program_id(0)

        @pl.when(i == 0)
        def _():
            for w_ref in w_refs:
                w_ref[...] = jnp.zeros_like(w_ref)

        hb = h_ref[...].astype(_MM)
        for d_ref, w_ref in zip(d_refs, w_refs):
            w_ref[...] += _dot_t0(d_ref[...].astype(_MM), hb)

    row = lambda w: pl.BlockSpec((t, w), lambda i: (i, 0))
    return pl.pallas_call(
        body, name=name, grid=(lp // t,),
        in_specs=[row(D_MODEL)] + [row(w) for w in widths],
        out_specs=[_full((w, D_MODEL)) for w in widths],
        out_shape=[jax.ShapeDtypeStruct((w, D_MODEL), _F32) for w in widths],
        compiler_params=_params(("arbitrary",)),
    )(h, *pieces)


def _adamw_math(w, g, m, v):
    m = ADAM_B1 * m + (1.0 - ADAM_B1) * g
    v = ADAM_B2 * v + (1.0 - ADAM_B2) * (g * g)
    m_hat = m / (1.0 - ADAM_B1 ** ADAM_STEP)
    v_hat = v / (1.0 - ADAM_B2 ** ADAM_STEP)
    delta = -ADAM_LR * (m_hat / (jnp.sqrt(v_hat) + ADAM_EPS) + ADAM_WD * w)
    return delta, m, v


def _adamw_tiled(w, g, m, v, rows, name):
    r, c = w.shape

    def body(w_ref, g_ref, m_ref, v_ref, go_ref, d_ref, nm_ref, nv_ref):
        g = g_ref[...]
        go_ref[...] = g
        d_ref[...], nm_ref[...], nv_ref[...] = _adamw_math(w_ref[...], g, m_ref[...], v_ref[...])

    spec = pl.BlockSpec((rows, c), lambda i: (i, 0))
    return pl.pallas_call(
        body, name=name, grid=(pl.cdiv(r, rows),), in_specs=[spec] * 4, out_specs=[spec] * 4,
        out_shape=[jax.ShapeDtypeStruct((r, c), _F32)] * 4,
        compiler_params=_params(("arbitrary",)),
    )(w, g, m, v)


def _adamw_small(ws, gs, ms, vs):
    n = len(ws)

    def body(*refs):
        ins, outs = refs[:4 * n], refs[4 * n:]
        for k in range(n):
            d, m, v = _adamw_math(ins[k][...], ins[n + k][...], ins[2 * n + k][...], ins[3 * n + k][...])
            outs[k][...], outs[n + k][...], outs[2 * n + k][...] = d, m, v

    shapes = [jax.ShapeDtypeStruct(w.shape, _F32) for w in ws]
    res = pl.pallas_call(body, name="adamw_small", out_shape=shapes * 3, compiler_params=_params())(*ws, *gs, *ms, *vs)
    return res[:n], res[n:2 * n], res[2 * n:]


def _place():
    x, y, c = lax.axis_index("x"), lax.axis_index("y"), lax.axis_index("c")
    chips = [(1 - x, y), (x, 1 - y), (1 - x, 1 - y)]
    return x, y, c, chips


def _half(ref, slot, c, rows):
    hr = rows // 2
    return ref.at[slot, pl.ds(pl.multiple_of(c * hr, 8), hr), :]


def _all_gather(shards):
    n = len(shards)

    def body(*refs):
        ins, outs = refs[:n], refs[n:2 * n]
        send, recv, local = refs[2 * n:]
        x, y, c, chips = _place()
        me = 2 * x + y
        sibling = (x, y, 1 - c)

        def copy(k, t, slot, half, to, src=None):
            rows = ins[t].shape[0]
            dst = _half(outs[t], slot, half, rows)
            return pltpu.make_async_remote_copy(
                src_ref=dst if src is None else src, dst_ref=dst,
                send_sem=send.at[k], recv_sem=recv.at[k], device_id=to, device_id_type=MESH)

        own = [pltpu.make_async_copy(ins[t], outs[t].at[me], local.at[t]) for t in range(n)]
        for cp in own:
            cp.start()
        first, passed = [], []
        for t in range(n):
            hr = ins[t].shape[0] // 2
            mine = ins[t].at[pl.ds(pl.multiple_of(c * hr, 8), hr), :]
            for j, chip in enumerate(chips):
                first.append(copy(6 * t + j, t, me, c, (*chip, c), src=mine))
        for cp in first:
            cp.start()
        for t in range(n):
            for j, (cx, cy) in enumerate(chips):
                copy(6 * t + j, t, 2 * cx + cy, c, sibling).wait_recv()
                fwd = copy(6 * t + 3 + j, t, 2 * cx + cy, c, sibling)
                fwd.start()
                passed.append(fwd)
        for t in range(n):
            for j, (cx, cy) in enumerate(chips):
                copy(6 * t + 3 + j, t, 2 * cx + cy, 1 - c, sibling).wait_recv()
        for cp in first + passed:
            cp.wait_send()
        for cp in own:
            cp.wait()

    any_spec = pl.BlockSpec(memory_space=pl.ANY)
    return pl.pallas_call(
        body, name="all_gather",
        in_specs=[any_spec] * n, out_specs=[any_spec] * n,
        out_shape=[jax.ShapeDtypeStruct((N_SHARD,) + s.shape, s.dtype) for s in shards],
        scratch_shapes=[pltpu.SemaphoreType.DMA((6 * n,)), pltpu.SemaphoreType.DMA((6 * n,)),
                        pltpu.SemaphoreType.DMA((n,))],
    )(*shards)


def _sibling_exchange(bufs, windows):
    n = len(bufs)

    def body(*refs):
        ins, mine, got = refs[:n], refs[n:2 * n], refs[2 * n:3 * n]
        send, recv, local = refs[3 * n:]
        x, y, c, _ = _place()
        cps = []
        for t, (stride, rows) in enumerate(windows):
            hr = rows // 2
            for s in range(N_SHARD):
                half = lambda which: ins[t].at[pl.ds(pl.multiple_of(s * stride + which * hr, 8), hr), :]
                k = N_SHARD * t + s
                cps.append(pltpu.make_async_copy(half(c), mine[t].at[s], local.at[k]))
                cps.append(pltpu.make_async_remote_copy(
                    src_ref=half(1 - c), dst_ref=got[t].at[s], send_sem=send.at[k], recv_sem=recv.at[k],
                    device_id=(x, y, 1 - c), device_id_type=MESH))
        for cp in cps:
            cp.start()
        for cp in cps:
            cp.wait()

    any_spec = pl.BlockSpec(memory_space=pl.ANY)
    halves = [jax.ShapeDtypeStruct((N_SHARD, rows // 2, b.shape[1]), b.dtype) for b, (_, rows) in zip(bufs, windows)]
    res = pl.pallas_call(
        body, name="sibling_exchange",
        in_specs=[any_spec] * n, out_specs=[any_spec] * (2 * n), out_shape=halves * 2,
        scratch_shapes=[pltpu.SemaphoreType.DMA((N_SHARD * n,))] * 3,
    )(*bufs)
    return res[:n], res[n:]


def _half_sum(mine, got, wire):
    n = len(mine)

    def body(*refs):
        ins, gots, outs = refs[:n], refs[n:2 * n], refs[2 * n:]
        for t in range(n):
            outs[t][...] = (ins[t][...] + gots[t][...]).astype(outs[t].dtype)

    slot = lambda shape: pl.BlockSpec((1,) + tuple(shape[1:]), lambda s: (s, 0, 0))
    return pl.pallas_call(
        body, name="half_sum", grid=(N_SHARD,),
        in_specs=[slot(g.shape) for g in mine + got],
        out_specs=[slot(g.shape) for g in got],
        out_shape=[jax.ShapeDtypeStruct(g.shape, d) for g, d in zip(got, wire)],
        compiler_params=_params(("arbitrary",)),
    )(*mine, *got)


def _chip_exchange(parts):
    n = len(parts)

    def body(*refs):
        ins, outs = refs[:n], refs[n:2 * n]
        send, recv, local = refs[2 * n:]
        x, y, c, chips = _place()
        me = 2 * x + y
        own = [pltpu.make_async_copy(ins[t].at[me], outs[t].at[me], local.at[t]) for t in range(n)]
        cps = []
        for t in range(n):
            for j, (cx, cy) in enumerate(chips):
                cps.append(pltpu.make_async_remote_copy(
                    src_ref=ins[t].at[2 * cx + cy], dst_ref=outs[t].at[me],
                    send_sem=send.at[3 * t + j], recv_sem=recv.at[3 * t + j],
                    device_id=(cx, cy, c), device_id_type=MESH))
        for cp in own + cps:
            cp.start()
        for cp in cps + own:
            cp.wait()

    any_spec = pl.BlockSpec(memory_space=pl.ANY)
    return pl.pallas_call(
        body, name="chip_exchange",
        in_specs=[any_spec] * n, out_specs=[any_spec] * n,
        out_shape=[jax.ShapeDtypeStruct(p.shape, p.dtype) for p in parts],
        scratch_shapes=[pltpu.SemaphoreType.DMA((3 * n,)), pltpu.SemaphoreType.DMA((3 * n,)),
                        pltpu.SemaphoreType.DMA((n,))],
    )(*parts)


def _chip_sum(got):
    n = len(got)

    def body(*refs):
        gots, outs = refs[:n], refs[n:]
        for t in range(n):
            terms = [gots[t][s].astype(_F32) for s in range(N_SHARD)]
            outs[t][...] = ((terms[0] + terms[1]) + terms[2]) + terms[3]

    return pl.pallas_call(
        body, name="chip_sum", out_shape=[jax.ShapeDtypeStruct(g.shape[1:], _F32) for g in got],
        compiler_params=_params(),
    )(*got)


def _sibling_share(halves):
    n = len(halves)

    def body(*refs):
        ins, outs = refs[:n], refs[n:2 * n]
        send, recv, local = refs[2 * n:]
        x, y, c, _ = _place()
        cps, own = [], []
        for t in range(n):
            hr = ins[t].shape[0]
            dst = outs[t].at[pl.ds(pl.multiple_of(c * hr, 8), hr), :]
            own.append(pltpu.make_async_copy(ins[t], dst, local.at[t]))
            cps.append(pltpu.make_async_remote_copy(
                src_ref=ins[t], dst_ref=dst, send_sem=send.at[t], recv_sem=recv.at[t],
                device_id=(x, y, 1 - c), device_id_type=MESH))
        for cp in own + cps:
            cp.start()
        for cp in cps + own:
            cp.wait()

    any_spec = pl.BlockSpec(memory_space=pl.ANY)
    return pl.pallas_call(
        body, name="sibling_share",
        in_specs=[any_spec] * n, out_specs=[any_spec] * n,
        out_shape=[jax.ShapeDtypeStruct((2 * h.shape[0], h.shape[1]), h.dtype) for h in halves],
        scratch_shapes=[pltpu.SemaphoreType.DMA((n,)), pltpu.SemaphoreType.DMA((n,)), pltpu.SemaphoreType.DMA((n,))],
    )(*halves)


def _reduce_scatter(bufs, windows, wire):
    chip_part = _half_sum(*_sibling_exchange(bufs, windows), wire)
    half = _chip_sum(_chip_exchange(chip_part))
    return _sibling_share(half)


REPL_SIZES = (D_MODEL, D_MODEL, N_HEADS, D_CONV, D_CONV, D_CONV, D_MODEL, D_MODEL)
REPL_ROWS = 48


def _pack_repl(parts):
    flat = jnp.concatenate([p.reshape(-1).astype(_F32) for p in parts])
    return jnp.pad(flat, (0, REPL_ROWS * 128 - flat.shape[0])).reshape(REPL_ROWS, 128)


def _unpack_repl(packed, shapes):
    flat = packed.reshape(-1)
    out, off = [], 0
    for shape, size in zip(shapes, REPL_SIZES):
        out.append(flat[off:off + size].reshape(shape))
        off += size
    return out


def _projection_operand(w_in_t):
    zrow = jnp.zeros((128 - N_HEADS, D_MODEL), w_in_t.dtype)
    return jnp.concatenate([w_in_t[:OFF_F], w_in_t[OFF_F + N_HEADS:], w_in_t[OFF_F:OFF_F + N_HEADS], zrow], axis=0)


def _local_grads(x, target, meta, ln_in_g, ln_in_b, w_in_t, b_f, conv_w, conv_b, ln_conv_g, ln_conv_b,
                 w_pw, w_out, ln_out_g, ln_out_b):
    vec = lambda a: a.reshape(1, -1).astype(_F32)
    head = jnp.concatenate([jnp.zeros((ROW_PAD, D_MODEL), _F32), meta], axis=0)
    wpt = _projection_operand(w_in_t)
    bf = jnp.pad(vec(b_f), ((0, 0), (0, 128 - N_HEADS)))
    cw = jnp.pad(conv_w, ((0, 32 - CONV_WIDTH), (0, 0)))

    h, q, k, v, ga, u, ug, gc, f, vt = _ln_proj(x, head, vec(ln_in_g), vec(ln_in_b), wpt)
    ccol = _decay_cumsum(f, bf)
    o, lse = _attn_fwd(q, k, vt, ccol)
    z, pw = _conv_fwd(u, ug, cw, vec(conv_b), vec(ln_conv_g), vec(ln_conv_b), w_pw)
    dr, do, dga, dpw, dgc, delta, dw_out, dg_out, db_out, loss = _out_loss_bwd(
        o, ga, pw, gc, h, target, w_out, vec(ln_out_g), vec(ln_out_b))
    dq, dk, dv, dck, dcq = _attn_bwd(q, k, v, do, lse, delta, ccol)
    df, dbf = _decay_bwd(dck, dcq, f, bf)
    dz, dw_pw, dg_conv, db_conv, dconv_b = _conv_bwd_rows(dpw, z, vec(ln_conv_g), vec(ln_conv_b), w_pw)
    du, dug, dcw = _conv_bwd_taps(dz, u, ug, cw)
    pieces = (dq, dk, dv, dga, du, dug, dgc)
    grad_x, dmeta, dg_in, db_in = _proj_bwd_x(pieces, df, wpt, dr, x, head, vec(ln_in_g))
    dwq, dwk, dwv, dwga = _proj_bwd_w(h, pieces[:4], "proj_bwd_w_attn")
    dwu, dwug, dwgc, dwf = _proj_bwd_w(h, pieces[4:] + (df,), "proj_bwd_w_conv")
    dw_in_t = jnp.concatenate([dwq, dwk, dwv, dwf[:N_HEADS], dwga, dwu, dwug, dwgc], axis=0)
    dw_in_t = jnp.pad(dw_in_t.reshape(N_SHARD, W_IN_SHARD, D_MODEL), ((0, 0), (0, W_IN_SLOT - W_IN_SHARD), (0, 0)))
    dw_in_t = dw_in_t.reshape(N_SHARD * W_IN_SLOT, D_MODEL)
    grads = dict(meta=dmeta, ln_in_g=dg_in, ln_in_b=db_in, w_in_t=dw_in_t, b_f=dbf[:, :N_HEADS],
                 conv_w=dcw[:CONV_WIDTH], conv_b=dconv_b, ln_conv_g=dg_conv, ln_conv_b=db_conv,
                 w_pw=dw_pw, w_out=dw_out, ln_out_g=dg_out, ln_out_b=db_out)
    return loss, grad_x, grads


def kernel(x, meta, ln_in_g, ln_in_b, w_in, b_f, conv_w, conv_b, ln_conv_g, ln_conv_b, w_pw, w_out, ln_out_g, ln_out_b, loss_target, m_meta, m_ln_in_g, m_ln_in_b, m_w_in, m_b_f, m_conv_w, m_conv_b, m_ln_conv_g, m_ln_conv_b, m_w_pw, m_w_out, m_ln_out_g, m_ln_out_b, v_meta, v_ln_in_g, v_ln_in_b, v_w_in, v_b_f, v_conv_w, v_conv_b, v_ln_conv_g, v_ln_conv_b, v_w_pw, v_w_out, v_ln_out_g, v_ln_out_b):
    conv_w2, w_pw2, w_out2 = conv_w[0], w_pw[0], w_out[0]
    w_t = jnp.transpose(w_in[0])

    small = jnp.concatenate([jnp.pad(conv_w2, ((0, 1), (0, 0))), meta.reshape(32, 128)], axis=0)
    g_in, g_pw, g_out, g_small = _all_gather(
        [jnp.pad(w_t.astype(_MM), ((0, W_IN_SLOT - W_IN_SHARD), (0, 0))), w_pw2.astype(_MM), w_out2.astype(_MM), small])
    w_in_t_full = g_in[:, :W_IN_SHARD, :].reshape(D_IN, D_MODEL)
    w_pw_full = g_pw.reshape(D_CONV, D_CONV)
    w_out_full = g_out.reshape(D_MODEL, D_MODEL)
    conv_w_full = jnp.transpose(g_small[:, :CONV_WIDTH, :], (1, 0, 2)).reshape(CONV_WIDTH, D_CONV)
    meta_full = jnp.transpose(g_small[:, 32:64, :].reshape(N_SHARD, N_META, 256), (1, 0, 2)).reshape(N_META, D_MODEL)

    loss, grad_x, gr = _local_grads(x[0], loss_target[0], meta_full, ln_in_g, ln_in_b, w_in_t_full, b_f, conv_w_full,
                                    conv_b, ln_conv_g, ln_conv_b, w_pw_full, w_out_full, ln_out_g, ln_out_b)
    grad_x = grad_x[None]

    repl_names = ("ln_in_g", "ln_in_b", "b_f", "conv_b", "ln_conv_g", "ln_conv_b", "ln_out_g", "ln_out_b")
    repl = _pack_repl([gr[n] for n in repl_names] + [loss])
    gcw = jnp.transpose(jnp.pad(gr["conv_w"], ((0, 1), (0, 0))).reshape(32, N_SHARD, 128), (1, 0, 2))
    gmeta = jnp.transpose(gr["meta"].reshape(N_META, N_SHARD, 256), (1, 0, 2)).reshape(N_SHARD, 32, 128)
    small_g = jnp.concatenate([gcw, gmeta, jnp.broadcast_to(repl[None], (N_SHARD, REPL_ROWS, 128))], axis=1)
    small_rows = small_g.shape[1]
    bufs = [gr["w_in_t"], gr["w_pw"], gr["w_out"], small_g.reshape(N_SHARD * small_rows, 128)]
    windows = [(W_IN_SLOT,) * 2, (D_CONV // N_SHARD,) * 2, (D_MODEL // N_SHARD,) * 2, (small_rows,) * 2]
    gw_slot, gw_pw, gw_out, gsmall = _reduce_scatter(bufs, windows, [_MM, _MM, _MM, _F32])
    g_conv_w = gsmall[:CONV_WIDTH]
    g_meta = gsmall[32:64].reshape(N_META, 256)
    g_repl = gsmall[64:64 + REPL_ROWS]

    repl_w = (ln_in_g, ln_in_b, b_f, conv_b, ln_conv_g, ln_conv_b, ln_out_g, ln_out_b)
    repl_m = (m_ln_in_g, m_ln_in_b, m_b_f, m_conv_b, m_ln_conv_g, m_ln_conv_b, m_ln_out_g, m_ln_out_b)
    repl_v = (v_ln_in_g, v_ln_in_b, v_b_f, v_conv_b, v_ln_conv_g, v_ln_conv_b, v_ln_out_g, v_ln_out_b)
    zero = jnp.zeros((1,), _F32)
    res_t = _adamw_tiled(w_t, gw_slot, jnp.transpose(m_w_in[0]), jnp.transpose(v_w_in[0]), 256, "adamw_w_in")
    gw_in, d_in, nm_in, nv_in = (jnp.transpose(a) for a in res_t)
    ws = [w_pw2, w_out2, conv_w2, meta, _pack_repl(repl_w + (zero,))]
    gs = [gw_pw, gw_out, g_conv_w, g_meta, g_repl]
    ms = [m_w_pw[0], m_w_out[0], m_conv_w[0], m_meta, _pack_repl(repl_m + (zero,))]
    vs = [v_w_pw[0], v_w_out[0], v_conv_w[0], v_meta, _pack_repl(repl_v + (zero + 1.0,))]
    ds, nms, nvs = _adamw_small(ws, gs, ms, vs)

    shapes = [w.shape for w in repl_w]
    total_loss = g_repl.reshape(-1)[sum(REPL_SIZES)]
    names = ("meta", "ln_in_g", "ln_in_b", "w_in", "b_f", "conv_w", "conv_b", "ln_conv_g", "ln_conv_b",
             "w_pw", "w_out", "ln_out_g", "ln_out_b")

    def assemble(t_in, smalls):
        pw_, out_, cw_, meta_, packed = smalls
        r = dict(zip(repl_names, _unpack_repl(packed, shapes)))
        r.update(meta=meta_, w_in=t_in[None], conv_w=cw_[None], w_pw=pw_[None], w_out=out_[None])
        return [r[n] for n in names]

    return (total_loss, grad_x, *assemble(gw_in, gs), *assemble(d_in, ds), *assemble(nm_in, nms),
            *assemble(nv_in, nvs))
```

```python
import functools

import jax
import jax.numpy as jnp
from jax import lax
from jax.experimental import pallas as pl
from jax.experimental.pallas import tpu as pltpu

D_MODEL = 1024
N_META = 16
D_ATTN = 512
D_CONV = 512
N_HEADS = 8
HEAD_DIM = 64
N_PAIRS = 4
CONV_WIDTH = 31
LN_EPS = 1e-5
ALPHA = 2.0 ** 0.25
ATTN_SCALE = 0.125
D_IN = 3592
OFF_F = 3 * D_ATTN
D_IN_PAD = 3712
N_SHARD = 4
W_IN_SHARD = D_IN // N_SHARD
W_IN_SLOT = 928

ROW_PAD = 112
ROW_X = ROW_PAD + N_META
HALO = 32
MASKED = 1e30

ADAM_LR = 0.001
ADAM_B1 = 0.9
ADAM_B2 = 0.999
ADAM_EPS = 1e-08
ADAM_WD = 0.01
ADAM_STEP = 10

VMEM_LIMIT = 56 * 1024 * 1024

_MM = jnp.bfloat16
_F32 = jnp.float32
MESH = pl.DeviceIdType.MESH


def _params(sem=None):
    return pltpu.CompilerParams(dimension_semantics=sem, vmem_limit_bytes=VMEM_LIMIT)


def _row_tile(n_rows):
    for t in (384, 256, 128):
        if n_rows % t == 0:
            return t
    raise ValueError(f"padded sequence length {n_rows} is not a multiple of 128")


def _sigmoid(x):
    return 1.0 / (1.0 + jnp.exp(-x))


def _ln_stats(x):
    mu = jnp.mean(x, axis=-1, keepdims=True)
    xc = x - mu
    var = jnp.mean(xc * xc, axis=-1, keepdims=True)
    rstd = lax.rsqrt(var + LN_EPS)
    return xc * rstd, rstd


def _ln_bwd(dy, xhat, rstd, g):
    dxh = dy * g
    m1 = jnp.mean(dxh, axis=-1, keepdims=True)
    m2 = jnp.mean(dxh * xhat, axis=-1, keepdims=True)
    return rstd * (dxh - m1 - xhat * m2)


def _dot(a, b):
    return jnp.dot(a, b, preferred_element_type=_F32)


def _dot_t0(a, b):
    return lax.dot_general(a, b, (((0,), (0,)), ((), ())), preferred_element_type=_F32)


def _dot_t1(a, b):
    return lax.dot_general(a, b, (((1,), (1,)), ((), ())), preferred_element_type=_F32)


def _dot_exact(a, b):
    return jnp.dot(a, b, preferred_element_type=_F32, precision=lax.Precision.HIGHEST)


def _full(shape):
    return pl.BlockSpec(shape, lambda *_: (0,) * len(shape))


def _x_specs(t):
    n = t // ROW_X
    return [pl.BlockSpec((ROW_X, D_MODEL), lambda i, s=s: (jnp.maximum(i * n + s - 1, 0), 0)) for s in range(n)]


def _x_tile(i, refs, head_ref=None):
    first = refs[0][...]
    if head_ref is not None:
        first = jnp.where(i == 0, head_ref[...], first)
    return jnp.concatenate([first] + [r[...] for r in refs[1:]], axis=0)


def _ln_proj(x, head, g, b, wpt):
    lp = ROW_X + x.shape[0]
    t = _row_tile(lp)
    nx = t // ROW_X

    def body(*refs):
        x_refs, (head_ref, g_ref, b_ref, wt_ref) = refs[:nx], refs[nx:nx + 4]
        h_ref, q_ref, k_ref, v_ref, ga_ref, u_ref, ug_ref, gc_ref, f_ref, vt_ref = refs[nx + 4:]
        i = pl.program_id(0)
        xhat, _ = _ln_stats(_x_tile(i, x_refs, head_ref))
        rows = i * t + lax.broadcasted_iota(jnp.int32, (t, 1), 0)
        h = jnp.where(rows >= ROW_PAD, xhat * g_ref[...] + b_ref[...], 0.0)
        h_ref[...] = h
        hb = h.astype(_MM)
        q_ref[...] = (_dot_t1(hb, wt_ref[0:512, :]) * ATTN_SCALE).astype(q_ref.dtype)
        k_ref[...] = _dot_t1(hb, wt_ref[512:1024, :]).astype(k_ref.dtype)
        v = _dot_t1(hb, wt_ref[1024:1536, :])
        v_ref[...] = v.astype(v_ref.dtype)
        vt_ref[...] = v.T.astype(vt_ref.dtype)
        ga_ref[...] = _dot_t1(hb, wt_ref[1536:2048, :])
        u_ref[...] = _dot_t1(hb, wt_ref[2048:2560, :])
        ug_ref[...] = _dot_t1(hb, wt_ref[2560:3072, :])
        gc_ref[...] = _dot_t1(hb, wt_ref[3072:3584, :])
        f_ref[...] = _dot_t1(hb, wt_ref[3584:3712, :])

    row = lambda w: pl.BlockSpec((t, w), lambda i: (i, 0))
    f32 = lambda w: jax.ShapeDtypeStruct((lp, w), _F32)
    mm = lambda w: jax.ShapeDtypeStruct((lp, w), _MM)
    return pl.pallas_call(
        body, name="ln_proj", grid=(lp // t,),
        in_specs=_x_specs(t) + [_full((ROW_X, D_MODEL)), _full((1, D_MODEL)), _full((1, D_MODEL)),
                                _full((D_IN_PAD, D_MODEL))],
        out_specs=[row(D_MODEL)] + [row(512)] * 7 + [row(128), pl.BlockSpec((512, t), lambda i: (0, i))],
        out_shape=[f32(D_MODEL), mm(512), mm(512), mm(512), f32(512), f32(512), f32(512), f32(512), f32(128),
                   jax.ShapeDtypeStruct((512, lp), _MM)],
        compiler_params=_params(("arbitrary",)),
    )(*([x] * nx), head, g, b, wpt)


def _log_sigmoid(z):
    return jnp.minimum(z, 0.0) - jnp.log(1.0 + jnp.exp(-jnp.abs(z)))


def _decay_cumsum(f, bf):
    lp = f.shape[0]
    nb = lp // 128

    def body(f_ref, bf_ref, ccol_ref):
        r = lax.broadcasted_iota(jnp.int32, (128, 128), 0)
        c = lax.broadcasted_iota(jnp.int32, (128, 128), 1)
        tril = (c <= r).astype(_F32)

        def step(bi, carry):
            rows = pl.ds(pl.multiple_of(bi * 128, 128), 128)
            lf = _log_sigmoid(f_ref[rows, :] + bf_ref[...])
            cb = _dot_exact(tril, lf) + carry
            for p in range(N_PAIRS):
                ccol_ref[p, rows, :] = cb if p == 0 else pltpu.roll(cb, 128 - 2 * p, axis=1)
            return cb[127:128, :]

        lax.fori_loop(0, nb, step, jnp.zeros((1, 128), _F32))

    return pl.pallas_call(
        body, name="decay_cumsum",
        out_shape=jax.ShapeDtypeStruct((N_PAIRS, lp, 128), _F32),
        compiler_params=_params(),
    )(f, bf)


def _split3(x):
    hi = x.astype(_MM)
    r1 = x - hi.astype(_F32)
    mid = r1.astype(_MM)
    return hi, mid, (r1 - mid.astype(_F32)).astype(_MM)


def _augment(x, hd, first, second):
    lane = lax.broadcasted_iota(jnp.int32, (1, 128), 1)
    base = HEAD_DIM * (1 - hd)
    out = jnp.where(lane // HEAD_DIM == hd, x, jnp.zeros_like(x))
    for slot, val in enumerate((first, second)):
        lo = base + 3 * slot
        if val is None:
            continue
        if isinstance(val, float):
            out = jnp.where((lane >= lo) & (lane < lo + 3), jnp.asarray(val, x.dtype), out)
        else:
            for n, piece in enumerate(_split3(val)):
                out = jnp.where(lane == lo + n, piece, out)
    return out


def _transpose_mm(x):
    return x.astype(_F32).T.astype(_MM)


def _diag_valid(i, t):
    sub = lax.broadcasted_iota(jnp.int32, (t, t), 0)
    return (sub <= lax.broadcasted_iota(jnp.int32, (t, t), 1)) & (i * t + sub >= ROW_PAD)


def _key_operand(k, c, hd, first_row):
    pos = first_row + lax.broadcasted_iota(jnp.int32, (k.shape[0], 1), 0)
    return _augment(k, hd, jnp.where(pos >= ROW_PAD, -c, -MASKED), 1.0)


def _attn_fwd(q, k, vt, ccol):
    lp = q.shape[0]
    t = _row_tile(lp)
    nb = lp // t

    def body(q_ref, k_ref, vt_ref, ccol_ref, o_ref, lse_ref, ka_ref):
        i = pl.program_id(1)

        @pl.when(i == 0)
        def _():
            for hd in range(2):
                for n in range(nb):
                    rows = slice(n * t, (n + 1) * t)
                    ka_ref[hd, rows, :] = _key_operand(k_ref[rows, :], ccol_ref[0, rows, hd:hd + 1], hd, n * t)

        qats = [_transpose_mm(_augment(q_ref[...], hd, 1.0, None)) for hd in range(2)]

        def scores(j):
            keys = pl.ds(pl.multiple_of(j * t, 128), t)
            return tuple(_dot(ka_ref[hd, keys, :], qats[hd]) for hd in range(2))

        def update(j, sts, states, diag):
            vt = vt_ref[:, pl.ds(pl.multiple_of(j * t, 128), t)]
            new = []
            for st, (m, l, acc) in zip(sts, states):
                if diag:
                    st = jnp.where(_diag_valid(i, t), st, -MASKED)
                m_new = jnp.maximum(m, jnp.max(st, axis=0, keepdims=True))
                a = jnp.exp(m - m_new)
                pt = jnp.exp(st - m_new)
                l = a * l + jnp.sum(pt, axis=0, keepdims=True)
                new.append((m_new, l, a * acc + _dot(vt, pt.astype(_MM))))
            return tuple(new)

        def step(j, carry):
            sts, states = carry
            return scores(j + 1), update(j, sts, states, False)

        init = (jnp.full((1, t), -MASKED, _F32), jnp.zeros((1, t), _F32), jnp.zeros((128, t), _F32))
        sts, states = lax.fori_loop(0, i, step, (scores(0), (init, init)))
        (m0, l0, acc0), (m1, l1, acc1) = update(i, sts, states, True)
        row = lax.broadcasted_iota(jnp.int32, (128, t), 0)
        o_ref[...] = jnp.where(row < HEAD_DIM, acc0 / l0, acc1 / l1).T
        lse_ref[0] = jnp.where(row == 0, m0 + jnp.log(l0), jnp.where(row == 1, m1 + jnp.log(l1), 0.0)).T

    return pl.pallas_call(
        body, name="attn_fwd", grid=(N_PAIRS, nb),
        in_specs=[pl.BlockSpec((t, 128), lambda p, i: (i, p)),
                  pl.BlockSpec((lp, 128), lambda p, i: (0, p)),
                  pl.BlockSpec((128, lp), lambda p, i: (p, 0)),
                  pl.BlockSpec((1, lp, 128), lambda p, i: (p, 0, 0))],
        out_specs=[pl.BlockSpec((t, 128), lambda p, i: (i, p)),
                   pl.BlockSpec((1, t, 128), lambda p, i: (p, i, 0))],
        out_shape=[jax.ShapeDtypeStruct((lp, D_ATTN), _F32), jax.ShapeDtypeStruct((N_PAIRS, lp, 128), _F32)],
        scratch_shapes=[pltpu.VMEM((2, lp, 128), _MM)],
        compiler_params=_params(("arbitrary", "arbitrary")),
    )(q, k, vt, ccol)


def _conv_fwd(u, ug, cw, cb, g, b, wpw):
    lp = u.shape[0]
    t = _row_tile(lp)

    def body(u_ref, ug_ref, cw_ref, cb_ref, g_ref, b_ref, wpw_ref, z_ref, pw_ref, win):
        i = pl.program_id(0)

        @pl.when(i == 0)
        def _():
            win[0:HALO, :] = jnp.zeros((HALO, D_CONV), _F32)

        win[HALO:HALO + t, :] = u_ref[...] * _sigmoid(ug_ref[...])
        acc = jnp.broadcast_to(cb_ref[...], (t, D_CONV))
        for j in range(CONV_WIDTH):
            acc = acc + cw_ref[j:j + 1, :] * win[HALO - (CONV_WIDTH - 1) + j:HALO - (CONV_WIDTH - 1) + j + t, :]
        z_ref[...] = acc
        xhat, _ = _ln_stats(acc)
        zn = xhat * g_ref[...] + b_ref[...]
        a = zn * _sigmoid(zn)
        pw_ref[...] = _dot(a.astype(_MM), wpw_ref[...])
        win[0:HALO, :] = win[t:t + HALO, :]

    row = pl.BlockSpec((t, D_CONV), lambda i: (i, 0))
    vec = _full((1, D_CONV))
    return pl.pallas_call(
        body, name="conv_fwd", grid=(lp // t,),
        in_specs=[row, row, _full((32, D_CONV)), vec, vec, vec, _full((D_CONV, D_CONV))],
        out_specs=[row, row],
        out_shape=[jax.ShapeDtypeStruct((lp, D_CONV), _F32)] * 2,
        scratch_shapes=[pltpu.VMEM((t + HALO, D_CONV), _F32)],
        compiler_params=_params(("arbitrary",)),
    )(u, ug, cw, cb, g, b, wpw)


def _out_loss_bwd(o, ga, pw, gc, h, target, w_out, g, b):
    lp = o.shape[0]
    t = _row_tile(lp)
    nx = t // ROW_X

    def body(*refs):
        o_ref, ga_ref, pw_ref, gc_ref, h_ref = refs[:5]
        tgt_refs, (w_ref, g_ref, b_ref) = refs[5:5 + nx], refs[5 + nx:8 + nx]
        dr_ref, do_ref, dga_ref, dpw_ref, dgc_ref, delta_ref, dw_ref, dg_ref, db_ref, loss_ref = refs[8 + nx:]
        i = pl.program_id(0)

        @pl.when(i == 0)
        def _():
            dw_ref[...] = jnp.zeros_like(dw_ref)
            dg_ref[...] = jnp.zeros_like(dg_ref)
            db_ref[...] = jnp.zeros_like(db_ref)
            loss_ref[...] = jnp.zeros_like(loss_ref)

        o, ga, pw, gc = o_ref[...], ga_ref[...], pw_ref[...], gc_ref[...]
        sa, sc = _sigmoid(ga), _sigmoid(gc)
        silu_a, silu_c = ga * sa, gc * sc
        ycat = jnp.concatenate([o * silu_a, pw * silu_c], axis=-1).astype(_MM)
        r = ALPHA * h_ref[...] + _dot(ycat, w_ref[...])
        xhat, rstd = _ln_stats(r)
        out = xhat * g_ref[...] + b_ref[...]
        rows = i * t + lax.broadcasted_iota(jnp.int32, (t, 1), 0)
        diff = jnp.where(rows >= ROW_X, out - _x_tile(i, tgt_refs), 0.0)
        loss_ref[...] += (0.5 / D_MODEL) * jnp.sum(diff * diff, keepdims=True)
        dout = diff * (1.0 / D_MODEL)
        dg_ref[...] += jnp.sum(dout * xhat, axis=0, keepdims=True)
        db_ref[...] += jnp.sum(dout, axis=0, keepdims=True)
        dr = _ln_bwd(dout, xhat, rstd, g_ref[...])
        dr_ref[...] = dr
        drb = dr.astype(_MM)
        dw_ref[...] += _dot_t0(ycat, drb)
        dy = _dot_t1(drb, w_ref[...])
        dya, dyc = dy[:, :D_ATTN], dy[:, D_ATTN:]
        do = dya * silu_a
        do_ref[...] = do.astype(do_ref.dtype)
        dga_ref[...] = (dya * o * (sa * (1.0 + ga * (1.0 - sa)))).astype(dga_ref.dtype)
        dpw_ref[...] = (dyc * silu_c).astype(dpw_ref.dtype)
        dgc_ref[...] = (dyc * pw * (sc * (1.0 + gc * (1.0 - sc)))).astype(dgc_ref.dtype)
        sel = (lax.broadcasted_iota(jnp.int32, (128, 128), 0) // HEAD_DIM
               == lax.broadcasted_iota(jnp.int32, (128, 128), 1)).astype(_F32)
        prod = do * o
        for p in range(N_PAIRS):
            delta_ref[p] = _dot_exact(prod[:, 128 * p:128 * (p + 1)], sel)

    row = lambda w: pl.BlockSpec((t, w), lambda i: (i, 0))
    return pl.pallas_call(
        body, name="out_loss_bwd", grid=(lp // t,),
        in_specs=[row(512)] * 4 + [row(D_MODEL)] + _x_specs(t) + [_full((D_MODEL, D_MODEL))] + [_full((1, D_MODEL))] * 2,
        out_specs=[row(D_MODEL)] + [row(512)] * 4 + [pl.BlockSpec((N_PAIRS, t, 128), lambda i: (0, i, 0)),
                   _full((D_MODEL, D_MODEL)), _full((1, D_MODEL)), _full((1, D_MODEL)), _full((1, 1))],
        out_shape=[jax.ShapeDtypeStruct((lp, D_MODEL), _F32)] + [jax.ShapeDtypeStruct((lp, 512), _MM)] * 4
                  + [jax.ShapeDtypeStruct((N_PAIRS, lp, 128), _F32), jax.ShapeDtypeStruct((D_MODEL, D_MODEL), _F32),
                     jax.ShapeDtypeStruct((1, D_MODEL), _F32), jax.ShapeDtypeStruct((1, D_MODEL), _F32),
                     jax.ShapeDtypeStruct((1, 1), _F32)],
        compiler_params=_params(("arbitrary",)),
    )(o, ga, pw, gc, h, *([target] * nx), w_out, g, b)


def _attn_bwd(q, k, v, do, lse, delta, ccol):
    lp = q.shape[0]
    t = _row_tile(lp)
    nb = lp // t

    def body(q_ref, k_ref, v_ref, do_ref, lse_ref, delta_ref, ccol_ref,
             dq_ref, dk_ref, dv_ref, dck_ref, dcq_ref,
             qa_ref, qat_ref, doa_ref, doat_ref, dqt_ref):
        j = pl.program_id(1)

        @pl.when(j == 0)
        def _():
            dqt_ref[...] = jnp.zeros_like(dqt_ref)
            for hd in range(2):
                for n in range(nb):
                    rows = slice(n * t, (n + 1) * t)
                    qa = _augment(q_ref[rows, :], hd, 1.0, -lse_ref[0, rows, hd:hd + 1])
                    doa = _augment(do_ref[rows, :], hd, -delta_ref[0, rows, hd:hd + 1], None)
                    qa_ref[hd, rows, :] = qa
                    qat_ref[hd, :, rows] = _transpose_mm(qa)
                    doa_ref[hd, rows, :] = doa
                    doat_ref[hd, :, rows] = _transpose_mm(doa)

        kas = [_key_operand(k_ref[...], ccol_ref[0, :, hd:hd + 1], hd, j * t) for hd in range(2)]
        kats = [_transpose_mm(ka) for ka in kas]
        vas = [_augment(v_ref[...], hd, 1.0, None) for hd in range(2)]

        def block(i, carry, diag):
            rows = pl.ds(pl.multiple_of(i * t, 128), t)
            new = []
            for hd in range(2):
                dk, dv = carry[hd]
                st = _dot(kas[hd], qat_ref[hd, :, rows])
                if diag:
                    st = jnp.where(_diag_valid(j, t), st, -MASKED)
                pt = jnp.exp(st)
                dsb = (pt * _dot(vas[hd], doat_ref[hd, :, rows])).astype(_MM)
                dv = dv + _dot(pt.astype(_MM), doa_ref[hd, rows, :])
                dk = dk + _dot(dsb, qa_ref[hd, rows, :])
                dqt_ref[hd, :, rows] += _dot(kats[hd], dsb)
                new.append((dk, dv))
            return tuple(new)

        zero = jnp.zeros((t, 128), _F32)
        carry = block(j, ((zero, zero), (zero, zero)), True)
        (dk0, dv0), (dk1, dv1) = lax.fori_loop(j + 1, nb, functools.partial(block, diag=False), carry)
        lane = lax.broadcasted_iota(jnp.int32, (1, 128), 1)
        dk_ref[...] = jnp.where(lane < HEAD_DIM, dk0, dk1).astype(dk_ref.dtype)
        dv_ref[...] = jnp.where(lane < HEAD_DIM, dv0, dv1).astype(dv_ref.dtype)
        dck_ref[0] = jnp.where(lane == 0, dk0[:, HEAD_DIM:HEAD_DIM + 1], jnp.where(lane == 1, dk1[:, 0:1], 0.0))

        @pl.when(j == nb - 1)
        def _():
            row = lax.broadcasted_iota(jnp.int32, (128, t), 0)
            sub = lax.broadcasted_iota(jnp.int32, (8, t), 0)
            for n in range(nb):
                cols = slice(n * t, (n + 1) * t)
                t0, t1 = dqt_ref[0, :, cols], dqt_ref[1, :, cols]
                dq_ref[cols, :] = jnp.where(row < HEAD_DIM, t0, t1).T * ATTN_SCALE
                dcq_ref[0, :, cols] = jnp.where(sub == 0, t0[HEAD_DIM + 3:HEAD_DIM + 4, :],
                                                jnp.where(sub == 1, t1[3:4, :], 0.0))

    blk = pl.BlockSpec((t, 128), lambda p, j: (j, p))
    res = pl.BlockSpec((lp, 128), lambda p, j: (0, p))
    stat = pl.BlockSpec((1, lp, 128), lambda p, j: (p, 0, 0))
    stat_blk = pl.BlockSpec((1, t, 128), lambda p, j: (p, j, 0))
    return pl.pallas_call(
        body, name="attn_bwd", grid=(N_PAIRS, nb),
        in_specs=[res, blk, blk, res, stat, stat, stat_blk],
        out_specs=[res, blk, blk, stat_blk, pl.BlockSpec((1, 8, lp), lambda p, j: (p, 0, 0))],
        out_shape=[jax.ShapeDtypeStruct((lp, D_ATTN), _F32), jax.ShapeDtypeStruct((lp, D_ATTN), _MM),
                   jax.ShapeDtypeStruct((lp, D_ATTN), _MM), jax.ShapeDtypeStruct((N_PAIRS, lp, 128), _F32),
                   jax.ShapeDtypeStruct((N_PAIRS, 8, lp), _F32)],
        scratch_shapes=[pltpu.VMEM((2, lp, 128), _MM), pltpu.VMEM((2, 128, lp), _MM),
                        pltpu.VMEM((2, lp, 128), _MM), pltpu.VMEM((2, 128, lp), _MM),
                        pltpu.VMEM((2, 128, lp), _F32)],
        compiler_params=_params(("arbitrary", "arbitrary")),
    )(q, k, v, do, lse, delta, ccol)


def _decay_bwd(dck, dcq, f, bf):
    lp = f.shape[0]
    nb = lp // 128

    def body(dck_ref, dcq_ref, f_ref, bf_ref, df_ref, dbf_ref):
        r = lax.broadcasted_iota(jnp.int32, (128, 128), 0)
        c = lax.broadcasted_iota(jnp.int32, (128, 128), 1)
        triu = (c >= r).astype(_F32)

        def step(n, carry):
            tail, dbf = carry
            bi = nb - 1 - n
            rows = pl.ds(pl.multiple_of(bi * 128, 128), 128)
            cols = jnp.zeros((128, 128), _F32)
            for p in range(N_PAIRS):
                dcq = jnp.concatenate([dcq_ref[p, :, rows], jnp.zeros((120, 128), _F32)], axis=0).T
                dcp = dcq - dck_ref[p, rows, :]
                cols = cols + (dcp if p == 0 else pltpu.roll(dcp, 2 * p, axis=1))
            lane = lax.broadcasted_iota(jnp.int32, (128, 128), 1)
            cols = jnp.where(lane < N_HEADS, cols, 0.0)
            rev = _dot_exact(triu, cols) + tail
            pos = bi * 128 + lax.broadcasted_iota(jnp.int32, (128, 1), 0)
            df = jnp.where(pos >= ROW_PAD, rev * (1.0 - _sigmoid(f_ref[rows, :] + bf_ref[...])), 0.0)
            df_ref[rows, :] = df.astype(df_ref.dtype)
            return rev[0:1, :], dbf + jnp.sum(df, axis=0, keepdims=True)

        zero = jnp.zeros((1, 128), _F32)
        _, dbf = lax.fori_loop(0, nb, step, (zero, zero), unroll=True)
        dbf_ref[...] = dbf

    return pl.pallas_call(
        body, name="decay_bwd",
        out_shape=[jax.ShapeDtypeStruct((lp, 128), _MM), jax.ShapeDtypeStruct((1, 128), _F32)],
        compiler_params=_params(),
    )(dck, dcq, f, bf)


def _conv_bwd_rows(dpw, z, g, b, wpw):
    lp = z.shape[0]
    t = _row_tile(lp)

    def body(dpw_ref, z_ref, g_ref, b_ref, w_ref, dz_ref, dw_ref, dg_ref, db_ref, dcb_ref):
        i = pl.program_id(0)

        @pl.when(i == 0)
        def _():
            dw_ref[...] = jnp.zeros_like(dw_ref)
            dg_ref[...] = jnp.zeros_like(dg_ref)
            db_ref[...] = jnp.zeros_like(db_ref)
            dcb_ref[...] = jnp.zeros_like(dcb_ref)

        xhat, rstd = _ln_stats(z_ref[...])
        zn = xhat * g_ref[...] + b_ref[...]
        sg = _sigmoid(zn)
        dpw = dpw_ref[...]
        dw_ref[...] += _dot_t0((zn * sg).astype(_MM), dpw)
        dzn = _dot_t1(dpw, w_ref[...]) * (sg * (1.0 + zn * (1.0 - sg)))
        dg_ref[...] += jnp.sum(dzn * xhat, axis=0, keepdims=True)
        db_ref[...] += jnp.sum(dzn, axis=0, keepdims=True)
        dz = _ln_bwd(dzn, xhat, rstd, g_ref[...])
        dz_ref[...] = dz
        dcb_ref[...] += jnp.sum(dz, axis=0, keepdims=True)

    row = pl.BlockSpec((t, D_CONV), lambda i: (i, 0))
    vec = _full((1, D_CONV))
    vshape = jax.ShapeDtypeStruct((1, D_CONV), _F32)
    return pl.pallas_call(
        body, name="conv_bwd_rows", grid=(lp // t,),
        in_specs=[row, row, vec, vec, _full((D_CONV, D_CONV))],
        out_specs=[row, _full((D_CONV, D_CONV)), vec, vec, vec],
        out_shape=[jax.ShapeDtypeStruct((lp, D_CONV), _F32), jax.ShapeDtypeStruct((D_CONV, D_CONV), _F32),
                   vshape, vshape, vshape],
        compiler_params=_params(("arbitrary",)),
    )(dpw, z, g, b, wpw)


def _conv_bwd_taps(dz, u, ug, cw):
    lp = u.shape[0]
    t = _row_tile(lp)
    nt = lp // t

    def body(dz_ref, u_ref, ug_ref, cw_ref, du_ref, dug_ref, dcw_ref, win):
        i = pl.program_id(0)

        @pl.when(i == 0)
        def _():
            win[t:t + HALO, :] = jnp.zeros((HALO, D_CONV), _F32)
            dcw_ref[...] = jnp.zeros_like(dcw_ref)

        win[0:t, :] = dz_ref[...]
        u = u_ref[...]
        sg = _sigmoid(ug_ref[...])
        hh = u * sg
        dhh = jnp.zeros((t, D_CONV), _F32)
        for j in range(CONV_WIDTH):
            sh = win[CONV_WIDTH - 1 - j:CONV_WIDTH - 1 - j + t, :]
            dhh = dhh + cw_ref[j:j + 1, :] * sh
            dcw_ref[j:j + 1, :] += jnp.sum(hh * sh, axis=0, keepdims=True)
        du_ref[...] = (dhh * sg).astype(du_ref.dtype)
        dug_ref[...] = (dhh * u * sg * (1.0 - sg)).astype(dug_ref.dtype)
        win[t:t + HALO, :] = win[0:HALO, :]

    row = pl.BlockSpec((t, D_CONV), lambda i: (nt - 1 - i, 0))
    return pl.pallas_call(
        body, name="conv_bwd_taps", grid=(nt,),
        in_specs=[row, row, row, _full((32, D_CONV))],
        out_specs=[row, row, _full((32, D_CONV))],
        out_shape=[jax.ShapeDtypeStruct((lp, D_CONV), _MM)] * 2 + [jax.ShapeDtypeStruct((32, D_CONV), _F32)],
        scratch_shapes=[pltpu.VMEM((t + HALO, D_CONV), _F32)],
        compiler_params=_params(("arbitrary",)),
    )(dz, u, ug, cw)


def _proj_bwd_x(pieces, df, wpt, dr, x, head, g):
    seq = x.shape[0]
    lp = ROW_X + seq
    t = _row_tile(lp)
    nt = lp // t
    nx = t // ROW_X

    def body(*refs):
        dq_ref, dk_ref, dv_ref, dga_ref, du_ref, dug_ref, dgc_ref, df_ref, wt_ref, dr_ref = refs[:10]
        x_refs, (head_ref, g_ref) = refs[10:10 + nx], refs[10 + nx:12 + nx]
        gx_ref, dmeta_ref, dg_ref, db_ref, buf, sem = refs[12 + nx:]
        i = pl.program_id(0)

        @pl.when(i == 0)
        def _():
            dg_ref[...] = jnp.zeros_like(dg_ref)
            db_ref[...] = jnp.zeros_like(db_ref)

        dh = ALPHA * dr_ref[...]
        for n, ref in enumerate((dq_ref, dk_ref, dv_ref, dga_ref, du_ref, dug_ref, dgc_ref)):
            dh = dh + _dot(ref[...].astype(_MM), wt_ref[512 * n:512 * (n + 1), :])
        dh = dh + _dot(df_ref[...], wt_ref[3584:3712, :])
        rows = i * t + lax.broadcasted_iota(jnp.int32, (t, 1), 0)
        dh = jnp.where(rows >= ROW_PAD, dh, 0.0)
        xhat, rstd = _ln_stats(_x_tile(i, x_refs, head_ref))
        dg_ref[...] += jnp.sum(dh * xhat, axis=0, keepdims=True)
        db_ref[...] += jnp.sum(dh, axis=0, keepdims=True)
        dx = _ln_bwd(dh, xhat, rstd, g_ref[...])

        def first_copy():
            return pltpu.make_async_copy(buf.at[0, pl.ds(ROW_X, t - ROW_X), :], gx_ref.at[pl.ds(0, t - ROW_X), :], sem)

        def tile_copy(n):
            return pltpu.make_async_copy(buf.at[n % 2], gx_ref.at[pl.ds(pl.multiple_of(n * t - ROW_X, ROW_X), t), :], sem)

        pl.when(i == 1)(lambda: first_copy().wait())
        pl.when(i > 1)(lambda: tile_copy(i - 1).wait())
        buf[i % 2] = dx

        @pl.when(i == 0)
        def _():
            dmeta_ref[...] = dx[ROW_PAD:ROW_X, :]
            first_copy().start()

        pl.when(i > 0)(lambda: tile_copy(i).start())
        if nt == 1:
            first_copy().wait()
        else:
            pl.when(i == nt - 1)(lambda: tile_copy(i).wait())

    row = lambda w: pl.BlockSpec((t, w), lambda i: (i, 0))
    vec = _full((1, D_MODEL))
    return pl.pallas_call(
        body, name="proj_bwd_x", grid=(nt,),
        in_specs=[row(512)] * 7 + [row(128), _full((D_IN_PAD, D_MODEL)), row(D_MODEL)] + _x_specs(t)
                 + [_full((ROW_X, D_MODEL)), vec],
        out_specs=[pl.BlockSpec(memory_space=pl.ANY), _full((N_META, D_MODEL)), vec, vec],
        out_shape=[jax.ShapeDtypeStruct((seq, D_MODEL), _F32), jax.ShapeDtypeStruct((N_META, D_MODEL), _F32),
                   jax.ShapeDtypeStruct((1, D_MODEL), _F32), jax.ShapeDtypeStruct((1, D_MODEL), _F32)],
        scratch_shapes=[pltpu.VMEM((2, t, D_MODEL), _F32), pltpu.SemaphoreType.DMA(())],
        compiler_params=_params(("arbitrary",)),
    )(*pieces, df, wpt, dr, *([x] * nx), head, g)


def _proj_bwd_w(h, pieces, name):
    lp = h.shape[0]
    t = _row_tile(lp)
    n = len(pieces)
    widths = [p.shape[1] for p in pieces]

    def body(*refs):
        h_ref, d_refs, w_refs = refs[0], refs[1:1 + n], refs[1 + n:]
        i = pl.program_id(0)

        @pl.when(i == 0)
        def _():
            for w_ref in w_refs:
                w_ref[...] = jnp.zeros_like(w_ref)

        hb = h_ref[...].astype(_MM)
        for d_ref, w_ref in zip(d_refs, w_refs):
            w_ref[...] += _dot_t0(d_ref[...].astype(_MM), hb)

    row = lambda w: pl.BlockSpec((t, w), lambda i: (i, 0))
    return pl.pallas_call(
        body, name=name, grid=(lp // t,),
        in_specs=[row(D_MODEL)] + [row(w) for w in widths],
        out_specs=[_full((w, D_MODEL)) for w in widths],
        out_shape=[jax.ShapeDtypeStruct((w, D_MODEL), _F32) for w in widths],
        compiler_params=_params(("arbitrary",)),
    )(h, *pieces)


def _adamw_math(w, g, m, v):
    m = ADAM_B1 * m + (1.0 - ADAM_B1) * g
    v = ADAM_B2 * v + (1.0 - ADAM_B2) * (g * g)
    m_hat = m / (1.0 - ADAM_B1 ** ADAM_STEP)
    v_hat = v / (1.0 - ADAM_B2 ** ADAM_STEP)
    delta = -ADAM_LR * (m_hat / (jnp.sqrt(v_hat) + ADAM_EPS) + ADAM_WD * w)
    return delta, m, v


def _adamw_tiled(w, g, m, v, rows, name):
    r, c = w.shape

    def body(w_ref, g_ref, m_ref, v_ref, go_ref, d_ref, nm_ref, nv_ref):
        g = g_ref[...]
        go_ref[...] = g
        d_ref[...], nm_ref[...], nv_ref[...] = _adamw_math(w_ref[...], g, m_ref[...], v_ref[...])

    spec = pl.BlockSpec((rows, c), lambda i: (i, 0))
    return pl.pallas_call(
        body, name=name, grid=(pl.cdiv(r, rows),), in_specs=[spec] * 4, out_specs=[spec] * 4,
        out_shape=[jax.ShapeDtypeStruct((r, c), _F32)] * 4,
        compiler_params=_params(("arbitrary",)),
    )(w, g, m, v)


def _adamw_small(ws, gs, ms, vs):
    n = len(ws)

    def body(*refs):
        ins, outs = refs[:4 * n], refs[4 * n:]
        for k in range(n):
            d, m, v = _adamw_math(ins[k][...], ins[n + k][...], ins[2 * n + k][...], ins[3 * n + k][...])
            outs[k][...], outs[n + k][...], outs[2 * n + k][...] = d, m, v

    shapes = [jax.ShapeDtypeStruct(w.shape, _F32) for w in ws]
    res = pl.pallas_call(body, name="adamw_small", out_shape=shapes * 3, compiler_params=_params())(*ws, *gs, *ms, *vs)
    return res[:n], res[n:2 * n], res[2 * n:]


def _place():
    x, y, c = lax.axis_index("x"), lax.axis_index("y"), lax.axis_index("c")
    chips = [(1 - x, y), (x, 1 - y), (1 - x, 1 - y)]
    return x, y, c, chips


def _half(ref, slot, c, rows):
    hr = rows // 2
    return ref.at[slot, pl.ds(pl.multiple_of(c * hr, 8), hr), :]


def _all_gather(shards):
    n = len(shards)

    def body(*refs):
        ins, outs = refs[:n], refs[n:2 * n]
        send, recv = refs[2 * n:]
        x, y, c, chips = _place()
        me = 2 * x + y
        sibling = (x, y, 1 - c)

        def copy(k, t, slot, half, to, src=None):
            rows = ins[t].shape[0]
            dst = _half(outs[t], slot, half, rows)
            return pltpu.make_async_remote_copy(
                src_ref=dst if src is None else src, dst_ref=dst,
                send_sem=send.at[k], recv_sem=recv.at[k], device_id=to, device_id_type=MESH)

        first, passed = [], []
        for t in range(n):
            hr = ins[t].shape[0] // 2
            mine = ins[t].at[pl.ds(pl.multiple_of(c * hr, 8), hr), :]
            for j, chip in enumerate(chips):
                first.append(copy(6 * t + j, t, me, c, (*chip, c), src=mine))
        for cp in first:
            cp.start()
        for t in range(n):
            for j, (cx, cy) in enumerate(chips):
                copy(6 * t + j, t, 2 * cx + cy, c, sibling).wait_recv()
                fwd = copy(6 * t + 3 + j, t, 2 * cx + cy, c, sibling)
                fwd.start()
                passed.append(fwd)
        for t in range(n):
            for j, (cx, cy) in enumerate(chips):
                copy(6 * t + 3 + j, t, 2 * cx + cy, 1 - c, sibling).wait_recv()
        for cp in first + passed:
            cp.wait_send()

    any_spec = pl.BlockSpec(memory_space=pl.ANY)
    return pl.pallas_call(
        body, name="all_gather",
        in_specs=[any_spec] * n, out_specs=[any_spec] * n,
        out_shape=[jax.ShapeDtypeStruct((N_SHARD,) + s.shape, s.dtype) for s in shards],
        scratch_shapes=[pltpu.SemaphoreType.DMA((6 * n,)), pltpu.SemaphoreType.DMA((6 * n,))],
    )(*shards)


def _own_slot(gathered, own):
    me = 2 * lax.axis_index("x") + lax.axis_index("y")
    slot = lax.broadcasted_iota(jnp.int32, (N_SHARD,) + (1,) * own.ndim, 0)
    return jnp.where(slot == me, own[None], gathered)


def _sibling_exchange(bufs):
    n = len(bufs)

    def body(*refs):
        ins, got = refs[:n], refs[n:2 * n]
        send, recv = refs[2 * n:]
        x, y, c, _ = _place()
        cps = []
        for t in range(n):
            hr = ins[t].shape[0] // (2 * N_SHARD)
            for s in range(N_SHARD):
                k = N_SHARD * t + s
                cps.append(pltpu.make_async_remote_copy(
                    src_ref=ins[t].at[pl.ds(pl.multiple_of((2 * s + 1 - c) * hr, 8), hr), :], dst_ref=got[t].at[s],
                    send_sem=send.at[k], recv_sem=recv.at[k], device_id=(x, y, 1 - c), device_id_type=MESH))
        for cp in cps:
            cp.start()
        for cp in cps:
            cp.wait()

    any_spec = pl.BlockSpec(memory_space=pl.ANY)
    return pl.pallas_call(
        body, name="sibling_exchange",
        in_specs=[any_spec] * n, out_specs=[any_spec] * n,
        out_shape=[jax.ShapeDtypeStruct((N_SHARD, b.shape[0] // (2 * N_SHARD), b.shape[1]), b.dtype) for b in bufs],
        scratch_shapes=[pltpu.SemaphoreType.DMA((N_SHARD * n,))] * 2,
    )(*bufs)


def _half_sum(bufs, got, wire, c):
    n = len(bufs)

    def body(c_ref, *refs):
        ins, gots, outs = refs[:n], refs[n:2 * n], refs[2 * n:]
        for t in range(n):
            outs[t][0] = (ins[t][...] + gots[t][0]).astype(outs[t].dtype)

    slot = lambda g: pl.BlockSpec((1,) + tuple(g.shape[1:]), lambda s, c_ref: (s, 0, 0))
    grid_spec = pltpu.PrefetchScalarGridSpec(
        num_scalar_prefetch=1, grid=(N_SHARD,),
        in_specs=[pl.BlockSpec(tuple(g.shape[1:]), lambda s, c_ref: (2 * s + c_ref[0], 0)) for g in got]
                 + [slot(g) for g in got],
        out_specs=[slot(g) for g in got])
    return pl.pallas_call(
        body, name="half_sum", grid_spec=grid_spec,
        out_shape=[jax.ShapeDtypeStruct(g.shape, d) for g, d in zip(got, wire)],
        compiler_params=_params(("arbitrary",)),
    )(c, *bufs, *got)


def _chip_exchange(parts):
    n = len(parts)

    def body(*refs):
        ins, outs = refs[:n], refs[n:2 * n]
        send, recv = refs[2 * n:]
        x, y, c, chips = _place()
        me = 2 * x + y
        cps = []
        for t in range(n):
            for j, (cx, cy) in enumerate(chips):
                cps.append(pltpu.make_async_remote_copy(
                    src_ref=ins[t].at[2 * cx + cy], dst_ref=outs[t].at[me],
                    send_sem=send.at[3 * t + j], recv_sem=recv.at[3 * t + j],
                    device_id=(cx, cy, c), device_id_type=MESH))
        for cp in cps:
            cp.start()
        for cp in cps:
            cp.wait()

    any_spec = pl.BlockSpec(memory_space=pl.ANY)
    return pl.pallas_call(
        body, name="chip_exchange",
        in_specs=[any_spec] * n, out_specs=[any_spec] * n,
        out_shape=[jax.ShapeDtypeStruct(p.shape, p.dtype) for p in parts],
        scratch_shapes=[pltpu.SemaphoreType.DMA((3 * n,))] * 2,
    )(*parts)


def _chip_sum(parts, got, me):
    n = len(got)

    def body(me_ref, *refs):
        owns, gots, outs = refs[:n], refs[n:2 * n], refs[2 * n:]
        s = pl.program_id(0)
        for t in range(n):
            term = jnp.where(s == me_ref[0], owns[t][0], gots[t][0]).astype(_F32)

            @pl.when(s == 0)
            def _():
                outs[t][...] = term

            @pl.when(s > 0)
            def _():
                outs[t][...] += term

    blk = lambda g: (1,) + tuple(g.shape[1:])
    grid_spec = pltpu.PrefetchScalarGridSpec(
        num_scalar_prefetch=1, grid=(N_SHARD,),
        in_specs=[pl.BlockSpec(blk(g), lambda s, me_ref: (me_ref[0], 0, 0)) for g in got]
                 + [pl.BlockSpec(blk(g), lambda s, me_ref: (jnp.where(s == me_ref[0], (s + 1) % N_SHARD, s), 0, 0))
                    for g in got],
        out_specs=[pl.BlockSpec(tuple(g.shape[1:]), lambda s, me_ref: (0, 0)) for g in got])
    return pl.pallas_call(
        body, name="chip_sum", grid_spec=grid_spec,
        out_shape=[jax.ShapeDtypeStruct(g.shape[1:], _F32) for g in got],
        compiler_params=_params(("arbitrary",)),
    )(me, *parts, *got)


def _sibling_share(halves):
    n = len(halves)

    def body(*refs):
        ins, outs = refs[:n], refs[n:2 * n]
        send, recv = refs[2 * n:]
        x, y, c, _ = _place()
        cps = [pltpu.make_async_remote_copy(
            src_ref=ins[t], dst_ref=outs[t], send_sem=send.at[t], recv_sem=recv.at[t],
            device_id=(x, y, 1 - c), device_id_type=MESH) for t in range(n)]
        for cp in cps:
            cp.start()
        for cp in cps:
            cp.wait()

    any_spec = pl.BlockSpec(memory_space=pl.ANY)
    return pl.pallas_call(
        body, name="sibling_share",
        in_specs=[any_spec] * n, out_specs=[any_spec] * n,
        out_shape=[jax.ShapeDtypeStruct(h.shape, h.dtype) for h in halves],
        scratch_shapes=[pltpu.SemaphoreType.DMA((n,))] * 2,
    )(*halves)


def _reduce_scatter(bufs, wire):
    x, y, c = lax.axis_index("x"), lax.axis_index("y"), lax.axis_index("c")
    as_operand = lambda i: jnp.asarray(i, jnp.int32).reshape(1)
    chip_part = _half_sum(bufs, _sibling_exchange(bufs), wire, as_operand(c))
    half = _chip_sum(chip_part, _chip_exchange(chip_part), as_operand(2 * x + y))
    other = _sibling_share(half)
    return [jnp.concatenate([jnp.where(c == 0, h, o), jnp.where(c == 0, o, h)], axis=0) for h, o in zip(half, other)]


REPL_SIZES = (D_MODEL, D_MODEL, N_HEADS, D_CONV, D_CONV, D_CONV, D_MODEL, D_MODEL)
REPL_ROWS = 48


def _pack_repl(parts):
    flat = jnp.concatenate([p.reshape(-1).astype(_F32) for p in parts])
    return jnp.pad(flat, (0, REPL_ROWS * 128 - flat.shape[0])).reshape(REPL_ROWS, 128)


def _unpack_repl(packed, shapes):
    flat = packed.reshape(-1)
    out, off = [], 0
    for shape, size in zip(shapes, REPL_SIZES):
        out.append(flat[off:off + size].reshape(shape))
        off += size
    return out


def _projection_operand(w_in_t):
    zrow = jnp.zeros((128 - N_HEADS, D_MODEL), w_in_t.dtype)
    return jnp.concatenate([w_in_t[:OFF_F], w_in_t[OFF_F + N_HEADS:], w_in_t[OFF_F:OFF_F + N_HEADS], zrow], axis=0)


def _local_grads(x, target, meta, ln_in_g, ln_in_b, w_in_t, b_f, conv_w, conv_b, ln_conv_g, ln_conv_b,
                 w_pw, w_out, ln_out_g, ln_out_b):
    vec = lambda a: a.reshape(1, -1).astype(_F32)
    head = jnp.concatenate([jnp.zeros((ROW_PAD, D_MODEL), _F32), meta], axis=0)
    wpt = _projection_operand(w_in_t)
    bf = jnp.pad(vec(b_f), ((0, 0), (0, 128 - N_HEADS)))
    cw = jnp.pad(conv_w, ((0, 32 - CONV_WIDTH), (0, 0)))

    h, q, k, v, ga, u, ug, gc, f, vt = _ln_proj(x, head, vec(ln_in_g), vec(ln_in_b), wpt)
    ccol = _decay_cumsum(f, bf)
    o, lse = _attn_fwd(q, k, vt, ccol)
    z, pw = _conv_fwd(u, ug, cw, vec(conv_b), vec(ln_conv_g), vec(ln_conv_b), w_pw)
    dr, do, dga, dpw, dgc, delta, dw_out, dg_out, db_out, loss = _out_loss_bwd(
        o, ga, pw, gc, h, target, w_out, vec(ln_out_g), vec(ln_out_b))
    dq, dk, dv, dck, dcq = _attn_bwd(q, k, v, do, lse, delta, ccol)
    df, dbf = _decay_bwd(dck, dcq, f, bf)
    dz, dw_pw, dg_conv, db_conv, dconv_b = _conv_bwd_rows(dpw, z, vec(ln_conv_g), vec(ln_conv_b), w_pw)
    du, dug, dcw = _conv_bwd_taps(dz, u, ug, cw)
    pieces = (dq, dk, dv, dga, du, dug, dgc)
    grad_x, dmeta, dg_in, db_in = _proj_bwd_x(pieces, df, wpt, dr, x, head, vec(ln_in_g))
    dwq, dwk, dwv, dwga = _proj_bwd_w(h, pieces[:4], "proj_bwd_w_attn")
    dwu, dwug, dwgc, dwf = _proj_bwd_w(h, pieces[4:] + (df,), "proj_bwd_w_conv")
    dw_in_t = jnp.concatenate([dwq, dwk, dwv, dwf[:N_HEADS], dwga, dwu, dwug, dwgc], axis=0)
    dw_in_t = jnp.pad(dw_in_t.reshape(N_SHARD, W_IN_SHARD, D_MODEL), ((0, 0), (0, W_IN_SLOT - W_IN_SHARD), (0, 0)))
    dw_in_t = dw_in_t.reshape(N_SHARD * W_IN_SLOT, D_MODEL)
    grads = dict(meta=dmeta, ln_in_g=dg_in, ln_in_b=db_in, w_in_t=dw_in_t, b_f=dbf[:, :N_HEADS],
                 conv_w=dcw[:CONV_WIDTH], conv_b=dconv_b, ln_conv_g=dg_conv, ln_conv_b=db_conv,
                 w_pw=dw_pw, w_out=dw_out, ln_out_g=dg_out, ln_out_b=db_out)
    return loss, grad_x, grads


def kernel(x, meta, ln_in_g, ln_in_b, w_in, b_f, conv_w, conv_b, ln_conv_g, ln_conv_b, w_pw, w_out, ln_out_g, ln_out_b, loss_target, m_meta, m_ln_in_g, m_ln_in_b, m_w_in, m_b_f, m_conv_w, m_conv_b, m_ln_conv_g, m_ln_conv_b, m_w_pw, m_w_out, m_ln_out_g, m_ln_out_b, v_meta, v_ln_in_g, v_ln_in_b, v_w_in, v_b_f, v_conv_w, v_conv_b, v_ln_conv_g, v_ln_conv_b, v_w_pw, v_w_out, v_ln_out_g, v_ln_out_b):
    conv_w2, w_pw2, w_out2 = conv_w[0], w_pw[0], w_out[0]
    w_t = jnp.transpose(w_in[0])

    small = jnp.concatenate([jnp.pad(conv_w2, ((0, 1), (0, 0))), meta.reshape(32, 128)], axis=0)
    shards = [jnp.pad(w_t.astype(_MM), ((0, W_IN_SLOT - W_IN_SHARD), (0, 0))), w_pw2.astype(_MM), w_out2.astype(_MM), small]
    g_in, g_pw, g_out, g_small = (_own_slot(g, s) for g, s in zip(_all_gather(shards), shards))
    w_in_t_full = g_in[:, :W_IN_SHARD, :].reshape(D_IN, D_MODEL)
    w_pw_full = g_pw.reshape(D_CONV, D_CONV)
    w_out_full = g_out.reshape(D_MODEL, D_MODEL)
    conv_w_full = jnp.transpose(g_small[:, :CONV_WIDTH, :], (1, 0, 2)).reshape(CONV_WIDTH, D_CONV)
    meta_full = jnp.transpose(g_small[:, 32:64, :].reshape(N_SHARD, N_META, 256), (1, 0, 2)).reshape(N_META, D_MODEL)

    loss, grad_x, gr = _local_grads(x[0], loss_target[0], meta_full, ln_in_g, ln_in_b, w_in_t_full, b_f, conv_w_full,
                                    conv_b, ln_conv_g, ln_conv_b, w_pw_full, w_out_full, ln_out_g, ln_out_b)
    grad_x = grad_x[None]

    repl_names = ("ln_in_g", "ln_in_b", "b_f", "conv_b", "ln_conv_g", "ln_conv_b", "ln_out_g", "ln_out_b")
    repl = _pack_repl([gr[n] for n in repl_names] + [loss])
    gcw = jnp.transpose(jnp.pad(gr["conv_w"], ((0, 1), (0, 0))).reshape(32, N_SHARD, 128), (1, 0, 2))
    gmeta = jnp.transpose(gr["meta"].reshape(N_META, N_SHARD, 256), (1, 0, 2)).reshape(N_SHARD, 32, 128)
    small_g = jnp.concatenate([gcw, gmeta, jnp.broadcast_to(repl[None], (N_SHARD, REPL_ROWS, 128))], axis=1)
    small_rows = small_g.shape[1]
    bufs = [gr["w_in_t"], gr["w_pw"], gr["w_out"], small_g.reshape(N_SHARD * small_rows, 128)]
    gw_slot, gw_pw, gw_out, gsmall = _reduce_scatter(bufs, [_MM, _MM, _MM, _F32])
    g_conv_w = gsmall[:CONV_WIDTH]
    g_meta = gsmall[32:64].reshape(N_META, 256)
    g_repl = gsmall[64:64 + REPL_ROWS]

    repl_w = (ln_in_g, ln_in_b, b_f, conv_b, ln_conv_g, ln_conv_b, ln_out_g, ln_out_b)
    repl_m = (m_ln_in_g, m_ln_in_b, m_b_f, m_conv_b, m_ln_conv_g, m_ln_conv_b, m_ln_out_g, m_ln_out_b)
    repl_v = (v_ln_in_g, v_ln_in_b, v_b_f, v_conv_b, v_ln_conv_g, v_ln_conv_b, v_ln_out_g, v_ln_out_b)
    zero = jnp.zeros((1,), _F32)
    res_t = _adamw_tiled(w_t, gw_slot, jnp.transpose(m_w_in[0]), jnp.transpose(v_w_in[0]), 256, "adamw_w_in")
    gw_in, d_in, nm_in, nv_in = (jnp.transpose(a) for a in res_t)
    ws = [w_pw2, w_out2, conv_w2, meta, _pack_repl(repl_w + (zero,))]
    gs = [gw_pw, gw_out, g_conv_w, g_meta, g_repl]
    ms = [m_w_pw[0], m_w_out[0], m_conv_w[0], m_meta, _pack_repl(repl_m + (zero,))]
    vs = [v_w_pw[0], v_w_out[0], v_conv_w[0], v_meta, _pack_repl(repl_v + (zero + 1.0,))]
    ds, nms, nvs = _adamw_small(ws, gs, ms, vs)

    shapes = [w.shape for w in repl_w]
    total_loss = g_repl.reshape(-1)[sum(REPL_SIZES)]
    names = ("meta", "ln_in_g", "ln_in_b", "w_in", "b_f", "conv_w", "conv_b", "ln_conv_g", "ln_conv_b",
             "w_pw", "w_out", "ln_out_g", "ln_out_b")

    def assemble(t_in, smalls):
        pw_, out_, cw_, meta_, packed = smalls
        r = dict(zip(repl_names, _unpack_repl(packed, shapes)))
        r.update(meta=meta_, w_in=t_in[None], conv_w=cw_[None], w_pw=pw_[None], w_out=out_[None])
        return [r[n] for n in names]

    return (total_loss, grad_x, *assemble(gw_in, gs), *assemble(d_in, ds), *assemble(nm_in, nms),
            *assemble(nv_in, nvs))
```

```python
import functools

import jax
import jax.numpy as jnp
from jax import lax
from jax.experimental import pallas as pl
from jax.experimental.pallas import tpu as pltpu

D_MODEL = 1024
N_META = 16
D_ATTN = 512
D_CONV = 512
N_HEADS = 8
HEAD_DIM = 64
N_PAIRS = 4
CONV_WIDTH = 31
LN_EPS = 1e-5
ALPHA = 2.0 ** 0.25
ATTN_SCALE = 0.125
D_IN = 3592
OFF_F = 3 * D_ATTN
D_IN_PAD = 3712
N_SHARD = 4
W_IN_SHARD = D_IN // N_SHARD
W_IN_SLOT = 928

ROW_PAD = 112
ROW_X = ROW_PAD + N_META
HALO = 32
MASKED = 1e30

ADAM_LR = 0.001
ADAM_B1 = 0.9
ADAM_B2 = 0.999
ADAM_EPS = 1e-08
ADAM_WD = 0.01
ADAM_STEP = 10

VMEM_LIMIT = 56 * 1024 * 1024

_MM = jnp.bfloat16
_F32 = jnp.float32
MESH = pl.DeviceIdType.MESH


def _params(sem=None):
    return pltpu.CompilerParams(dimension_semantics=sem, vmem_limit_bytes=VMEM_LIMIT)


def _row_tile(n_rows):
    for t in (384, 256, 128):
        if n_rows % t == 0:
            return t
    raise ValueError(f"padded sequence length {n_rows} is not a multiple of 128")


def _sigmoid(x):
    return 1.0 / (1.0 + jnp.exp(-x))


def _ln_stats(x):
    mu = jnp.mean(x, axis=-1, keepdims=True)
    xc = x - mu
    var = jnp.mean(xc * xc, axis=-1, keepdims=True)
    rstd = lax.rsqrt(var + LN_EPS)
    return xc * rstd, rstd


def _ln_bwd(dy, xhat, rstd, g):
    dxh = dy * g
    m1 = jnp.mean(dxh, axis=-1, keepdims=True)
    m2 = jnp.mean(dxh * xhat, axis=-1, keepdims=True)
    return rstd * (dxh - m1 - xhat * m2)


def _dot(a, b):
    return jnp.dot(a, b, preferred_element_type=_F32)


def _dot_t0(a, b):
    return lax.dot_general(a, b, (((0,), (0,)), ((), ())), preferred_element_type=_F32)


def _dot_t1(a, b):
    return lax.dot_general(a, b, (((1,), (1,)), ((), ())), preferred_element_type=_F32)


def _dot_exact(a, b):
    return jnp.dot(a, b, preferred_element_type=_F32, precision=lax.Precision.HIGHEST)


def _full(shape):
    return pl.BlockSpec(shape, lambda *_: (0,) * len(shape))


def _x_specs(t):
    n = t // ROW_X
    return [pl.BlockSpec((ROW_X, D_MODEL), lambda i, s=s: (jnp.maximum(i * n + s - 1, 0), 0)) for s in range(n)]


def _x_tile(i, refs, head_ref=None):
    first = refs[0][...]
    if head_ref is not None:
        first = jnp.where(i == 0, head_ref[...], first)
    return jnp.concatenate([first] + [r[...] for r in refs[1:]], axis=0)


def _ln_proj(x, head, g, b, wpt):
    lp = ROW_X + x.shape[0]
    t = _row_tile(lp)
    nx = t // ROW_X

    def body(*refs):
        x_refs, (head_ref, g_ref, b_ref, wt_ref) = refs[:nx], refs[nx:nx + 4]
        h_ref, q_ref, k_ref, v_ref, ga_ref, u_ref, ug_ref, gc_ref, f_ref, vt_ref = refs[nx + 4:]
        i = pl.program_id(0)
        xhat, _ = _ln_stats(_x_tile(i, x_refs, head_ref))
        rows = i * t + lax.broadcasted_iota(jnp.int32, (t, 1), 0)
        h = jnp.where(rows >= ROW_PAD, xhat * g_ref[...] + b_ref[...], 0.0)
        h_ref[...] = h
        hb = h.astype(_MM)
        q_ref[...] = (_dot_t1(hb, wt_ref[0:512, :]) * ATTN_SCALE).astype(q_ref.dtype)
        k_ref[...] = _dot_t1(hb, wt_ref[512:1024, :]).astype(k_ref.dtype)
        v = _dot_t1(hb, wt_ref[1024:1536, :])
        v_ref[...] = v.astype(v_ref.dtype)
        vt_ref[...] = v.T.astype(vt_ref.dtype)
        ga_ref[...] = _dot_t1(hb, wt_ref[1536:2048, :])
        u_ref[...] = _dot_t1(hb, wt_ref[2048:2560, :])
        ug_ref[...] = _dot_t1(hb, wt_ref[2560:3072, :])
        gc_ref[...] = _dot_t1(hb, wt_ref[3072:3584, :])
        f_ref[...] = _dot_t1(hb, wt_ref[3584:3712, :])

    row = lambda w: pl.BlockSpec((t, w), lambda i: (i, 0))
    f32 = lambda w: jax.ShapeDtypeStruct((lp, w), _F32)
    mm = lambda w: jax.ShapeDtypeStruct((lp, w), _MM)
    return pl.pallas_call(
        body, name="ln_proj", grid=(lp // t,),
        in_specs=_x_specs(t) + [_full((ROW_X, D_MODEL)), _full((1, D_MODEL)), _full((1, D_MODEL)),
                                _full((D_IN_PAD, D_MODEL))],
        out_specs=[row(D_MODEL)] + [row(512)] * 7 + [row(128), pl.BlockSpec((512, t), lambda i: (0, i))],
        out_shape=[f32(D_MODEL), mm(512), mm(512), mm(512), f32(512), f32(512), f32(512), f32(512), f32(128),
                   jax.ShapeDtypeStruct((512, lp), _MM)],
        compiler_params=_params(("arbitrary",)),
    )(*([x] * nx), head, g, b, wpt)


def _log_sigmoid(z):
    return jnp.minimum(z, 0.0) - jnp.log(1.0 + jnp.exp(-jnp.abs(z)))


def _decay_cumsum(f, bf):
    lp = f.shape[0]
    nb = lp // 128

    def body(f_ref, bf_ref, ccol_ref):
        r = lax.broadcasted_iota(jnp.int32, (128, 128), 0)
        c = lax.broadcasted_iota(jnp.int32, (128, 128), 1)
        tril = (c <= r).astype(_F32)

        def step(bi, carry):
            rows = pl.ds(pl.multiple_of(bi * 128, 128), 128)
            lf = _log_sigmoid(f_ref[rows, :] + bf_ref[...])
            cb = _dot_exact(tril, lf) + carry
            for p in range(N_PAIRS):
                ccol_ref[p, rows, :] = cb if p == 0 else pltpu.roll(cb, 128 - 2 * p, axis=1)
            return cb[127:128, :]

        lax.fori_loop(0, nb, step, jnp.zeros((1, 128), _F32))

    return pl.pallas_call(
        body, name="decay_cumsum",
        out_shape=jax.ShapeDtypeStruct((N_PAIRS, lp, 128), _F32),
        compiler_params=_params(),
    )(f, bf)


def _split3(x):
    hi = x.astype(_MM)
    r1 = x - hi.astype(_F32)
    mid = r1.astype(_MM)
    return hi, mid, (r1 - mid.astype(_F32)).astype(_MM)


def _augment(x, hd, first, second):
    lane = lax.broadcasted_iota(jnp.int32, (1, 128), 1)
    base = HEAD_DIM * (1 - hd)
    out = jnp.where(lane // HEAD_DIM == hd, x, jnp.zeros_like(x))
    for slot, val in enumerate((first, second)):
        lo = base + 3 * slot
        if val is None:
            continue
        if isinstance(val, float):
            out = jnp.where((lane >= lo) & (lane < lo + 3), jnp.asarray(val, x.dtype), out)
        else:
            for n, piece in enumerate(_split3(val)):
                out = jnp.where(lane == lo + n, piece, out)
    return out


def _transpose_mm(x):
    return x.astype(_F32).T.astype(_MM)


def _diag_valid(i, t):
    sub = lax.broadcasted_iota(jnp.int32, (t, t), 0)
    return (sub <= lax.broadcasted_iota(jnp.int32, (t, t), 1)) & (i * t + sub >= ROW_PAD)


def _key_operand(k, c, hd, first_row):
    pos = first_row + lax.broadcasted_iota(jnp.int32, (k.shape[0], 1), 0)
    return _augment(k, hd, jnp.where(pos >= ROW_PAD, -c, -MASKED), 1.0)


def _attn_fwd(q, k, vt, ccol):
    lp = q.shape[0]
    t = _row_tile(lp)
    nb = lp // t

    def body(q_ref, k_ref, vt_ref, ccol_ref, o_ref, lse_ref, ka_ref):
        i = pl.program_id(1)

        @pl.when(i == 0)
        def _():
            for hd in range(2):
                for n in range(nb):
                    rows = slice(n * t, (n + 1) * t)
                    ka_ref[hd, rows, :] = _key_operand(k_ref[rows, :], ccol_ref[0, rows, hd:hd + 1], hd, n * t)

        qats = [_transpose_mm(_augment(q_ref[...], hd, 1.0, None)) for hd in range(2)]

        def scores(j):
            keys = pl.ds(pl.multiple_of(j * t, 128), t)
            return tuple(_dot(ka_ref[hd, keys, :], qats[hd]) for hd in range(2))

        def update(j, sts, states, diag):
            vt = vt_ref[:, pl.ds(pl.multiple_of(j * t, 128), t)]
            new = []
            for st, (m, l, acc) in zip(sts, states):
                if diag:
                    st = jnp.where(_diag_valid(i, t), st, -MASKED)
                m_new = jnp.maximum(m, jnp.max(st, axis=0, keepdims=True))
                a = jnp.exp(m - m_new)
                pt = jnp.exp(st - m_new)
                l = a * l + jnp.sum(pt, axis=0, keepdims=True)
                new.append((m_new, l, a * acc + _dot(vt, pt.astype(_MM))))
            return tuple(new)

        def step(j, carry):
            sts, states = carry
            return scores(j + 1), update(j, sts, states, False)

        init = (jnp.full((1, t), -MASKED, _F32), jnp.zeros((1, t), _F32), jnp.zeros((128, t), _F32))
        sts, states = lax.fori_loop(0, i, step, (scores(0), (init, init)))
        (m0, l0, acc0), (m1, l1, acc1) = update(i, sts, states, True)
        row = lax.broadcasted_iota(jnp.int32, (128, t), 0)
        o_ref[...] = jnp.where(row < HEAD_DIM, acc0 / l0, acc1 / l1).T
        lse_ref[0] = jnp.where(row == 0, m0 + jnp.log(l0), jnp.where(row == 1, m1 + jnp.log(l1), 0.0)).T

    return pl.pallas_call(
        body, name="attn_fwd", grid=(N_PAIRS, nb),
        in_specs=[pl.BlockSpec((t, 128), lambda p, i: (i, p)),
                  pl.BlockSpec((lp, 128), lambda p, i: (0, p)),
                  pl.BlockSpec((128, lp), lambda p, i: (p, 0)),
                  pl.BlockSpec((1, lp, 128), lambda p, i: (p, 0, 0))],
        out_specs=[pl.BlockSpec((t, 128), lambda p, i: (i, p)),
                   pl.BlockSpec((1, t, 128), lambda p, i: (p, i, 0))],
        out_shape=[jax.ShapeDtypeStruct((lp, D_ATTN), _F32), jax.ShapeDtypeStruct((N_PAIRS, lp, 128), _F32)],
        scratch_shapes=[pltpu.VMEM((2, lp, 128), _MM)],
        compiler_params=_params(("arbitrary", "arbitrary")),
    )(q, k, vt, ccol)


CONV_CHUNK = 32


def _shifted_windows(win, sh, rows):
    sh[0] = win[...]
    for b in range(1, 8):
        sh[b, 0:rows - 8, :] = win[b:b + rows - 8, :]


def _tap_rows(sh, offsets, r0):
    out = {}
    for b in range(8):
        ds = [d for d in offsets if d % 8 == b]
        if ds:
            lo, hi = min(ds) // 8, max(ds) // 8
            rows = sh[b, pl.ds(pl.multiple_of(r0 + 8 * lo, 8), CONV_CHUNK + 8 * (hi - lo)), :]
            for d in ds:
                out[d] = rows[8 * (d // 8 - lo):8 * (d // 8 - lo) + CONV_CHUNK, :]
    return out


def _conv_fwd(u, ug, cw, cb, g, b, wpw):
    lp = u.shape[0]
    t = _row_tile(lp)

    def body(u_ref, ug_ref, cw_ref, cb_ref, g_ref, b_ref, wpw_ref, z_ref, pw_ref, win, sh):
        i = pl.program_id(0)

        @pl.when(i == 0)
        def _():
            win[0:HALO, :] = jnp.zeros((HALO, D_CONV), _F32)

        win[HALO:HALO + t, :] = u_ref[...] * _sigmoid(ug_ref[...])
        _shifted_windows(win, sh, t + HALO)

        def chunk(c, carry):
            r0 = c * CONV_CHUNK
            acc = jnp.broadcast_to(cb_ref[...], (CONV_CHUNK, D_CONV))
            first = HALO - (CONV_WIDTH - 1)
            taps = _tap_rows(sh, range(first, first + CONV_WIDTH), r0)
            for j in range(CONV_WIDTH):
                acc = acc + cw_ref[j:j + 1, :] * taps[first + j]
            z_ref[pl.ds(pl.multiple_of(r0, 8), CONV_CHUNK), :] = acc
            return carry

        lax.fori_loop(0, t // CONV_CHUNK, chunk, 0)
        xhat, _ = _ln_stats(z_ref[...])
        zn = xhat * g_ref[...] + b_ref[...]
        a = zn * _sigmoid(zn)
        pw_ref[...] = _dot(a.astype(_MM), wpw_ref[...])
        win[0:HALO, :] = win[t:t + HALO, :]

    row = pl.BlockSpec((t, D_CONV), lambda i: (i, 0))
    vec = _full((1, D_CONV))
    return pl.pallas_call(
        body, name="conv_fwd", grid=(lp // t,),
        in_specs=[row, row, _full((32, D_CONV)), vec, vec, vec, _full((D_CONV, D_CONV))],
        out_specs=[row, row],
        out_shape=[jax.ShapeDtypeStruct((lp, D_CONV), _F32)] * 2,
        scratch_shapes=[pltpu.VMEM((t + HALO, D_CONV), _F32), pltpu.VMEM((8, t + HALO, D_CONV), _F32)],
        compiler_params=_params(("arbitrary",)),
    )(u, ug, cw, cb, g, b, wpw)


def _out_loss_bwd(o, ga, pw, gc, h, target, w_out, g, b):
    lp = o.shape[0]
    t = _row_tile(lp)
    nx = t // ROW_X

    def body(*refs):
        o_ref, ga_ref, pw_ref, gc_ref, h_ref = refs[:5]
        tgt_refs, (w_ref, g_ref, b_ref) = refs[5:5 + nx], refs[5 + nx:8 + nx]
        dr_ref, do_ref, dga_ref, dpw_ref, dgc_ref, delta_ref, dw_ref, dg_ref, db_ref, loss_ref = refs[8 + nx:]
        i = pl.program_id(0)

        @pl.when(i == 0)
        def _():
            dw_ref[...] = jnp.zeros_like(dw_ref)
            dg_ref[...] = jnp.zeros_like(dg_ref)
            db_ref[...] = jnp.zeros_like(db_ref)
            loss_ref[...] = jnp.zeros_like(loss_ref)

        o, ga, pw, gc = o_ref[...], ga_ref[...], pw_ref[...], gc_ref[...]
        sa, sc = _sigmoid(ga), _sigmoid(gc)
        silu_a, silu_c = ga * sa, gc * sc
        ycat = jnp.concatenate([o * silu_a, pw * silu_c], axis=-1).astype(_MM)
        r = ALPHA * h_ref[...] + _dot(ycat, w_ref[...])
        xhat, rstd = _ln_stats(r)
        out = xhat * g_ref[...] + b_ref[...]
        rows = i * t + lax.broadcasted_iota(jnp.int32, (t, 1), 0)
        diff = jnp.where(rows >= ROW_X, out - _x_tile(i, tgt_refs), 0.0)
        loss_ref[...] += (0.5 / D_MODEL) * jnp.sum(diff * diff, keepdims=True)
        dout = diff * (1.0 / D_MODEL)
        dg_ref[...] += jnp.sum(dout * xhat, axis=0, keepdims=True)
        db_ref[...] += jnp.sum(dout, axis=0, keepdims=True)
        dr = _ln_bwd(dout, xhat, rstd, g_ref[...])
        dr_ref[...] = dr
        drb = dr.astype(_MM)
        dw_ref[...] += _dot_t0(ycat, drb)
        dy = _dot_t1(drb, w_ref[...])
        dya, dyc = dy[:, :D_ATTN], dy[:, D_ATTN:]
        do = dya * silu_a
        do_ref[...] = do.astype(do_ref.dtype)
        dga_ref[...] = (dya * o * (sa * (1.0 + ga * (1.0 - sa)))).astype(dga_ref.dtype)
        dpw_ref[...] = (dyc * silu_c).astype(dpw_ref.dtype)
        dgc_ref[...] = (dyc * pw * (sc * (1.0 + gc * (1.0 - sc)))).astype(dgc_ref.dtype)
        sel = (lax.broadcasted_iota(jnp.int32, (128, 128), 0) // HEAD_DIM
               == lax.broadcasted_iota(jnp.int32, (128, 128), 1)).astype(_F32)
        prod = do * o
        for p in range(N_PAIRS):
            delta_ref[p] = _dot_exact(prod[:, 128 * p:128 * (p + 1)], sel)

    row = lambda w: pl.BlockSpec((t, w), lambda i: (i, 0))
    return pl.pallas_call(
        body, name="out_loss_bwd", grid=(lp // t,),
        in_specs=[row(512)] * 4 + [row(D_MODEL)] + _x_specs(t) + [_full((D_MODEL, D_MODEL))] + [_full((1, D_MODEL))] * 2,
        out_specs=[row(D_MODEL)] + [row(512)] * 4 + [pl.BlockSpec((N_PAIRS, t, 128), lambda i: (0, i, 0)),
                   _full((D_MODEL, D_MODEL)), _full((1, D_MODEL)), _full((1, D_MODEL)), _full((1, 1))],
        out_shape=[jax.ShapeDtypeStruct((lp, D_MODEL), _F32)] + [jax.ShapeDtypeStruct((lp, 512), _MM)] * 4
                  + [jax.ShapeDtypeStruct((N_PAIRS, lp, 128), _F32), jax.ShapeDtypeStruct((D_MODEL, D_MODEL), _F32),
                     jax.ShapeDtypeStruct((1, D_MODEL), _F32), jax.ShapeDtypeStruct((1, D_MODEL), _F32),
                     jax.ShapeDtypeStruct((1, 1), _F32)],
        compiler_params=_params(("arbitrary",)),
    )(o, ga, pw, gc, h, *([target] * nx), w_out, g, b)


def _attn_bwd(q, k, v, do, lse, delta, ccol):
    lp = q.shape[0]
    t = _row_tile(lp)
    nb = lp // t

    def body(q_ref, k_ref, v_ref, do_ref, lse_ref, delta_ref, ccol_ref,
             dq_ref, dk_ref, dv_ref, dck_ref, dcq_ref,
             qa_ref, qat_ref, doa_ref, doat_ref, dqt_ref):
        j = pl.program_id(1)

        @pl.when(j == 0)
        def _():
            dqt_ref[...] = jnp.zeros_like(dqt_ref)
            for hd in range(2):
                for n in range(nb):
                    rows = slice(n * t, (n + 1) * t)
                    qa = _augment(q_ref[rows, :], hd, 1.0, -lse_ref[0, rows, hd:hd + 1])
                    doa = _augment(do_ref[rows, :], hd, -delta_ref[0, rows, hd:hd + 1], None)
                    qa_ref[hd, rows, :] = qa
                    qat_ref[hd, :, rows] = _transpose_mm(qa)
                    doa_ref[hd, rows, :] = doa
                    doat_ref[hd, :, rows] = _transpose_mm(doa)

        kas = [_key_operand(k_ref[...], ccol_ref[0, :, hd:hd + 1], hd, j * t) for hd in range(2)]
        kats = [_transpose_mm(ka) for ka in kas]
        vas = [_augment(v_ref[...], hd, 1.0, None) for hd in range(2)]

        def block(i, carry, diag):
            rows = pl.ds(pl.multiple_of(i * t, 128), t)
            new = []
            for hd in range(2):
                dk, dv = carry[hd]
                st = _dot(kas[hd], qat_ref[hd, :, rows])
                if diag:
                    st = jnp.where(_diag_valid(j, t), st, -MASKED)
                pt = jnp.exp(st)
                dsb = (pt * _dot(vas[hd], doat_ref[hd, :, rows])).astype(_MM)
                dv = dv + _dot(pt.astype(_MM), doa_ref[hd, rows, :])
                dk = dk + _dot(dsb, qa_ref[hd, rows, :])
                dqt_ref[hd, :, rows] += _dot(kats[hd], dsb)
                new.append((dk, dv))
            return tuple(new)

        zero = jnp.zeros((t, 128), _F32)
        carry = block(j, ((zero, zero), (zero, zero)), True)
        (dk0, dv0), (dk1, dv1) = lax.fori_loop(j + 1, nb, functools.partial(block, diag=False), carry)
        lane = lax.broadcasted_iota(jnp.int32, (1, 128), 1)
        dk_ref[...] = jnp.where(lane < HEAD_DIM, dk0, dk1).astype(dk_ref.dtype)
        dv_ref[...] = jnp.where(lane < HEAD_DIM, dv0, dv1).astype(dv_ref.dtype)
        dck_ref[0] = jnp.where(lane == 0, dk0[:, HEAD_DIM:HEAD_DIM + 1], jnp.where(lane == 1, dk1[:, 0:1], 0.0))

        @pl.when(j == nb - 1)
        def _():
            row = lax.broadcasted_iota(jnp.int32, (128, t), 0)
            sub = lax.broadcasted_iota(jnp.int32, (8, t), 0)
            for n in range(nb):
                cols = slice(n * t, (n + 1) * t)
                t0, t1 = dqt_ref[0, :, cols], dqt_ref[1, :, cols]
                dq_ref[cols, :] = jnp.where(row < HEAD_DIM, t0, t1).T * ATTN_SCALE
                dcq_ref[0, :, cols] = jnp.where(sub == 0, t0[HEAD_DIM + 3:HEAD_DIM + 4, :],
                                                jnp.where(sub == 1, t1[3:4, :], 0.0))

    blk = pl.BlockSpec((t, 128), lambda p, j: (j, p))
    res = pl.BlockSpec((lp, 128), lambda p, j: (0, p))
    stat = pl.BlockSpec((1, lp, 128), lambda p, j: (p, 0, 0))
    stat_blk = pl.BlockSpec((1, t, 128), lambda p, j: (p, j, 0))
    return pl.pallas_call(
        body, name="attn_bwd", grid=(N_PAIRS, nb),
        in_specs=[res, blk, blk, res, stat, stat, stat_blk],
        out_specs=[res, blk, blk, stat_blk, pl.BlockSpec((1, 8, lp), lambda p, j: (p, 0, 0))],
        out_shape=[jax.ShapeDtypeStruct((lp, D_ATTN), _F32), jax.ShapeDtypeStruct((lp, D_ATTN), _MM),
                   jax.ShapeDtypeStruct((lp, D_ATTN), _MM), jax.ShapeDtypeStruct((N_PAIRS, lp, 128), _F32),
                   jax.ShapeDtypeStruct((N_PAIRS, 8, lp), _F32)],
        scratch_shapes=[pltpu.VMEM((2, lp, 128), _MM), pltpu.VMEM((2, 128, lp), _MM),
                        pltpu.VMEM((2, lp, 128), _MM), pltpu.VMEM((2, 128, lp), _MM),
                        pltpu.VMEM((2, 128, lp), _F32)],
        compiler_params=_params(("arbitrary", "arbitrary")),
    )(q, k, v, do, lse, delta, ccol)


def _decay_bwd(dck, dcq, f, bf):
    lp = f.shape[0]
    nb = lp // 128

    def body(dck_ref, dcq_ref, f_ref, bf_ref, df_ref, dbf_ref):
        r = lax.broadcasted_iota(jnp.int32, (128, 128), 0)
        c = lax.broadcasted_iota(jnp.int32, (128, 128), 1)
        triu = (c >= r).astype(_F32)

        def step(n, carry):
            tail, dbf = carry
            bi = nb - 1 - n
            rows = pl.ds(pl.multiple_of(bi * 128, 128), 128)
            cols = jnp.zeros((128, 128), _F32)
            for p in range(N_PAIRS):
                dcq = jnp.concatenate([dcq_ref[p, :, rows], jnp.zeros((120, 128), _F32)], axis=0).T
                dcp = dcq - dck_ref[p, rows, :]
                cols = cols + (dcp if p == 0 else pltpu.roll(dcp, 2 * p, axis=1))
            lane = lax.broadcasted_iota(jnp.int32, (128, 128), 1)
            cols = jnp.where(lane < N_HEADS, cols, 0.0)
            rev = _dot_exact(triu, cols) + tail
            pos = bi * 128 + lax.broadcasted_iota(jnp.int32, (128, 1), 0)
            df = jnp.where(pos >= ROW_PAD, rev * (1.0 - _sigmoid(f_ref[rows, :] + bf_ref[...])), 0.0)
            df_ref[rows, :] = df.astype(df_ref.dtype)
            return rev[0:1, :], dbf + jnp.sum(df, axis=0, keepdims=True)

        zero = jnp.zeros((1, 128), _F32)
        _, dbf = lax.fori_loop(0, nb, step, (zero, zero), unroll=True)
        dbf_ref[...] = dbf

    return pl.pallas_call(
        body, name="decay_bwd",
        out_shape=[jax.ShapeDtypeStruct((lp, 128), _MM), jax.ShapeDtypeStruct((1, 128), _F32)],
        compiler_params=_params(),
    )(dck, dcq, f, bf)


def _conv_bwd_rows(dpw, z, g, b, wpw):
    lp = z.shape[0]
    t = _row_tile(lp)

    def body(dpw_ref, z_ref, g_ref, b_ref, w_ref, dz_ref, dw_ref, dg_ref, db_ref, dcb_ref):
        i = pl.program_id(0)

        @pl.when(i == 0)
        def _():
            dw_ref[...] = jnp.zeros_like(dw_ref)
            dg_ref[...] = jnp.zeros_like(dg_ref)
            db_ref[...] = jnp.zeros_like(db_ref)
            dcb_ref[...] = jnp.zeros_like(dcb_ref)

        xhat, rstd = _ln_stats(z_ref[...])
        zn = xhat * g_ref[...] + b_ref[...]
        sg = _sigmoid(zn)
        dpw = dpw_ref[...]
        dw_ref[...] += _dot_t0((zn * sg).astype(_MM), dpw)
        dzn = _dot_t1(dpw, w_ref[...]) * (sg * (1.0 + zn * (1.0 - sg)))
        dg_ref[...] += jnp.sum(dzn * xhat, axis=0, keepdims=True)
        db_ref[...] += jnp.sum(dzn, axis=0, keepdims=True)
        dz = _ln_bwd(dzn, xhat, rstd, g_ref[...])
        dz_ref[...] = dz
        dcb_ref[...] += jnp.sum(dz, axis=0, keepdims=True)

    row = pl.BlockSpec((t, D_CONV), lambda i: (i, 0))
    vec = _full((1, D_CONV))
    vshape = jax.ShapeDtypeStruct((1, D_CONV), _F32)
    return pl.pallas_call(
        body, name="conv_bwd_rows", grid=(lp // t,),
        in_specs=[row, row, vec, vec, _full((D_CONV, D_CONV))],
        out_specs=[row, _full((D_CONV, D_CONV)), vec, vec, vec],
        out_shape=[jax.ShapeDtypeStruct((lp, D_CONV), _F32), jax.ShapeDtypeStruct((D_CONV, D_CONV), _F32),
                   vshape, vshape, vshape],
        compiler_params=_params(("arbitrary",)),
    )(dpw, z, g, b, wpw)


def _conv_bwd_taps(dz, u, ug, cw):
    lp = u.shape[0]
    t = _row_tile(lp)
    nt = lp // t

    def body(dz_ref, u_ref, ug_ref, cw_ref, du_ref, dug_ref, dcw_ref, win, sh, wacc):
        i = pl.program_id(0)

        @pl.when(i == 0)
        def _():
            win[t:t + HALO, :] = jnp.zeros((HALO, D_CONV), _F32)
            wacc[...] = jnp.zeros_like(wacc)

        win[0:t, :] = dz_ref[...]
        _shifted_windows(win, sh, t + HALO)

        def chunk(c, carry):
            r0 = c * CONV_CHUNK
            rows = pl.ds(pl.multiple_of(r0, 8), CONV_CHUNK)
            u = u_ref[rows, :]
            sg = _sigmoid(ug_ref[rows, :])
            hh = u * sg
            dhh = jnp.zeros((CONV_CHUNK, D_CONV), _F32)
            for j in range(CONV_WIDTH):
                d = CONV_WIDTH - 1 - j
                s = _tap_rows(sh, [d], r0)[d]
                dhh = dhh + cw_ref[j:j + 1, :] * s
                prod = hh * s
                wacc[j] += sum(prod[8 * n:8 * (n + 1), :] for n in range(CONV_CHUNK // 8))
            du_ref[rows, :] = (dhh * sg).astype(du_ref.dtype)
            dug_ref[rows, :] = (dhh * u * sg * (1.0 - sg)).astype(dug_ref.dtype)
            return carry

        lax.fori_loop(0, t // CONV_CHUNK, chunk, 0)
        win[t:t + HALO, :] = win[0:HALO, :]

        @pl.when(i == nt - 1)
        def _():
            dcw_ref[...] = jnp.sum(wacc[...], axis=1)

    row = pl.BlockSpec((t, D_CONV), lambda i: (nt - 1 - i, 0))
    return pl.pallas_call(
        body, name="conv_bwd_taps", grid=(nt,),
        in_specs=[row, row, row, _full((32, D_CONV))],
        out_specs=[row, row, _full((32, D_CONV))],
        out_shape=[jax.ShapeDtypeStruct((lp, D_CONV), _MM)] * 2 + [jax.ShapeDtypeStruct((32, D_CONV), _F32)],
        scratch_shapes=[pltpu.VMEM((t + HALO, D_CONV), _F32), pltpu.VMEM((8, t + HALO, D_CONV), _F32),
                        pltpu.VMEM((32, 8, D_CONV), _F32)],
        compiler_params=_params(("arbitrary",)),
    )(dz, u, ug, cw)


def _proj_bwd_x(pieces, df, wpt, dr, x, head, g):
    seq = x.shape[0]
    lp = ROW_X + seq
    t = _row_tile(lp)
    nt = lp // t
    nx = t // ROW_X

    def body(*refs):
        dq_ref, dk_ref, dv_ref, dga_ref, du_ref, dug_ref, dgc_ref, df_ref, wt_ref, dr_ref = refs[:10]
        x_refs, (head_ref, g_ref) = refs[10:10 + nx], refs[10 + nx:12 + nx]
        gx_ref, dmeta_ref, dg_ref, db_ref, buf, sem = refs[12 + nx:]
        i = pl.program_id(0)

        @pl.when(i == 0)
        def _():
            dg_ref[...] = jnp.zeros_like(dg_ref)
            db_ref[...] = jnp.zeros_like(db_ref)

        dh = ALPHA * dr_ref[...]
        for n, ref in enumerate((dq_ref, dk_ref, dv_ref, dga_ref, du_ref, dug_ref, dgc_ref)):
            dh = dh + _dot(ref[...].astype(_MM), wt_ref[512 * n:512 * (n + 1), :])
        dh = dh + _dot(df_ref[...], wt_ref[3584:3712, :])
        rows = i * t + lax.broadcasted_iota(jnp.int32, (t, 1), 0)
        dh = jnp.where(rows >= ROW_PAD, dh, 0.0)
        xhat, rstd = _ln_stats(_x_tile(i, x_refs, head_ref))
        dg_ref[...] += jnp.sum(dh * xhat, axis=0, keepdims=True)
        db_ref[...] += jnp.sum(dh, axis=0, keepdims=True)
        dx = _ln_bwd(dh, xhat, rstd, g_ref[...])

        def first_copy():
            return pltpu.make_async_copy(buf.at[0, pl.ds(ROW_X, t - ROW_X), :], gx_ref.at[pl.ds(0, t - ROW_X), :], sem)

        def tile_copy(n):
            return pltpu.make_async_copy(buf.at[n % 2], gx_ref.at[pl.ds(pl.multiple_of(n * t - ROW_X, ROW_X), t), :], sem)

        pl.when(i == 1)(lambda: first_copy().wait())
        pl.when(i > 1)(lambda: tile_copy(i - 1).wait())
        buf[i % 2] = dx

        @pl.when(i == 0)
        def _():
            dmeta_ref[...] = dx[ROW_PAD:ROW_X, :]
            first_copy().start()

        pl.when(i > 0)(lambda: tile_copy(i).start())
        if nt == 1:
            first_copy().wait()
        else:
            pl.when(i == nt - 1)(lambda: tile_copy(i).wait())

    row = lambda w: pl.BlockSpec((t, w), lambda i: (i, 0))
    vec = _full((1, D_MODEL))
    return pl.pallas_call(
        body, name="proj_bwd_x", grid=(nt,),
        in_specs=[row(512)] * 7 + [row(128), _full((D_IN_PAD, D_MODEL)), row(D_MODEL)] + _x_specs(t)
                 + [_full((ROW_X, D_MODEL)), vec],
        out_specs=[pl.BlockSpec(memory_space=pl.ANY), _full((N_META, D_MODEL)), vec, vec],
        out_shape=[jax.ShapeDtypeStruct((seq, D_MODEL), _F32), jax.ShapeDtypeStruct((N_META, D_MODEL), _F32),
                   jax.ShapeDtypeStruct((1, D_MODEL), _F32), jax.ShapeDtypeStruct((1, D_MODEL), _F32)],
        scratch_shapes=[pltpu.VMEM((2, t, D_MODEL), _F32), pltpu.SemaphoreType.DMA(())],
        compiler_params=_params(("arbitrary",)),
    )(*pieces, df, wpt, dr, *([x] * nx), head, g)


def _proj_bwd_w(h, pieces, name):
    lp = h.shape[0]
    t = _row_tile(lp)
    n = len(pieces)
    widths = [p.shape[1] for p in pieces]

    def body(*refs):
        h_ref, d_refs, w_refs = refs[0], refs[1:1 + n], refs[1 + n:]
        i = pl.program_id(0)

        @pl.when(i == 0)
        def _():
            for w_ref in w_refs:
                w_ref[...] = jnp.zeros_like(w_ref)

        hb = h_ref[...].astype(_MM)
        for d_ref, w_ref in zip(d_refs, w_refs):
            w_ref[...] += _dot_t0(d_ref[...].astype(_MM), hb)

    row = lambda w: pl.BlockSpec((t, w), lambda i: (i, 0))
    return pl.pallas_call(
        body, name=name, grid=(lp // t,),
        in_specs=[row(D_MODEL)] + [row(w) for w in widths],
        out_specs=[_full((w, D_MODEL)) for w in widths],
        out_shape=[jax.ShapeDtypeStruct((w, D_MODEL), _F32) for w in widths],
        compiler_params=_params(("arbitrary",)),
    )(h, *pieces)


def _adamw_math(w, g, m, v):
    m = ADAM_B1 * m + (1.0 - ADAM_B1) * g
    v = ADAM_B2 * v + (1.0 - ADAM_B2) * (g * g)
    m_hat = m / (1.0 - ADAM_B1 ** ADAM_STEP)
    v_hat = v / (1.0 - ADAM_B2 ** ADAM_STEP)
    delta = -ADAM_LR * (m_hat / (jnp.sqrt(v_hat) + ADAM_EPS) + ADAM_WD * w)
    return delta, m, v


def _adamw_tiled(w, g, m, v, rows, name):
    r, c = w.shape

    def body(w_ref, g_ref, m_ref, v_ref, go_ref, d_ref, nm_ref, nv_ref):
        g = g_ref[...]
        go_ref[...] = g
        d_ref[...], nm_ref[...], nv_ref[...] = _adamw_math(w_ref[...], g, m_ref[...], v_ref[...])

    spec = pl.BlockSpec((rows, c), lambda i: (i, 0))
    return pl.pallas_call(
        body, name=name, grid=(pl.cdiv(r, rows),), in_specs=[spec] * 4, out_specs=[spec] * 4,
        out_shape=[jax.ShapeDtypeStruct((r, c), _F32)] * 4,
        compiler_params=_params(("arbitrary",)),
    )(w, g, m, v)


def _adamw_small(ws, gs, ms, vs):
    n = len(ws)

    def body(*refs):
        ins, outs = refs[:4 * n], refs[4 * n:]
        for k in range(n):
            d, m, v = _adamw_math(ins[k][...], ins[n + k][...], ins[2 * n + k][...], ins[3 * n + k][...])
            outs[k][...], outs[n + k][...], outs[2 * n + k][...] = d, m, v

    shapes = [jax.ShapeDtypeStruct(w.shape, _F32) for w in ws]
    res = pl.pallas_call(body, name="adamw_small", out_shape=shapes * 3, compiler_params=_params())(*ws, *gs, *ms, *vs)
    return res[:n], res[n:2 * n], res[2 * n:]


def _place():
    x, y, c = lax.axis_index("x"), lax.axis_index("y"), lax.axis_index("c")
    chips = [(1 - x, y), (x, 1 - y), (1 - x, 1 - y)]
    return x, y, c, chips


def _half(ref, slot, c, rows):
    hr = rows // 2
    return ref.at[slot, pl.ds(pl.multiple_of(c * hr, 8), hr), :]


def _all_gather(shards):
    n = len(shards)

    def body(*refs):
        ins, outs = refs[:n], refs[n:2 * n]
        send, recv = refs[2 * n:]
        x, y, c, chips = _place()
        me = 2 * x + y
        sibling = (x, y, 1 - c)

        def copy(k, t, slot, half, to, src=None):
            rows = ins[t].shape[0]
            dst = _half(outs[t], slot, half, rows)
            return pltpu.make_async_remote_copy(
                src_ref=dst if src is None else src, dst_ref=dst,
                send_sem=send.at[k], recv_sem=recv.at[k], device_id=to, device_id_type=MESH)

        first, passed = [], []
        for t in range(n):
            hr = ins[t].shape[0] // 2
            mine = ins[t].at[pl.ds(pl.multiple_of(c * hr, 8), hr), :]
            for j, chip in enumerate(chips):
                first.append(copy(6 * t + j, t, me, c, (*chip, c), src=mine))
        for cp in first:
            cp.start()
        for t in range(n):
            for j, (cx, cy) in enumerate(chips):
                copy(6 * t + j, t, 2 * cx + cy, c, sibling).wait_recv()
                fwd = copy(6 * t + 3 + j, t, 2 * cx + cy, c, sibling)
                fwd.start()
                passed.append(fwd)
        for t in range(n):
            for j, (cx, cy) in enumerate(chips):
                copy(6 * t + 3 + j, t, 2 * cx + cy, 1 - c, sibling).wait_recv()
        for cp in first + passed:
            cp.wait_send()

    any_spec = pl.BlockSpec(memory_space=pl.ANY)
    return pl.pallas_call(
        body, name="all_gather",
        in_specs=[any_spec] * n, out_specs=[any_spec] * n,
        out_shape=[jax.ShapeDtypeStruct((N_SHARD,) + s.shape, s.dtype) for s in shards],
        scratch_shapes=[pltpu.SemaphoreType.DMA((6 * n,)), pltpu.SemaphoreType.DMA((6 * n,))],
    )(*shards)


def _own_slot(gathered, own):
    me = 2 * lax.axis_index("x") + lax.axis_index("y")
    slot = lax.broadcasted_iota(jnp.int32, (N_SHARD,) + (1,) * own.ndim, 0)
    return jnp.where(slot == me, own[None], gathered)


def _sibling_exchange(bufs):
    n = len(bufs)

    def body(*refs):
        ins, got = refs[:n], refs[n:2 * n]
        send, recv = refs[2 * n:]
        x, y, c, _ = _place()
        cps = []
        for t in range(n):
            hr = ins[t].shape[0] // (2 * N_SHARD)
            for s in range(N_SHARD):
                k = N_SHARD * t + s
                cps.append(pltpu.make_async_remote_copy(
                    src_ref=ins[t].at[pl.ds(pl.multiple_of((2 * s + 1 - c) * hr, 8), hr), :], dst_ref=got[t].at[s],
                    send_sem=send.at[k], recv_sem=recv.at[k], device_id=(x, y, 1 - c), device_id_type=MESH))
        for cp in cps:
            cp.start()
        for cp in cps:
            cp.wait()

    any_spec = pl.BlockSpec(memory_space=pl.ANY)
    return pl.pallas_call(
        body, name="sibling_exchange",
        in_specs=[any_spec] * n, out_specs=[any_spec] * n,
        out_shape=[jax.ShapeDtypeStruct((N_SHARD, b.shape[0] // (2 * N_SHARD), b.shape[1]), b.dtype) for b in bufs],
        scratch_shapes=[pltpu.SemaphoreType.DMA((N_SHARD * n,))] * 2,
    )(*bufs)


def _half_sum(bufs, got, wire, c):
    n = len(bufs)

    def body(c_ref, *refs):
        ins, gots, outs = refs[:n], refs[n:2 * n], refs[2 * n:]
        for t in range(n):
            outs[t][0] = (ins[t][...] + gots[t][0]).astype(outs[t].dtype)

    slot = lambda g: pl.BlockSpec((1,) + tuple(g.shape[1:]), lambda s, c_ref: (s, 0, 0))
    grid_spec = pltpu.PrefetchScalarGridSpec(
        num_scalar_prefetch=1, grid=(N_SHARD,),
        in_specs=[pl.BlockSpec(tuple(g.shape[1:]), lambda s, c_ref: (2 * s + c_ref[0], 0)) for g in got]
                 + [slot(g) for g in got],
        out_specs=[slot(g) for g in got])
    return pl.pallas_call(
        body, name="half_sum", grid_spec=grid_spec,
        out_shape=[jax.ShapeDtypeStruct(g.shape, d) for g, d in zip(got, wire)],
        compiler_params=_params(("arbitrary",)),
    )(c, *bufs, *got)


def _chip_exchange(parts):
    n = len(parts)

    def body(*refs):
        ins, outs = refs[:n], refs[n:2 * n]
        send, recv = refs[2 * n:]
        x, y, c, chips = _place()
        me = 2 * x + y
        cps = []
        for t in range(n):
            for j, (cx, cy) in enumerate(chips):
                cps.append(pltpu.make_async_remote_copy(
                    src_ref=ins[t].at[2 * cx + cy], dst_ref=outs[t].at[me],
                    send_sem=send.at[3 * t + j], recv_sem=recv.at[3 * t + j],
                    device_id=(cx, cy, c), device_id_type=MESH))
        for cp in cps:
            cp.start()
        for cp in cps:
            cp.wait()

    any_spec = pl.BlockSpec(memory_space=pl.ANY)
    return pl.pallas_call(
        body, name="chip_exchange",
        in_specs=[any_spec] * n, out_specs=[any_spec] * n,
        out_shape=[jax.ShapeDtypeStruct(p.shape, p.dtype) for p in parts],
        scratch_shapes=[pltpu.SemaphoreType.DMA((3 * n,))] * 2,
    )(*parts)


def _chip_sum(parts, got, me):
    n = len(got)

    def body(me_ref, *refs):
        owns, gots, outs = refs[:n], refs[n:2 * n], refs[2 * n:]
        s = pl.program_id(0)
        for t in range(n):
            term = jnp.where(s == me_ref[0], owns[t][0], gots[t][0]).astype(_F32)

            @pl.when(s == 0)
            def _():
                outs[t][...] = term

            @pl.when(s > 0)
            def _():
                outs[t][...] += term

    blk = lambda g: (1,) + tuple(g.shape[1:])
    grid_spec = pltpu.PrefetchScalarGridSpec(
        num_scalar_prefetch=1, grid=(N_SHARD,),
        in_specs=[pl.BlockSpec(blk(g), lambda s, me_ref: (me_ref[0], 0, 0)) for g in got]
                 + [pl.BlockSpec(blk(g), lambda s, me_ref: (jnp.where(s == me_ref[0], (s + 1) % N_SHARD, s), 0, 0))
                    for g in got],
        out_specs=[pl.BlockSpec(tuple(g.shape[1:]), lambda s, me_ref: (0, 0)) for g in got])
    return pl.pallas_call(
        body, name="chip_sum", grid_spec=grid_spec,
        out_shape=[jax.ShapeDtypeStruct(g.shape[1:], _F32) for g in got],
        compiler_params=_params(("arbitrary",)),
    )(me, *parts, *got)


def _sibling_share(halves):
    n = len(halves)

    def body(*refs):
        ins, outs = refs[:n], refs[n:2 * n]
        send, recv = refs[2 * n:]
        x, y, c, _ = _place()
        cps = [pltpu.make_async_remote_copy(
            src_ref=ins[t], dst_ref=outs[t], send_sem=send.at[t], recv_sem=recv.at[t],
            device_id=(x, y, 1 - c), device_id_type=MESH) for t in range(n)]
        for cp in cps:
            cp.start()
        for cp in cps:
            cp.wait()

    any_spec = pl.BlockSpec(memory_space=pl.ANY)
    return pl.pallas_call(
        body, name="sibling_share",
        in_specs=[any_spec] * n, out_specs=[any_spec] * n,
        out_shape=[jax.ShapeDtypeStruct(h.shape, h.dtype) for h in halves],
        scratch_shapes=[pltpu.SemaphoreType.DMA((n,))] * 2,
    )(*halves)


def _reduce_scatter(bufs, wire):
    x, y, c = lax.axis_index("x"), lax.axis_index("y"), lax.axis_index("c")
    as_operand = lambda i: jnp.asarray(i, jnp.int32).reshape(1)
    chip_part = _half_sum(bufs, _sibling_exchange(bufs), wire, as_operand(c))
    half = _chip_sum(chip_part, _chip_exchange(chip_part), as_operand(2 * x + y))
    other = _sibling_share(half)
    return [jnp.concatenate([jnp.where(c == 0, h, o), jnp.where(c == 0, o, h)], axis=0) for h, o in zip(half, other)]


REPL_SIZES = (D_MODEL, D_MODEL, N_HEADS, D_CONV, D_CONV, D_CONV, D_MODEL, D_MODEL)
REPL_ROWS = 48


def _pack_repl(parts):
    flat = jnp.concatenate([p.reshape(-1).astype(_F32) for p in parts])
    return jnp.pad(flat, (0, REPL_ROWS * 128 - flat.shape[0])).reshape(REPL_ROWS, 128)


def _unpack_repl(packed, shapes):
    flat = packed.reshape(-1)
    out, off = [], 0
    for shape, size in zip(shapes, REPL_SIZES):
        out.append(flat[off:off + size].reshape(shape))
        off += size
    return out


def _projection_operand(w_in_t):
    zrow = jnp.zeros((128 - N_HEADS, D_MODEL), w_in_t.dtype)
    return jnp.concatenate([w_in_t[:OFF_F], w_in_t[OFF_F + N_HEADS:], w_in_t[OFF_F:OFF_F + N_HEADS], zrow], axis=0)


def _local_grads(x, target, meta, ln_in_g, ln_in_b, w_in_t, b_f, conv_w, conv_b, ln_conv_g, ln_conv_b,
                 w_pw, w_out, ln_out_g, ln_out_b):
    vec = lambda a: a.reshape(1, -1).astype(_F32)
    head = jnp.concatenate([jnp.zeros((ROW_PAD, D_MODEL), _F32), meta], axis=0)
    wpt = _projection_operand(w_in_t)
    bf = jnp.pad(vec(b_f), ((0, 0), (0, 128 - N_HEADS)))
    cw = jnp.pad(conv_w, ((0, 32 - CONV_WIDTH), (0, 0)))

    h, q, k, v, ga, u, ug, gc, f, vt = _ln_proj(x, head, vec(ln_in_g), vec(ln_in_b), wpt)
    ccol = _decay_cumsum(f, bf)
    o, lse = _attn_fwd(q, k, vt, ccol)
    z, pw = _conv_fwd(u, ug, cw, vec(conv_b), vec(ln_conv_g), vec(ln_conv_b), w_pw)
    dr, do, dga, dpw, dgc, delta, dw_out, dg_out, db_out, loss = _out_loss_bwd(
        o, ga, pw, gc, h, target, w_out, vec(ln_out_g), vec(ln_out_b))
    dq, dk, dv, dck, dcq = _attn_bwd(q, k, v, do, lse, delta, ccol)
    df, dbf = _decay_bwd(dck, dcq, f, bf)
    dz, dw_pw, dg_conv, db_conv, dconv_b = _conv_bwd_rows(dpw, z, vec(ln_conv_g), vec(ln_conv_b), w_pw)
    du, dug, dcw = _conv_bwd_taps(dz, u, ug, cw)
    pieces = (dq, dk, dv, dga, du, dug, dgc)
    grad_x, dmeta, dg_in, db_in = _proj_bwd_x(pieces, df, wpt, dr, x, head, vec(ln_in_g))
    dwq, dwk, dwv, dwga = _proj_bwd_w(h, pieces[:4], "proj_bwd_w_attn")
    dwu, dwug, dwgc, dwf = _proj_bwd_w(h, pieces[4:] + (df,), "proj_bwd_w_conv")
    dw_in_t = jnp.concatenate([dwq, dwk, dwv, dwf[:N_HEADS], dwga, dwu, dwug, dwgc], axis=0)
    dw_in_t = jnp.pad(dw_in_t.reshape(N_SHARD, W_IN_SHARD, D_MODEL), ((0, 0), (0, W_IN_SLOT - W_IN_SHARD), (0, 0)))
    dw_in_t = dw_in_t.reshape(N_SHARD * W_IN_SLOT, D_MODEL)
    grads = dict(meta=dmeta, ln_in_g=dg_in, ln_in_b=db_in, w_in_t=dw_in_t, b_f=dbf[:, :N_HEADS],
                 conv_w=dcw[:CONV_WIDTH], conv_b=dconv_b, ln_conv_g=dg_conv, ln_conv_b=db_conv,
                 w_pw=dw_pw, w_out=dw_out, ln_out_g=dg_out, ln_out_b=db_out)
    return loss, grad_x, grads


def kernel(x, meta, ln_in_g, ln_in_b, w_in, b_f, conv_w, conv_b, ln_conv_g, ln_conv_b, w_pw, w_out, ln_out_g, ln_out_b, loss_target, m_meta, m_ln_in_g, m_ln_in_b, m_w_in, m_b_f, m_conv_w, m_conv_b, m_ln_conv_g, m_ln_conv_b, m_w_pw, m_w_out, m_ln_out_g, m_ln_out_b, v_meta, v_ln_in_g, v_ln_in_b, v_w_in, v_b_f, v_conv_w, v_conv_b, v_ln_conv_g, v_ln_conv_b, v_w_pw, v_w_out, v_ln_out_g, v_ln_out_b):
    conv_w2, w_pw2, w_out2 = conv_w[0], w_pw[0], w_out[0]
    w_t = jnp.transpose(w_in[0])

    small = jnp.concatenate([jnp.pad(conv_w2, ((0, 1), (0, 0))), meta.reshape(32, 128)], axis=0)
    shards = [jnp.pad(w_t.astype(_MM), ((0, W_IN_SLOT - W_IN_SHARD), (0, 0))), w_pw2.astype(_MM), w_out2.astype(_MM), small]
    g_in, g_pw, g_out, g_small = (_own_slot(g, s) for g, s in zip(_all_gather(shards), shards))
    w_in_t_full = g_in[:, :W_IN_SHARD, :].reshape(D_IN, D_MODEL)
    w_pw_full = g_pw.reshape(D_CONV, D_CONV)
    w_out_full = g_out.reshape(D_MODEL, D_MODEL)
    conv_w_full = jnp.transpose(g_small[:, :CONV_WIDTH, :], (1, 0, 2)).reshape(CONV_WIDTH, D_CONV)
    meta_full = jnp.transpose(g_small[:, 32:64, :].reshape(N_SHARD, N_META, 256), (1, 0, 2)).reshape(N_META, D_MODEL)

    loss, grad_x, gr = _local_grads(x[0], loss_target[0], meta_full, ln_in_g, ln_in_b, w_in_t_full, b_f, conv_w_full,
                                    conv_b, ln_conv_g, ln_conv_b, w_pw_full, w_out_full, ln_out_g, ln_out_b)
    grad_x = grad_x[None]

    repl_names = ("ln_in_g", "ln_in_b", "b_f", "conv_b", "ln_conv_g", "ln_conv_b", "ln_out_g", "ln_out_b")
    repl = _pack_repl([gr[n] for n in repl_names] + [loss])
    gcw = jnp.transpose(jnp.pad(gr["conv_w"], ((0, 1), (0, 0))).reshape(32, N_SHARD, 128), (1, 0, 2))
    gmeta = jnp.transpose(gr["meta"].reshape(N_META, N_SHARD, 256), (1, 0, 2)).reshape(N_SHARD, 32, 128)
    small_g = jnp.concatenate([gcw, gmeta, jnp.broadcast_to(repl[None], (N_SHARD, REPL_ROWS, 128))], axis=1)
    small_rows = small_g.shape[1]
    bufs = [gr["w_in_t"], gr["w_pw"], gr["w_out"], small_g.reshape(N_SHARD * small_rows, 128)]
    gw_slot, gw_pw, gw_out, gsmall = _reduce_scatter(bufs, [_MM, _MM, _MM, _F32])
    g_conv_w = gsmall[:CONV_WIDTH]
    g_meta = gsmall[32:64].reshape(N_META, 256)
    g_repl = gsmall[64:64 + REPL_ROWS]

    repl_w = (ln_in_g, ln_in_b, b_f, conv_b, ln_conv_g, ln_conv_b, ln_out_g, ln_out_b)
    repl_m = (m_ln_in_g, m_ln_in_b, m_b_f, m_conv_b, m_ln_conv_g, m_ln_conv_b, m_ln_out_g, m_ln_out_b)
    repl_v = (v_ln_in_g, v_ln_in_b, v_b_f, v_conv_b, v_ln_conv_g, v_ln_conv_b, v_ln_out_g, v_ln_out_b)
    zero = jnp.zeros((1,), _F32)
    res_t = _adamw_tiled(w_t, gw_slot, jnp.transpose(m_w_in[0]), jnp.transpose(v_w_in[0]), 256, "adamw_w_in")
    gw_in, d_in, nm_in, nv_in = (jnp.transpose(a) for a in res_t)
    ws = [w_pw2, w_out2, conv_w2, meta, _pack_repl(repl_w + (zero,))]
    gs = [gw_pw, gw_out, g_conv_w, g_meta, g_repl]
    ms = [m_w_pw[0], m_w_out[0], m_conv_w[0], m_meta, _pack_repl(repl_m + (zero,))]
    vs = [v_w_pw[0], v_w_out[0], v_conv_w[0], v_meta, _pack_repl(repl_v + (zero + 1.0,))]
    ds, nms, nvs = _adamw_small(ws, gs, ms, vs)

    shapes = [w.shape for w in repl_w]
    total_loss = g_repl.reshape(-1)[sum(REPL_SIZES)]
    names = ("meta", "ln_in_g", "ln_in_b", "w_in", "b_f", "conv_w", "conv_b", "ln_conv_g", "ln_conv_b",
             "w_pw", "w_out", "ln_out_g", "ln_out_b")

    def assemble(t_in, smalls):
        pw_, out_, cw_, meta_, packed = smalls
        r = dict(zip(repl_names, _unpack_repl(packed, shapes)))
        r.update(meta=meta_, w_in=t_in[None], conv_w=cw_[None], w_pw=pw_[None], w_out=out_[None])
        return [r[n] for n in names]

    return (total_loss, grad_x, *assemble(gw_in, gs), *assemble(d_in, ds), *assemble(nm_in, nms),
            *assemble(nv_in, nvs))
```

```python
import functools

import jax
import jax.numpy as jnp
from jax import lax
from jax.experimental import pallas as pl
from jax.experimental.pallas import tpu as pltpu

D_MODEL = 1024
N_META = 16
D_ATTN = 512
D_CONV = 512
N_HEADS = 8
HEAD_DIM = 64
N_PAIRS = 4
CONV_WIDTH = 31
LN_EPS = 1e-5
ALPHA = 2.0 ** 0.25
ATTN_SCALE = 0.125
D_IN = 3592
OFF_F = 3 * D_ATTN
D_IN_PAD = 3712
N_SHARD = 4
W_IN_SHARD = D_IN // N_SHARD
W_IN_SLOT = 928

ROW_PAD = 112
ROW_X = ROW_PAD + N_META
HALO = 32
MASKED = 1e30

ADAM_LR = 0.001
ADAM_B1 = 0.9
ADAM_B2 = 0.999
ADAM_EPS = 1e-08
ADAM_WD = 0.01
ADAM_STEP = 10

VMEM_LIMIT = 56 * 1024 * 1024

_MM = jnp.bfloat16
_F32 = jnp.float32
MESH = pl.DeviceIdType.MESH


def _params(sem=None):
    return pltpu.CompilerParams(dimension_semantics=sem, vmem_limit_bytes=VMEM_LIMIT)


def _row_tile(n_rows):
    for t in (384, 256, 128):
        if n_rows % t == 0:
            return t
    raise ValueError(f"padded sequence length {n_rows} is not a multiple of 128")


def _sigmoid(x):
    return 1.0 / (1.0 + jnp.exp(-x))


def _ln_stats(x):
    mu = jnp.mean(x, axis=-1, keepdims=True)
    xc = x - mu
    var = jnp.mean(xc * xc, axis=-1, keepdims=True)
    rstd = lax.rsqrt(var + LN_EPS)
    return xc * rstd, rstd


def _ln_bwd(dy, xhat, rstd, g):
    dxh = dy * g
    m1 = jnp.mean(dxh, axis=-1, keepdims=True)
    m2 = jnp.mean(dxh * xhat, axis=-1, keepdims=True)
    return rstd * (dxh - m1 - xhat * m2)


def _dot(a, b):
    return jnp.dot(a, b, preferred_element_type=_F32)


def _dot_t0(a, b):
    return lax.dot_general(a, b, (((0,), (0,)), ((), ())), preferred_element_type=_F32)


def _dot_t1(a, b):
    return lax.dot_general(a, b, (((1,), (1,)), ((), ())), preferred_element_type=_F32)


def _dot_exact(a, b):
    return jnp.dot(a, b, preferred_element_type=_F32, precision=lax.Precision.HIGHEST)


def _full(shape):
    return pl.BlockSpec(shape, lambda *_: (0,) * len(shape))


def _x_specs(t):
    n = t // ROW_X
    return [pl.BlockSpec((ROW_X, D_MODEL), lambda i, s=s: (jnp.maximum(i * n + s - 1, 0), 0)) for s in range(n)]


def _x_tile(i, refs, head_ref=None):
    first = refs[0][...]
    if head_ref is not None:
        first = jnp.where(i == 0, head_ref[...], first)
    return jnp.concatenate([first] + [r[...] for r in refs[1:]], axis=0)


def _ln_proj(x, head, g, b, wpt):
    lp = ROW_X + x.shape[0]
    t = _row_tile(lp)
    nx = t // ROW_X

    def body(*refs):
        x_refs, (head_ref, g_ref, b_ref, wt_ref) = refs[:nx], refs[nx:nx + 4]
        h_ref, q_ref, k_ref, v_ref, ga_ref, u_ref, ug_ref, gc_ref, f_ref, vt_ref = refs[nx + 4:]
        i = pl.program_id(0)
        xhat, _ = _ln_stats(_x_tile(i, x_refs, head_ref))
        rows = i * t + lax.broadcasted_iota(jnp.int32, (t, 1), 0)
        h = jnp.where(rows >= ROW_PAD, xhat * g_ref[...] + b_ref[...], 0.0)
        h_ref[...] = h
        hb = h.astype(_MM)
        q_ref[...] = (_dot_t1(hb, wt_ref[0:512, :]) * ATTN_SCALE).astype(q_ref.dtype)
        k_ref[...] = _dot_t1(hb, wt_ref[512:1024, :]).astype(k_ref.dtype)
        v = _dot_t1(hb, wt_ref[1024:1536, :])
        v_ref[...] = v.astype(v_ref.dtype)
        vt_ref[...] = v.T.astype(vt_ref.dtype)
        ga_ref[...] = _dot_t1(hb, wt_ref[1536:2048, :])
        u_ref[...] = _dot_t1(hb, wt_ref[2048:2560, :])
        ug_ref[...] = _dot_t1(hb, wt_ref[2560:3072, :])
        gc_ref[...] = _dot_t1(hb, wt_ref[3072:3584, :])
        f_ref[...] = _dot_t1(hb, wt_ref[3584:3712, :])

    row = lambda w: pl.BlockSpec((t, w), lambda i: (i, 0))
    f32 = lambda w: jax.ShapeDtypeStruct((lp, w), _F32)
    mm = lambda w: jax.ShapeDtypeStruct((lp, w), _MM)
    return pl.pallas_call(
        body, name="ln_proj", grid=(lp // t,),
        in_specs=_x_specs(t) + [_full((ROW_X, D_MODEL)), _full((1, D_MODEL)), _full((1, D_MODEL)),
                                _full((D_IN_PAD, D_MODEL))],
        out_specs=[row(D_MODEL)] + [row(512)] * 7 + [row(128), pl.BlockSpec((512, t), lambda i: (0, i))],
        out_shape=[f32(D_MODEL), mm(512), mm(512), mm(512), f32(512), f32(512), f32(512), f32(512), f32(128),
                   jax.ShapeDtypeStruct((512, lp), _MM)],
        compiler_params=_params(("arbitrary",)),
    )(*([x] * nx), head, g, b, wpt)


def _log_sigmoid(z):
    return jnp.minimum(z, 0.0) - jnp.log(1.0 + jnp.exp(-jnp.abs(z)))


def _decay_cumsum(f, bf):
    lp = f.shape[0]
    nb = lp // 128

    def body(f_ref, bf_ref, ccol_ref):
        r = lax.broadcasted_iota(jnp.int32, (128, 128), 0)
        c = lax.broadcasted_iota(jnp.int32, (128, 128), 1)
        tril = (c <= r).astype(_F32)

        def step(bi, carry):
            rows = pl.ds(pl.multiple_of(bi * 128, 128), 128)
            lf = _log_sigmoid(f_ref[rows, :] + bf_ref[...])
            cb = _dot_exact(tril, lf) + carry
            for p in range(N_PAIRS):
                ccol_ref[p, rows, :] = cb if p == 0 else pltpu.roll(cb, 128 - 2 * p, axis=1)
            return cb[127:128, :]

        lax.fori_loop(0, nb, step, jnp.zeros((1, 128), _F32))

    return pl.pallas_call(
        body, name="decay_cumsum",
        out_shape=jax.ShapeDtypeStruct((N_PAIRS, lp, 128), _F32),
        compiler_params=_params(),
    )(f, bf)


def _split3(x):
    hi = x.astype(_MM)
    r1 = x - hi.astype(_F32)
    mid = r1.astype(_MM)
    return hi, mid, (r1 - mid.astype(_F32)).astype(_MM)


def _augment(x, hd, first, second, transposed=False):
    lane = lax.broadcasted_iota(jnp.int32, (1, 128), 1)
    base = HEAD_DIM * (1 - hd)
    out = jnp.where(lane // HEAD_DIM == hd, x, jnp.zeros_like(x))
    for slot, val in enumerate((first, second)):
        lo = base + 3 * slot
        if val is None:
            continue
        if isinstance(val, float):
            out = jnp.where((lane >= lo) & (lane < lo + 3), jnp.asarray(val, x.dtype), out)
        else:
            for n, piece in enumerate(val):
                out = jnp.where(lane == lo + n, piece[:, hd:hd + 1], out)
    return _transpose_mm(out) if transposed else out


def _transpose_mm(x):
    return x.astype(_F32).T.astype(_MM)


def _diag_valid(i, t):
    sub = lax.broadcasted_iota(jnp.int32, (t, t), 0)
    return (sub <= lax.broadcasted_iota(jnp.int32, (t, t), 1)) & (i * t + sub >= ROW_PAD)


def _key_pieces(c, first_row):
    pos = first_row + lax.broadcasted_iota(jnp.int32, (c.shape[0], 1), 0)
    return _split3(jnp.where(pos >= ROW_PAD, -c, -MASKED))


def _key_operand(k, pieces, hd, transposed=False):
    return _augment(k, hd, pieces, 1.0, transposed)


def _attn_fwd(q, k, vt, ccol):
    lp = q.shape[0]
    t = _row_tile(lp)
    nb = lp // t

    def body(q_ref, k_ref, vt_ref, ccol_ref, o_ref, lse_ref, ka_ref):
        i = pl.program_id(1)

        @pl.when(i == 0)
        def _():
            for n in range(nb):
                rows = slice(n * t, (n + 1) * t)
                pieces = _key_pieces(ccol_ref[0, rows, :], n * t)
                for hd in range(2):
                    ka_ref[hd, rows, :] = _key_operand(k_ref[rows, :], pieces, hd)

        qats = [_augment(q_ref[...], hd, 1.0, None, transposed=True) for hd in range(2)]

        def scores(j):
            keys = pl.ds(pl.multiple_of(j * t, 128), t)
            return tuple(_dot(ka_ref[hd, keys, :], qats[hd]) for hd in range(2))

        def update(j, sts, states, diag):
            vt = vt_ref[:, pl.ds(pl.multiple_of(j * t, 128), t)]
            new = []
            for st, (m, l, acc) in zip(sts, states):
                if diag:
                    st = jnp.where(_diag_valid(i, t), st, -MASKED)
                m_new = jnp.maximum(m, jnp.max(st, axis=0, keepdims=True))
                a = jnp.exp(m - m_new)
                pt = jnp.exp(st - m_new)
                l = a * l + jnp.sum(pt, axis=0, keepdims=True)
                new.append((m_new, l, a * acc + _dot(vt, pt.astype(_MM))))
            return tuple(new)

        def step(j, carry):
            sts, states = carry
            return scores(j + 1), update(j, sts, states, False)

        init = (jnp.full((1, t), -MASKED, _F32), jnp.zeros((1, t), _F32), jnp.zeros((128, t), _F32))
        sts, states = lax.fori_loop(0, i, step, (scores(0), (init, init)))
        (m0, l0, acc0), (m1, l1, acc1) = update(i, sts, states, True)
        row = lax.broadcasted_iota(jnp.int32, (128, t), 0)
        o_ref[...] = jnp.where(row < HEAD_DIM, acc0 / l0, acc1 / l1).T
        lse_ref[0] = jnp.where(row == 0, m0 + jnp.log(l0), jnp.where(row == 1, m1 + jnp.log(l1), 0.0)).T

    return pl.pallas_call(
        body, name="attn_fwd", grid=(N_PAIRS, nb),
        in_specs=[pl.BlockSpec((t, 128), lambda p, i: (i, p)),
                  pl.BlockSpec((lp, 128), lambda p, i: (0, p)),
                  pl.BlockSpec((128, lp), lambda p, i: (p, 0)),
                  pl.BlockSpec((1, lp, 128), lambda p, i: (p, 0, 0))],
        out_specs=[pl.BlockSpec((t, 128), lambda p, i: (i, p)),
                   pl.BlockSpec((1, t, 128), lambda p, i: (p, i, 0))],
        out_shape=[jax.ShapeDtypeStruct((lp, D_ATTN), _F32), jax.ShapeDtypeStruct((N_PAIRS, lp, 128), _F32)],
        scratch_shapes=[pltpu.VMEM((2, lp, 128), _MM)],
        compiler_params=_params(("arbitrary", "arbitrary")),
    )(q, k, vt, ccol)


CONV_CHUNK = 32


def _shifted_windows(win, sh, rows):
    sh[0] = win[...]
    for b in range(1, 8):
        sh[b, 0:rows - 8, :] = win[b:b + rows - 8, :]


def _tap_rows(sh, offsets, r0):
    out = {}
    for b in range(8):
        ds = [d for d in offsets if d % 8 == b]
        if ds:
            lo, hi = min(ds) // 8, max(ds) // 8
            rows = sh[b, pl.ds(pl.multiple_of(r0 + 8 * lo, 8), CONV_CHUNK + 8 * (hi - lo)), :]
            for d in ds:
                out[d] = rows[8 * (d // 8 - lo):8 * (d // 8 - lo) + CONV_CHUNK, :]
    return out


def _conv_fwd(u, ug, cw, cb, g, b, wpw):
    lp = u.shape[0]
    t = _row_tile(lp)

    def body(u_ref, ug_ref, cw_ref, cb_ref, g_ref, b_ref, wpw_ref, z_ref, pw_ref, win, sh):
        i = pl.program_id(0)

        @pl.when(i == 0)
        def _():
            win[0:HALO, :] = jnp.zeros((HALO, D_CONV), _F32)

        win[HALO:HALO + t, :] = u_ref[...] * _sigmoid(ug_ref[...])
        _shifted_windows(win, sh, t + HALO)

        def chunk(c, carry):
            r0 = c * CONV_CHUNK
            acc = jnp.broadcast_to(cb_ref[...], (CONV_CHUNK, D_CONV))
            first = HALO - (CONV_WIDTH - 1)
            taps = _tap_rows(sh, range(first, first + CONV_WIDTH), r0)
            for j in range(CONV_WIDTH):
                acc = acc + cw_ref[j:j + 1, :] * taps[first + j]
            z_ref[pl.ds(pl.multiple_of(r0, 8), CONV_CHUNK), :] = acc
            return carry

        lax.fori_loop(0, t // CONV_CHUNK, chunk, 0)
        xhat, _ = _ln_stats(z_ref[...])
        zn = xhat * g_ref[...] + b_ref[...]
        a = zn * _sigmoid(zn)
        pw_ref[...] = _dot(a.astype(_MM), wpw_ref[...])
        win[0:HALO, :] = win[t:t + HALO, :]

    row = pl.BlockSpec((t, D_CONV), lambda i: (i, 0))
    vec = _full((1, D_CONV))
    return pl.pallas_call(
        body, name="conv_fwd", grid=(lp // t,),
        in_specs=[row, row, _full((32, D_CONV)), vec, vec, vec, _full((D_CONV, D_CONV))],
        out_specs=[row, row],
        out_shape=[jax.ShapeDtypeStruct((lp, D_CONV), _F32)] * 2,
        scratch_shapes=[pltpu.VMEM((t + HALO, D_CONV), _F32), pltpu.VMEM((8, t + HALO, D_CONV), _F32)],
        compiler_params=_params(("arbitrary",)),
    )(u, ug, cw, cb, g, b, wpw)


def _out_loss_bwd(o, ga, pw, gc, h, target, w_out, g, b):
    lp = o.shape[0]
    t = _row_tile(lp)
    nx = t // ROW_X

    def body(*refs):
        o_ref, ga_ref, pw_ref, gc_ref, h_ref = refs[:5]
        tgt_refs, (w_ref, g_ref, b_ref) = refs[5:5 + nx], refs[5 + nx:8 + nx]
        dr_ref, do_ref, dga_ref, dpw_ref, dgc_ref, delta_ref, dw_ref, dg_ref, db_ref, loss_ref = refs[8 + nx:]
        i = pl.program_id(0)

        @pl.when(i == 0)
        def _():
            dw_ref[...] = jnp.zeros_like(dw_ref)
            dg_ref[...] = jnp.zeros_like(dg_ref)
            db_ref[...] = jnp.zeros_like(db_ref)
            loss_ref[...] = jnp.zeros_like(loss_ref)

        o, ga, pw, gc = o_ref[...], ga_ref[...], pw_ref[...], gc_ref[...]
        sa, sc = _sigmoid(ga), _sigmoid(gc)
        silu_a, silu_c = ga * sa, gc * sc
        ycat = jnp.concatenate([o * silu_a, pw * silu_c], axis=-1).astype(_MM)
        r = ALPHA * h_ref[...] + _dot(ycat, w_ref[...])
        xhat, rstd = _ln_stats(r)
        out = xhat * g_ref[...] + b_ref[...]
        rows = i * t + lax.broadcasted_iota(jnp.int32, (t, 1), 0)
        diff = jnp.where(rows >= ROW_X, out - _x_tile(i, tgt_refs), 0.0)
        loss_ref[...] += (0.5 / D_MODEL) * jnp.sum(diff * diff, keepdims=True)
        dout = diff * (1.0 / D_MODEL)
        dg_ref[...] += jnp.sum(dout * xhat, axis=0, keepdims=True)
        db_ref[...] += jnp.sum(dout, axis=0, keepdims=True)
        dr = _ln_bwd(dout, xhat, rstd, g_ref[...])
        dr_ref[...] = dr
        drb = dr.astype(_MM)
        dw_ref[...] += _dot_t0(ycat, drb)
        dy = _dot_t1(drb, w_ref[...])
        dya, dyc = dy[:, :D_ATTN], dy[:, D_ATTN:]
        do = dya * silu_a
        do_ref[...] = do.astype(do_ref.dtype)
        dga_ref[...] = (dya * o * (sa * (1.0 + ga * (1.0 - sa)))).astype(dga_ref.dtype)
        dpw_ref[...] = (dyc * silu_c).astype(dpw_ref.dtype)
        dgc_ref[...] = (dyc * pw * (sc * (1.0 + gc * (1.0 - sc)))).astype(dgc_ref.dtype)
        sel = (lax.broadcasted_iota(jnp.int32, (128, 128), 0) // HEAD_DIM
               == lax.broadcasted_iota(jnp.int32, (128, 128), 1)).astype(_F32)
        prod = do * o
        for p in range(N_PAIRS):
            delta_ref[p] = _dot_exact(prod[:, 128 * p:128 * (p + 1)], sel)

    row = lambda w: pl.BlockSpec((t, w), lambda i: (i, 0))
    return pl.pallas_call(
        body, name="out_loss_bwd", grid=(lp // t,),
        in_specs=[row(512)] * 4 + [row(D_MODEL)] + _x_specs(t) + [_full((D_MODEL, D_MODEL))] + [_full((1, D_MODEL))] * 2,
        out_specs=[row(D_MODEL)] + [row(512)] * 4 + [pl.BlockSpec((N_PAIRS, t, 128), lambda i: (0, i, 0)),
                   _full((D_MODEL, D_MODEL)), _full((1, D_MODEL)), _full((1, D_MODEL)), _full((1, 1))],
        out_shape=[jax.ShapeDtypeStruct((lp, D_MODEL), _F32)] + [jax.ShapeDtypeStruct((lp, 512), _MM)] * 4
                  + [jax.ShapeDtypeStruct((N_PAIRS, lp, 128), _F32), jax.ShapeDtypeStruct((D_MODEL, D_MODEL), _F32),
                     jax.ShapeDtypeStruct((1, D_MODEL), _F32), jax.ShapeDtypeStruct((1, D_MODEL), _F32),
                     jax.ShapeDtypeStruct((1, 1), _F32)],
        compiler_params=_params(("arbitrary",)),
    )(o, ga, pw, gc, h, *([target] * nx), w_out, g, b)


def _attn_bwd(q, k, v, do, lse, delta, ccol):
    lp = q.shape[0]
    t = _row_tile(lp)
    nb = lp // t

    def body(q_ref, k_ref, v_ref, do_ref, lse_ref, delta_ref, ccol_ref,
             dq_ref, dk_ref, dv_ref, dck_ref, dcq_ref,
             qa_ref, qat_ref, doa_ref, doat_ref, dqt_ref):
        j = pl.program_id(1)

        @pl.when(j == 0)
        def _():
            dqt_ref[...] = jnp.zeros_like(dqt_ref)
            for n in range(nb):
                rows = slice(n * t, (n + 1) * t)
                lse_pieces, delta_pieces = _split3(-lse_ref[0, rows, :]), _split3(-delta_ref[0, rows, :])
                for hd in range(2):
                    qa = _augment(q_ref[rows, :], hd, 1.0, lse_pieces)
                    doa = _augment(do_ref[rows, :], hd, delta_pieces, None)
                    qa_ref[hd, rows, :] = qa
                    qat_ref[hd, :, rows] = _transpose_mm(qa)
                    doa_ref[hd, rows, :] = doa
                    doat_ref[hd, :, rows] = _transpose_mm(doa)

        pieces = _key_pieces(ccol_ref[0], j * t)
        kas = [_key_operand(k_ref[...], pieces, hd) for hd in range(2)]
        kats = [_transpose_mm(ka) for ka in kas]
        vas = [_augment(v_ref[...], hd, 1.0, None) for hd in range(2)]

        def block(i, carry, diag, width=1):
            rows = pl.ds(pl.multiple_of(i * t, 128), width * t)
            new = []
            for hd in range(2):
                dk, dv = carry[hd]
                st = _dot(kas[hd], qat_ref[hd, :, rows])
                if diag:
                    st = jnp.where(_diag_valid(j, t), st, -MASKED)
                pt = jnp.exp(st)
                dsb = (pt * _dot(vas[hd], doat_ref[hd, :, rows])).astype(_MM)
                dv = dv + _dot(pt.astype(_MM), doa_ref[hd, rows, :])
                dk = dk + _dot(dsb, qa_ref[hd, rows, :])
                dqt_ref[hd, :, rows] += _dot(kats[hd], dsb)
                new.append((dk, dv))
            return tuple(new)

        zero = jnp.zeros((t, 128), _F32)
        carry = block(j, ((zero, zero), (zero, zero)), True)
        rest = nb - 1 - j
        carry = lax.fori_loop(0, rest // 2, lambda n, c: block(j + 1 + 2 * n, c, False, 2), carry)
        (dk0, dv0), (dk1, dv1) = lax.cond(rest % 2 == 1, lambda c: block(nb - 1, c, False), lambda c: c, carry)
        lane = lax.broadcasted_iota(jnp.int32, (1, 128), 1)
        dk_ref[...] = jnp.where(lane < HEAD_DIM, dk0, dk1).astype(dk_ref.dtype)
        dv_ref[...] = jnp.where(lane < HEAD_DIM, dv0, dv1).astype(dv_ref.dtype)
        dck_ref[0] = jnp.where(lane == 0, dk0[:, HEAD_DIM:HEAD_DIM + 1], jnp.where(lane == 1, dk1[:, 0:1], 0.0))

        @pl.when(j == nb - 1)
        def _():
            row = lax.broadcasted_iota(jnp.int32, (128, t), 0)
            sub = lax.broadcasted_iota(jnp.int32, (8, t), 0)
            for n in range(nb):
                cols = slice(n * t, (n + 1) * t)
                t0, t1 = dqt_ref[0, :, cols], dqt_ref[1, :, cols]
                dq_ref[cols, :] = jnp.where(row < HEAD_DIM, t0, t1).T * ATTN_SCALE
                dcq_ref[0, :, cols] = jnp.where(sub == 0, t0[HEAD_DIM + 3:HEAD_DIM + 4, :],
                                                jnp.where(sub == 1, t1[3:4, :], 0.0))

    blk = pl.BlockSpec((t, 128), lambda p, j: (j, p))
    res = pl.BlockSpec((lp, 128), lambda p, j: (0, p))
    stat = pl.BlockSpec((1, lp, 128), lambda p, j: (p, 0, 0))
    stat_blk = pl.BlockSpec((1, t, 128), lambda p, j: (p, j, 0))
    return pl.pallas_call(
        body, name="attn_bwd", grid=(N_PAIRS, nb),
        in_specs=[res, blk, blk, res, stat, stat, stat_blk],
        out_specs=[res, blk, blk, stat_blk, pl.BlockSpec((1, 8, lp), lambda p, j: (p, 0, 0))],
        out_shape=[jax.ShapeDtypeStruct((lp, D_ATTN), _F32), jax.ShapeDtypeStruct((lp, D_ATTN), _MM),
                   jax.ShapeDtypeStruct((lp, D_ATTN), _MM), jax.ShapeDtypeStruct((N_PAIRS, lp, 128), _F32),
                   jax.ShapeDtypeStruct((N_PAIRS, 8, lp), _F32)],
        scratch_shapes=[pltpu.VMEM((2, lp, 128), _MM), pltpu.VMEM((2, 128, lp), _MM),
                        pltpu.VMEM((2, lp, 128), _MM), pltpu.VMEM((2, 128, lp), _MM),
                        pltpu.VMEM((2, 128, lp), _F32)],
        compiler_params=_params(("arbitrary", "arbitrary")),
    )(q, k, v, do, lse, delta, ccol)


def _decay_bwd(dck, dcq, f, bf):
    lp = f.shape[0]
    nb = lp // 128

    def body(dck_ref, dcq_ref, f_ref, bf_ref, df_ref, dbf_ref):
        r = lax.broadcasted_iota(jnp.int32, (128, 128), 0)
        c = lax.broadcasted_iota(jnp.int32, (128, 128), 1)
        triu = (c >= r).astype(_F32)

        def step(n, carry):
            tail, dbf = carry
            bi = nb - 1 - n
            rows = pl.ds(pl.multiple_of(bi * 128, 128), 128)
            cols = jnp.zeros((128, 128), _F32)
            for p in range(N_PAIRS):
                dcq = jnp.concatenate([dcq_ref[p, :, rows], jnp.zeros((120, 128), _F32)], axis=0).T
                dcp = dcq - dck_ref[p, rows, :]
                cols = cols + (dcp if p == 0 else pltpu.roll(dcp, 2 * p, axis=1))
            lane = lax.broadcasted_iota(jnp.int32, (128, 128), 1)
            cols = jnp.where(lane < N_HEADS, cols, 0.0)
            rev = _dot_exact(triu, cols) + tail
            pos = bi * 128 + lax.broadcasted_iota(jnp.int32, (128, 1), 0)
            df = jnp.where(pos >= ROW_PAD, rev * (1.0 - _sigmoid(f_ref[rows, :] + bf_ref[...])), 0.0)
            df_ref[rows, :] = df.astype(df_ref.dtype)
            return rev[0:1, :], dbf + jnp.sum(df, axis=0, keepdims=True)

        zero = jnp.zeros((1, 128), _F32)
        _, dbf = lax.fori_loop(0, nb, step, (zero, zero), unroll=True)
        dbf_ref[...] = dbf

    return pl.pallas_call(
        body, name="decay_bwd",
        out_shape=[jax.ShapeDtypeStruct((lp, 128), _MM), jax.ShapeDtypeStruct((1, 128), _F32)],
        compiler_params=_params(),
    )(dck, dcq, f, bf)


def _conv_bwd_rows(dpw, z, g, b, wpw):
    lp = z.shape[0]
    t = _row_tile(lp)

    def body(dpw_ref, z_ref, g_ref, b_ref, w_ref, dz_ref, dw_ref, dg_ref, db_ref, dcb_ref):
        i = pl.program_id(0)

        @pl.when(i == 0)
        def _():
            dw_ref[...] = jnp.zeros_like(dw_ref)
            dg_ref[...] = jnp.zeros_like(dg_ref)
            db_ref[...] = jnp.zeros_like(db_ref)
            dcb_ref[...] = jnp.zeros_like(dcb_ref)

        xhat, rstd = _ln_stats(z_ref[...])
        zn = xhat * g_ref[...] + b_ref[...]
        sg = _sigmoid(zn)
        dpw = dpw_ref[...]
        dw_ref[...] += _dot_t0((zn * sg).astype(_MM), dpw)
        dzn = _dot_t1(dpw, w_ref[...]) * (sg * (1.0 + zn * (1.0 - sg)))
        dg_ref[...] += jnp.sum(dzn * xhat, axis=0, keepdims=True)
        db_ref[...] += jnp.sum(dzn, axis=0, keepdims=True)
        dz = _ln_bwd(dzn, xhat, rstd, g_ref[...])
        dz_ref[...] = dz
        dcb_ref[...] += jnp.sum(dz, axis=0, keepdims=True)

    row = pl.BlockSpec((t, D_CONV), lambda i: (i, 0))
    vec = _full((1, D_CONV))
    vshape = jax.ShapeDtypeStruct((1, D_CONV), _F32)
    return pl.pallas_call(
        body, name="conv_bwd_rows", grid=(lp // t,),
        in_specs=[row, row, vec, vec, _full((D_CONV, D_CONV))],
        out_specs=[row, _full((D_CONV, D_CONV)), vec, vec, vec],
        out_shape=[jax.ShapeDtypeStruct((lp, D_CONV), _F32), jax.ShapeDtypeStruct((D_CONV, D_CONV), _F32),
                   vshape, vshape, vshape],
        compiler_params=_params(("arbitrary",)),
    )(dpw, z, g, b, wpw)


def _conv_bwd_taps(dz, u, ug, cw):
    lp = u.shape[0]
    t = _row_tile(lp)
    nt = lp // t

    def body(dz_ref, u_ref, ug_ref, cw_ref, du_ref, dug_ref, dcw_ref, win, sh, wacc):
        i = pl.program_id(0)

        @pl.when(i == 0)
        def _():
            win[t:t + HALO, :] = jnp.zeros((HALO, D_CONV), _F32)
            wacc[...] = jnp.zeros_like(wacc)

        win[0:t, :] = dz_ref[...]
        _shifted_windows(win, sh, t + HALO)

        def chunk(c, carry):
            r0 = c * CONV_CHUNK
            rows = pl.ds(pl.multiple_of(r0, 8), CONV_CHUNK)
            u = u_ref[rows, :]
            sg = _sigmoid(ug_ref[rows, :])
            hh = u * sg
            dhh = jnp.zeros((CONV_CHUNK, D_CONV), _F32)
            for j in range(CONV_WIDTH):
                d = CONV_WIDTH - 1 - j
                s = _tap_rows(sh, [d], r0)[d]
                dhh = dhh + cw_ref[j:j + 1, :] * s
                prod = hh * s
                wacc[j] += sum(prod[8 * n:8 * (n + 1), :] for n in range(CONV_CHUNK // 8))
            du_ref[rows, :] = (dhh * sg).astype(du_ref.dtype)
            dug_ref[rows, :] = (dhh * u * sg * (1.0 - sg)).astype(dug_ref.dtype)
            return carry

        lax.fori_loop(0, t // CONV_CHUNK, chunk, 0)
        win[t:t + HALO, :] = win[0:HALO, :]

        @pl.when(i == nt - 1)
        def _():
            dcw_ref[...] = jnp.sum(wacc[...], axis=1)

    row = pl.BlockSpec((t, D_CONV), lambda i: (nt - 1 - i, 0))
    return pl.pallas_call(
        body, name="conv_bwd_taps", grid=(nt,),
        in_specs=[row, row, row, _full((32, D_CONV))],
        out_specs=[row, row, _full((32, D_CONV))],
        out_shape=[jax.ShapeDtypeStruct((lp, D_CONV), _MM)] * 2 + [jax.ShapeDtypeStruct((32, D_CONV), _F32)],
        scratch_shapes=[pltpu.VMEM((t + HALO, D_CONV), _F32), pltpu.VMEM((8, t + HALO, D_CONV), _F32),
                        pltpu.VMEM((32, 8, D_CONV), _F32)],
        compiler_params=_params(("arbitrary",)),
    )(dz, u, ug, cw)


def _proj_bwd_x(pieces, df, wpt, dr, x, head, g):
    seq = x.shape[0]
    lp = ROW_X + seq
    t = _row_tile(lp)
    nt = lp // t
    nx = t // ROW_X

    def body(*refs):
        dq_ref, dk_ref, dv_ref, dga_ref, du_ref, dug_ref, dgc_ref, df_ref, wt_ref, dr_ref = refs[:10]
        x_refs, (head_ref, g_ref) = refs[10:10 + nx], refs[10 + nx:12 + nx]
        gx_ref, dmeta_ref, dg_ref, db_ref, buf, sem = refs[12 + nx:]
        i = pl.program_id(0)

        @pl.when(i == 0)
        def _():
            dg_ref[...] = jnp.zeros_like(dg_ref)
            db_ref[...] = jnp.zeros_like(db_ref)

        dh = ALPHA * dr_ref[...]
        for n, ref in enumerate((dq_ref, dk_ref, dv_ref, dga_ref, du_ref, dug_ref, dgc_ref)):
            dh = dh + _dot(ref[...].astype(_MM), wt_ref[512 * n:512 * (n + 1), :])
        dh = dh + _dot(df_ref[...], wt_ref[3584:3712, :])
        rows = i * t + lax.broadcasted_iota(jnp.int32, (t, 1), 0)
        dh = jnp.where(rows >= ROW_PAD, dh, 0.0)
        xhat, rstd = _ln_stats(_x_tile(i, x_refs, head_ref))
        dg_ref[...] += jnp.sum(dh * xhat, axis=0, keepdims=True)
        db_ref[...] += jnp.sum(dh, axis=0, keepdims=True)
        dx = _ln_bwd(dh, xhat, rstd, g_ref[...])

        def first_copy():
            return pltpu.make_async_copy(buf.at[0, pl.ds(ROW_X, t - ROW_X), :], gx_ref.at[pl.ds(0, t - ROW_X), :], sem)

        def tile_copy(n):
            return pltpu.make_async_copy(buf.at[n % 2], gx_ref.at[pl.ds(pl.multiple_of(n * t - ROW_X, ROW_X), t), :], sem)

        pl.when(i == 1)(lambda: first_copy().wait())
        pl.when(i > 1)(lambda: tile_copy(i - 1).wait())
        buf[i % 2] = dx

        @pl.when(i == 0)
        def _():
            dmeta_ref[...] = dx[ROW_PAD:ROW_X, :]
            first_copy().start()

        pl.when(i > 0)(lambda: tile_copy(i).start())
        if nt == 1:
            first_copy().wait()
        else:
            pl.when(i == nt - 1)(lambda: tile_copy(i).wait())

    row = lambda w: pl.BlockSpec((t, w), lambda i: (i, 0))
    vec = _full((1, D_MODEL))
    return pl.pallas_call(
        body, name="proj_bwd_x", grid=(nt,),
        in_specs=[row(512)] * 7 + [row(128), _full((D_IN_PAD, D_MODEL)), row(D_MODEL)] + _x_specs(t)
                 + [_full((ROW_X, D_MODEL)), vec],
        out_specs=[pl.BlockSpec(memory_space=pl.ANY), _full((N_META, D_MODEL)), vec, vec],
        out_shape=[jax.ShapeDtypeStruct((seq, D_MODEL), _F32), jax.ShapeDtypeStruct((N_META, D_MODEL), _F32),
                   jax.ShapeDtypeStruct((1, D_MODEL), _F32), jax.ShapeDtypeStruct((1, D_MODEL), _F32)],
        scratch_shapes=[pltpu.VMEM((2, t, D_MODEL), _F32), pltpu.SemaphoreType.DMA(())],
        compiler_params=_params(("arbitrary",)),
    )(*pieces, df, wpt, dr, *([x] * nx), head, g)


def _proj_bwd_w(h, pieces, name):
    lp = h.shape[0]
    t = _row_tile(lp)
    n = len(pieces)
    widths = [p.shape[1] for p in pieces]

    def body(*refs):
        h_ref, d_refs, w_refs = refs[0], refs[1:1 + n], refs[1 + n:]
        i = pl.program_id(0)

        @pl.when(i == 0)
        def _():
            for w_ref in w_refs:
                w_ref[...] = jnp.zeros_like(w_ref)

        hb = h_ref[...].astype(_MM)
        for d_ref, w_ref in zip(d_refs, w_refs):
            w_ref[...] += _dot_t0(d_ref[...].astype(_MM), hb)

    row = lambda w: pl.BlockSpec((t, w), lambda i: (i, 0))
    return pl.pallas_call(
        body, name=name, grid=(lp // t,),
        in_specs=[row(D_MODEL)] + [row(w) for w in widths],
        out_specs=[_full((w, D_MODEL)) for w in widths],
        out_shape=[jax.ShapeDtypeStruct((w, D_MODEL), _F32) for w in widths],
        compiler_params=_params(("arbitrary",)),
    )(h, *pieces)


def _adamw_math(w, g, m, v):
    m = ADAM_B1 * m + (1.0 - ADAM_B1) * g
    v = ADAM_B2 * v + (1.0 - ADAM_B2) * (g * g)
    m_hat = m / (1.0 - ADAM_B1 ** ADAM_STEP)
    v_hat = v / (1.0 - ADAM_B2 ** ADAM_STEP)
    delta = -ADAM_LR * (m_hat / (jnp.sqrt(v_hat) + ADAM_EPS) + ADAM_WD * w)
    return delta, m, v


def _adamw_tiled(w, g, m, v, rows, name):
    r, c = w.shape

    def body(w_ref, g_ref, m_ref, v_ref, go_ref, d_ref, nm_ref, nv_ref):
        g = g_ref[...]
        go_ref[...] = g
        d_ref[...], nm_ref[...], nv_ref[...] = _adamw_math(w_ref[...], g, m_ref[...], v_ref[...])

    spec = pl.BlockSpec((rows, c), lambda i: (i, 0))
    return pl.pallas_call(
        body, name=name, grid=(pl.cdiv(r, rows),), in_specs=[spec] * 4, out_specs=[spec] * 4,
        out_shape=[jax.ShapeDtypeStruct((r, c), _F32)] * 4,
        compiler_params=_params(("arbitrary",)),
    )(w, g, m, v)


def _adamw_small(ws, gs, ms, vs):
    n = len(ws)

    def body(*refs):
        ins, outs = refs[:4 * n], refs[4 * n:]
        for k in range(n):
            d, m, v = _adamw_math(ins[k][...], ins[n + k][...], ins[2 * n + k][...], ins[3 * n + k][...])
            outs[k][...], outs[n + k][...], outs[2 * n + k][...] = d, m, v

    shapes = [jax.ShapeDtypeStruct(w.shape, _F32) for w in ws]
    res = pl.pallas_call(body, name="adamw_small", out_shape=shapes * 3, compiler_params=_params())(*ws, *gs, *ms, *vs)
    return res[:n], res[n:2 * n], res[2 * n:]


def _place():
    x, y, c = lax.axis_index("x"), lax.axis_index("y"), lax.axis_index("c")
    chips = [(1 - x, y), (x, 1 - y), (1 - x, 1 - y)]
    return x, y, c, chips


def _half(ref, slot, c, rows):
    hr = rows // 2
    return ref.at[slot, pl.ds(pl.multiple_of(c * hr, 8), hr), :]


def _all_gather(shards):
    n = len(shards)

    def body(*refs):
        ins, outs = refs[:n], refs[n:2 * n]
        send, recv = refs[2 * n:]
        x, y, c, chips = _place()
        me = 2 * x + y
        sibling = (x, y, 1 - c)

        def copy(k, t, slot, half, to, src=None):
            rows = ins[t].shape[0]
            dst = _half(outs[t], slot, half, rows)
            return pltpu.make_async_remote_copy(
                src_ref=dst if src is None else src, dst_ref=dst,
                send_sem=send.at[k], recv_sem=recv.at[k], device_id=to, device_id_type=MESH)

        first, passed = [], []
        for t in range(n):
            hr = ins[t].shape[0] // 2
            mine = ins[t].at[pl.ds(pl.multiple_of(c * hr, 8), hr), :]
            for j, chip in enumerate(chips):
                first.append(copy(6 * t + j, t, me, c, (*chip, c), src=mine))
        for cp in first:
            cp.start()
        for t in range(n):
            for j, (cx, cy) in enumerate(chips):
                copy(6 * t + j, t, 2 * cx + cy, c, sibling).wait_recv()
                fwd = copy(6 * t + 3 + j, t, 2 * cx + cy, c, sibling)
                fwd.start()
                passed.append(fwd)
        for t in range(n):
            for j, (cx, cy) in enumerate(chips):
                copy(6 * t + 3 + j, t, 2 * cx + cy, 1 - c, sibling).wait_recv()
        for cp in first + passed:
            cp.wait_send()

    any_spec = pl.BlockSpec(memory_space=pl.ANY)
    return pl.pallas_call(
        body, name="all_gather",
        in_specs=[any_spec] * n, out_specs=[any_spec] * n,
        out_shape=[jax.ShapeDtypeStruct((N_SHARD,) + s.shape, s.dtype) for s in shards],
        scratch_shapes=[pltpu.SemaphoreType.DMA((6 * n,)), pltpu.SemaphoreType.DMA((6 * n,))],
    )(*shards)


def _own_slot(gathered, own):
    me = 2 * lax.axis_index("x") + lax.axis_index("y")
    slot = lax.broadcasted_iota(jnp.int32, (N_SHARD,) + (1,) * own.ndim, 0)
    return jnp.where(slot == me, own[None], gathered)


def _sibling_exchange(bufs):
    n = len(bufs)

    def body(*refs):
        ins, got = refs[:n], refs[n:2 * n]
        send, recv = refs[2 * n:]
        x, y, c, _ = _place()
        cps = []
        for t in range(n):
            hr = ins[t].shape[0] // (2 * N_SHARD)
            for s in range(N_SHARD):
                k = N_SHARD * t + s
                cps.append(pltpu.make_async_remote_copy(
                    src_ref=ins[t].at[pl.ds(pl.multiple_of((2 * s + 1 - c) * hr, 8), hr), :], dst_ref=got[t].at[s],
                    send_sem=send.at[k], recv_sem=recv.at[k], device_id=(x, y, 1 - c), device_id_type=MESH))
        for cp in cps:
            cp.start()
        for cp in cps:
            cp.wait()

    any_spec = pl.BlockSpec(memory_space=pl.ANY)
    return pl.pallas_call(
        body, name="sibling_exchange",
        in_specs=[any_spec] * n, out_specs=[any_spec] * n,
        out_shape=[jax.ShapeDtypeStruct((N_SHARD, b.shape[0] // (2 * N_SHARD), b.shape[1]), b.dtype) for b in bufs],
        scratch_shapes=[pltpu.SemaphoreType.DMA((N_SHARD * n,))] * 2,
    )(*bufs)


def _half_sum(bufs, got, wire, c):
    n = len(bufs)

    def body(c_ref, *refs):
        ins, gots, outs = refs[:n], refs[n:2 * n], refs[2 * n:]
        for t in range(n):
            outs[t][0] = (ins[t][...] + gots[t][0]).astype(outs[t].dtype)

    slot = lambda g: pl.BlockSpec((1,) + tuple(g.shape[1:]), lambda s, c_ref: (s, 0, 0))
    grid_spec = pltpu.PrefetchScalarGridSpec(
        num_scalar_prefetch=1, grid=(N_SHARD,),
        in_specs=[pl.BlockSpec(tuple(g.shape[1:]), lambda s, c_ref: (2 * s + c_ref[0], 0)) for g in got]
                 + [slot(g) for g in got],
        out_specs=[slot(g) for g in got])
    return pl.pallas_call(
        body, name="half_sum", grid_spec=grid_spec,
        out_shape=[jax.ShapeDtypeStruct(g.shape, d) for g, d in zip(got, wire)],
        compiler_params=_params(("arbitrary",)),
    )(c, *bufs, *got)


def _chip_exchange(parts):
    n = len(parts)

    def body(*refs):
        ins, outs = refs[:n], refs[n:2 * n]
        send, recv = refs[2 * n:]
        x, y, c, chips = _place()
        me = 2 * x + y
        cps = []
        for t in range(n):
            for j, (cx, cy) in enumerate(chips):
                cps.append(pltpu.make_async_remote_copy(
                    src_ref=ins[t].at[2 * cx + cy], dst_ref=outs[t].at[me],
                    send_sem=send.at[3 * t + j], recv_sem=recv.at[3 * t + j],
                    device_id=(cx, cy, c), device_id_type=MESH))
        for cp in cps:
            cp.start()
        for cp in cps:
            cp.wait()

    any_spec = pl.BlockSpec(memory_space=pl.ANY)
    return pl.pallas_call(
        body, name="chip_exchange",
        in_specs=[any_spec] * n, out_specs=[any_spec] * n,
        out_shape=[jax.ShapeDtypeStruct(p.shape, p.dtype) for p in parts],
        scratch_shapes=[pltpu.SemaphoreType.DMA((3 * n,))] * 2,
    )(*parts)


def _chip_sum(parts, got, me):
    n = len(got)

    def body(me_ref, *refs):
        owns, gots, outs = refs[:n], refs[n:2 * n], refs[2 * n:]
        s = pl.program_id(0)
        for t in range(n):
            term = jnp.where(s == me_ref[0], owns[t][0], gots[t][0]).astype(_F32)

            @pl.when(s == 0)
            def _():
                outs[t][...] = term

            @pl.when(s > 0)
            def _():
                outs[t][...] += term

    blk = lambda g: (1,) + tuple(g.shape[1:])
    grid_spec = pltpu.PrefetchScalarGridSpec(
        num_scalar_prefetch=1, grid=(N_SHARD,),
        in_specs=[pl.BlockSpec(blk(g), lambda s, me_ref: (me_ref[0], 0, 0)) for g in got]
                 + [pl.BlockSpec(blk(g), lambda s, me_ref: (jnp.where(s == me_ref[0], (s + 1) % N_SHARD, s), 0, 0))
                    for g in got],
        out_specs=[pl.BlockSpec(tuple(g.shape[1:]), lambda s, me_ref: (0, 0)) for g in got])
    return pl.pallas_call(
        body, name="chip_sum", grid_spec=grid_spec,
        out_shape=[jax.ShapeDtypeStruct(g.shape[1:], _F32) for g in got],
        compiler_params=_params(("arbitrary",)),
    )(me, *parts, *got)


def _sibling_share(halves):
    n = len(halves)

    def body(*refs):
        ins, outs = refs[:n], refs[n:2 * n]
        send, recv = refs[2 * n:]
        x, y, c, _ = _place()
        cps = [pltpu.make_async_remote_copy(
            src_ref=ins[t], dst_ref=outs[t], send_sem=send.at[t], recv_sem=recv.at[t],
            device_id=(x, y, 1 - c), device_id_type=MESH) for t in range(n)]
        for cp in cps:
            cp.start()
        for cp in cps:
            cp.wait()

    any_spec = pl.BlockSpec(memory_space=pl.ANY)
    return pl.pallas_call(
        body, name="sibling_share",
        in_specs=[any_spec] * n, out_specs=[any_spec] * n,
        out_shape=[jax.ShapeDtypeStruct(h.shape, h.dtype) for h in halves],
        scratch_shapes=[pltpu.SemaphoreType.DMA((n,))] * 2,
    )(*halves)


def _reduce_scatter(bufs, wire):
    x, y, c = lax.axis_index("x"), lax.axis_index("y"), lax.axis_index("c")
    as_operand = lambda i: jnp.asarray(i, jnp.int32).reshape(1)
    chip_part = _half_sum(bufs, _sibling_exchange(bufs), wire, as_operand(c))
    half = _chip_sum(chip_part, _chip_exchange(chip_part), as_operand(2 * x + y))
    other = _sibling_share(half)
    return [jnp.concatenate([jnp.where(c == 0, h, o), jnp.where(c == 0, o, h)], axis=0) for h, o in zip(half, other)]


REPL_SIZES = (D_MODEL, D_MODEL, N_HEADS, D_CONV, D_CONV, D_CONV, D_MODEL, D_MODEL)
REPL_ROWS = 48


def _pack_repl(parts):
    flat = jnp.concatenate([p.reshape(-1).astype(_F32) for p in parts])
    return jnp.pad(flat, (0, REPL_ROWS * 128 - flat.shape[0])).reshape(REPL_ROWS, 128)


def _unpack_repl(packed, shapes):
    flat = packed.reshape(-1)
    out, off = [], 0
    for shape, size in zip(shapes, REPL_SIZES):
        out.append(flat[off:off + size].reshape(shape))
        off += size
    return out


def _projection_operand(w_in_t):
    zrow = jnp.zeros((128 - N_HEADS, D_MODEL), w_in_t.dtype)
    return jnp.concatenate([w_in_t[:OFF_F], w_in_t[OFF_F + N_HEADS:], w_in_t[OFF_F:OFF_F + N_HEADS], zrow], axis=0)


def _local_grads(x, target, meta, ln_in_g, ln_in_b, w_in_t, b_f, conv_w, conv_b, ln_conv_g, ln_conv_b,
                 w_pw, w_out, ln_out_g, ln_out_b):
    vec = lambda a: a.reshape(1, -1).astype(_F32)
    head = jnp.concatenate([jnp.zeros((ROW_PAD, D_MODEL), _F32), meta], axis=0)
    wpt = _projection_operand(w_in_t)
    bf = jnp.pad(vec(b_f), ((0, 0), (0, 128 - N_HEADS)))
    cw = jnp.pad(conv_w, ((0, 32 - CONV_WIDTH), (0, 0)))

    h, q, k, v, ga, u, ug, gc, f, vt = _ln_proj(x, head, vec(ln_in_g), vec(ln_in_b), wpt)
    ccol = _decay_cumsum(f, bf)
    o, lse = _attn_fwd(q, k, vt, ccol)
    z, pw = _conv_fwd(u, ug, cw, vec(conv_b), vec(ln_conv_g), vec(ln_conv_b), w_pw)
    dr, do, dga, dpw, dgc, delta, dw_out, dg_out, db_out, loss = _out_loss_bwd(
        o, ga, pw, gc, h, target, w_out, vec(ln_out_g), vec(ln_out_b))
    dq, dk, dv, dck, dcq = _attn_bwd(q, k, v, do, lse, delta, ccol)
    df, dbf = _decay_bwd(dck, dcq, f, bf)
    dz, dw_pw, dg_conv, db_conv, dconv_b = _conv_bwd_rows(dpw, z, vec(ln_conv_g), vec(ln_conv_b), w_pw)
    du, dug, dcw = _conv_bwd_taps(dz, u, ug, cw)
    pieces = (dq, dk, dv, dga, du, dug, dgc)
    grad_x, dmeta, dg_in, db_in = _proj_bwd_x(pieces, df, wpt, dr, x, head, vec(ln_in_g))
    dwq, dwk, dwv, dwga = _proj_bwd_w(h, pieces[:4], "proj_bwd_w_attn")
    dwu, dwug, dwgc, dwf = _proj_bwd_w(h, pieces[4:] + (df,), "proj_bwd_w_conv")
    dw_in_t = jnp.concatenate([dwq, dwk, dwv, dwf[:N_HEADS], dwga, dwu, dwug, dwgc], axis=0)
    dw_in_t = jnp.pad(dw_in_t.reshape(N_SHARD, W_IN_SHARD, D_MODEL), ((0, 0), (0, W_IN_SLOT - W_IN_SHARD), (0, 0)))
    dw_in_t = dw_in_t.reshape(N_SHARD * W_IN_SLOT, D_MODEL)
    grads = dict(meta=dmeta, ln_in_g=dg_in, ln_in_b=db_in, w_in_t=dw_in_t, b_f=dbf[:, :N_HEADS],
                 conv_w=dcw[:CONV_WIDTH], conv_b=dconv_b, ln_conv_g=dg_conv, ln_conv_b=db_conv,
                 w_pw=dw_pw, w_out=dw_out, ln_out_g=dg_out, ln_out_b=db_out)
    return loss, grad_x, grads


def kernel(x, meta, ln_in_g, ln_in_b, w_in, b_f, conv_w, conv_b, ln_conv_g, ln_conv_b, w_pw, w_out, ln_out_g, ln_out_b, loss_target, m_meta, m_ln_in_g, m_ln_in_b, m_w_in, m_b_f, m_conv_w, m_conv_b, m_ln_conv_g, m_ln_conv_b, m_w_pw, m_w_out, m_ln_out_g, m_ln_out_b, v_meta, v_ln_in_g, v_ln_in_b, v_w_in, v_b_f, v_conv_w, v_conv_b, v_ln_conv_g, v_ln_conv_b, v_w_pw, v_w_out, v_ln_out_g, v_ln_out_b):
    conv_w2, w_pw2, w_out2 = conv_w[0], w_pw[0], w_out[0]
    w_t = jnp.transpose(w_in[0])

    small = jnp.concatenate([jnp.pad(conv_w2, ((0, 1), (0, 0))), meta.reshape(32, 128)], axis=0)
    shards = [jnp.pad(w_t.astype(_MM), ((0, W_IN_SLOT - W_IN_SHARD), (0, 0))), w_pw2.astype(_MM), w_out2.astype(_MM), small]
    g_in, g_pw, g_out, g_small = (_own_slot(g, s) for g, s in zip(_all_gather(shards), shards))
    w_in_t_full = g_in[:, :W_IN_SHARD, :].reshape(D_IN, D_MODEL)
    w_pw_full = g_pw.reshape(D_CONV, D_CONV)
    w_out_full = g_out.reshape(D_MODEL, D_MODEL)
    conv_w_full = jnp.transpose(g_small[:, :CONV_WIDTH, :], (1, 0, 2)).reshape(CONV_WIDTH, D_CONV)
    meta_full = jnp.transpose(g_small[:, 32:64, :].reshape(N_SHARD, N_META, 256), (1, 0, 2)).reshape(N_META, D_MODEL)

    loss, grad_x, gr = _local_grads(x[0], loss_target[0], meta_full, ln_in_g, ln_in_b, w_in_t_full, b_f, conv_w_full,
                                    conv_b, ln_conv_g, ln_conv_b, w_pw_full, w_out_full, ln_out_g, ln_out_b)
    grad_x = grad_x[None]

    repl_names = ("ln_in_g", "ln_in_b", "b_f", "conv_b", "ln_conv_g", "ln_conv_b", "ln_out_g", "ln_out_b")
    repl = _pack_repl([gr[n] for n in repl_names] + [loss])
    gcw = jnp.transpose(jnp.pad(gr["conv_w"], ((0, 1), (0, 0))).reshape(32, N_SHARD, 128), (1, 0, 2))
    gmeta = jnp.transpose(gr["meta"].reshape(N_META, N_SHARD, 256), (1, 0, 2)).reshape(N_SHARD, 32, 128)
    small_g = jnp.concatenate([gcw, gmeta, jnp.broadcast_to(repl[None], (N_SHARD, REPL_ROWS, 128))], axis=1)
    small_rows = small_g.shape[1]
    bufs = [gr["w_in_t"], gr["w_pw"], gr["w_out"], small_g.reshape(N_SHARD * small_rows, 128)]
    gw_slot, gw_pw, gw_out, gsmall = _reduce_scatter(bufs, [_MM, _MM, _MM, _F32])
    g_conv_w = gsmall[:CONV_WIDTH]
    g_meta = gsmall[32:64].reshape(N_META, 256)
    g_repl = gsmall[64:64 + REPL_ROWS]

    repl_w = (ln_in_g, ln_in_b, b_f, conv_b, ln_conv_g, ln_conv_b, ln_out_g, ln_out_b)
    repl_m = (m_ln_in_g, m_ln_in_b, m_b_f, m_conv_b, m_ln_conv_g, m_ln_conv_b, m_ln_out_g, m_ln_out_b)
    repl_v = (v_ln_in_g, v_ln_in_b, v_b_f, v_conv_b, v_ln_conv_g, v_ln_conv_b, v_ln_out_g, v_ln_out_b)
    zero = jnp.zeros((1,), _F32)
    res_t = _adamw_tiled(w_t, gw_slot, jnp.transpose(m_w_in[0]), jnp.transpose(v_w_in[0]), 256, "adamw_w_in")
    gw_in, d_in, nm_in, nv_in = (jnp.transpose(a) for a in res_t)
    ws = [w_pw2, w_out2, conv_w2, meta, _pack_repl(repl_w + (zero,))]
    gs = [gw_pw, gw_out, g_conv_w, g_meta, g_repl]
    ms = [m_w_pw[0], m_w_out[0], m_conv_w[0], m_meta, _pack_repl(repl_m + (zero,))]
    vs = [v_w_pw[0], v_w_out[0], v_conv_w[0], v_meta, _pack_repl(repl_v + (zero + 1.0,))]
    ds, nms, nvs = _adamw_small(ws, gs, ms, vs)

    shapes = [w.shape for w in repl_w]
    total_loss = g_repl.reshape(-1)[sum(REPL_SIZES)]
    names = ("meta", "ln_in_g", "ln_in_b", "w_in", "b_f", "conv_w", "conv_b", "ln_conv_g", "ln_conv_b",
             "w_pw", "w_out", "ln_out_g", "ln_out_b")

    def assemble(t_in, smalls):
        pw_, out_, cw_, meta_, packed = smalls
        r = dict(zip(repl_names, _unpack_repl(packed, shapes)))
        r.update(meta=meta_, w_in=t_in[None], conv_w=cw_[None], w_pw=pw_[None], w_out=out_[None])
        return [r[n] for n in names]

    return (total_loss, grad_x, *assemble(gw_in, gs), *assemble(d_in, ds), *assemble(nm_in, nms),
            *assemble(nv_in, nvs))
```

```python
import functools

import jax
import jax.numpy as jnp
from jax import lax
from jax.experimental import pallas as pl
from jax.experimental.pallas import tpu as pltpu

D_MODEL = 1024
N_META = 16
D_ATTN = 512
D_CONV = 512
N_HEADS = 8
HEAD_DIM = 64
N_PAIRS = 4
CONV_WIDTH = 31
LN_EPS = 1e-5
ALPHA = 2.0 ** 0.25
ATTN_SCALE = 0.125
D_IN = 3592
OFF_F = 3 * D_ATTN
D_IN_PAD = 3712
N_SHARD = 4
W_IN_SHARD = D_IN // N_SHARD
W_IN_SLOT = 928

ROW_PAD = 112
ROW_X = ROW_PAD + N_META
HALO = 32
MASKED = 1e30

ADAM_LR = 0.001
ADAM_B1 = 0.9
ADAM_B2 = 0.999
ADAM_EPS = 1e-08
ADAM_WD = 0.01
ADAM_STEP = 10

VMEM_LIMIT = 56 * 1024 * 1024

_MM = jnp.bfloat16
_F32 = jnp.float32
MESH = pl.DeviceIdType.MESH


def _params(sem=None):
    return pltpu.CompilerParams(dimension_semantics=sem, vmem_limit_bytes=VMEM_LIMIT)


def _row_tile(n_rows):
    for t in (384, 256, 128):
        if n_rows % t == 0:
            return t
    raise ValueError(f"padded sequence length {n_rows} is not a multiple of 128")


def _sigmoid(x):
    return 1.0 / (1.0 + jnp.exp(-x))


def _ln_stats(x):
    mu = jnp.mean(x, axis=-1, keepdims=True)
    xc = x - mu
    var = jnp.mean(xc * xc, axis=-1, keepdims=True)
    rstd = lax.rsqrt(var + LN_EPS)
    return xc * rstd, rstd


def _ln_bwd(dy, xhat, rstd, g):
    dxh = dy * g
    m1 = jnp.mean(dxh, axis=-1, keepdims=True)
    m2 = jnp.mean(dxh * xhat, axis=-1, keepdims=True)
    return rstd * (dxh - m1 - xhat * m2)


def _dot(a, b):
    return jnp.dot(a, b, preferred_element_type=_F32)


def _dot_t0(a, b):
    return lax.dot_general(a, b, (((0,), (0,)), ((), ())), preferred_element_type=_F32)


def _dot_t1(a, b):
    return lax.dot_general(a, b, (((1,), (1,)), ((), ())), preferred_element_type=_F32)


def _dot_exact(a, b):
    return jnp.dot(a, b, preferred_element_type=_F32, precision=lax.Precision.HIGHEST)


def _full(shape):
    return pl.BlockSpec(shape, lambda *_: (0,) * len(shape))


def _x_specs(t):
    n = t // ROW_X
    return [pl.BlockSpec((ROW_X, D_MODEL), lambda i, s=s: (jnp.maximum(i * n + s - 1, 0), 0)) for s in range(n)]


def _x_tile(i, refs, head_ref=None):
    first = refs[0][...]
    if head_ref is not None:
        first = jnp.where(i == 0, head_ref[...], first)
    return jnp.concatenate([first] + [r[...] for r in refs[1:]], axis=0)


def _ln_proj(x, head, g, b, wpt):
    lp = ROW_X + x.shape[0]
    t = _row_tile(lp)
    nx = t // ROW_X

    def body(*refs):
        x_refs, (head_ref, g_ref, b_ref, wt_ref) = refs[:nx], refs[nx:nx + 4]
        h_ref, q_ref, k_ref, v_ref, ga_ref, u_ref, ug_ref, gc_ref, f_ref, vt_ref = refs[nx + 4:]
        i = pl.program_id(0)
        xhat, _ = _ln_stats(_x_tile(i, x_refs, head_ref))
        rows = i * t + lax.broadcasted_iota(jnp.int32, (t, 1), 0)
        h = jnp.where(rows >= ROW_PAD, xhat * g_ref[...] + b_ref[...], 0.0)
        h_ref[...] = h
        hb = h.astype(_MM)
        q_ref[...] = (_dot_t1(hb, wt_ref[0:512, :]) * ATTN_SCALE).astype(q_ref.dtype)
        k_ref[...] = _dot_t1(hb, wt_ref[512:1024, :]).astype(k_ref.dtype)
        v = _dot_t1(hb, wt_ref[1024:1536, :])
        v_ref[...] = v.astype(v_ref.dtype)
        vt_ref[...] = v.T.astype(vt_ref.dtype)
        ga_ref[...] = _dot_t1(hb, wt_ref[1536:2048, :])
        u_ref[...] = _dot_t1(hb, wt_ref[2048:2560, :])
        ug_ref[...] = _dot_t1(hb, wt_ref[2560:3072, :])
        gc_ref[...] = _dot_t1(hb, wt_ref[3072:3584, :])
        f_ref[...] = _dot_t1(hb, wt_ref[3584:3712, :])

    row = lambda w: pl.BlockSpec((t, w), lambda i: (i, 0))
    f32 = lambda w: jax.ShapeDtypeStruct((lp, w), _F32)
    mm = lambda w: jax.ShapeDtypeStruct((lp, w), _MM)
    return pl.pallas_call(
        body, name="ln_proj", grid=(lp // t,),
        in_specs=_x_specs(t) + [_full((ROW_X, D_MODEL)), _full((1, D_MODEL)), _full((1, D_MODEL)),
                                _full((D_IN_PAD, D_MODEL))],
        out_specs=[row(D_MODEL)] + [row(512)] * 7 + [row(128), pl.BlockSpec((512, t), lambda i: (0, i))],
        out_shape=[f32(D_MODEL), mm(512), mm(512), mm(512), f32(512), f32(512), f32(512), f32(512), f32(128),
                   jax.ShapeDtypeStruct((512, lp), _MM)],
        compiler_params=_params(("arbitrary",)),
    )(*([x] * nx), head, g, b, wpt)


def _log_sigmoid(z):
    return jnp.minimum(z, 0.0) - jnp.log(1.0 + jnp.exp(-jnp.abs(z)))


def _decay_cumsum(f, bf):
    lp = f.shape[0]
    nb = lp // 128

    def body(f_ref, bf_ref, ccol_ref):
        r = lax.broadcasted_iota(jnp.int32, (128, 128), 0)
        c = lax.broadcasted_iota(jnp.int32, (128, 128), 1)
        tril = (c <= r).astype(_F32)

        def step(bi, carry):
            rows = pl.ds(pl.multiple_of(bi * 128, 128), 128)
            lf = _log_sigmoid(f_ref[rows, :] + bf_ref[...])
            cb = _dot_exact(tril, lf) + carry
            for p in range(N_PAIRS):
                ccol_ref[p, rows, :] = cb if p == 0 else pltpu.roll(cb, 128 - 2 * p, axis=1)
            return cb[127:128, :]

        lax.fori_loop(0, nb, step, jnp.zeros((1, 128), _F32))

    return pl.pallas_call(
        body, name="decay_cumsum",
        out_shape=jax.ShapeDtypeStruct((N_PAIRS, lp, 128), _F32),
        compiler_params=_params(),
    )(f, bf)


def _split3(x):
    hi = x.astype(_MM)
    r1 = x - hi.astype(_F32)
    mid = r1.astype(_MM)
    return hi, mid, (r1 - mid.astype(_F32)).astype(_MM)


def _augment(x, hd, first, second, transposed=False):
    lane = lax.broadcasted_iota(jnp.int32, (1, 128), 1)
    base = HEAD_DIM * (1 - hd)
    out = jnp.where(lane // HEAD_DIM == hd, x, jnp.zeros_like(x))
    for slot, val in enumerate((first, second)):
        lo = base + 3 * slot
        if val is None:
            continue
        if isinstance(val, float):
            out = jnp.where((lane >= lo) & (lane < lo + 3), jnp.asarray(val, x.dtype), out)
        else:
            for n, piece in enumerate(val):
                out = jnp.where(lane == lo + n, piece[:, hd:hd + 1], out)
    return _transpose_mm(out) if transposed else out


def _transpose_mm(x):
    return x.astype(_F32).T.astype(_MM)


def _diag_valid(i, t):
    sub = lax.broadcasted_iota(jnp.int32, (t, t), 0)
    return (sub <= lax.broadcasted_iota(jnp.int32, (t, t), 1)) & (i * t + sub >= ROW_PAD)


def _key_pieces(c, first_row):
    pos = first_row + lax.broadcasted_iota(jnp.int32, (c.shape[0], 1), 0)
    return _split3(jnp.where(pos >= ROW_PAD, -c, -MASKED))


def _key_operand(k, pieces, hd, transposed=False):
    return _augment(k, hd, pieces, 1.0, transposed)


def _attn_fwd(q, k, vt, ccol):
    lp = q.shape[0]
    t = _row_tile(lp)
    nb = lp // t

    def body(q_ref, k_ref, vt_ref, ccol_ref, o_ref, lse_ref, ka_ref):
        i = pl.program_id(1)

        @pl.when(i == 0)
        def _():
            for n in range(nb):
                rows = slice(n * t, (n + 1) * t)
                pieces = _key_pieces(ccol_ref[0, rows, :], n * t)
                for hd in range(2):
                    ka_ref[hd, rows, :] = _key_operand(k_ref[rows, :], pieces, hd)

        qats = [_augment(q_ref[...], hd, 1.0, None, transposed=True) for hd in range(2)]

        def scores(j):
            keys = pl.ds(pl.multiple_of(j * t, 128), t)
            return tuple(_dot(ka_ref[hd, keys, :], qats[hd]) for hd in range(2))

        def update(j, sts, states, diag):
            vt = vt_ref[:, pl.ds(pl.multiple_of(j * t, 128), t)]
            new = []
            for st, (m, l, acc) in zip(sts, states):
                if diag:
                    st = jnp.where(_diag_valid(i, t), st, -MASKED)
                m_new = jnp.maximum(m, jnp.max(st, axis=0, keepdims=True))
                a = jnp.exp(m - m_new)
                pt = jnp.exp(st - m_new)
                l = a * l + jnp.sum(pt, axis=0, keepdims=True)
                new.append((m_new, l, a * acc + _dot(vt, pt.astype(_MM))))
            return tuple(new)

        def step(j, carry):
            sts, states = carry
            return scores(j + 1), update(j, sts, states, False)

        init = (jnp.full((1, t), -MASKED, _F32), jnp.zeros((1, t), _F32), jnp.zeros((128, t), _F32))
        def step2(n, carry):
            sts, states = carry
            nxt, last = scores(2 * n + 1), scores(2 * n + 2)
            return last, update(2 * n + 1, nxt, update(2 * n, sts, states, False), False)

        carry = lax.fori_loop(0, i // 2, step2, (scores(0), (init, init)))
        sts, states = lax.fori_loop(0, i % 2, lambda _, c: step(i - 1, c), carry)
        (m0, l0, acc0), (m1, l1, acc1) = update(i, sts, states, True)
        row = lax.broadcasted_iota(jnp.int32, (128, t), 0)
        o_ref[...] = jnp.where(row < HEAD_DIM, acc0 / l0, acc1 / l1).T
        lse_ref[0] = jnp.where(row == 0, m0 + jnp.log(l0), jnp.where(row == 1, m1 + jnp.log(l1), 0.0)).T

    return pl.pallas_call(
        body, name="attn_fwd", grid=(N_PAIRS, nb),
        in_specs=[pl.BlockSpec((t, 128), lambda p, i: (i, p)),
                  pl.BlockSpec((lp, 128), lambda p, i: (0, p)),
                  pl.BlockSpec((128, lp), lambda p, i: (p, 0)),
                  pl.BlockSpec((1, lp, 128), lambda p, i: (p, 0, 0))],
        out_specs=[pl.BlockSpec((t, 128), lambda p, i: (i, p)),
                   pl.BlockSpec((1, t, 128), lambda p, i: (p, i, 0))],
        out_shape=[jax.ShapeDtypeStruct((lp, D_ATTN), _F32), jax.ShapeDtypeStruct((N_PAIRS, lp, 128), _F32)],
        scratch_shapes=[pltpu.VMEM((2, lp, 128), _MM)],
        compiler_params=_params(("arbitrary", "arbitrary")),
    )(q, k, vt, ccol)


CONV_CHUNK = 32


def _shifted_windows(win, sh, rows):
    sh[0] = win[...]
    for b in range(1, 8):
        sh[b, 0:rows - 8, :] = win[b:b + rows - 8, :]


def _tap_rows(sh, offsets, r0):
    out = {}
    for b in range(8):
        ds = [d for d in offsets if d % 8 == b]
        if ds:
            lo, hi = min(ds) // 8, max(ds) // 8
            rows = sh[b, pl.ds(pl.multiple_of(r0 + 8 * lo, 8), CONV_CHUNK + 8 * (hi - lo)), :]
            for d in ds:
                out[d] = rows[8 * (d // 8 - lo):8 * (d // 8 - lo) + CONV_CHUNK, :]
    return out


def _conv_fwd(u, ug, cw, cb, g, b, wpw):
    lp = u.shape[0]
    t = _row_tile(lp)

    def body(u_ref, ug_ref, cw_ref, cb_ref, g_ref, b_ref, wpw_ref, z_ref, pw_ref, win, sh):
        i = pl.program_id(0)

        @pl.when(i == 0)
        def _():
            win[0:HALO, :] = jnp.zeros((HALO, D_CONV), _F32)

        win[HALO:HALO + t, :] = u_ref[...] * _sigmoid(ug_ref[...])
        _shifted_windows(win, sh, t + HALO)

        def chunk(c, carry):
            r0 = c * CONV_CHUNK
            acc = jnp.broadcast_to(cb_ref[...], (CONV_CHUNK, D_CONV))
            first = HALO - (CONV_WIDTH - 1)
            taps = _tap_rows(sh, range(first, first + CONV_WIDTH), r0)
            for j in range(CONV_WIDTH):
                acc = acc + cw_ref[j:j + 1, :] * taps[first + j]
            z_ref[pl.ds(pl.multiple_of(r0, 8), CONV_CHUNK), :] = acc
            return carry

        lax.fori_loop(0, t // CONV_CHUNK, chunk, 0)
        xhat, _ = _ln_stats(z_ref[...])
        zn = xhat * g_ref[...] + b_ref[...]
        a = zn * _sigmoid(zn)
        pw_ref[...] = _dot(a.astype(_MM), wpw_ref[...])
        win[0:HALO, :] = win[t:t + HALO, :]

    row = pl.BlockSpec((t, D_CONV), lambda i: (i, 0))
    vec = _full((1, D_CONV))
    return pl.pallas_call(
        body, name="conv_fwd", grid=(lp // t,),
        in_specs=[row, row, _full((32, D_CONV)), vec, vec, vec, _full((D_CONV, D_CONV))],
        out_specs=[row, row],
        out_shape=[jax.ShapeDtypeStruct((lp, D_CONV), _F32)] * 2,
        scratch_shapes=[pltpu.VMEM((t + HALO, D_CONV), _F32), pltpu.VMEM((8, t + HALO, D_CONV), _F32)],
        compiler_params=_params(("arbitrary",)),
    )(u, ug, cw, cb, g, b, wpw)


def _out_loss_bwd(o, ga, pw, gc, h, target, w_out, g, b):
    lp = o.shape[0]
    t = _row_tile(lp)
    nx = t // ROW_X

    def body(*refs):
        o_ref, ga_ref, pw_ref, gc_ref, h_ref = refs[:5]
        tgt_refs, (w_ref, g_ref, b_ref) = refs[5:5 + nx], refs[5 + nx:8 + nx]
        dr_ref, do_ref, dga_ref, dpw_ref, dgc_ref, delta_ref, dw_ref, dg_ref, db_ref, loss_ref = refs[8 + nx:]
        i = pl.program_id(0)

        @pl.when(i == 0)
        def _():
            dw_ref[...] = jnp.zeros_like(dw_ref)
            dg_ref[...] = jnp.zeros_like(dg_ref)
            db_ref[...] = jnp.zeros_like(db_ref)
            loss_ref[...] = jnp.zeros_like(loss_ref)

        o, ga, pw, gc = o_ref[...], ga_ref[...], pw_ref[...], gc_ref[...]
        sa, sc = _sigmoid(ga), _sigmoid(gc)
        silu_a, silu_c = ga * sa, gc * sc
        ycat = jnp.concatenate([o * silu_a, pw * silu_c], axis=-1).astype(_MM)
        r = ALPHA * h_ref[...] + _dot(ycat, w_ref[...])
        xhat, rstd = _ln_stats(r)
        out = xhat * g_ref[...] + b_ref[...]
        rows = i * t + lax.broadcasted_iota(jnp.int32, (t, 1), 0)
        diff = jnp.where(rows >= ROW_X, out - _x_tile(i, tgt_refs), 0.0)
        loss_ref[...] += (0.5 / D_MODEL) * jnp.sum(diff * diff, keepdims=True)
        dout = diff * (1.0 / D_MODEL)
        dg_ref[...] += jnp.sum(dout * xhat, axis=0, keepdims=True)
        db_ref[...] += jnp.sum(dout, axis=0, keepdims=True)
        dr = _ln_bwd(dout, xhat, rstd, g_ref[...])
        dr_ref[...] = dr
        drb = dr.astype(_MM)
        dw_ref[...] += _dot_t0(ycat, drb)
        dy = _dot_t1(drb, w_ref[...])
        dya, dyc = dy[:, :D_ATTN], dy[:, D_ATTN:]
        do = dya * silu_a
        do_ref[...] = do.astype(do_ref.dtype)
        dga_ref[...] = (dya * o * (sa * (1.0 + ga * (1.0 - sa)))).astype(dga_ref.dtype)
        dpw_ref[...] = (dyc * silu_c).astype(dpw_ref.dtype)
        dgc_ref[...] = (dyc * pw * (sc * (1.0 + gc * (1.0 - sc)))).astype(dgc_ref.dtype)
        sel = (lax.broadcasted_iota(jnp.int32, (128, 128), 0) // HEAD_DIM
               == lax.broadcasted_iota(jnp.int32, (128, 128), 1)).astype(_F32)
        prod = do * o
        for p in range(N_PAIRS):
            delta_ref[p] = _dot_exact(prod[:, 128 * p:128 * (p + 1)], sel)

    row = lambda w: pl.BlockSpec((t, w), lambda i: (i, 0))
    return pl.pallas_call(
        body, name="out_loss_bwd", grid=(lp // t,),
        in_specs=[row(512)] * 4 + [row(D_MODEL)] + _x_specs(t) + [_full((D_MODEL, D_MODEL))] + [_full((1, D_MODEL))] * 2,
        out_specs=[row(D_MODEL)] + [row(512)] * 4 + [pl.BlockSpec((N_PAIRS, t, 128), lambda i: (0, i, 0)),
                   _full((D_MODEL, D_MODEL)), _full((1, D_MODEL)), _full((1, D_MODEL)), _full((1, 1))],
        out_shape=[jax.ShapeDtypeStruct((lp, D_MODEL), _F32)] + [jax.ShapeDtypeStruct((lp, 512), _MM)] * 4
                  + [jax.ShapeDtypeStruct((N_PAIRS, lp, 128), _F32), jax.ShapeDtypeStruct((D_MODEL, D_MODEL), _F32),
                     jax.ShapeDtypeStruct((1, D_MODEL), _F32), jax.ShapeDtypeStruct((1, D_MODEL), _F32),
                     jax.ShapeDtypeStruct((1, 1), _F32)],
        compiler_params=_params(("arbitrary",)),
    )(o, ga, pw, gc, h, *([target] * nx), w_out, g, b)


def _attn_bwd(q, k, v, do, lse, delta, ccol):
    lp = q.shape[0]
    t = _row_tile(lp)
    nb = lp // t

    def body(q_ref, k_ref, v_ref, do_ref, lse_ref, delta_ref, ccol_ref,
             dq_ref, dk_ref, dv_ref, dck_ref, dcq_ref,
             qa_ref, qat_ref, doa_ref, doat_ref, dqt_ref):
        j = pl.program_id(1)

        @pl.when(j == 0)
        def _():
            dqt_ref[...] = jnp.zeros_like(dqt_ref)
            for n in range(nb):
                rows = slice(n * t, (n + 1) * t)
                lse_pieces, delta_pieces = _split3(-lse_ref[0, rows, :]), _split3(-delta_ref[0, rows, :])
                for hd in range(2):
                    qa = _augment(q_ref[rows, :], hd, 1.0, lse_pieces)
                    doa = _augment(do_ref[rows, :], hd, delta_pieces, None)
                    qa_ref[hd, rows, :] = qa
                    qat_ref[hd, :, rows] = _transpose_mm(qa)
                    doa_ref[hd, rows, :] = doa
                    doat_ref[hd, :, rows] = _transpose_mm(doa)

        pieces = _key_pieces(ccol_ref[0], j * t)
        kas = [_key_operand(k_ref[...], pieces, hd) for hd in range(2)]
        kats = [_transpose_mm(ka) for ka in kas]
        vas = [_augment(v_ref[...], hd, 1.0, None) for hd in range(2)]

        def block(i, carry, diag, width=1):
            rows = pl.ds(pl.multiple_of(i * t, 128), width * t)
            new = []
            for hd in range(2):
                dk, dv = carry[hd]
                st = _dot(kas[hd], qat_ref[hd, :, rows])
                if diag:
                    st = jnp.where(_diag_valid(j, t), st, -MASKED)
                pt = jnp.exp(st)
                dsb = (pt * _dot(vas[hd], doat_ref[hd, :, rows])).astype(_MM)
                dv = dv + _dot(pt.astype(_MM), doa_ref[hd, rows, :])
                dk = dk + _dot(dsb, qa_ref[hd, rows, :])
                dqt_ref[hd, :, rows] += _dot(kats[hd], dsb)
                new.append((dk, dv))
            return tuple(new)

        zero = jnp.zeros((t, 128), _F32)
        carry = block(j, ((zero, zero), (zero, zero)), True)
        rest = nb - 1 - j
        if nb >= 3:
            carry = lax.fori_loop(0, rest // 2, lambda n, c: block(j + 1 + 2 * n, c, False, 2), carry)
            rest = rest % 2
        (dk0, dv0), (dk1, dv1) = lax.fori_loop(nb - rest, nb, functools.partial(block, diag=False), carry)
        lane = lax.broadcasted_iota(jnp.int32, (1, 128), 1)
        dk_ref[...] = jnp.where(lane < HEAD_DIM, dk0, dk1).astype(dk_ref.dtype)
        dv_ref[...] = jnp.where(lane < HEAD_DIM, dv0, dv1).astype(dv_ref.dtype)
        dck_ref[0] = jnp.where(lane == 0, dk0[:, HEAD_DIM:HEAD_DIM + 1], jnp.where(lane == 1, dk1[:, 0:1], 0.0))

        @pl.when(j == nb - 1)
        def _():
            row = lax.broadcasted_iota(jnp.int32, (128, t), 0)
            sub = lax.broadcasted_iota(jnp.int32, (8, t), 0)
            for n in range(nb):
                cols = slice(n * t, (n + 1) * t)
                t0, t1 = dqt_ref[0, :, cols], dqt_ref[1, :, cols]
                dq_ref[cols, :] = jnp.where(row < HEAD_DIM, t0, t1).T * ATTN_SCALE
                dcq_ref[0, :, cols] = jnp.where(sub == 0, t0[HEAD_DIM + 3:HEAD_DIM + 4, :],
                                                jnp.where(sub == 1, t1[3:4, :], 0.0))

    blk = pl.BlockSpec((t, 128), lambda p, j: (j, p))
    res = pl.BlockSpec((lp, 128), lambda p, j: (0, p))
    stat = pl.BlockSpec((1, lp, 128), lambda p, j: (p, 0, 0))
    stat_blk = pl.BlockSpec((1, t, 128), lambda p, j: (p, j, 0))
    return pl.pallas_call(
        body, name="attn_bwd", grid=(N_PAIRS, nb),
        in_specs=[res, blk, blk, res, stat, stat, stat_blk],
        out_specs=[res, blk, blk, stat_blk, pl.BlockSpec((1, 8, lp), lambda p, j: (p, 0, 0))],
        out_shape=[jax.ShapeDtypeStruct((lp, D_ATTN), _F32), jax.ShapeDtypeStruct((lp, D_ATTN), _MM),
                   jax.ShapeDtypeStruct((lp, D_ATTN), _MM), jax.ShapeDtypeStruct((N_PAIRS, lp, 128), _F32),
                   jax.ShapeDtypeStruct((N_PAIRS, 8, lp), _F32)],
        scratch_shapes=[pltpu.VMEM((2, lp, 128), _MM), pltpu.VMEM((2, 128, lp), _MM),
                        pltpu.VMEM((2, lp, 128), _MM), pltpu.VMEM((2, 128, lp), _MM),
                        pltpu.VMEM((2, 128, lp), _F32)],
        compiler_params=_params(("arbitrary", "arbitrary")),
    )(q, k, v, do, lse, delta, ccol)


def _decay_bwd(dck, dcq, f, bf):
    lp = f.shape[0]
    nb = lp // 128

    def body(dck_ref, dcq_ref, f_ref, bf_ref, df_ref, dbf_ref):
        r = lax.broadcasted_iota(jnp.int32, (128, 128), 0)
        c = lax.broadcasted_iota(jnp.int32, (128, 128), 1)
        triu = (c >= r).astype(_F32)

        def step(n, carry):
            tail, dbf = carry
            bi = nb - 1 - n
            rows = pl.ds(pl.multiple_of(bi * 128, 128), 128)
            cols = jnp.zeros((128, 128), _F32)
            for p in range(N_PAIRS):
                dcq = jnp.concatenate([dcq_ref[p, :, rows], jnp.zeros((120, 128), _F32)], axis=0).T
                dcp = dcq - dck_ref[p, rows, :]
                cols = cols + (dcp if p == 0 else pltpu.roll(dcp, 2 * p, axis=1))
            lane = lax.broadcasted_iota(jnp.int32, (128, 128), 1)
            cols = jnp.where(lane < N_HEADS, cols, 0.0)
            rev = _dot_exact(triu, cols) + tail
            pos = bi * 128 + lax.broadcasted_iota(jnp.int32, (128, 1), 0)
            df = jnp.where(pos >= ROW_PAD, rev * (1.0 - _sigmoid(f_ref[rows, :] + bf_ref[...])), 0.0)
            df_ref[rows, :] = df.astype(df_ref.dtype)
            return rev[0:1, :], dbf + jnp.sum(df, axis=0, keepdims=True)

        zero = jnp.zeros((1, 128), _F32)
        _, dbf = lax.fori_loop(0, nb, step, (zero, zero), unroll=True)
        dbf_ref[...] = dbf

    return pl.pallas_call(
        body, name="decay_bwd",
        out_shape=[jax.ShapeDtypeStruct((lp, 128), _MM), jax.ShapeDtypeStruct((1, 128), _F32)],
        compiler_params=_params(),
    )(dck, dcq, f, bf)


def _conv_bwd_rows(dpw, z, g, b, wpw):
    lp = z.shape[0]
    t = _row_tile(lp)

    def body(dpw_ref, z_ref, g_ref, b_ref, w_ref, dz_ref, dw_ref, dg_ref, db_ref, dcb_ref):
        i = pl.program_id(0)

        @pl.when(i == 0)
        def _():
            dw_ref[...] = jnp.zeros_like(dw_ref)
            dg_ref[...] = jnp.zeros_like(dg_ref)
            db_ref[...] = jnp.zeros_like(db_ref)
            dcb_ref[...] = jnp.zeros_like(dcb_ref)

        xhat, rstd = _ln_stats(z_ref[...])
        zn = xhat * g_ref[...] + b_ref[...]
        sg = _sigmoid(zn)
        dpw = dpw_ref[...]
        dw_ref[...] += _dot_t0((zn * sg).astype(_MM), dpw)
        dzn = _dot_t1(dpw, w_ref[...]) * (sg * (1.0 + zn * (1.0 - sg)))
        dg_ref[...] += jnp.sum(dzn * xhat, axis=0, keepdims=True)
        db_ref[...] += jnp.sum(dzn, axis=0, keepdims=True)
        dz = _ln_bwd(dzn, xhat, rstd, g_ref[...])
        dz_ref[...] = dz
        dcb_ref[...] += jnp.sum(dz, axis=0, keepdims=True)

    row = pl.BlockSpec((t, D_CONV), lambda i: (i, 0))
    vec = _full((1, D_CONV))
    vshape = jax.ShapeDtypeStruct((1, D_CONV), _F32)
    return pl.pallas_call(
        body, name="conv_bwd_rows", grid=(lp // t,),
        in_specs=[row, row, vec, vec, _full((D_CONV, D_CONV))],
        out_specs=[row, _full((D_CONV, D_CONV)), vec, vec, vec],
        out_shape=[jax.ShapeDtypeStruct((lp, D_CONV), _F32), jax.ShapeDtypeStruct((D_CONV, D_CONV), _F32),
                   vshape, vshape, vshape],
        compiler_params=_params(("arbitrary",)),
    )(dpw, z, g, b, wpw)


def _conv_bwd_taps(dz, u, ug, cw):
    lp = u.shape[0]
    t = _row_tile(lp)
    nt = lp // t

    def body(dz_ref, u_ref, ug_ref, cw_ref, du_ref, dug_ref, dcw_ref, win, sh, wacc):
        i = pl.program_id(0)

        @pl.when(i == 0)
        def _():
            win[t:t + HALO, :] = jnp.zeros((HALO, D_CONV), _F32)
            wacc[...] = jnp.zeros_like(wacc)

        win[0:t, :] = dz_ref[...]
        _shifted_windows(win, sh, t + HALO)

        def chunk(c, carry):
            r0 = c * CONV_CHUNK
            rows = pl.ds(pl.multiple_of(r0, 8), CONV_CHUNK)
            u = u_ref[rows, :]
            sg = _sigmoid(ug_ref[rows, :])
            hh = u * sg
            dhh = jnp.zeros((CONV_CHUNK, D_CONV), _F32)
            for j in range(CONV_WIDTH):
                d = CONV_WIDTH - 1 - j
                s = _tap_rows(sh, [d], r0)[d]
                dhh = dhh + cw_ref[j:j + 1, :] * s
                prod = hh * s
                wacc[j] += sum(prod[8 * n:8 * (n + 1), :] for n in range(CONV_CHUNK // 8))
            du_ref[rows, :] = (dhh * sg).astype(du_ref.dtype)
            dug_ref[rows, :] = (dhh * u * sg * (1.0 - sg)).astype(dug_ref.dtype)
            return carry

        lax.fori_loop(0, t // CONV_CHUNK, chunk, 0)
        win[t:t + HALO, :] = win[0:HALO, :]

        @pl.when(i == nt - 1)
        def _():
            dcw_ref[...] = jnp.sum(wacc[...], axis=1)

    row = pl.BlockSpec((t, D_CONV), lambda i: (nt - 1 - i, 0))
    return pl.pallas_call(
        body, name="conv_bwd_taps", grid=(nt,),
        in_specs=[row, row, row, _full((32, D_CONV))],
        out_specs=[row, row, _full((32, D_CONV))],
        out_shape=[jax.ShapeDtypeStruct((lp, D_CONV), _MM)] * 2 + [jax.ShapeDtypeStruct((32, D_CONV), _F32)],
        scratch_shapes=[pltpu.VMEM((t + HALO, D_CONV), _F32), pltpu.VMEM((8, t + HALO, D_CONV), _F32),
                        pltpu.VMEM((32, 8, D_CONV), _F32)],
        compiler_params=_params(("arbitrary",)),
    )(dz, u, ug, cw)


def _proj_bwd_x(pieces, df, wpt, dr, x, head, g, parts=()):
    seq = x.shape[0]
    lp = ROW_X + seq
    t = _row_tile(lp)
    nt = lp // t
    nx = t // ROW_X
    n_parts = len(parts)

    def body(*refs):
        dq_ref, dk_ref, dv_ref, dga_ref, du_ref, dug_ref, dgc_ref, df_ref, wt_ref, dr_ref = refs[:10]
        x_refs, (head_ref, g_ref) = refs[10:10 + nx], refs[10 + nx:12 + nx]
        part_refs, refs = refs[12 + nx:12 + nx + n_parts], refs[12 + nx + n_parts:]
        gx_ref, dmeta_ref, dg_ref, db_ref = refs[:4]
        land_refs, (buf, sem), comm_sems = refs[4:4 + n_parts], refs[4 + n_parts:6 + n_parts], refs[6 + n_parts:]
        i = pl.program_id(0)

        @pl.when(i == 0)
        def _():
            dg_ref[...] = jnp.zeros_like(dg_ref)
            db_ref[...] = jnp.zeros_like(db_ref)
            if n_parts:
                for cp in _chip_exchange_copies(part_refs, land_refs, *comm_sems):
                    cp.start()

        dh = ALPHA * dr_ref[...]
        for n, ref in enumerate((dq_ref, dk_ref, dv_ref, dga_ref, du_ref, dug_ref, dgc_ref)):
            dh = dh + _dot(ref[...].astype(_MM), wt_ref[512 * n:512 * (n + 1), :])
        dh = dh + _dot(df_ref[...], wt_ref[3584:3712, :])
        rows = i * t + lax.broadcasted_iota(jnp.int32, (t, 1), 0)
        dh = jnp.where(rows >= ROW_PAD, dh, 0.0)
        xhat, rstd = _ln_stats(_x_tile(i, x_refs, head_ref))
        dg_ref[...] += jnp.sum(dh * xhat, axis=0, keepdims=True)
        db_ref[...] += jnp.sum(dh, axis=0, keepdims=True)
        dx = _ln_bwd(dh, xhat, rstd, g_ref[...])

        def first_copy():
            return pltpu.make_async_copy(buf.at[0, pl.ds(ROW_X, t - ROW_X), :], gx_ref.at[pl.ds(0, t - ROW_X), :], sem)

        def tile_copy(n):
            return pltpu.make_async_copy(buf.at[n % 2], gx_ref.at[pl.ds(pl.multiple_of(n * t - ROW_X, ROW_X), t), :], sem)

        pl.when(i == 1)(lambda: first_copy().wait())
        pl.when(i > 1)(lambda: tile_copy(i - 1).wait())
        buf[i % 2] = dx

        @pl.when(i == 0)
        def _():
            dmeta_ref[...] = dx[ROW_PAD:ROW_X, :]
            first_copy().start()

        pl.when(i > 0)(lambda: tile_copy(i).start())
        if nt == 1:
            first_copy().wait()
        else:
            pl.when(i == nt - 1)(lambda: tile_copy(i).wait())

        @pl.when(i == nt - 1)
        def _():
            if n_parts:
                for cp in _chip_exchange_copies(part_refs, land_refs, *comm_sems):
                    cp.wait()

    row = lambda w: pl.BlockSpec((t, w), lambda i: (i, 0))
    vec = _full((1, D_MODEL))
    any_spec = pl.BlockSpec(memory_space=pl.ANY)
    return pl.pallas_call(
        body, name="proj_bwd_x", grid=(nt,),
        in_specs=[row(512)] * 7 + [row(128), _full((D_IN_PAD, D_MODEL)), row(D_MODEL)] + _x_specs(t)
                 + [_full((ROW_X, D_MODEL)), vec] + [any_spec] * n_parts,
        out_specs=[any_spec, _full((N_META, D_MODEL)), vec, vec] + [any_spec] * n_parts,
        out_shape=[jax.ShapeDtypeStruct((seq, D_MODEL), _F32), jax.ShapeDtypeStruct((N_META, D_MODEL), _F32),
                   jax.ShapeDtypeStruct((1, D_MODEL), _F32), jax.ShapeDtypeStruct((1, D_MODEL), _F32)]
                  + [jax.ShapeDtypeStruct(p.shape, p.dtype) for p in parts],
        scratch_shapes=[pltpu.VMEM((2, t, D_MODEL), _F32), pltpu.SemaphoreType.DMA(())]
                       + [pltpu.SemaphoreType.DMA((3 * n_parts,))] * (2 if n_parts else 0),
        compiler_params=_params(("arbitrary",)),
    )(*pieces, df, wpt, dr, *([x] * nx), head, g, *parts)


def _proj_bwd_w(h, pieces, name):
    lp = h.shape[0]
    t = _row_tile(lp)
    n = len(pieces)
    widths = [p.shape[1] for p in pieces]

    def body(*refs):
        h_ref, d_refs, w_refs = refs[0], refs[1:1 + n], refs[1 + n:]
        i = pl.program_id(0)

        @pl.when(i == 0)
        def _():
            for w_ref in w_refs:
                w_ref[...] = jnp.zeros_like(w_ref)

        hb = h_ref[...].astype(_MM)
        for d_ref, w_ref in zip(d_refs, w_refs):
            w_ref[...] += _dot_t0(d_ref[...].astype(_MM), hb)

    row = lambda w: pl.BlockSpec((t, w), lambda i: (i, 0))
    return pl.pallas_call(
        body, name=name, grid=(lp // t,),
        in_specs=[row(D_MODEL)] + [row(w) for w in widths],
        out_specs=[_full((w, D_MODEL)) for w in widths],
        out_shape=[jax.ShapeDtypeStruct((w, D_MODEL), _F32) for w in widths],
        compiler_params=_params(("arbitrary",)),
    )(h, *pieces)


def _adamw_math(w, g, m, v):
    m = ADAM_B1 * m + (1.0 - ADAM_B1) * g
    v = ADAM_B2 * v + (1.0 - ADAM_B2) * (g * g)
    m_hat = m / (1.0 - ADAM_B1 ** ADAM_STEP)
    v_hat = v / (1.0 - ADAM_B2 ** ADAM_STEP)
    delta = -ADAM_LR * (m_hat / (jnp.sqrt(v_hat) + ADAM_EPS) + ADAM_WD * w)
    return delta, m, v


def _adamw_tiled(w, g, m, v, rows, name):
    r, c = w.shape

    def body(w_ref, g_ref, m_ref, v_ref, go_ref, d_ref, nm_ref, nv_ref):
        g = g_ref[...]
        go_ref[...] = g
        d_ref[...], nm_ref[...], nv_ref[...] = _adamw_math(w_ref[...], g, m_ref[...], v_ref[...])

    spec = pl.BlockSpec((rows, c), lambda i: (i, 0))
    return pl.pallas_call(
        body, name=name, grid=(pl.cdiv(r, rows),), in_specs=[spec] * 4, out_specs=[spec] * 4,
        out_shape=[jax.ShapeDtypeStruct((r, c), _F32)] * 4,
        compiler_params=_params(("arbitrary",)),
    )(w, g, m, v)


def _adamw_small(ws, gs, ms, vs):
    n = len(ws)

    def body(*refs):
        ins, outs = refs[:4 * n], refs[4 * n:]
        for k in range(n):
            d, m, v = _adamw_math(ins[k][...], ins[n + k][...], ins[2 * n + k][...], ins[3 * n + k][...])
            outs[k][...], outs[n + k][...], outs[2 * n + k][...] = d, m, v

    shapes = [jax.ShapeDtypeStruct(w.shape, _F32) for w in ws]
    res = pl.pallas_call(body, name="adamw_small", out_shape=shapes * 3, compiler_params=_params())(*ws, *gs, *ms, *vs)
    return res[:n], res[n:2 * n], res[2 * n:]


def _place():
    x, y, c = lax.axis_index("x"), lax.axis_index("y"), lax.axis_index("c")
    chips = [(1 - x, y), (x, 1 - y), (1 - x, 1 - y)]
    return x, y, c, chips


def _half(ref, slot, c, rows):
    hr = rows // 2
    return ref.at[slot, pl.ds(pl.multiple_of(c * hr, 8), hr), :]


def _all_gather(shards):
    n = len(shards)

    def body(*refs):
        ins, outs = refs[:n], refs[n:2 * n]
        send, recv = refs[2 * n:]
        x, y, c, chips = _place()
        me = 2 * x + y
        sibling = (x, y, 1 - c)

        def copy(k, t, slot, half, to, src=None):
            rows = ins[t].shape[0]
            dst = _half(outs[t], slot, half, rows)
            return pltpu.make_async_remote_copy(
                src_ref=dst if src is None else src, dst_ref=dst,
                send_sem=send.at[k], recv_sem=recv.at[k], device_id=to, device_id_type=MESH)

        first, passed = [], []
        for t in range(n):
            hr = ins[t].shape[0] // 2
            mine = ins[t].at[pl.ds(pl.multiple_of(c * hr, 8), hr), :]
            for j, chip in enumerate(chips):
                first.append(copy(6 * t + j, t, me, c, (*chip, c), src=mine))
        for cp in first:
            cp.start()
        for t in range(n):
            for j, (cx, cy) in enumerate(chips):
                copy(6 * t + j, t, 2 * cx + cy, c, sibling).wait_recv()
                fwd = copy(6 * t + 3 + j, t, 2 * cx + cy, c, sibling)
                fwd.start()
                passed.append(fwd)
        for t in range(n):
            for j, (cx, cy) in enumerate(chips):
                copy(6 * t + 3 + j, t, 2 * cx + cy, 1 - c, sibling).wait_recv()
        for cp in first + passed:
            cp.wait_send()

    any_spec = pl.BlockSpec(memory_space=pl.ANY)
    return pl.pallas_call(
        body, name="all_gather",
        in_specs=[any_spec] * n, out_specs=[any_spec] * n,
        out_shape=[jax.ShapeDtypeStruct((N_SHARD,) + s.shape, s.dtype) for s in shards],
        scratch_shapes=[pltpu.SemaphoreType.DMA((6 * n,)), pltpu.SemaphoreType.DMA((6 * n,))],
    )(*shards)


def _own_slot(gathered, own):
    me = 2 * lax.axis_index("x") + lax.axis_index("y")
    slot = lax.broadcasted_iota(jnp.int32, (N_SHARD,) + (1,) * own.ndim, 0)
    return jnp.where(slot == me, own[None], gathered)


def _sibling_exchange(bufs, tag=""):
    n = len(bufs)

    def body(*refs):
        ins, got = refs[:n], refs[n:2 * n]
        send, recv = refs[2 * n:]
        x, y, c, _ = _place()
        cps = []
        for t in range(n):
            hr = ins[t].shape[0] // (2 * N_SHARD)
            for s in range(N_SHARD):
                k = N_SHARD * t + s
                cps.append(pltpu.make_async_remote_copy(
                    src_ref=ins[t].at[pl.ds(pl.multiple_of((2 * s + 1 - c) * hr, 8), hr), :], dst_ref=got[t].at[s],
                    send_sem=send.at[k], recv_sem=recv.at[k], device_id=(x, y, 1 - c), device_id_type=MESH))
        for cp in cps:
            cp.start()
        for cp in cps:
            cp.wait()

    any_spec = pl.BlockSpec(memory_space=pl.ANY)
    return pl.pallas_call(
        body, name="sibling_exchange" + tag,
        in_specs=[any_spec] * n, out_specs=[any_spec] * n,
        out_shape=[jax.ShapeDtypeStruct((N_SHARD, b.shape[0] // (2 * N_SHARD), b.shape[1]), b.dtype) for b in bufs],
        scratch_shapes=[pltpu.SemaphoreType.DMA((N_SHARD * n,))] * 2,
    )(*bufs)


def _half_sum(bufs, got, wire, c, tag=""):
    n = len(bufs)

    def body(c_ref, *refs):
        ins, gots, outs = refs[:n], refs[n:2 * n], refs[2 * n:]
        for t in range(n):
            outs[t][0] = (ins[t][...] + gots[t][0]).astype(outs[t].dtype)

    slot = lambda g: pl.BlockSpec((1,) + tuple(g.shape[1:]), lambda s, c_ref: (s, 0, 0))
    grid_spec = pltpu.PrefetchScalarGridSpec(
        num_scalar_prefetch=1, grid=(N_SHARD,),
        in_specs=[pl.BlockSpec(tuple(g.shape[1:]), lambda s, c_ref: (2 * s + c_ref[0], 0)) for g in got]
                 + [slot(g) for g in got],
        out_specs=[slot(g) for g in got])
    return pl.pallas_call(
        body, name="half_sum" + tag, grid_spec=grid_spec,
        out_shape=[jax.ShapeDtypeStruct(g.shape, d) for g, d in zip(got, wire)],
        compiler_params=_params(("arbitrary",)),
    )(c, *bufs, *got)


def _chip_exchange_copies(ins, outs, send, recv):
    x, y, c, chips = _place()
    me = 2 * x + y
    return [pltpu.make_async_remote_copy(
        src_ref=ins[t].at[2 * cx + cy], dst_ref=outs[t].at[me], send_sem=send.at[3 * t + j], recv_sem=recv.at[3 * t + j],
        device_id=(cx, cy, c), device_id_type=MESH) for t in range(len(ins)) for j, (cx, cy) in enumerate(chips)]


def _chip_exchange(parts):
    n = len(parts)

    def body(*refs):
        cps = _chip_exchange_copies(refs[:n], refs[n:2 * n], *refs[2 * n:])
        for cp in cps:
            cp.start()
        for cp in cps:
            cp.wait()

    any_spec = pl.BlockSpec(memory_space=pl.ANY)
    return pl.pallas_call(
        body, name="chip_exchange",
        in_specs=[any_spec] * n, out_specs=[any_spec] * n,
        out_shape=[jax.ShapeDtypeStruct(p.shape, p.dtype) for p in parts],
        scratch_shapes=[pltpu.SemaphoreType.DMA((3 * n,))] * 2,
    )(*parts)


def _chip_sum(parts, got, me, tag=""):
    n = len(got)

    def body(me_ref, *refs):
        owns, gots, outs = refs[:n], refs[n:2 * n], refs[2 * n:]
        s = pl.program_id(0)
        for t in range(n):
            term = jnp.where(s == me_ref[0], owns[t][0], gots[t][0]).astype(_F32)

            @pl.when(s == 0)
            def _():
                outs[t][...] = term

            @pl.when(s > 0)
            def _():
                outs[t][...] += term

    blk = lambda g: (1,) + tuple(g.shape[1:])
    grid_spec = pltpu.PrefetchScalarGridSpec(
        num_scalar_prefetch=1, grid=(N_SHARD,),
        in_specs=[pl.BlockSpec(blk(g), lambda s, me_ref: (me_ref[0], 0, 0)) for g in got]
                 + [pl.BlockSpec(blk(g), lambda s, me_ref: (jnp.where(s == me_ref[0], (s + 1) % N_SHARD, s), 0, 0))
                    for g in got],
        out_specs=[pl.BlockSpec(tuple(g.shape[1:]), lambda s, me_ref: (0, 0)) for g in got])
    return pl.pallas_call(
        body, name="chip_sum" + tag, grid_spec=grid_spec,
        out_shape=[jax.ShapeDtypeStruct(g.shape[1:], _F32) for g in got],
        compiler_params=_params(("arbitrary",)),
    )(me, *parts, *got)


def _sibling_share(halves, tag=""):
    n = len(halves)

    def body(*refs):
        ins, outs = refs[:n], refs[n:2 * n]
        send, recv = refs[2 * n:]
        x, y, c, _ = _place()
        cps = [pltpu.make_async_remote_copy(
            src_ref=ins[t], dst_ref=outs[t], send_sem=send.at[t], recv_sem=recv.at[t],
            device_id=(x, y, 1 - c), device_id_type=MESH) for t in range(n)]
        for cp in cps:
            cp.start()
        for cp in cps:
            cp.wait()

    any_spec = pl.BlockSpec(memory_space=pl.ANY)
    return pl.pallas_call(
        body, name="sibling_share" + tag,
        in_specs=[any_spec] * n, out_specs=[any_spec] * n,
        out_shape=[jax.ShapeDtypeStruct(h.shape, h.dtype) for h in halves],
        scratch_shapes=[pltpu.SemaphoreType.DMA((n,))] * 2,
    )(*halves)


def _as_operand(index):
    return jnp.asarray(index, jnp.int32).reshape(1)


def _chip_partials(bufs, wire, tag=""):
    return _half_sum(bufs, _sibling_exchange(bufs, tag), wire, _as_operand(lax.axis_index("c")), tag)


def _finish_reduction(parts, landed, tag=""):
    c = lax.axis_index("c")
    half = _chip_sum(parts, landed, _as_operand(2 * lax.axis_index("x") + lax.axis_index("y")), tag)
    other = _sibling_share(half, tag)
    return [jnp.concatenate([jnp.where(c == 0, h, o), jnp.where(c == 0, o, h)], axis=0) for h, o in zip(half, other)]


def _reduce_scatter(bufs, wire, tag=""):
    parts = _chip_partials(bufs, wire, tag)
    return _finish_reduction(parts, _chip_exchange(parts), tag)


REPL_SIZES = (D_MODEL, D_MODEL, N_HEADS, D_CONV, D_CONV, D_CONV, D_MODEL, D_MODEL)
REPL_ROWS = 48


def _pack_repl(parts):
    flat = jnp.concatenate([p.reshape(-1).astype(_F32) for p in parts])
    return jnp.pad(flat, (0, REPL_ROWS * 128 - flat.shape[0])).reshape(REPL_ROWS, 128)


def _unpack_repl(packed, shapes):
    flat = packed.reshape(-1)
    out, off = [], 0
    for shape, size in zip(shapes, REPL_SIZES):
        out.append(flat[off:off + size].reshape(shape))
        off += size
    return out


def _projection_operand(w_in_t):
    zrow = jnp.zeros((128 - N_HEADS, D_MODEL), w_in_t.dtype)
    return jnp.concatenate([w_in_t[:OFF_F], w_in_t[OFF_F + N_HEADS:], w_in_t[OFF_F:OFF_F + N_HEADS], zrow], axis=0)


def _local_grads(x, target, meta, ln_in_g, ln_in_b, w_in_t, b_f, conv_w, conv_b, ln_conv_g, ln_conv_b,
                 w_pw, w_out, ln_out_g, ln_out_b, chip_partials=None):
    vec = lambda a: a.reshape(1, -1).astype(_F32)
    head = jnp.concatenate([jnp.zeros((ROW_PAD, D_MODEL), _F32), meta], axis=0)
    wpt = _projection_operand(w_in_t)
    bf = jnp.pad(vec(b_f), ((0, 0), (0, 128 - N_HEADS)))
    cw = jnp.pad(conv_w, ((0, 32 - CONV_WIDTH), (0, 0)))

    h, q, k, v, ga, u, ug, gc, f, vt = _ln_proj(x, head, vec(ln_in_g), vec(ln_in_b), wpt)
    ccol = _decay_cumsum(f, bf)
    o, lse = _attn_fwd(q, k, vt, ccol)
    z, pw = _conv_fwd(u, ug, cw, vec(conv_b), vec(ln_conv_g), vec(ln_conv_b), w_pw)
    dr, do, dga, dpw, dgc, delta, dw_out, dg_out, db_out, loss = _out_loss_bwd(
        o, ga, pw, gc, h, target, w_out, vec(ln_out_g), vec(ln_out_b))
    dq, dk, dv, dck, dcq = _attn_bwd(q, k, v, do, lse, delta, ccol)
    df, dbf = _decay_bwd(dck, dcq, f, bf)
    dz, dw_pw, dg_conv, db_conv, dconv_b = _conv_bwd_rows(dpw, z, vec(ln_conv_g), vec(ln_conv_b), w_pw)
    du, dug, dcw = _conv_bwd_taps(dz, u, ug, cw)
    pieces = (dq, dk, dv, dga, du, dug, dgc)
    dwq, dwk, dwv, dwga = _proj_bwd_w(h, pieces[:4], "proj_bwd_w_attn")
    dwu, dwug, dwgc, dwf = _proj_bwd_w(h, pieces[4:] + (df,), "proj_bwd_w_conv")
    dw_in_t = jnp.concatenate([dwq, dwk, dwv, dwf[:N_HEADS], dwga, dwu, dwug, dwgc], axis=0)
    dw_in_t = jnp.pad(dw_in_t.reshape(N_SHARD, W_IN_SHARD, D_MODEL), ((0, 0), (0, W_IN_SLOT - W_IN_SHARD), (0, 0)))
    dw_in_t = dw_in_t.reshape(N_SHARD * W_IN_SLOT, D_MODEL)
    parts = chip_partials([dw_in_t, dw_pw, dw_out]) if chip_partials else []
    grad_x, dmeta, dg_in, db_in, *landed = _proj_bwd_x(pieces, df, wpt, dr, x, head, vec(ln_in_g), parts)
    grads = dict(meta=dmeta, ln_in_g=dg_in, ln_in_b=db_in, w_in_t=dw_in_t, b_f=dbf[:, :N_HEADS],
                 conv_w=dcw[:CONV_WIDTH], conv_b=dconv_b, ln_conv_g=dg_conv, ln_conv_b=db_conv,
                 w_pw=dw_pw, w_out=dw_out, ln_out_g=dg_out, ln_out_b=db_out, reduced=(parts, landed))
    return loss, grad_x, grads


def kernel(x, meta, ln_in_g, ln_in_b, w_in, b_f, conv_w, conv_b, ln_conv_g, ln_conv_b, w_pw, w_out, ln_out_g, ln_out_b, loss_target, m_meta, m_ln_in_g, m_ln_in_b, m_w_in, m_b_f, m_conv_w, m_conv_b, m_ln_conv_g, m_ln_conv_b, m_w_pw, m_w_out, m_ln_out_g, m_ln_out_b, v_meta, v_ln_in_g, v_ln_in_b, v_w_in, v_b_f, v_conv_w, v_conv_b, v_ln_conv_g, v_ln_conv_b, v_w_pw, v_w_out, v_ln_out_g, v_ln_out_b):
    conv_w2, w_pw2, w_out2 = conv_w[0], w_pw[0], w_out[0]
    w_t = jnp.transpose(w_in[0])

    small = jnp.concatenate([jnp.pad(conv_w2, ((0, 1), (0, 0))), meta.reshape(32, 128)], axis=0)
    shards = [jnp.pad(w_t.astype(_MM), ((0, W_IN_SLOT - W_IN_SHARD), (0, 0))), w_pw2.astype(_MM), w_out2.astype(_MM), small]
    g_in, g_pw, g_out, g_small = (_own_slot(g, s) for g, s in zip(_all_gather(shards), shards))
    w_in_t_full = g_in[:, :W_IN_SHARD, :].reshape(D_IN, D_MODEL)
    w_pw_full = g_pw.reshape(D_CONV, D_CONV)
    w_out_full = g_out.reshape(D_MODEL, D_MODEL)
    conv_w_full = jnp.transpose(g_small[:, :CONV_WIDTH, :], (1, 0, 2)).reshape(CONV_WIDTH, D_CONV)
    meta_full = jnp.transpose(g_small[:, 32:64, :].reshape(N_SHARD, N_META, 256), (1, 0, 2)).reshape(N_META, D_MODEL)

    loss, grad_x, gr = _local_grads(x[0], loss_target[0], meta_full, ln_in_g, ln_in_b, w_in_t_full, b_f, conv_w_full,
                                    conv_b, ln_conv_g, ln_conv_b, w_pw_full, w_out_full, ln_out_g, ln_out_b,
                                    chip_partials=lambda big: _chip_partials(big, [_MM, _MM, _MM]))
    grad_x = grad_x[None]
    gw_slot, gw_pw, gw_out = _finish_reduction(*gr["reduced"])

    repl_names = ("ln_in_g", "ln_in_b", "b_f", "conv_b", "ln_conv_g", "ln_conv_b", "ln_out_g", "ln_out_b")
    repl = _pack_repl([gr[n] for n in repl_names] + [loss])
    gcw = jnp.transpose(jnp.pad(gr["conv_w"], ((0, 1), (0, 0))).reshape(32, N_SHARD, 128), (1, 0, 2))
    gmeta = jnp.transpose(gr["meta"].reshape(N_META, N_SHARD, 256), (1, 0, 2)).reshape(N_SHARD, 32, 128)
    small_g = jnp.concatenate([gcw, gmeta, jnp.broadcast_to(repl[None], (N_SHARD, REPL_ROWS, 128))], axis=1)
    small_rows = small_g.shape[1]
    gsmall, = _reduce_scatter([small_g.reshape(N_SHARD * small_rows, 128)], [_F32], "_small")
    g_conv_w = gsmall[:CONV_WIDTH]
    g_meta = gsmall[32:64].reshape(N_META, 256)
    g_repl = gsmall[64:64 + REPL_ROWS]

    repl_w = (ln_in_g, ln_in_b, b_f, conv_b, ln_conv_g, ln_conv_b, ln_out_g, ln_out_b)
    repl_m = (m_ln_in_g, m_ln_in_b, m_b_f, m_conv_b, m_ln_conv_g, m_ln_conv_b, m_ln_out_g, m_ln_out_b)
    repl_v = (v_ln_in_g, v_ln_in_b, v_b_f, v_conv_b, v_ln_conv_g, v_ln_conv_b, v_ln_out_g, v_ln_out_b)
    zero = jnp.zeros((1,), _F32)
    res_t = _adamw_tiled(w_t, gw_slot, jnp.transpose(m_w_in[0]), jnp.transpose(v_w_in[0]), 256, "adamw_w_in")
    gw_in, d_in, nm_in, nv_in = (jnp.transpose(a) for a in res_t)
    ws = [w_pw2, w_out2, conv_w2, meta, _pack_repl(repl_w + (zero,))]
    gs = [gw_pw, gw_out, g_conv_w, g_meta, g_repl]
    ms = [m_w_pw[0], m_w_out[0], m_conv_w[0], m_meta, _pack_repl(repl_m + (zero,))]
    vs = [v_w_pw[0], v_w_out[0], v_conv_w[0], v_meta, _pack_repl(repl_v + (zero + 1.0,))]
    ds, nms, nvs = _adamw_small(ws, gs, ms, vs)

    shapes = [w.shape for w in repl_w]
    total_loss = g_repl.reshape(-1)[sum(REPL_SIZES)]
    names = ("meta", "ln_in_g", "ln_in_b", "w_in", "b_f", "conv_w", "conv_b", "ln_conv_g", "ln_conv_b",
             "w_pw", "w_out", "ln_out_g", "ln_out_b")

    def assemble(t_in, smalls):
        pw_, out_, cw_, meta_, packed = smalls
        r = dict(zip(repl_names, _unpack_repl(packed, shapes)))
        r.update(meta=meta_, w_in=t_in[None], conv_w=cw_[None], w_pw=pw_[None], w_out=out_[None])
        return [r[n] for n in names]

    return (total_loss, grad_x, *assemble(gw_in, gs), *assemble(d_in, ds), *assemble(nm_in, nms),
            *assemble(nv_in, nvs))
```

```python
import functools

import jax
import jax.numpy as jnp
from jax import lax
from jax.experimental import pallas as pl
from jax.experimental.pallas import tpu as pltpu

D_MODEL = 1024
N_META = 16
D_ATTN = 512
D_CONV = 512
N_HEADS = 8
HEAD_DIM = 64
N_PAIRS = 4
CONV_WIDTH = 31
LN_EPS = 1e-5
ALPHA = 2.0 ** 0.25
ATTN_SCALE = 0.125
D_IN = 3592
OFF_F = 3 * D_ATTN
D_IN_PAD = 3712
N_SHARD = 4
W_IN_SHARD = D_IN // N_SHARD
W_IN_SLOT = 928

ROW_PAD = 112
ROW_X = ROW_PAD + N_META
HALO = 32
MASKED = 1e30

ADAM_LR = 0.001
ADAM_B1 = 0.9
ADAM_B2 = 0.999
ADAM_EPS = 1e-08
ADAM_WD = 0.01
ADAM_STEP = 10

VMEM_LIMIT = 56 * 1024 * 1024

_MM = jnp.bfloat16
_F32 = jnp.float32
MESH = pl.DeviceIdType.MESH


def _params(sem=None):
    return pltpu.CompilerParams(dimension_semantics=sem, vmem_limit_bytes=VMEM_LIMIT)


def _row_tile(n_rows):
    for t in (384, 256, 128):
        if n_rows % t == 0:
            return t
    raise ValueError(f"padded sequence length {n_rows} is not a multiple of 128")


def _sigmoid(x):
    return 1.0 / (1.0 + jnp.exp(-x))


def _ln_stats(x):
    mu = jnp.mean(x, axis=-1, keepdims=True)
    xc = x - mu
    var = jnp.mean(xc * xc, axis=-1, keepdims=True)
    rstd = lax.rsqrt(var + LN_EPS)
    return xc * rstd, rstd


def _ln_bwd(dy, xhat, rstd, g):
    dxh = dy * g
    m1 = jnp.mean(dxh, axis=-1, keepdims=True)
    m2 = jnp.mean(dxh * xhat, axis=-1, keepdims=True)
    return rstd * (dxh - m1 - xhat * m2)


def _dot(a, b):
    return jnp.dot(a, b, preferred_element_type=_F32)


def _dot_t0(a, b):
    return lax.dot_general(a, b, (((0,), (0,)), ((), ())), preferred_element_type=_F32)


def _dot_t1(a, b):
    return lax.dot_general(a, b, (((1,), (1,)), ((), ())), preferred_element_type=_F32)


def _dot_exact(a, b):
    return jnp.dot(a, b, preferred_element_type=_F32, precision=lax.Precision.HIGHEST)


def _full(shape):
    return pl.BlockSpec(shape, lambda *_: (0,) * len(shape))


def _x_specs(t):
    n = t // ROW_X
    return [pl.BlockSpec((ROW_X, D_MODEL), lambda i, s=s: (jnp.maximum(i * n + s - 1, 0), 0)) for s in range(n)]


def _x_tile(i, refs, head_ref=None):
    first = refs[0][...]
    if head_ref is not None:
        first = jnp.where(i == 0, head_ref[...], first)
    return jnp.concatenate([first] + [r[...] for r in refs[1:]], axis=0)


def _ln_proj(x, head, g, b, wpt):
    lp = ROW_X + x.shape[0]
    t = _row_tile(lp)
    nx = t // ROW_X

    def body(*refs):
        x_refs, (head_ref, g_ref, b_ref, wt_ref) = refs[:nx], refs[nx:nx + 4]
        h_ref, q_ref, k_ref, v_ref, ga_ref, u_ref, ug_ref, gc_ref, f_ref, vt_ref = refs[nx + 4:]
        i = pl.program_id(0)
        xhat, _ = _ln_stats(_x_tile(i, x_refs, head_ref))
        rows = i * t + lax.broadcasted_iota(jnp.int32, (t, 1), 0)
        h = jnp.where(rows >= ROW_PAD, xhat * g_ref[...] + b_ref[...], 0.0)
        h_ref[...] = h
        hb = h.astype(_MM)
        q_ref[...] = (_dot_t1(hb, wt_ref[0:512, :]) * ATTN_SCALE).astype(q_ref.dtype)
        k_ref[...] = _dot_t1(hb, wt_ref[512:1024, :]).astype(k_ref.dtype)
        v = _dot_t1(hb, wt_ref[1024:1536, :])
        v_ref[...] = v.astype(v_ref.dtype)
        vt_ref[...] = v.T.astype(vt_ref.dtype)
        ga_ref[...] = _dot_t1(hb, wt_ref[1536:2048, :])
        u_ref[...] = _dot_t1(hb, wt_ref[2048:2560, :])
        ug_ref[...] = _dot_t1(hb, wt_ref[2560:3072, :])
        gc_ref[...] = _dot_t1(hb, wt_ref[3072:3584, :])
        f_ref[...] = _dot_t1(hb, wt_ref[3584:3712, :])

    row = lambda w: pl.BlockSpec((t, w), lambda i: (i, 0))
    f32 = lambda w: jax.ShapeDtypeStruct((lp, w), _F32)
    mm = lambda w: jax.ShapeDtypeStruct((lp, w), _MM)
    return pl.pallas_call(
        body, name="ln_proj", grid=(lp // t,),
        in_specs=_x_specs(t) + [_full((ROW_X, D_MODEL)), _full((1, D_MODEL)), _full((1, D_MODEL)),
                                _full((D_IN_PAD, D_MODEL))],
        out_specs=[row(D_MODEL)] + [row(512)] * 7 + [row(128), pl.BlockSpec((512, t), lambda i: (0, i))],
        out_shape=[f32(D_MODEL), mm(512), mm(512), mm(512), f32(512), f32(512), f32(512), f32(512), f32(128),
                   jax.ShapeDtypeStruct((512, lp), _MM)],
        compiler_params=_params(("arbitrary",)),
    )(*([x] * nx), head, g, b, wpt)


def _log_sigmoid(z):
    return jnp.minimum(z, 0.0) - jnp.log(1.0 + jnp.exp(-jnp.abs(z)))


def _decay_cumsum(f, bf):
    lp = f.shape[0]
    t = _row_tile(lp)

    def body(f_ref, bf_ref, ccol_ref):
        r = lax.broadcasted_iota(jnp.int32, (t, t), 0)
        c = lax.broadcasted_iota(jnp.int32, (t, t), 1)
        tril = (c <= r).astype(_F32)

        def step(bi, carry):
            rows = pl.ds(pl.multiple_of(bi * t, 128), t)
            lf = _log_sigmoid(f_ref[rows, :] + bf_ref[...])
            cb = _dot_exact(tril, lf) + carry
            for p in range(N_PAIRS):
                ccol_ref[p, rows, :] = cb if p == 0 else pltpu.roll(cb, 128 - 2 * p, axis=1)
            return cb[t - 1:t, :]

        lax.fori_loop(0, lp // t, step, jnp.zeros((1, 128), _F32))

    return pl.pallas_call(
        body, name="decay_cumsum",
        out_shape=jax.ShapeDtypeStruct((N_PAIRS, lp, 128), _F32),
        compiler_params=_params(),
    )(f, bf)


def _split3(x):
    hi = x.astype(_MM)
    r1 = x - hi.astype(_F32)
    mid = r1.astype(_MM)
    return hi, mid, (r1 - mid.astype(_F32)).astype(_MM)


def _augment(x, hd, first, second, transposed=False):
    lane = lax.broadcasted_iota(jnp.int32, (1, 128), 1)
    base = HEAD_DIM * (1 - hd)
    out = jnp.where(lane // HEAD_DIM == hd, x, jnp.zeros_like(x))
    for slot, val in enumerate((first, second)):
        lo = base + 3 * slot
        if val is None:
            continue
        if isinstance(val, float):
            out = jnp.where((lane >= lo) & (lane < lo + 3), jnp.asarray(val, x.dtype), out)
        else:
            for n, piece in enumerate(val):
                out = jnp.where(lane == lo + n, piece[:, hd:hd + 1], out)
    return _transpose_mm(out) if transposed else out


def _transpose_mm(x):
    return x.astype(_F32).T.astype(_MM)


def _diag_valid(i, t):
    sub = lax.broadcasted_iota(jnp.int32, (t, t), 0)
    return (sub <= lax.broadcasted_iota(jnp.int32, (t, t), 1)) & (i * t + sub >= ROW_PAD)


def _key_pieces(c, first_row):
    pos = first_row + lax.broadcasted_iota(jnp.int32, (c.shape[0], 1), 0)
    return _split3(jnp.where(pos >= ROW_PAD, -c, -MASKED))


def _key_operand(k, pieces, hd, transposed=False):
    return _augment(k, hd, pieces, 1.0, transposed)


def _attn_fwd(q, k, vt, ccol):
    lp = q.shape[0]
    t = _row_tile(lp)
    nb = lp // t

    def body(q_ref, k_ref, vt_ref, ccol_ref, o_ref, lse_ref, ka_ref):
        i = pl.program_id(1)

        @pl.when(i == 0)
        def _():
            for n in range(nb):
                rows = slice(n * t, (n + 1) * t)
                pieces = _key_pieces(ccol_ref[0, rows, :], n * t)
                for hd in range(2):
                    ka_ref[hd, rows, :] = _key_operand(k_ref[rows, :], pieces, hd)

        qats = [_augment(q_ref[...], hd, 1.0, None, transposed=True) for hd in range(2)]

        def scores(j):
            keys = pl.ds(pl.multiple_of(j * t, 128), t)
            return tuple(_dot(ka_ref[hd, keys, :], qats[hd]) for hd in range(2))

        def update(j, sts, states, diag):
            vt = vt_ref[:, pl.ds(pl.multiple_of(j * t, 128), t)]
            new = []
            for st, (m, l, acc) in zip(sts, states):
                if diag:
                    st = jnp.where(_diag_valid(i, t), st, -MASKED)
                m_new = jnp.maximum(m, jnp.max(st, axis=0, keepdims=True))
                a = jnp.exp(m - m_new)
                pt = jnp.exp(st - m_new)
                l = a * l + jnp.sum(pt, axis=0, keepdims=True)
                new.append((m_new, l, a * acc + _dot(vt, pt.astype(_MM))))
            return tuple(new)

        def step(j, carry):
            sts, states = carry
            return scores(j + 1), update(j, sts, states, False)

        init = (jnp.full((1, t), -MASKED, _F32), jnp.zeros((1, t), _F32), jnp.zeros((128, t), _F32))
        def step2(n, carry):
            sts, states = carry
            nxt, last = scores(2 * n + 1), scores(2 * n + 2)
            return last, update(2 * n + 1, nxt, update(2 * n, sts, states, False), False)

        carry = lax.fori_loop(0, i // 2, step2, (scores(0), (init, init)))
        sts, states = lax.fori_loop(0, i % 2, lambda _, c: step(i - 1, c), carry)
        (m0, l0, acc0), (m1, l1, acc1) = update(i, sts, states, True)
        row = lax.broadcasted_iota(jnp.int32, (128, t), 0)
        o_ref[...] = jnp.where(row < HEAD_DIM, acc0 / l0, acc1 / l1).T
        lse_ref[0] = jnp.where(row == 0, m0 + jnp.log(l0), jnp.where(row == 1, m1 + jnp.log(l1), 0.0)).T

    return pl.pallas_call(
        body, name="attn_fwd", grid=(N_PAIRS, nb),
        in_specs=[pl.BlockSpec((t, 128), lambda p, i: (i, p)),
                  pl.BlockSpec((lp, 128), lambda p, i: (0, p)),
                  pl.BlockSpec((128, lp), lambda p, i: (p, 0)),
                  pl.BlockSpec((1, lp, 128), lambda p, i: (p, 0, 0))],
        out_specs=[pl.BlockSpec((t, 128), lambda p, i: (i, p)),
                   pl.BlockSpec((1, t, 128), lambda p, i: (p, i, 0))],
        out_shape=[jax.ShapeDtypeStruct((lp, D_ATTN), _F32), jax.ShapeDtypeStruct((N_PAIRS, lp, 128), _F32)],
        scratch_shapes=[pltpu.VMEM((2, lp, 128), _MM)],
        compiler_params=_params(("arbitrary", "arbitrary")),
    )(q, k, vt, ccol)


CONV_CHUNK = 32


def _shifted_windows(win, sh, rows):
    sh[0] = win[...]
    for b in range(1, 8):
        sh[b, 0:rows - 8, :] = win[b:b + rows - 8, :]


def _tap_rows(sh, offsets, r0):
    out = {}
    for b in range(8):
        ds = [d for d in offsets if d % 8 == b]
        if ds:
            lo, hi = min(ds) // 8, max(ds) // 8
            rows = sh[b, pl.ds(pl.multiple_of(r0 + 8 * lo, 8), CONV_CHUNK + 8 * (hi - lo)), :]
            for d in ds:
                out[d] = rows[8 * (d // 8 - lo):8 * (d // 8 - lo) + CONV_CHUNK, :]
    return out


def _conv_fwd(u, ug, cw, cb, g, b, wpw):
    lp = u.shape[0]
    t = _row_tile(lp)

    def body(u_ref, ug_ref, cw_ref, cb_ref, g_ref, b_ref, wpw_ref, z_ref, pw_ref, win, sh):
        i = pl.program_id(0)

        @pl.when(i == 0)
        def _():
            win[0:HALO, :] = jnp.zeros((HALO, D_CONV), _F32)

        win[HALO:HALO + t, :] = u_ref[...] * _sigmoid(ug_ref[...])
        _shifted_windows(win, sh, t + HALO)

        def chunk(c, carry):
            r0 = c * CONV_CHUNK
            acc = jnp.broadcast_to(cb_ref[...], (CONV_CHUNK, D_CONV))
            first = HALO - (CONV_WIDTH - 1)
            taps = _tap_rows(sh, range(first, first + CONV_WIDTH), r0)
            for j in range(CONV_WIDTH):
                acc = acc + cw_ref[j:j + 1, :] * taps[first + j]
            z_ref[pl.ds(pl.multiple_of(r0, 8), CONV_CHUNK), :] = acc
            return carry

        lax.fori_loop(0, t // CONV_CHUNK, chunk, 0)
        xhat, _ = _ln_stats(z_ref[...])
        zn = xhat * g_ref[...] + b_ref[...]
        a = zn * _sigmoid(zn)
        pw_ref[...] = _dot(a.astype(_MM), wpw_ref[...])
        win[0:HALO, :] = win[t:t + HALO, :]

    row = pl.BlockSpec((t, D_CONV), lambda i: (i, 0))
    vec = _full((1, D_CONV))
    return pl.pallas_call(
        body, name="conv_fwd", grid=(lp // t,),
        in_specs=[row, row, _full((32, D_CONV)), vec, vec, vec, _full((D_CONV, D_CONV))],
        out_specs=[row, row],
        out_shape=[jax.ShapeDtypeStruct((lp, D_CONV), _F32)] * 2,
        scratch_shapes=[pltpu.VMEM((t + HALO, D_CONV), _F32), pltpu.VMEM((8, t + HALO, D_CONV), _F32)],
        compiler_params=_params(("arbitrary",)),
    )(u, ug, cw, cb, g, b, wpw)


def _out_loss_bwd(o, ga, pw, gc, h, target, w_out, g, b):
    lp = o.shape[0]
    t = _row_tile(lp)
    nx = t // ROW_X

    def body(*refs):
        o_ref, ga_ref, pw_ref, gc_ref, h_ref = refs[:5]
        tgt_refs, (w_ref, g_ref, b_ref) = refs[5:5 + nx], refs[5 + nx:8 + nx]
        dr_ref, do_ref, dga_ref, dpw_ref, dgc_ref, delta_ref, dw_ref, dg_ref, db_ref, loss_ref = refs[8 + nx:]
        i = pl.program_id(0)

        @pl.when(i == 0)
        def _():
            dw_ref[...] = jnp.zeros_like(dw_ref)
            dg_ref[...] = jnp.zeros_like(dg_ref)
            db_ref[...] = jnp.zeros_like(db_ref)
            loss_ref[...] = jnp.zeros_like(loss_ref)

        o, ga, pw, gc = o_ref[...], ga_ref[...], pw_ref[...], gc_ref[...]
        sa, sc = _sigmoid(ga), _sigmoid(gc)
        silu_a, silu_c = ga * sa, gc * sc
        ycat = jnp.concatenate([o * silu_a, pw * silu_c], axis=-1).astype(_MM)
        r = ALPHA * h_ref[...] + _dot(ycat, w_ref[...])
        xhat, rstd = _ln_stats(r)
        out = xhat * g_ref[...] + b_ref[...]
        rows = i * t + lax.broadcasted_iota(jnp.int32, (t, 1), 0)
        diff = jnp.where(rows >= ROW_X, out - _x_tile(i, tgt_refs), 0.0)
        loss_ref[...] += (0.5 / D_MODEL) * jnp.sum(diff * diff, keepdims=True)
        dout = diff * (1.0 / D_MODEL)
        dg_ref[...] += jnp.sum(dout * xhat, axis=0, keepdims=True)
        db_ref[...] += jnp.sum(dout, axis=0, keepdims=True)
        dr = _ln_bwd(dout, xhat, rstd, g_ref[...])
        dr_ref[...] = dr
        drb = dr.astype(_MM)
        dw_ref[...] += _dot_t0(ycat, drb)
        dy = _dot_t1(drb, w_ref[...])
        dya, dyc = dy[:, :D_ATTN], dy[:, D_ATTN:]
        do = dya * silu_a
        do_ref[...] = do.astype(do_ref.dtype)
        dga_ref[...] = (dya * o * (sa * (1.0 + ga * (1.0 - sa)))).astype(dga_ref.dtype)
        dpw_ref[...] = (dyc * silu_c).astype(dpw_ref.dtype)
        dgc_ref[...] = (dyc * pw * (sc * (1.0 + gc * (1.0 - sc)))).astype(dgc_ref.dtype)
        sel = (lax.broadcasted_iota(jnp.int32, (128, 128), 0) // HEAD_DIM
               == lax.broadcasted_iota(jnp.int32, (128, 128), 1)).astype(_F32)
        prod = do * o
        for p in range(N_PAIRS):
            delta_ref[p] = _dot_exact(prod[:, 128 * p:128 * (p + 1)], sel)

    row = lambda w: pl.BlockSpec((t, w), lambda i: (i, 0))
    return pl.pallas_call(
        body, name="out_loss_bwd", grid=(lp // t,),
        in_specs=[row(512)] * 4 + [row(D_MODEL)] + _x_specs(t) + [_full((D_MODEL, D_MODEL))] + [_full((1, D_MODEL))] * 2,
        out_specs=[row(D_MODEL)] + [row(512)] * 4 + [pl.BlockSpec((N_PAIRS, t, 128), lambda i: (0, i, 0)),
                   _full((D_MODEL, D_MODEL)), _full((1, D_MODEL)), _full((1, D_MODEL)), _full((1, 1))],
        out_shape=[jax.ShapeDtypeStruct((lp, D_MODEL), _F32)] + [jax.ShapeDtypeStruct((lp, 512), _MM)] * 4
                  + [jax.ShapeDtypeStruct((N_PAIRS, lp, 128), _F32), jax.ShapeDtypeStruct((D_MODEL, D_MODEL), _F32),
                     jax.ShapeDtypeStruct((1, D_MODEL), _F32), jax.ShapeDtypeStruct((1, D_MODEL), _F32),
                     jax.ShapeDtypeStruct((1, 1), _F32)],
        compiler_params=_params(("arbitrary",)),
    )(o, ga, pw, gc, h, *([target] * nx), w_out, g, b)


def _attn_bwd(q, k, v, do, lse, delta, ccol):
    lp = q.shape[0]
    t = _row_tile(lp)
    nb = lp // t

    def body(q_ref, k_ref, v_ref, do_ref, lse_ref, delta_ref, ccol_ref,
             dq_ref, dk_ref, dv_ref, dck_ref, dcq_ref,
             qa_ref, qat_ref, doa_ref, doat_ref, dqt_ref):
        j = pl.program_id(1)

        @pl.when(j == 0)
        def _():
            dqt_ref[...] = jnp.zeros_like(dqt_ref)
            for n in range(nb):
                rows = slice(n * t, (n + 1) * t)
                lse_pieces, delta_pieces = _split3(-lse_ref[0, rows, :]), _split3(-delta_ref[0, rows, :])
                for hd in range(2):
                    qa = _augment(q_ref[rows, :], hd, 1.0, lse_pieces)
                    doa = _augment(do_ref[rows, :], hd, delta_pieces, None)
                    qa_ref[hd, rows, :] = qa
                    qat_ref[hd, :, rows] = _transpose_mm(qa)
                    doa_ref[hd, rows, :] = doa
                    doat_ref[hd, :, rows] = _transpose_mm(doa)

        pieces = _key_pieces(ccol_ref[0], j * t)
        kas = [_key_operand(k_ref[...], pieces, hd) for hd in range(2)]
        kats = [_transpose_mm(ka) for ka in kas]
        vas = [_augment(v_ref[...], hd, 1.0, None) for hd in range(2)]

        def block(i, carry, diag, width=1):
            rows = pl.ds(pl.multiple_of(i * t, 128), width * t)
            new = []
            for hd in range(2):
                dk, dv = carry[hd]
                st = _dot(kas[hd], qat_ref[hd, :, rows])
                if diag:
                    st = jnp.where(_diag_valid(j, t), st, -MASKED)
                pt = jnp.exp(st)
                dsb = (pt * _dot(vas[hd], doat_ref[hd, :, rows])).astype(_MM)
                dv = dv + _dot(pt.astype(_MM), doa_ref[hd, rows, :])
                dk = dk + _dot(dsb, qa_ref[hd, rows, :])
                dqt_ref[hd, :, rows] += _dot(kats[hd], dsb)
                new.append((dk, dv))
            return tuple(new)

        zero = jnp.zeros((t, 128), _F32)
        carry = block(j, ((zero, zero), (zero, zero)), True)
        rest = nb - 1 - j
        if nb >= 3:
            carry = lax.fori_loop(0, rest // 2, lambda n, c: block(j + 1 + 2 * n, c, False, 2), carry)
            rest = rest % 2
        (dk0, dv0), (dk1, dv1) = lax.fori_loop(nb - rest, nb, functools.partial(block, diag=False), carry)
        lane = lax.broadcasted_iota(jnp.int32, (1, 128), 1)
        dk_ref[...] = jnp.where(lane < HEAD_DIM, dk0, dk1).astype(dk_ref.dtype)
        dv_ref[...] = jnp.where(lane < HEAD_DIM, dv0, dv1).astype(dv_ref.dtype)
        dck_ref[0] = jnp.where(lane == 0, dk0[:, HEAD_DIM:HEAD_DIM + 1], jnp.where(lane == 1, dk1[:, 0:1], 0.0))

        @pl.when(j == nb - 1)
        def _():
            row = lax.broadcasted_iota(jnp.int32, (128, t), 0)
            sub = lax.broadcasted_iota(jnp.int32, (8, t), 0)
            for n in range(nb):
                cols = slice(n * t, (n + 1) * t)
                t0, t1 = dqt_ref[0, :, cols], dqt_ref[1, :, cols]
                dq_ref[cols, :] = jnp.where(row < HEAD_DIM, t0, t1).T * ATTN_SCALE
                dcq_ref[0, :, cols] = jnp.where(sub == 0, t0[HEAD_DIM + 3:HEAD_DIM + 4, :],
                                                jnp.where(sub == 1, t1[3:4, :], 0.0))

    blk = pl.BlockSpec((t, 128), lambda p, j: (j, p))
    res = pl.BlockSpec((lp, 128), lambda p, j: (0, p))
    stat = pl.BlockSpec((1, lp, 128), lambda p, j: (p, 0, 0))
    stat_blk = pl.BlockSpec((1, t, 128), lambda p, j: (p, j, 0))
    return pl.pallas_call(
        body, name="attn_bwd", grid=(N_PAIRS, nb),
        in_specs=[res, blk, blk, res, stat, stat, stat_blk],
        out_specs=[res, blk, blk, stat_blk, pl.BlockSpec((1, 8, lp), lambda p, j: (p, 0, 0))],
        out_shape=[jax.ShapeDtypeStruct((lp, D_ATTN), _F32), jax.ShapeDtypeStruct((lp, D_ATTN), _MM),
                   jax.ShapeDtypeStruct((lp, D_ATTN), _MM), jax.ShapeDtypeStruct((N_PAIRS, lp, 128), _F32),
                   jax.ShapeDtypeStruct((N_PAIRS, 8, lp), _F32)],
        scratch_shapes=[pltpu.VMEM((2, lp, 128), _MM), pltpu.VMEM((2, 128, lp), _MM),
                        pltpu.VMEM((2, lp, 128), _MM), pltpu.VMEM((2, 128, lp), _MM),
                        pltpu.VMEM((2, 128, lp), _F32)],
        compiler_params=_params(("arbitrary", "arbitrary")),
    )(q, k, v, do, lse, delta, ccol)


def _decay_bwd(dck, dcq, f, bf):
    lp = f.shape[0]
    nb = lp // 128

    def body(dck_ref, dcq_ref, f_ref, bf_ref, df_ref, dbf_ref):
        r = lax.broadcasted_iota(jnp.int32, (128, 128), 0)
        c = lax.broadcasted_iota(jnp.int32, (128, 128), 1)
        triu = (c >= r).astype(_F32)

        def step(n, carry):
            tail, dbf = carry
            bi = nb - 1 - n
            rows = pl.ds(pl.multiple_of(bi * 128, 128), 128)
            cols = jnp.zeros((128, 128), _F32)
            for p in range(N_PAIRS):
                dcq = jnp.concatenate([dcq_ref[p, :, rows], jnp.zeros((120, 128), _F32)], axis=0).T
                dcp = dcq - dck_ref[p, rows, :]
                cols = cols + (dcp if p == 0 else pltpu.roll(dcp, 2 * p, axis=1))
            lane = lax.broadcasted_iota(jnp.int32, (128, 128), 1)
            cols = jnp.where(lane < N_HEADS, cols, 0.0)
            rev = _dot_exact(triu, cols) + tail
            pos = bi * 128 + lax.broadcasted_iota(jnp.int32, (128, 1), 0)
            df = jnp.where(pos >= ROW_PAD, rev * (1.0 - _sigmoid(f_ref[rows, :] + bf_ref[...])), 0.0)
            df_ref[rows, :] = df.astype(df_ref.dtype)
            return rev[0:1, :], dbf + jnp.sum(df, axis=0, keepdims=True)

        zero = jnp.zeros((1, 128), _F32)
        _, dbf = lax.fori_loop(0, nb, step, (zero, zero), unroll=True)
        dbf_ref[...] = dbf

    return pl.pallas_call(
        body, name="decay_bwd",
        out_shape=[jax.ShapeDtypeStruct((lp, 128), _MM), jax.ShapeDtypeStruct((1, 128), _F32)],
        compiler_params=_params(),
    )(dck, dcq, f, bf)


def _conv_bwd_rows(dpw, z, g, b, wpw):
    lp = z.shape[0]
    t = _row_tile(lp)

    def body(dpw_ref, z_ref, g_ref, b_ref, w_ref, dz_ref, dw_ref, dg_ref, db_ref, dcb_ref):
        i = pl.program_id(0)

        @pl.when(i == 0)
        def _():
            dw_ref[...] = jnp.zeros_like(dw_ref)
            dg_ref[...] = jnp.zeros_like(dg_ref)
            db_ref[...] = jnp.zeros_like(db_ref)
            dcb_ref[...] = jnp.zeros_like(dcb_ref)

        xhat, rstd = _ln_stats(z_ref[...])
        zn = xhat * g_ref[...] + b_ref[...]
        sg = _sigmoid(zn)
        dpw = dpw_ref[...]
        dw_ref[...] += _dot_t0((zn * sg).astype(_MM), dpw)
        dzn = _dot_t1(dpw, w_ref[...]) * (sg * (1.0 + zn * (1.0 - sg)))
        dg_ref[...] += jnp.sum(dzn * xhat, axis=0, keepdims=True)
        db_ref[...] += jnp.sum(dzn, axis=0, keepdims=True)
        dz = _ln_bwd(dzn, xhat, rstd, g_ref[...])
        dz_ref[...] = dz
        dcb_ref[...] += jnp.sum(dz, axis=0, keepdims=True)

    row = pl.BlockSpec((t, D_CONV), lambda i: (i, 0))
    vec = _full((1, D_CONV))
    vshape = jax.ShapeDtypeStruct((1, D_CONV), _F32)
    return pl.pallas_call(
        body, name="conv_bwd_rows", grid=(lp // t,),
        in_specs=[row, row, vec, vec, _full((D_CONV, D_CONV))],
        out_specs=[row, _full((D_CONV, D_CONV)), vec, vec, vec],
        out_shape=[jax.ShapeDtypeStruct((lp, D_CONV), _F32), jax.ShapeDtypeStruct((D_CONV, D_CONV), _F32),
                   vshape, vshape, vshape],
        compiler_params=_params(("arbitrary",)),
    )(dpw, z, g, b, wpw)


def _conv_bwd_taps(dz, u, ug, cw):
    lp = u.shape[0]
    t = _row_tile(lp)
    nt = lp // t

    def body(dz_ref, u_ref, ug_ref, cw_ref, du_ref, dug_ref, dcw_ref, win, sh, wacc):
        i = pl.program_id(0)

        @pl.when(i == 0)
        def _():
            win[t:t + HALO, :] = jnp.zeros((HALO, D_CONV), _F32)
            wacc[...] = jnp.zeros_like(wacc)

        win[0:t, :] = dz_ref[...]
        _shifted_windows(win, sh, t + HALO)

        def chunk(c, carry):
            r0 = c * CONV_CHUNK
            rows = pl.ds(pl.multiple_of(r0, 8), CONV_CHUNK)
            u = u_ref[rows, :]
            sg = _sigmoid(ug_ref[rows, :])
            hh = u * sg
            dhh = jnp.zeros((CONV_CHUNK, D_CONV), _F32)
            for j in range(CONV_WIDTH):
                d = CONV_WIDTH - 1 - j
                s = _tap_rows(sh, [d], r0)[d]
                dhh = dhh + cw_ref[j:j + 1, :] * s
                prod = hh * s
                wacc[j] += sum(prod[8 * n:8 * (n + 1), :] for n in range(CONV_CHUNK // 8))
            du_ref[rows, :] = (dhh * sg).astype(du_ref.dtype)
            dug_ref[rows, :] = (dhh * u * sg * (1.0 - sg)).astype(dug_ref.dtype)
            return carry

        lax.fori_loop(0, t // CONV_CHUNK, chunk, 0)
        win[t:t + HALO, :] = win[0:HALO, :]

        @pl.when(i == nt - 1)
        def _():
            dcw_ref[...] = jnp.sum(wacc[...], axis=1)

    row = pl.BlockSpec((t, D_CONV), lambda i: (nt - 1 - i, 0))
    return pl.pallas_call(
        body, name="conv_bwd_taps", grid=(nt,),
        in_specs=[row, row, row, _full((32, D_CONV))],
        out_specs=[row, row, _full((32, D_CONV))],
        out_shape=[jax.ShapeDtypeStruct((lp, D_CONV), _MM)] * 2 + [jax.ShapeDtypeStruct((32, D_CONV), _F32)],
        scratch_shapes=[pltpu.VMEM((t + HALO, D_CONV), _F32), pltpu.VMEM((8, t + HALO, D_CONV), _F32),
                        pltpu.VMEM((32, 8, D_CONV), _F32)],
        compiler_params=_params(("arbitrary",)),
    )(dz, u, ug, cw)


def _proj_bwd_x(pieces, df, wpt, dr, x, head, g, parts=()):
    seq = x.shape[0]
    lp = ROW_X + seq
    t = _row_tile(lp)
    nt = lp // t
    nx = t // ROW_X
    n_parts = len(parts)

    def body(*refs):
        dq_ref, dk_ref, dv_ref, dga_ref, du_ref, dug_ref, dgc_ref, df_ref, wt_ref, dr_ref = refs[:10]
        x_refs, (head_ref, g_ref) = refs[10:10 + nx], refs[10 + nx:12 + nx]
        part_refs, refs = refs[12 + nx:12 + nx + n_parts], refs[12 + nx + n_parts:]
        gx_ref, dmeta_ref, dg_ref, db_ref = refs[:4]
        land_refs, (buf, sem), comm_sems = refs[4:4 + n_parts], refs[4 + n_parts:6 + n_parts], refs[6 + n_parts:]
        i = pl.program_id(0)

        @pl.when(i == 0)
        def _():
            dg_ref[...] = jnp.zeros_like(dg_ref)
            db_ref[...] = jnp.zeros_like(db_ref)
            if n_parts:
                for cp in _chip_exchange_copies(part_refs, land_refs, *comm_sems):
                    cp.start()

        dh = ALPHA * dr_ref[...]
        for n, ref in enumerate((dq_ref, dk_ref, dv_ref, dga_ref, du_ref, dug_ref, dgc_ref)):
            dh = dh + _dot(ref[...].astype(_MM), wt_ref[512 * n:512 * (n + 1), :])
        dh = dh + _dot(df_ref[...], wt_ref[3584:3712, :])
        rows = i * t + lax.broadcasted_iota(jnp.int32, (t, 1), 0)
        dh = jnp.where(rows >= ROW_PAD, dh, 0.0)
        xhat, rstd = _ln_stats(_x_tile(i, x_refs, head_ref))
        dg_ref[...] += jnp.sum(dh * xhat, axis=0, keepdims=True)
        db_ref[...] += jnp.sum(dh, axis=0, keepdims=True)
        dx = _ln_bwd(dh, xhat, rstd, g_ref[...])

        def first_copy():
            return pltpu.make_async_copy(buf.at[0, pl.ds(ROW_X, t - ROW_X), :], gx_ref.at[pl.ds(0, t - ROW_X), :], sem)

        def tile_copy(n):
            return pltpu.make_async_copy(buf.at[n % 2], gx_ref.at[pl.ds(pl.multiple_of(n * t - ROW_X, ROW_X), t), :], sem)

        pl.when(i == 1)(lambda: first_copy().wait())
        pl.when(i > 1)(lambda: tile_copy(i - 1).wait())
        buf[i % 2] = dx

        @pl.when(i == 0)
        def _():
            dmeta_ref[...] = dx[ROW_PAD:ROW_X, :]
            first_copy().start()

        pl.when(i > 0)(lambda: tile_copy(i).start())
        if nt == 1:
            first_copy().wait()
        else:
            pl.when(i == nt - 1)(lambda: tile_copy(i).wait())

        @pl.when(i == nt - 1)
        def _():
            if n_parts:
                for cp in _chip_exchange_copies(part_refs, land_refs, *comm_sems):
                    cp.wait()

    row = lambda w: pl.BlockSpec((t, w), lambda i: (i, 0))
    vec = _full((1, D_MODEL))
    any_spec = pl.BlockSpec(memory_space=pl.ANY)
    return pl.pallas_call(
        body, name="proj_bwd_x", grid=(nt,),
        in_specs=[row(512)] * 7 + [row(128), _full((D_IN_PAD, D_MODEL)), row(D_MODEL)] + _x_specs(t)
                 + [_full((ROW_X, D_MODEL)), vec] + [any_spec] * n_parts,
        out_specs=[any_spec, _full((N_META, D_MODEL)), vec, vec] + [any_spec] * n_parts,
        out_shape=[jax.ShapeDtypeStruct((seq, D_MODEL), _F32), jax.ShapeDtypeStruct((N_META, D_MODEL), _F32),
                   jax.ShapeDtypeStruct((1, D_MODEL), _F32), jax.ShapeDtypeStruct((1, D_MODEL), _F32)]
                  + [jax.ShapeDtypeStruct(p.shape, p.dtype) for p in parts],
        scratch_shapes=[pltpu.VMEM((2, t, D_MODEL), _F32), pltpu.SemaphoreType.DMA(())]
                       + [pltpu.SemaphoreType.DMA((3 * n_parts,))] * (2 if n_parts else 0),
        compiler_params=_params(("arbitrary",)),
    )(*pieces, df, wpt, dr, *([x] * nx), head, g, *parts)


def _proj_bwd_w(h, pieces, name):
    lp = h.shape[0]
    t = _row_tile(lp)
    n = len(pieces)
    widths = [p.shape[1] for p in pieces]

    def body(*refs):
        h_ref, d_refs, w_refs = refs[0], refs[1:1 + n], refs[1 + n:]
        i = pl.program_id(0)

        @pl.when(i == 0)
        def _():
            for w_ref in w_refs:
                w_ref[...] = jnp.zeros_like(w_ref)

        hb = h_ref[...].astype(_MM)
        for d_ref, w_ref in zip(d_refs, w_refs):
            w_ref[...] += _dot_t0(d_ref[...].astype(_MM), hb)

    row = lambda w: pl.BlockSpec((t, w), lambda i: (i, 0))
    return pl.pallas_call(
        body, name=name, grid=(lp // t,),
        in_specs=[row(D_MODEL)] + [row(w) for w in widths],
        out_specs=[_full((w, D_MODEL)) for w in widths],
        out_shape=[jax.ShapeDtypeStruct((w, D_MODEL), _F32) for w in widths],
        compiler_params=_params(("arbitrary",)),
    )(h, *pieces)


def _adamw_math(w, g, m, v):
    m = ADAM_B1 * m + (1.0 - ADAM_B1) * g
    v = ADAM_B2 * v + (1.0 - ADAM_B2) * (g * g)
    m_hat = m / (1.0 - ADAM_B1 ** ADAM_STEP)
    v_hat = v / (1.0 - ADAM_B2 ** ADAM_STEP)
    delta = -ADAM_LR * (m_hat / (jnp.sqrt(v_hat) + ADAM_EPS) + ADAM_WD * w)
    return delta, m, v


def _adamw_tiled(w, g, m, v, rows, name):
    r, c = w.shape

    def body(w_ref, g_ref, m_ref, v_ref, go_ref, d_ref, nm_ref, nv_ref):
        g = g_ref[...]
        go_ref[...] = g
        d_ref[...], nm_ref[...], nv_ref[...] = _adamw_math(w_ref[...], g, m_ref[...], v_ref[...])

    spec = pl.BlockSpec((rows, c), lambda i: (i, 0))
    return pl.pallas_call(
        body, name=name, grid=(pl.cdiv(r, rows),), in_specs=[spec] * 4, out_specs=[spec] * 4,
        out_shape=[jax.ShapeDtypeStruct((r, c), _F32)] * 4,
        compiler_params=_params(("arbitrary",)),
    )(w, g, m, v)


def _adamw_small(ws, gs, ms, vs):
    n = len(ws)

    def body(*refs):
        ins, outs = refs[:4 * n], refs[4 * n:]
        for k in range(n):
            d, m, v = _adamw_math(ins[k][...], ins[n + k][...], ins[2 * n + k][...], ins[3 * n + k][...])
            outs[k][...], outs[n + k][...], outs[2 * n + k][...] = d, m, v

    shapes = [jax.ShapeDtypeStruct(w.shape, _F32) for w in ws]
    res = pl.pallas_call(body, name="adamw_small", out_shape=shapes * 3, compiler_params=_params())(*ws, *gs, *ms, *vs)
    return res[:n], res[n:2 * n], res[2 * n:]


def _place():
    x, y, c = lax.axis_index("x"), lax.axis_index("y"), lax.axis_index("c")
    chips = [(1 - x, y), (x, 1 - y), (1 - x, 1 - y)]
    return x, y, c, chips


def _half(ref, slot, c, rows):
    hr = rows // 2
    return ref.at[slot, pl.ds(pl.multiple_of(c * hr, 8), hr), :]


def _all_gather(shards):
    n = len(shards)

    def body(*refs):
        ins, outs = refs[:n], refs[n:2 * n]
        send, recv = refs[2 * n:]
        x, y, c, chips = _place()
        me = 2 * x + y
        sibling = (x, y, 1 - c)

        def copy(k, t, slot, half, to, src=None):
            rows = ins[t].shape[0]
            dst = _half(outs[t], slot, half, rows)
            return pltpu.make_async_remote_copy(
                src_ref=dst if src is None else src, dst_ref=dst,
                send_sem=send.at[k], recv_sem=recv.at[k], device_id=to, device_id_type=MESH)

        first, passed = [], []
        for t in range(n):
            hr = ins[t].shape[0] // 2
            mine = ins[t].at[pl.ds(pl.multiple_of(c * hr, 8), hr), :]
            for j, chip in enumerate(chips):
                first.append(copy(6 * t + j, t, me, c, (*chip, c), src=mine))
        for cp in first:
            cp.start()
        for t in range(n):
            for j, (cx, cy) in enumerate(chips):
                copy(6 * t + j, t, 2 * cx + cy, c, sibling).wait_recv()
                fwd = copy(6 * t + 3 + j, t, 2 * cx + cy, c, sibling)
                fwd.start()
                passed.append(fwd)
        for t in range(n):
            for j, (cx, cy) in enumerate(chips):
                copy(6 * t + 3 + j, t, 2 * cx + cy, 1 - c, sibling).wait_recv()
        for cp in first + passed:
            cp.wait_send()

    any_spec = pl.BlockSpec(memory_space=pl.ANY)
    return pl.pallas_call(
        body, name="all_gather",
        in_specs=[any_spec] * n, out_specs=[any_spec] * n,
        out_shape=[jax.ShapeDtypeStruct((N_SHARD,) + s.shape, s.dtype) for s in shards],
        scratch_shapes=[pltpu.SemaphoreType.DMA((6 * n,)), pltpu.SemaphoreType.DMA((6 * n,))],
    )(*shards)


def _own_slot(gathered, own):
    me = 2 * lax.axis_index("x") + lax.axis_index("y")
    slot = lax.broadcasted_iota(jnp.int32, (N_SHARD,) + (1,) * own.ndim, 0)
    return jnp.where(slot == me, own[None], gathered)


def _sibling_exchange(bufs, tag=""):
    n = len(bufs)

    def body(*refs):
        ins, got = refs[:n], refs[n:2 * n]
        send, recv = refs[2 * n:]
        x, y, c, _ = _place()
        cps = []
        for t in range(n):
            hr = ins[t].shape[0] // (2 * N_SHARD)
            for s in range(N_SHARD):
                k = N_SHARD * t + s
                cps.append(pltpu.make_async_remote_copy(
                    src_ref=ins[t].at[pl.ds(pl.multiple_of((2 * s + 1 - c) * hr, 8), hr), :], dst_ref=got[t].at[s],
                    send_sem=send.at[k], recv_sem=recv.at[k], device_id=(x, y, 1 - c), device_id_type=MESH))
        for cp in cps:
            cp.start()
        for cp in cps:
            cp.wait()

    any_spec = pl.BlockSpec(memory_space=pl.ANY)
    return pl.pallas_call(
        body, name="sibling_exchange" + tag,
        in_specs=[any_spec] * n, out_specs=[any_spec] * n,
        out_shape=[jax.ShapeDtypeStruct((N_SHARD, b.shape[0] // (2 * N_SHARD), b.shape[1]), b.dtype) for b in bufs],
        scratch_shapes=[pltpu.SemaphoreType.DMA((N_SHARD * n,))] * 2,
    )(*bufs)


def _half_sum(bufs, got, wire, c, tag=""):
    n = len(bufs)

    def body(c_ref, *refs):
        ins, gots, outs = refs[:n], refs[n:2 * n], refs[2 * n:]
        for t in range(n):
            outs[t][0] = (ins[t][...] + gots[t][0]).astype(outs[t].dtype)

    slot = lambda g: pl.BlockSpec((1,) + tuple(g.shape[1:]), lambda s, c_ref: (s, 0, 0))
    grid_spec = pltpu.PrefetchScalarGridSpec(
        num_scalar_prefetch=1, grid=(N_SHARD,),
        in_specs=[pl.BlockSpec(tuple(g.shape[1:]), lambda s, c_ref: (2 * s + c_ref[0], 0)) for g in got]
                 + [slot(g) for g in got],
        out_specs=[slot(g) for g in got])
    return pl.pallas_call(
        body, name="half_sum" + tag, grid_spec=grid_spec,
        out_shape=[jax.ShapeDtypeStruct(g.shape, d) for g, d in zip(got, wire)],
        compiler_params=_params(("arbitrary",)),
    )(c, *bufs, *got)


def _chip_exchange_copies(ins, outs, send, recv):
    x, y, c, chips = _place()
    me = 2 * x + y
    return [pltpu.make_async_remote_copy(
        src_ref=ins[t].at[2 * cx + cy], dst_ref=outs[t].at[me], send_sem=send.at[3 * t + j], recv_sem=recv.at[3 * t + j],
        device_id=(cx, cy, c), device_id_type=MESH) for t in range(len(ins)) for j, (cx, cy) in enumerate(chips)]


def _chip_exchange(parts):
    n = len(parts)

    def body(*refs):
        cps = _chip_exchange_copies(refs[:n], refs[n:2 * n], *refs[2 * n:])
        for cp in cps:
            cp.start()
        for cp in cps:
            cp.wait()

    any_spec = pl.BlockSpec(memory_space=pl.ANY)
    return pl.pallas_call(
        body, name="chip_exchange",
        in_specs=[any_spec] * n, out_specs=[any_spec] * n,
        out_shape=[jax.ShapeDtypeStruct(p.shape, p.dtype) for p in parts],
        scratch_shapes=[pltpu.SemaphoreType.DMA((3 * n,))] * 2,
    )(*parts)


def _chip_sum(parts, got, me, tag=""):
    n = len(got)

    def body(me_ref, *refs):
        owns, gots, outs = refs[:n], refs[n:2 * n], refs[2 * n:]
        s = pl.program_id(0)
        for t in range(n):
            term = jnp.where(s == me_ref[0], owns[t][0], gots[t][0]).astype(_F32)

            @pl.when(s == 0)
            def _():
                outs[t][...] = term

            @pl.when(s > 0)
            def _():
                outs[t][...] += term

    blk = lambda g: (1,) + tuple(g.shape[1:])
    grid_spec = pltpu.PrefetchScalarGridSpec(
        num_scalar_prefetch=1, grid=(N_SHARD,),
        in_specs=[pl.BlockSpec(blk(g), lambda s, me_ref: (me_ref[0], 0, 0)) for g in got]
                 + [pl.BlockSpec(blk(g), lambda s, me_ref: (jnp.where(s == me_ref[0], (s + 1) % N_SHARD, s), 0, 0))
                    for g in got],
        out_specs=[pl.BlockSpec(tuple(g.shape[1:]), lambda s, me_ref: (0, 0)) for g in got])
    return pl.pallas_call(
        body, name="chip_sum" + tag, grid_spec=grid_spec,
        out_shape=[jax.ShapeDtypeStruct(g.shape[1:], _F32) for g in got],
        compiler_params=_params(("arbitrary",)),
    )(me, *parts, *got)


def _sibling_share(halves, tag=""):
    n = len(halves)

    def body(*refs):
        ins, outs = refs[:n], refs[n:2 * n]
        send, recv = refs[2 * n:]
        x, y, c, _ = _place()
        cps = [pltpu.make_async_remote_copy(
            src_ref=ins[t], dst_ref=outs[t], send_sem=send.at[t], recv_sem=recv.at[t],
            device_id=(x, y, 1 - c), device_id_type=MESH) for t in range(n)]
        for cp in cps:
            cp.start()
        for cp in cps:
            cp.wait()

    any_spec = pl.BlockSpec(memory_space=pl.ANY)
    return pl.pallas_call(
        body, name="sibling_share" + tag,
        in_specs=[any_spec] * n, out_specs=[any_spec] * n,
        out_shape=[jax.ShapeDtypeStruct(h.shape, h.dtype) for h in halves],
        scratch_shapes=[pltpu.SemaphoreType.DMA((n,))] * 2,
    )(*halves)


def _as_operand(index):
    return jnp.asarray(index, jnp.int32).reshape(1)


def _chip_partials(bufs, wire, tag=""):
    return _half_sum(bufs, _sibling_exchange(bufs, tag), wire, _as_operand(lax.axis_index("c")), tag)


def _finish_reduction(parts, landed, tag=""):
    c = lax.axis_index("c")
    half = _chip_sum(parts, landed, _as_operand(2 * lax.axis_index("x") + lax.axis_index("y")), tag)
    other = _sibling_share(half, tag)
    return [jnp.concatenate([jnp.where(c == 0, h, o), jnp.where(c == 0, o, h)], axis=0) for h, o in zip(half, other)]


def _reduce_scatter(bufs, wire, tag=""):
    parts = _chip_partials(bufs, wire, tag)
    return _finish_reduction(parts, _chip_exchange(parts), tag)


REPL_SIZES = (D_MODEL, D_MODEL, N_HEADS, D_CONV, D_CONV, D_CONV, D_MODEL, D_MODEL)
REPL_ROWS = 48


def _pack_repl(parts):
    flat = jnp.concatenate([p.reshape(-1).astype(_F32) for p in parts])
    return jnp.pad(flat, (0, REPL_ROWS * 128 - flat.shape[0])).reshape(REPL_ROWS, 128)


def _unpack_repl(packed, shapes):
    flat = packed.reshape(-1)
    out, off = [], 0
    for shape, size in zip(shapes, REPL_SIZES):
        out.append(flat[off:off + size].reshape(shape))
        off += size
    return out


W_IN_PIECES = (("q", 0, 512), ("k", 512, 1024), ("v", 1024, 1536), ("f", 1536, 1544), ("ga", 1544, 2056),
               ("u", 2056, 2568), ("ug", 2568, 3080), ("gc", 3080, 3592))


def _projection_operand(slots, own=None, me=None):
    rows = dict((name, (a, b)) for name, a, b in W_IN_PIECES)
    segments = []
    for name in ("q", "k", "v", "ga", "u", "ug", "gc", "f"):
        a, b = rows[name]
        for s in range(N_SHARD):
            lo, hi = max(a, W_IN_SHARD * s), min(b, W_IN_SHARD * (s + 1))
            if lo < hi:
                seg = slots[s, lo - W_IN_SHARD * s:hi - W_IN_SHARD * s, :]
                if own is not None:
                    seg = jnp.where(me == s, own[lo - W_IN_SHARD * s:hi - W_IN_SHARD * s, :], seg)
                segments.append(seg)
    return jnp.concatenate(segments + [jnp.zeros((128 - N_HEADS, D_MODEL), slots.dtype)], axis=0)


def _gradient_slots(pieces):
    segments = []
    for s in range(N_SHARD):
        for name, a, b in W_IN_PIECES:
            lo, hi = max(a, W_IN_SHARD * s), min(b, W_IN_SHARD * (s + 1))
            if lo < hi:
                segments.append(pieces[name][lo - a:hi - a, :])
        segments.append(jnp.zeros((W_IN_SLOT - W_IN_SHARD, D_MODEL), _F32))
    return jnp.concatenate(segments, axis=0)


def _local_grads(x, target, meta, ln_in_g, ln_in_b, w_in_slots, b_f, conv_w, conv_b, ln_conv_g, ln_conv_b,
                 w_pw, w_out, ln_out_g, ln_out_b, chip_partials=None, w_in_own=None):
    vec = lambda a: a.reshape(1, -1).astype(_F32)
    head = jnp.concatenate([jnp.zeros((ROW_PAD, D_MODEL), _F32), meta], axis=0)
    wpt = _projection_operand(w_in_slots, w_in_own, 2 * lax.axis_index("x") + lax.axis_index("y") if w_in_own is not None else None)
    bf = jnp.pad(vec(b_f), ((0, 0), (0, 128 - N_HEADS)))
    cw = jnp.pad(conv_w, ((0, 32 - CONV_WIDTH), (0, 0)))

    h, q, k, v, ga, u, ug, gc, f, vt = _ln_proj(x, head, vec(ln_in_g), vec(ln_in_b), wpt)
    ccol = _decay_cumsum(f, bf)
    o, lse = _attn_fwd(q, k, vt, ccol)
    z, pw = _conv_fwd(u, ug, cw, vec(conv_b), vec(ln_conv_g), vec(ln_conv_b), w_pw)
    dr, do, dga, dpw, dgc, delta, dw_out, dg_out, db_out, loss = _out_loss_bwd(
        o, ga, pw, gc, h, target, w_out, vec(ln_out_g), vec(ln_out_b))
    dq, dk, dv, dck, dcq = _attn_bwd(q, k, v, do, lse, delta, ccol)
    df, dbf = _decay_bwd(dck, dcq, f, bf)
    dz, dw_pw, dg_conv, db_conv, dconv_b = _conv_bwd_rows(dpw, z, vec(ln_conv_g), vec(ln_conv_b), w_pw)
    du, dug, dcw = _conv_bwd_taps(dz, u, ug, cw)
    pieces = (dq, dk, dv, dga, du, dug, dgc)
    dwq, dwk, dwv, dwga = _proj_bwd_w(h, pieces[:4], "proj_bwd_w_attn")
    dwu, dwug, dwgc, dwf = _proj_bwd_w(h, pieces[4:] + (df,), "proj_bwd_w_conv")
    dw_in_t = _gradient_slots(dict(q=dwq, k=dwk, v=dwv, f=dwf, ga=dwga, u=dwu, ug=dwug, gc=dwgc))
    parts = chip_partials([dw_in_t, dw_pw, dw_out]) if chip_partials else []
    grad_x, dmeta, dg_in, db_in, *landed = _proj_bwd_x(pieces, df, wpt, dr, x, head, vec(ln_in_g), parts)
    grads = dict(meta=dmeta, ln_in_g=dg_in, ln_in_b=db_in, w_in_t=dw_in_t, b_f=dbf[:, :N_HEADS],
                 conv_w=dcw[:CONV_WIDTH], conv_b=dconv_b, ln_conv_g=dg_conv, ln_conv_b=db_conv,
                 w_pw=dw_pw, w_out=dw_out, ln_out_g=dg_out, ln_out_b=db_out, reduced=(parts, landed))
    return loss, grad_x, grads


def kernel(x, meta, ln_in_g, ln_in_b, w_in, b_f, conv_w, conv_b, ln_conv_g, ln_conv_b, w_pw, w_out, ln_out_g, ln_out_b, loss_target, m_meta, m_ln_in_g, m_ln_in_b, m_w_in, m_b_f, m_conv_w, m_conv_b, m_ln_conv_g, m_ln_conv_b, m_w_pw, m_w_out, m_ln_out_g, m_ln_out_b, v_meta, v_ln_in_g, v_ln_in_b, v_w_in, v_b_f, v_conv_w, v_conv_b, v_ln_conv_g, v_ln_conv_b, v_w_pw, v_w_out, v_ln_out_g, v_ln_out_b):
    conv_w2, w_pw2, w_out2 = conv_w[0], w_pw[0], w_out[0]
    w_t = jnp.transpose(w_in[0])

    small = jnp.concatenate([jnp.pad(conv_w2, ((0, 1), (0, 0))), meta.reshape(32, 128)], axis=0)
    shards = [jnp.pad(w_t.astype(_MM), ((0, W_IN_SLOT - W_IN_SHARD), (0, 0))), w_pw2.astype(_MM), w_out2.astype(_MM), small]
    g_in, *rest = _all_gather(shards)
    g_pw, g_out, g_small = (_own_slot(g, s) for g, s in zip(rest, shards[1:]))
    w_pw_full = g_pw.reshape(D_CONV, D_CONV)
    w_out_full = g_out.reshape(D_MODEL, D_MODEL)
    conv_w_full = jnp.transpose(g_small[:, :CONV_WIDTH, :], (1, 0, 2)).reshape(CONV_WIDTH, D_CONV)
    meta_full = jnp.transpose(g_small[:, 32:64, :].reshape(N_SHARD, N_META, 256), (1, 0, 2)).reshape(N_META, D_MODEL)

    loss, grad_x, gr = _local_grads(x[0], loss_target[0], meta_full, ln_in_g, ln_in_b, g_in, b_f, conv_w_full,
                                    conv_b, ln_conv_g, ln_conv_b, w_pw_full, w_out_full, ln_out_g, ln_out_b,
                                    chip_partials=lambda big: _chip_partials(big, [_MM, _MM, _MM]), w_in_own=shards[0])
    grad_x = grad_x[None]
    gw_slot, gw_pw, gw_out = _finish_reduction(*gr["reduced"])

    repl_names = ("ln_in_g", "ln_in_b", "b_f", "conv_b", "ln_conv_g", "ln_conv_b", "ln_out_g", "ln_out_b")
    repl = _pack_repl([gr[n] for n in repl_names] + [loss])
    gcw = jnp.transpose(jnp.pad(gr["conv_w"], ((0, 1), (0, 0))).reshape(32, N_SHARD, 128), (1, 0, 2))
    gmeta = jnp.transpose(gr["meta"].reshape(N_META, N_SHARD, 256), (1, 0, 2)).reshape(N_SHARD, 32, 128)
    small_g = jnp.concatenate([gcw, gmeta, jnp.broadcast_to(repl[None], (N_SHARD, REPL_ROWS, 128))], axis=1)
    small_rows = small_g.shape[1]
    gsmall, = _reduce_scatter([small_g.reshape(N_SHARD * small_rows, 128)], [_F32], "_small")
    g_conv_w = gsmall[:CONV_WIDTH]
    g_meta = gsmall[32:64].reshape(N_META, 256)
    g_repl = gsmall[64:64 + REPL_ROWS]

    repl_w = (ln_in_g, ln_in_b, b_f, conv_b, ln_conv_g, ln_conv_b, ln_out_g, ln_out_b)
    repl_m = (m_ln_in_g, m_ln_in_b, m_b_f, m_conv_b, m_ln_conv_g, m_ln_conv_b, m_ln_out_g, m_ln_out_b)
    repl_v = (v_ln_in_g, v_ln_in_b, v_b_f, v_conv_b, v_ln_conv_g, v_ln_conv_b, v_ln_out_g, v_ln_out_b)
    zero = jnp.zeros((1,), _F32)
    res_t = _adamw_tiled(w_t, gw_slot, jnp.transpose(m_w_in[0]), jnp.transpose(v_w_in[0]), 256, "adamw_w_in")
    gw_in, d_in, nm_in, nv_in = (jnp.transpose(a) for a in res_t)
    ws = [w_pw2, w_out2, conv_w2, meta, _pack_repl(repl_w + (zero,))]
    gs = [gw_pw, gw_out, g_conv_w, g_meta, g_repl]
    ms = [m_w_pw[0], m_w_out[0], m_conv_w[0], m_meta, _pack_repl(repl_m + (zero,))]
    vs = [v_w_pw[0], v_w_out[0], v_conv_w[0], v_meta, _pack_repl(repl_v + (zero + 1.0,))]
    ds, nms, nvs = _adamw_small(ws, gs, ms, vs)

    shapes = [w.shape for w in repl_w]
    total_loss = g_repl.reshape(-1)[sum(REPL_SIZES)]
    names = ("meta", "ln_in_g", "ln_in_b", "w_in", "b_f", "conv_w", "conv_b", "ln_conv_g", "ln_conv_b",
             "w_pw", "w_out", "ln_out_g", "ln_out_b")

    def assemble(t_in, smalls):
        pw_, out_, cw_, meta_, packed = smalls
        r = dict(zip(repl_names, _unpack_repl(packed, shapes)))
        r.update(meta=meta_, w_in=t_in[None], conv_w=cw_[None], w_pw=pw_[None], w_out=out_[None])
        return [r[n] for n in names]

    return (total_loss, grad_x, *assemble(gw_in, gs), *assemble(d_in, ds), *assemble(nm_in, nms),
            *assemble(nv_in, nvs))
```

```python
import functools

import jax
import jax.numpy as jnp
from jax import lax
from jax.experimental import pallas as pl
from jax.experimental.pallas import tpu as pltpu

D_MODEL = 1024
N_META = 16
D_ATTN = 512
D_CONV = 512
N_HEADS = 8
HEAD_DIM = 64
N_PAIRS = 4
CONV_WIDTH = 31
LN_EPS = 1e-5
ALPHA = 2.0 ** 0.25
ATTN_SCALE = 0.125
D_IN = 3592
OFF_F = 3 * D_ATTN
D_IN_PAD = 3712
N_SHARD = 4
W_IN_SHARD = D_IN // N_SHARD
W_IN_SLOT = 928

ROW_PAD = 112
ROW_X = ROW_PAD + N_META
HALO = 32
MASKED = 1e30

ADAM_LR = 0.001
ADAM_B1 = 0.9
ADAM_B2 = 0.999
ADAM_EPS = 1e-08
ADAM_WD = 0.01
ADAM_STEP = 10

VMEM_LIMIT = 56 * 1024 * 1024

_MM = jnp.bfloat16
_F32 = jnp.float32
MESH = pl.DeviceIdType.MESH


def _params(sem=None):
    return pltpu.CompilerParams(dimension_semantics=sem, vmem_limit_bytes=VMEM_LIMIT)


def _row_tile(n_rows):
    for t in (384, 256, 128):
        if n_rows % t == 0:
            return t
    raise ValueError(f"padded sequence length {n_rows} is not a multiple of 128")


def _sigmoid(x):
    return 1.0 / (1.0 + jnp.exp(-x))


def _ln_stats(x):
    mu = jnp.mean(x, axis=-1, keepdims=True)
    xc = x - mu
    var = jnp.mean(xc * xc, axis=-1, keepdims=True)
    rstd = lax.rsqrt(var + LN_EPS)
    return xc * rstd, rstd


def _ln_bwd(dy, xhat, rstd, g):
    dxh = dy * g
    m1 = jnp.mean(dxh, axis=-1, keepdims=True)
    m2 = jnp.mean(dxh * xhat, axis=-1, keepdims=True)
    return rstd * (dxh - m1 - xhat * m2)


def _dot(a, b):
    return jnp.dot(a, b, preferred_element_type=_F32)


def _dot_t0(a, b):
    return lax.dot_general(a, b, (((0,), (0,)), ((), ())), preferred_element_type=_F32)


def _dot_t1(a, b):
    return lax.dot_general(a, b, (((1,), (1,)), ((), ())), preferred_element_type=_F32)


def _dot_exact(a, b):
    return jnp.dot(a, b, preferred_element_type=_F32, precision=lax.Precision.HIGHEST)


def _full(shape):
    return pl.BlockSpec(shape, lambda *_: (0,) * len(shape))


def _x_specs(t):
    n = t // ROW_X
    return [pl.BlockSpec((ROW_X, D_MODEL), lambda i, s=s: (jnp.maximum(i * n + s - 1, 0), 0)) for s in range(n)]


def _x_tile(i, refs, head_ref=None):
    first = refs[0][...]
    if head_ref is not None:
        first = jnp.where(i == 0, head_ref[...], first)
    return jnp.concatenate([first] + [r[...] for r in refs[1:]], axis=0)


def _ln_proj(x, head, g, b, wpt):
    lp = ROW_X + x.shape[0]
    t = _row_tile(lp)
    nx = t // ROW_X

    def body(*refs):
        x_refs, (head_ref, g_ref, b_ref, wt_ref) = refs[:nx], refs[nx:nx + 4]
        h_ref, q_ref, k_ref, v_ref, ga_ref, u_ref, ug_ref, gc_ref, f_ref, vt_ref = refs[nx + 4:]
        i = pl.program_id(0)
        xhat, _ = _ln_stats(_x_tile(i, x_refs, head_ref))
        rows = i * t + lax.broadcasted_iota(jnp.int32, (t, 1), 0)
        h = jnp.where(rows >= ROW_PAD, xhat * g_ref[...] + b_ref[...], 0.0)
        h_ref[...] = h
        hb = h.astype(_MM)
        q_ref[...] = (_dot_t1(hb, wt_ref[0:512, :]) * ATTN_SCALE).astype(q_ref.dtype)
        k_ref[...] = _dot_t1(hb, wt_ref[512:1024, :]).astype(k_ref.dtype)
        v = _dot_t1(hb, wt_ref[1024:1536, :])
        v_ref[...] = v.astype(v_ref.dtype)
        vt_ref[...] = v.T.astype(vt_ref.dtype)
        ga_ref[...] = _dot_t1(hb, wt_ref[1536:2048, :])
        u_ref[...] = _dot_t1(hb, wt_ref[2048:2560, :])
        ug_ref[...] = _dot_t1(hb, wt_ref[2560:3072, :])
        gc_ref[...] = _dot_t1(hb, wt_ref[3072:3584, :])
        f_ref[...] = _dot_t1(hb, wt_ref[3584:3712, :])

    row = lambda w: pl.BlockSpec((t, w), lambda i: (i, 0))
    f32 = lambda w: jax.ShapeDtypeStruct((lp, w), _F32)
    mm = lambda w: jax.ShapeDtypeStruct((lp, w), _MM)
    return pl.pallas_call(
        body, name="ln_proj", grid=(lp // t,),
        in_specs=_x_specs(t) + [_full((ROW_X, D_MODEL)), _full((1, D_MODEL)), _full((1, D_MODEL)),
                                _full((D_IN_PAD, D_MODEL))],
        out_specs=[row(D_MODEL)] + [row(512)] * 7 + [row(128), pl.BlockSpec((512, t), lambda i: (0, i))],
        out_shape=[f32(D_MODEL), mm(512), mm(512), mm(512), f32(512), f32(512), f32(512), f32(512), f32(128),
                   jax.ShapeDtypeStruct((512, lp), _MM)],
        compiler_params=_params(("arbitrary",)),
    )(*([x] * nx), head, g, b, wpt)


def _log_sigmoid(z):
    return jnp.minimum(z, 0.0) - jnp.log(1.0 + jnp.exp(-jnp.abs(z)))


def _decay_cumsum(f, bf):
    lp = f.shape[0]
    t = _row_tile(lp)

    def body(f_ref, bf_ref, ccol_ref):
        r = lax.broadcasted_iota(jnp.int32, (t, t), 0)
        c = lax.broadcasted_iota(jnp.int32, (t, t), 1)
        tril = (c <= r).astype(_F32)

        def step(bi, carry):
            rows = pl.ds(pl.multiple_of(bi * t, 128), t)
            lf = _log_sigmoid(f_ref[rows, :] + bf_ref[...])
            cb = _dot_exact(tril, lf) + carry
            for p in range(N_PAIRS):
                ccol_ref[p, rows, :] = cb if p == 0 else pltpu.roll(cb, 128 - 2 * p, axis=1)
            return cb[t - 1:t, :]

        lax.fori_loop(0, lp // t, step, jnp.zeros((1, 128), _F32))

    return pl.pallas_call(
        body, name="decay_cumsum",
        out_shape=jax.ShapeDtypeStruct((N_PAIRS, lp, 128), _F32),
        compiler_params=_params(),
    )(f, bf)


def _split3(x):
    hi = x.astype(_MM)
    r1 = x - hi.astype(_F32)
    mid = r1.astype(_MM)
    return hi, mid, (r1 - mid.astype(_F32)).astype(_MM)


def _augment(x, hd, first, second, transposed=False):
    lane = lax.broadcasted_iota(jnp.int32, (1, 128), 1)
    base = HEAD_DIM * (1 - hd)
    out = jnp.where(lane // HEAD_DIM == hd, x, jnp.zeros_like(x))
    for slot, val in enumerate((first, second)):
        lo = base + 3 * slot
        if val is None:
            continue
        if isinstance(val, float):
            out = jnp.where((lane >= lo) & (lane < lo + 3), jnp.asarray(val, x.dtype), out)
        else:
            for n, piece in enumerate(val):
                out = jnp.where(lane == lo + n, piece[:, hd:hd + 1], out)
    return _transpose_mm(out) if transposed else out


def _transpose_mm(x):
    return x.astype(_F32).T.astype(_MM)


def _diag_valid(i, t):
    sub = lax.broadcasted_iota(jnp.int32, (t, t), 0)
    return (sub <= lax.broadcasted_iota(jnp.int32, (t, t), 1)) & (i * t + sub >= ROW_PAD)


def _key_pieces(c, first_row):
    pos = first_row + lax.broadcasted_iota(jnp.int32, (c.shape[0], 1), 0)
    return _split3(jnp.where(pos >= ROW_PAD, -c, -MASKED))


def _key_operand(k, pieces, hd, transposed=False):
    return _augment(k, hd, pieces, 1.0, transposed)


def _attn_fwd(q, k, vt, ccol):
    lp = q.shape[0]
    t = _row_tile(lp)
    nb = lp // t

    def body(q_ref, k_ref, vt_ref, ccol_ref, o_ref, lse_ref, ka_ref):
        i = pl.program_id(1)

        @pl.when(i == 0)
        def _():
            for n in range(nb):
                rows = slice(n * t, (n + 1) * t)
                pieces = _key_pieces(ccol_ref[0, rows, :], n * t)
                for hd in range(2):
                    ka_ref[hd, rows, :] = _key_operand(k_ref[rows, :], pieces, hd)

        qats = [_augment(q_ref[...], hd, 1.0, None, transposed=True) for hd in range(2)]

        def scores(j):
            keys = pl.ds(pl.multiple_of(j * t, 128), t)
            return tuple(_dot(ka_ref[hd, keys, :], qats[hd]) for hd in range(2))

        def update(j, sts, states, diag):
            vt = vt_ref[:, pl.ds(pl.multiple_of(j * t, 128), t)]
            new = []
            for st, (m, l, acc) in zip(sts, states):
                if diag:
                    st = jnp.where(_diag_valid(i, t), st, -MASKED)
                m_new = jnp.maximum(m, jnp.max(st, axis=0, keepdims=True))
                a = jnp.exp(m - m_new)
                pt = jnp.exp(st - m_new)
                l = a * l + jnp.sum(pt, axis=0, keepdims=True)
                new.append((m_new, l, a * acc + _dot(vt, pt.astype(_MM))))
            return tuple(new)

        def step(j, carry):
            sts, states = carry
            return scores(j + 1), update(j, sts, states, False)

        def step2(n, carry):
            sts, states = carry
            nxt, last = scores(2 * n + 1), scores(2 * n + 2)
            return last, update(2 * n + 1, nxt, update(2 * n, sts, states, False), False)

        init = (jnp.full((1, t), -MASKED, _F32), jnp.zeros((1, t), _F32), jnp.zeros((128, t), _F32))
        carry = lax.fori_loop(0, i // 2, step2, (scores(0), (init, init)))
        sts, states = lax.fori_loop(0, i % 2, lambda _, c: step(i - 1, c), carry)
        (m0, l0, acc0), (m1, l1, acc1) = update(i, sts, states, True)
        row = lax.broadcasted_iota(jnp.int32, (128, t), 0)
        o_ref[...] = jnp.where(row < HEAD_DIM, acc0 / l0, acc1 / l1).T
        lse_ref[0] = jnp.where(row == 0, m0 + jnp.log(l0), jnp.where(row == 1, m1 + jnp.log(l1), 0.0)).T

    return pl.pallas_call(
        body, name="attn_fwd", grid=(N_PAIRS, nb),
        in_specs=[pl.BlockSpec((t, 128), lambda p, i: (i, p)),
                  pl.BlockSpec((lp, 128), lambda p, i: (0, p)),
                  pl.BlockSpec((128, lp), lambda p, i: (p, 0)),
                  pl.BlockSpec((1, lp, 128), lambda p, i: (p, 0, 0))],
        out_specs=[pl.BlockSpec((t, 128), lambda p, i: (i, p)),
                   pl.BlockSpec((1, t, 128), lambda p, i: (p, i, 0))],
        out_shape=[jax.ShapeDtypeStruct((lp, D_ATTN), _F32), jax.ShapeDtypeStruct((N_PAIRS, lp, 128), _F32)],
        scratch_shapes=[pltpu.VMEM((2, lp, 128), _MM)],
        compiler_params=_params(("arbitrary", "arbitrary")),
    )(q, k, vt, ccol)


CONV_CHUNK = 32


def _shifted_windows(win, sh, rows):
    sh[0] = win[...]
    for b in range(1, 8):
        sh[b, 0:rows - 8, :] = win[b:b + rows - 8, :]


def _tap_rows(sh, offsets, r0):
    out = {}
    for b in range(8):
        ds = [d for d in offsets if d % 8 == b]
        if ds:
            lo, hi = min(ds) // 8, max(ds) // 8
            rows = sh[b, pl.ds(pl.multiple_of(r0 + 8 * lo, 8), CONV_CHUNK + 8 * (hi - lo)), :]
            for d in ds:
                out[d] = rows[8 * (d // 8 - lo):8 * (d // 8 - lo) + CONV_CHUNK, :]
    return out


def _conv_fwd(u, ug, cw, cb, g, b, wpw):
    lp = u.shape[0]
    t = _row_tile(lp)

    def body(u_ref, ug_ref, cw_ref, cb_ref, g_ref, b_ref, wpw_ref, z_ref, pw_ref, win, sh):
        i = pl.program_id(0)

        @pl.when(i == 0)
        def _():
            win[0:HALO, :] = jnp.zeros((HALO, D_CONV), _F32)

        win[HALO:HALO + t, :] = u_ref[...] * _sigmoid(ug_ref[...])
        _shifted_windows(win, sh, t + HALO)

        def chunk(c, carry):
            r0 = c * CONV_CHUNK
            acc = jnp.broadcast_to(cb_ref[...], (CONV_CHUNK, D_CONV))
            first = HALO - (CONV_WIDTH - 1)
            taps = _tap_rows(sh, range(first, first + CONV_WIDTH), r0)
            for j in range(CONV_WIDTH):
                acc = acc + cw_ref[j:j + 1, :] * taps[first + j]
            z_ref[pl.ds(pl.multiple_of(r0, 8), CONV_CHUNK), :] = acc
            return carry

        lax.fori_loop(0, t // CONV_CHUNK, chunk, 0)
        xhat, _ = _ln_stats(z_ref[...])
        zn = xhat * g_ref[...] + b_ref[...]
        a = zn * _sigmoid(zn)
        pw_ref[...] = _dot(a.astype(_MM), wpw_ref[...])
        win[0:HALO, :] = win[t:t + HALO, :]

    row = pl.BlockSpec((t, D_CONV), lambda i: (i, 0))
    vec = _full((1, D_CONV))
    return pl.pallas_call(
        body, name="conv_fwd", grid=(lp // t,),
        in_specs=[row, row, _full((32, D_CONV)), vec, vec, vec, _full((D_CONV, D_CONV))],
        out_specs=[row, row],
        out_shape=[jax.ShapeDtypeStruct((lp, D_CONV), _F32)] * 2,
        scratch_shapes=[pltpu.VMEM((t + HALO, D_CONV), _F32), pltpu.VMEM((8, t + HALO, D_CONV), _F32)],
        compiler_params=_params(("arbitrary",)),
    )(u, ug, cw, cb, g, b, wpw)


def _out_loss_bwd(o, ga, pw, gc, h, target, w_out, g, b):
    lp = o.shape[0]
    t = _row_tile(lp)
    nx = t // ROW_X

    def body(*refs):
        o_ref, ga_ref, pw_ref, gc_ref, h_ref = refs[:5]
        tgt_refs, (w_ref, g_ref, b_ref) = refs[5:5 + nx], refs[5 + nx:8 + nx]
        dr_ref, do_ref, dga_ref, dpw_ref, dgc_ref, delta_ref, dw_ref, dg_ref, db_ref, loss_ref = refs[8 + nx:]
        i = pl.program_id(0)

        @pl.when(i == 0)
        def _():
            dw_ref[...] = jnp.zeros_like(dw_ref)
            dg_ref[...] = jnp.zeros_like(dg_ref)
            db_ref[...] = jnp.zeros_like(db_ref)
            loss_ref[...] = jnp.zeros_like(loss_ref)

        o, ga, pw, gc = o_ref[...], ga_ref[...], pw_ref[...], gc_ref[...]
        sa, sc = _sigmoid(ga), _sigmoid(gc)
        silu_a, silu_c = ga * sa, gc * sc
        ycat = jnp.concatenate([o * silu_a, pw * silu_c], axis=-1).astype(_MM)
        r = ALPHA * h_ref[...] + _dot(ycat, w_ref[...])
        xhat, rstd = _ln_stats(r)
        out = xhat * g_ref[...] + b_ref[...]
        rows = i * t + lax.broadcasted_iota(jnp.int32, (t, 1), 0)
        diff = jnp.where(rows >= ROW_X, out - _x_tile(i, tgt_refs), 0.0)
        loss_ref[...] += (0.5 / D_MODEL) * jnp.sum(diff * diff, keepdims=True)
        dout = diff * (1.0 / D_MODEL)
        dg_ref[...] += jnp.sum(dout * xhat, axis=0, keepdims=True)
        db_ref[...] += jnp.sum(dout, axis=0, keepdims=True)
        dr = _ln_bwd(dout, xhat, rstd, g_ref[...])
        dr_ref[...] = dr
        drb = dr.astype(_MM)
        dw_ref[...] += _dot_t0(ycat, drb)
        dy = _dot_t1(drb, w_ref[...])
        dya, dyc = dy[:, :D_ATTN], dy[:, D_ATTN:]
        do = dya * silu_a
        do_ref[...] = do.astype(do_ref.dtype)
        dga_ref[...] = (dya * o * (sa * (1.0 + ga * (1.0 - sa)))).astype(dga_ref.dtype)
        dpw_ref[...] = (dyc * silu_c).astype(dpw_ref.dtype)
        dgc_ref[...] = (dyc * pw * (sc * (1.0 + gc * (1.0 - sc)))).astype(dgc_ref.dtype)
        sel = (lax.broadcasted_iota(jnp.int32, (128, 128), 0) // HEAD_DIM
               == lax.broadcasted_iota(jnp.int32, (128, 128), 1)).astype(_F32)
        prod = do * o
        for p in range(N_PAIRS):
            delta_ref[p] = _dot_exact(prod[:, 128 * p:128 * (p + 1)], sel)

    row = lambda w: pl.BlockSpec((t, w), lambda i: (i, 0))
    return pl.pallas_call(
        body, name="out_loss_bwd", grid=(lp // t,),
        in_specs=[row(512)] * 4 + [row(D_MODEL)] + _x_specs(t) + [_full((D_MODEL, D_MODEL))] + [_full((1, D_MODEL))] * 2,
        out_specs=[row(D_MODEL)] + [row(512)] * 4 + [pl.BlockSpec((N_PAIRS, t, 128), lambda i: (0, i, 0)),
                   _full((D_MODEL, D_MODEL)), _full((1, D_MODEL)), _full((1, D_MODEL)), _full((1, 1))],
        out_shape=[jax.ShapeDtypeStruct((lp, D_MODEL), _F32)] + [jax.ShapeDtypeStruct((lp, 512), _MM)] * 4
                  + [jax.ShapeDtypeStruct((N_PAIRS, lp, 128), _F32), jax.ShapeDtypeStruct((D_MODEL, D_MODEL), _F32),
                     jax.ShapeDtypeStruct((1, D_MODEL), _F32), jax.ShapeDtypeStruct((1, D_MODEL), _F32),
                     jax.ShapeDtypeStruct((1, 1), _F32)],
        compiler_params=_params(("arbitrary",)),
    )(o, ga, pw, gc, h, *([target] * nx), w_out, g, b)


BWD_WIDTH = 2


def _attn_bwd(q, k, v, do, lse, delta, ccol):
    lp = q.shape[0]
    t = _row_tile(lp)
    nb = lp // t

    def body(q_ref, k_ref, v_ref, do_ref, lse_ref, delta_ref, ccol_ref,
             dq_ref, dk_ref, dv_ref, dck_ref, dcq_ref,
             qa_ref, qat_ref, doa_ref, doat_ref, dqt_ref):
        j = pl.program_id(1)

        @pl.when(j == 0)
        def _():
            dqt_ref[...] = jnp.zeros_like(dqt_ref)
            for n in range(nb):
                rows = slice(n * t, (n + 1) * t)
                lse_pieces, delta_pieces = _split3(-lse_ref[0, rows, :]), _split3(-delta_ref[0, rows, :])
                for hd in range(2):
                    qa = _augment(q_ref[rows, :], hd, 1.0, lse_pieces)
                    doa = _augment(do_ref[rows, :], hd, delta_pieces, None)
                    qa_ref[hd, rows, :] = qa
                    qat_ref[hd, :, rows] = _transpose_mm(qa)
                    doa_ref[hd, rows, :] = doa
                    doat_ref[hd, :, rows] = _transpose_mm(doa)

        pieces = _key_pieces(ccol_ref[0], j * t)
        kas = [_key_operand(k_ref[...], pieces, hd) for hd in range(2)]
        kats = [_transpose_mm(ka) for ka in kas]
        vas = [_augment(v_ref[...], hd, 1.0, None) for hd in range(2)]

        def block(i, carry, diag, width=1):
            rows = pl.ds(pl.multiple_of(i * t, 128), width * t)
            new = []
            for hd in range(2):
                dk, dv = carry[hd]
                st = _dot(kas[hd], qat_ref[hd, :, rows])
                if diag:
                    st = jnp.where(_diag_valid(j, t), st, -MASKED)
                pt = jnp.exp(st)
                dsb = (pt * _dot(vas[hd], doat_ref[hd, :, rows])).astype(_MM)
                dv = dv + _dot(pt.astype(_MM), doa_ref[hd, rows, :])
                dk = dk + _dot(dsb, qa_ref[hd, rows, :])
                dqt_ref[hd, :, rows] += _dot(kats[hd], dsb)
                new.append((dk, dv))
            return tuple(new)

        zero = jnp.zeros((t, 128), _F32)
        carry = block(j, ((zero, zero), (zero, zero)), True)
        rest = nb - 1 - j
        if nb > BWD_WIDTH:
            carry = lax.fori_loop(0, rest // BWD_WIDTH,
                                  lambda n, c: block(j + 1 + BWD_WIDTH * n, c, False, BWD_WIDTH), carry)
            rest = rest % BWD_WIDTH
        (dk0, dv0), (dk1, dv1) = lax.fori_loop(nb - rest, nb, functools.partial(block, diag=False), carry)
        lane = lax.broadcasted_iota(jnp.int32, (1, 128), 1)
        dk_ref[...] = jnp.where(lane < HEAD_DIM, dk0, dk1).astype(dk_ref.dtype)
        dv_ref[...] = jnp.where(lane < HEAD_DIM, dv0, dv1).astype(dv_ref.dtype)
        dck_ref[0] = jnp.where(lane == 0, dk0[:, HEAD_DIM:HEAD_DIM + 1], jnp.where(lane == 1, dk1[:, 0:1], 0.0))

        @pl.when(j == nb - 1)
        def _():
            row = lax.broadcasted_iota(jnp.int32, (128, t), 0)
            sub = lax.broadcasted_iota(jnp.int32, (8, t), 0)
            for n in range(nb):
                cols = slice(n * t, (n + 1) * t)
                t0, t1 = dqt_ref[0, :, cols], dqt_ref[1, :, cols]
                dq_ref[cols, :] = jnp.where(row < HEAD_DIM, t0, t1).T * ATTN_SCALE
                dcq_ref[0, :, cols] = jnp.where(sub == 0, t0[HEAD_DIM + 3:HEAD_DIM + 4, :],
                                                jnp.where(sub == 1, t1[3:4, :], 0.0))

    blk = pl.BlockSpec((t, 128), lambda p, j: (j, p))
    res = pl.BlockSpec((lp, 128), lambda p, j: (0, p))
    stat = pl.BlockSpec((1, lp, 128), lambda p, j: (p, 0, 0))
    stat_blk = pl.BlockSpec((1, t, 128), lambda p, j: (p, j, 0))
    return pl.pallas_call(
        body, name="attn_bwd", grid=(N_PAIRS, nb),
        in_specs=[res, blk, blk, res, stat, stat, stat_blk],
        out_specs=[res, blk, blk, stat_blk, pl.BlockSpec((1, 8, lp), lambda p, j: (p, 0, 0))],
        out_shape=[jax.ShapeDtypeStruct((lp, D_ATTN), _F32), jax.ShapeDtypeStruct((lp, D_ATTN), _MM),
                   jax.ShapeDtypeStruct((lp, D_ATTN), _MM), jax.ShapeDtypeStruct((N_PAIRS, lp, 128), _F32),
                   jax.ShapeDtypeStruct((N_PAIRS, 8, lp), _F32)],
        scratch_shapes=[pltpu.VMEM((2, lp, 128), _MM), pltpu.VMEM((2, 128, lp), _MM),
                        pltpu.VMEM((2, lp, 128), _MM), pltpu.VMEM((2, 128, lp), _MM),
                        pltpu.VMEM((2, 128, lp), _F32)],
        compiler_params=_params(("arbitrary", "arbitrary")),
    )(q, k, v, do, lse, delta, ccol)


def _decay_bwd(dck, dcq, f, bf):
    lp = f.shape[0]
    nb = lp // 128

    def body(dck_ref, dcq_ref, f_ref, bf_ref, df_ref, dbf_ref):
        r = lax.broadcasted_iota(jnp.int32, (128, 128), 0)
        c = lax.broadcasted_iota(jnp.int32, (128, 128), 1)
        triu = (c >= r).astype(_F32)

        def step(n, carry):
            tail, dbf = carry
            bi = nb - 1 - n
            rows = pl.ds(pl.multiple_of(bi * 128, 128), 128)
            cols = jnp.zeros((128, 128), _F32)
            for p in range(N_PAIRS):
                dcq = jnp.concatenate([dcq_ref[p, :, rows], jnp.zeros((120, 128), _F32)], axis=0).T
                dcp = dcq - dck_ref[p, rows, :]
                cols = cols + (dcp if p == 0 else pltpu.roll(dcp, 2 * p, axis=1))
            lane = lax.broadcasted_iota(jnp.int32, (128, 128), 1)
            cols = jnp.where(lane < N_HEADS, cols, 0.0)
            rev = _dot_exact(triu, cols) + tail
            pos = bi * 128 + lax.broadcasted_iota(jnp.int32, (128, 1), 0)
            df = jnp.where(pos >= ROW_PAD, rev * (1.0 - _sigmoid(f_ref[rows, :] + bf_ref[...])), 0.0)
            df_ref[rows, :] = df.astype(df_ref.dtype)
            return rev[0:1, :], dbf + jnp.sum(df, axis=0, keepdims=True)

        zero = jnp.zeros((1, 128), _F32)
        _, dbf = lax.fori_loop(0, nb, step, (zero, zero), unroll=True)
        dbf_ref[...] = dbf

    return pl.pallas_call(
        body, name="decay_bwd",
        out_shape=[jax.ShapeDtypeStruct((lp, 128), _MM), jax.ShapeDtypeStruct((1, 128), _F32)],
        compiler_params=_params(),
    )(dck, dcq, f, bf)


def _conv_bwd_rows(dpw, z, g, b, wpw):
    lp = z.shape[0]
    t = _row_tile(lp)

    def body(dpw_ref, z_ref, g_ref, b_ref, w_ref, dz_ref, dw_ref, dg_ref, db_ref, dcb_ref):
        i = pl.program_id(0)

        @pl.when(i == 0)
        def _():
            dw_ref[...] = jnp.zeros_like(dw_ref)
            dg_ref[...] = jnp.zeros_like(dg_ref)
            db_ref[...] = jnp.zeros_like(db_ref)
            dcb_ref[...] = jnp.zeros_like(dcb_ref)

        xhat, rstd = _ln_stats(z_ref[...])
        zn = xhat * g_ref[...] + b_ref[...]
        sg = _sigmoid(zn)
        dpw = dpw_ref[...]
        dw_ref[...] += _dot_t0((zn * sg).astype(_MM), dpw)
        dzn = _dot_t1(dpw, w_ref[...]) * (sg * (1.0 + zn * (1.0 - sg)))
        dg_ref[...] += jnp.sum(dzn * xhat, axis=0, keepdims=True)
        db_ref[...] += jnp.sum(dzn, axis=0, keepdims=True)
        dz = _ln_bwd(dzn, xhat, rstd, g_ref[...])
        dz_ref[...] = dz
        dcb_ref[...] += jnp.sum(dz, axis=0, keepdims=True)

    row = pl.BlockSpec((t, D_CONV), lambda i: (i, 0))
    vec = _full((1, D_CONV))
    vshape = jax.ShapeDtypeStruct((1, D_CONV), _F32)
    return pl.pallas_call(
        body, name="conv_bwd_rows", grid=(lp // t,),
        in_specs=[row, row, vec, vec, _full((D_CONV, D_CONV))],
        out_specs=[row, _full((D_CONV, D_CONV)), vec, vec, vec],
        out_shape=[jax.ShapeDtypeStruct((lp, D_CONV), _F32), jax.ShapeDtypeStruct((D_CONV, D_CONV), _F32),
                   vshape, vshape, vshape],
        compiler_params=_params(("arbitrary",)),
    )(dpw, z, g, b, wpw)


def _conv_bwd_taps(dz, u, ug, cw):
    lp = u.shape[0]
    t = _row_tile(lp)
    nt = lp // t

    def body(dz_ref, u_ref, ug_ref, cw_ref, du_ref, dug_ref, dcw_ref, win, sh, wacc):
        i = pl.program_id(0)

        @pl.when(i == 0)
        def _():
            win[t:t + HALO, :] = jnp.zeros((HALO, D_CONV), _F32)
            wacc[...] = jnp.zeros_like(wacc)

        win[0:t, :] = dz_ref[...]
        _shifted_windows(win, sh, t + HALO)

        def chunk(c, carry):
            r0 = c * CONV_CHUNK
            rows = pl.ds(pl.multiple_of(r0, 8), CONV_CHUNK)
            u = u_ref[rows, :]
            sg = _sigmoid(ug_ref[rows, :])
            hh = u * sg
            dhh = jnp.zeros((CONV_CHUNK, D_CONV), _F32)
            for j in range(CONV_WIDTH):
                d = CONV_WIDTH - 1 - j
                s = _tap_rows(sh, [d], r0)[d]
                dhh = dhh + cw_ref[j:j + 1, :] * s
                prod = hh * s
                wacc[j] += sum(prod[8 * n:8 * (n + 1), :] for n in range(CONV_CHUNK // 8))
            du_ref[rows, :] = (dhh * sg).astype(du_ref.dtype)
            dug_ref[rows, :] = (dhh * u * sg * (1.0 - sg)).astype(dug_ref.dtype)
            return carry

        lax.fori_loop(0, t // CONV_CHUNK, chunk, 0)
        win[t:t + HALO, :] = win[0:HALO, :]

        @pl.when(i == nt - 1)
        def _():
            dcw_ref[...] = jnp.sum(wacc[...], axis=1)

    row = pl.BlockSpec((t, D_CONV), lambda i: (nt - 1 - i, 0))
    return pl.pallas_call(
        body, name="conv_bwd_taps", grid=(nt,),
        in_specs=[row, row, row, _full((32, D_CONV))],
        out_specs=[row, row, _full((32, D_CONV))],
        out_shape=[jax.ShapeDtypeStruct((lp, D_CONV), _MM)] * 2 + [jax.ShapeDtypeStruct((32, D_CONV), _F32)],
        scratch_shapes=[pltpu.VMEM((t + HALO, D_CONV), _F32), pltpu.VMEM((8, t + HALO, D_CONV), _F32),
                        pltpu.VMEM((32, 8, D_CONV), _F32)],
        compiler_params=_params(("arbitrary",)),
    )(dz, u, ug, cw)


def _proj_bwd_x(pieces, df, wpt, dr, x, head, g, parts=()):
    seq = x.shape[0]
    lp = ROW_X + seq
    t = _row_tile(lp)
    nt = lp // t
    nx = t // ROW_X
    n_parts = len(parts)

    def body(*refs):
        dq_ref, dk_ref, dv_ref, dga_ref, du_ref, dug_ref, dgc_ref, df_ref, wt_ref, dr_ref = refs[:10]
        x_refs, (head_ref, g_ref) = refs[10:10 + nx], refs[10 + nx:12 + nx]
        part_refs, refs = refs[12 + nx:12 + nx + n_parts], refs[12 + nx + n_parts:]
        gx_ref, dmeta_ref, dg_ref, db_ref = refs[:4]
        land_refs, (buf, sem), comm_sems = refs[4:4 + n_parts], refs[4 + n_parts:6 + n_parts], refs[6 + n_parts:]
        i = pl.program_id(0)

        @pl.when(i == 0)
        def _():
            dg_ref[...] = jnp.zeros_like(dg_ref)
            db_ref[...] = jnp.zeros_like(db_ref)
            if n_parts:
                for cp in _chip_exchange_copies(part_refs, land_refs, *comm_sems):
                    cp.start()

        dh = ALPHA * dr_ref[...]
        for n, ref in enumerate((dq_ref, dk_ref, dv_ref, dga_ref, du_ref, dug_ref, dgc_ref)):
            dh = dh + _dot(ref[...].astype(_MM), wt_ref[512 * n:512 * (n + 1), :])
        dh = dh + _dot(df_ref[...], wt_ref[3584:3712, :])
        rows = i * t + lax.broadcasted_iota(jnp.int32, (t, 1), 0)
        dh = jnp.where(rows >= ROW_PAD, dh, 0.0)
        xhat, rstd = _ln_stats(_x_tile(i, x_refs, head_ref))
        dg_ref[...] += jnp.sum(dh * xhat, axis=0, keepdims=True)
        db_ref[...] += jnp.sum(dh, axis=0, keepdims=True)
        dx = _ln_bwd(dh, xhat, rstd, g_ref[...])

        def first_copy():
            return pltpu.make_async_copy(buf.at[0, pl.ds(ROW_X, t - ROW_X), :], gx_ref.at[pl.ds(0, t - ROW_X), :], sem)

        def tile_copy(n):
            return pltpu.make_async_copy(buf.at[n % 2], gx_ref.at[pl.ds(pl.multiple_of(n * t - ROW_X, ROW_X), t), :], sem)

        pl.when(i == 1)(lambda: first_copy().wait())
        pl.when(i > 1)(lambda: tile_copy(i - 1).wait())
        buf[i % 2] = dx

        @pl.when(i == 0)
        def _():
            dmeta_ref[...] = dx[ROW_PAD:ROW_X, :]
            first_copy().start()

        pl.when(i > 0)(lambda: tile_copy(i).start())
        if nt == 1:
            first_copy().wait()
        else:
            pl.when(i == nt - 1)(lambda: tile_copy(i).wait())

        @pl.when(i == nt - 1)
        def _():
            if n_parts:
                for cp in _chip_exchange_copies(part_refs, land_refs, *comm_sems):
                    cp.wait()

    row = lambda w: pl.BlockSpec((t, w), lambda i: (i, 0))
    vec = _full((1, D_MODEL))
    any_spec = pl.BlockSpec(memory_space=pl.ANY)
    return pl.pallas_call(
        body, name="proj_bwd_x", grid=(nt,),
        in_specs=[row(512)] * 7 + [row(128), _full((D_IN_PAD, D_MODEL)), row(D_MODEL)] + _x_specs(t)
                 + [_full((ROW_X, D_MODEL)), vec] + [any_spec] * n_parts,
        out_specs=[any_spec, _full((N_META, D_MODEL)), vec, vec] + [any_spec] * n_parts,
        out_shape=[jax.ShapeDtypeStruct((seq, D_MODEL), _F32), jax.ShapeDtypeStruct((N_META, D_MODEL), _F32),
                   jax.ShapeDtypeStruct((1, D_MODEL), _F32), jax.ShapeDtypeStruct((1, D_MODEL), _F32)]
                  + [jax.ShapeDtypeStruct(p.shape, p.dtype) for p in parts],
        scratch_shapes=[pltpu.VMEM((2, t, D_MODEL), _F32), pltpu.SemaphoreType.DMA(())]
                       + [pltpu.SemaphoreType.DMA((3 * n_parts,))] * (2 if n_parts else 0),
        compiler_params=_params(("arbitrary",)),
    )(*pieces, df, wpt, dr, *([x] * nx), head, g, *parts)


def _proj_bwd_w(h, pieces, name):
    lp = h.shape[0]
    t = _row_tile(lp)
    n = len(pieces)
    widths = [p.shape[1] for p in pieces]

    def body(*refs):
        h_ref, d_refs, w_refs = refs[0], refs[1:1 + n], refs[1 + n:]
        i = pl.program_id(0)

        @pl.when(i == 0)
        def _():
            for w_ref in w_refs:
                w_ref[...] = jnp.zeros_like(w_ref)

        hb = h_ref[...].astype(_MM)
        for d_ref, w_ref in zip(d_refs, w_refs):
            w_ref[...] += _dot_t0(d_ref[...].astype(_MM), hb)

    row = lambda w: pl.BlockSpec((t, w), lambda i: (i, 0))
    return pl.pallas_call(
        body, name=name, grid=(lp // t,),
        in_specs=[row(D_MODEL)] + [row(w) for w in widths],
        out_specs=[_full((w, D_MODEL)) for w in widths],
        out_shape=[jax.ShapeDtypeStruct((w, D_MODEL), _F32) for w in widths],
        compiler_params=_params(("arbitrary",)),
    )(h, *pieces)


def _adamw_math(w, g, m, v):
    m = ADAM_B1 * m + (1.0 - ADAM_B1) * g
    v = ADAM_B2 * v + (1.0 - ADAM_B2) * (g * g)
    m_hat = m / (1.0 - ADAM_B1 ** ADAM_STEP)
    v_hat = v / (1.0 - ADAM_B2 ** ADAM_STEP)
    delta = -ADAM_LR * (m_hat / (jnp.sqrt(v_hat) + ADAM_EPS) + ADAM_WD * w)
    return delta, m, v


def _adamw_tiled(w, g, m, v, rows, name):
    r, c = w.shape

    def body(w_ref, g_ref, m_ref, v_ref, go_ref, d_ref, nm_ref, nv_ref):
        g = g_ref[...]
        go_ref[...] = g
        d_ref[...], nm_ref[...], nv_ref[...] = _adamw_math(w_ref[...], g, m_ref[...], v_ref[...])

    spec = pl.BlockSpec((rows, c), lambda i: (i, 0))
    return pl.pallas_call(
        body, name=name, grid=(pl.cdiv(r, rows),), in_specs=[spec] * 4, out_specs=[spec] * 4,
        out_shape=[jax.ShapeDtypeStruct((r, c), _F32)] * 4,
        compiler_params=_params(("arbitrary",)),
    )(w, g, m, v)


def _adamw_small(ws, gs, ms, vs):
    n = len(ws)

    def body(*refs):
        ins, outs = refs[:4 * n], refs[4 * n:]
        for k in range(n):
            d, m, v = _adamw_math(ins[k][...], ins[n + k][...], ins[2 * n + k][...], ins[3 * n + k][...])
            outs[k][...], outs[n + k][...], outs[2 * n + k][...] = d, m, v

    shapes = [jax.ShapeDtypeStruct(w.shape, _F32) for w in ws]
    res = pl.pallas_call(body, name="adamw_small", out_shape=shapes * 3, compiler_params=_params())(*ws, *gs, *ms, *vs)
    return res[:n], res[n:2 * n], res[2 * n:]


def _place():
    x, y, c = lax.axis_index("x"), lax.axis_index("y"), lax.axis_index("c")
    chips = [(1 - x, y), (x, 1 - y), (1 - x, 1 - y)]
    return x, y, c, chips


def _half(ref, slot, c, rows):
    hr = rows // 2
    return ref.at[slot, pl.ds(pl.multiple_of(c * hr, 8), hr), :]


def _all_gather(shards):
    n = len(shards)

    def body(*refs):
        ins, outs = refs[:n], refs[n:2 * n]
        send, recv = refs[2 * n:]
        x, y, c, chips = _place()
        me = 2 * x + y
        sibling = (x, y, 1 - c)

        def copy(k, t, slot, half, to, src=None):
            rows = ins[t].shape[0]
            dst = _half(outs[t], slot, half, rows)
            return pltpu.make_async_remote_copy(
                src_ref=dst if src is None else src, dst_ref=dst,
                send_sem=send.at[k], recv_sem=recv.at[k], device_id=to, device_id_type=MESH)

        first, passed = [], []
        for t in range(n):
            hr = ins[t].shape[0] // 2
            mine = ins[t].at[pl.ds(pl.multiple_of(c * hr, 8), hr), :]
            for j, chip in enumerate(chips):
                first.append(copy(6 * t + j, t, me, c, (*chip, c), src=mine))
        for cp in first:
            cp.start()
        for t in range(n):
            for j, (cx, cy) in enumerate(chips):
                copy(6 * t + j, t, 2 * cx + cy, c, sibling).wait_recv()
                fwd = copy(6 * t + 3 + j, t, 2 * cx + cy, c, sibling)
                fwd.start()
                passed.append(fwd)
        for t in range(n):
            for j, (cx, cy) in enumerate(chips):
                copy(6 * t + 3 + j, t, 2 * cx + cy, 1 - c, sibling).wait_recv()
        for cp in first + passed:
            cp.wait_send()

    any_spec = pl.BlockSpec(memory_space=pl.ANY)
    return pl.pallas_call(
        body, name="all_gather",
        in_specs=[any_spec] * n, out_specs=[any_spec] * n,
        out_shape=[jax.ShapeDtypeStruct((N_SHARD,) + s.shape, s.dtype) for s in shards],
        scratch_shapes=[pltpu.SemaphoreType.DMA((6 * n,)), pltpu.SemaphoreType.DMA((6 * n,))],
    )(*shards)


def _own_slot(gathered, own):
    me = 2 * lax.axis_index("x") + lax.axis_index("y")
    slot = lax.broadcasted_iota(jnp.int32, (N_SHARD,) + (1,) * own.ndim, 0)
    return jnp.where(slot == me, own[None], gathered)


def _sibling_exchange(bufs, tag=""):
    n = len(bufs)

    def body(*refs):
        ins, got = refs[:n], refs[n:2 * n]
        send, recv = refs[2 * n:]
        x, y, c, _ = _place()
        cps = []
        for t in range(n):
            hr = ins[t].shape[0] // (2 * N_SHARD)
            for s in range(N_SHARD):
                k = N_SHARD * t + s
                cps.append(pltpu.make_async_remote_copy(
                    src_ref=ins[t].at[pl.ds(pl.multiple_of((2 * s + 1 - c) * hr, 32 // ins[t].dtype.itemsize), hr), :],
                    dst_ref=got[t].at[s],
                    send_sem=send.at[k], recv_sem=recv.at[k], device_id=(x, y, 1 - c), device_id_type=MESH))
        for cp in cps:
            cp.start()
        for cp in cps:
            cp.wait()

    any_spec = pl.BlockSpec(memory_space=pl.ANY)
    return pl.pallas_call(
        body, name="sibling_exchange" + tag,
        in_specs=[any_spec] * n, out_specs=[any_spec] * n,
        out_shape=[jax.ShapeDtypeStruct((N_SHARD, b.shape[0] // (2 * N_SHARD), b.shape[1]), b.dtype) for b in bufs],
        scratch_shapes=[pltpu.SemaphoreType.DMA((N_SHARD * n,))] * 2,
    )(*bufs)


def _half_sum(bufs, got, wire, c, tag=""):
    n = len(bufs)

    def body(c_ref, *refs):
        ins, gots, outs = refs[:n], refs[n:2 * n], refs[2 * n:]
        for t in range(n):
            outs[t][0] = (ins[t][...].astype(_F32) + gots[t][0].astype(_F32)).astype(outs[t].dtype)

    slot = lambda g: pl.BlockSpec((1,) + tuple(g.shape[1:]), lambda s, c_ref: (s, 0, 0))
    grid_spec = pltpu.PrefetchScalarGridSpec(
        num_scalar_prefetch=1, grid=(N_SHARD,),
        in_specs=[pl.BlockSpec(tuple(g.shape[1:]), lambda s, c_ref: (2 * s + c_ref[0], 0)) for g in got]
                 + [slot(g) for g in got],
        out_specs=[slot(g) for g in got])
    return pl.pallas_call(
        body, name="half_sum" + tag, grid_spec=grid_spec,
        out_shape=[jax.ShapeDtypeStruct(g.shape, d) for g, d in zip(got, wire)],
        compiler_params=_params(("arbitrary",)),
    )(c, *bufs, *got)


def _chip_exchange_copies(ins, outs, send, recv):
    x, y, c, chips = _place()
    me = 2 * x + y
    return [pltpu.make_async_remote_copy(
        src_ref=ins[t].at[2 * cx + cy], dst_ref=outs[t].at[me], send_sem=send.at[3 * t + j], recv_sem=recv.at[3 * t + j],
        device_id=(cx, cy, c), device_id_type=MESH) for t in range(len(ins)) for j, (cx, cy) in enumerate(chips)]


def _chip_exchange(parts):
    n = len(parts)

    def body(*refs):
        cps = _chip_exchange_copies(refs[:n], refs[n:2 * n], *refs[2 * n:])
        for cp in cps:
            cp.start()
        for cp in cps:
            cp.wait()

    any_spec = pl.BlockSpec(memory_space=pl.ANY)
    return pl.pallas_call(
        body, name="chip_exchange",
        in_specs=[any_spec] * n, out_specs=[any_spec] * n,
        out_shape=[jax.ShapeDtypeStruct(p.shape, p.dtype) for p in parts],
        scratch_shapes=[pltpu.SemaphoreType.DMA((3 * n,))] * 2,
    )(*parts)


def _chip_sum(parts, got, me, tag=""):
    n = len(got)

    def body(me_ref, *refs):
        owns, gots, outs = refs[:n], refs[n:2 * n], refs[2 * n:]
        s = pl.program_id(0)
        for t in range(n):
            term = jnp.where(s == me_ref[0], owns[t][0], gots[t][0]).astype(_F32)

            @pl.when(s == 0)
            def _():
                outs[t][...] = term

            @pl.when(s > 0)
            def _():
                outs[t][...] += term

    blk = lambda g: (1,) + tuple(g.shape[1:])
    grid_spec = pltpu.PrefetchScalarGridSpec(
        num_scalar_prefetch=1, grid=(N_SHARD,),
        in_specs=[pl.BlockSpec(blk(g), lambda s, me_ref: (me_ref[0], 0, 0)) for g in got]
                 + [pl.BlockSpec(blk(g), lambda s, me_ref: (jnp.where(s == me_ref[0], (s + 1) % N_SHARD, s), 0, 0))
                    for g in got],
        out_specs=[pl.BlockSpec(tuple(g.shape[1:]), lambda s, me_ref: (0, 0)) for g in got])
    return pl.pallas_call(
        body, name="chip_sum" + tag, grid_spec=grid_spec,
        out_shape=[jax.ShapeDtypeStruct(g.shape[1:], _F32) for g in got],
        compiler_params=_params(("arbitrary",)),
    )(me, *parts, *got)


def _sibling_share(halves, tag=""):
    n = len(halves)

    def body(*refs):
        ins, outs = refs[:n], refs[n:2 * n]
        send, recv = refs[2 * n:]
        x, y, c, _ = _place()
        cps = [pltpu.make_async_remote_copy(
            src_ref=ins[t], dst_ref=outs[t], send_sem=send.at[t], recv_sem=recv.at[t],
            device_id=(x, y, 1 - c), device_id_type=MESH) for t in range(n)]
        for cp in cps:
            cp.start()
        for cp in cps:
            cp.wait()

    any_spec = pl.BlockSpec(memory_space=pl.ANY)
    return pl.pallas_call(
        body, name="sibling_share" + tag,
        in_specs=[any_spec] * n, out_specs=[any_spec] * n,
        out_shape=[jax.ShapeDtypeStruct(h.shape, h.dtype) for h in halves],
        scratch_shapes=[pltpu.SemaphoreType.DMA((n,))] * 2,
    )(*halves)


def _as_operand(index):
    return jnp.asarray(index, jnp.int32).reshape(1)


def _chip_partials(bufs, wire, tag=""):
    return _half_sum(bufs, _sibling_exchange(bufs, tag), wire, _as_operand(lax.axis_index("c")), tag)


def _finish_reduction(parts, landed, tag=""):
    c = lax.axis_index("c")
    half = _chip_sum(parts, landed, _as_operand(2 * lax.axis_index("x") + lax.axis_index("y")), tag)
    other = _sibling_share(half, tag)
    return [jnp.concatenate([jnp.where(c == 0, h, o), jnp.where(c == 0, o, h)], axis=0) for h, o in zip(half, other)]


def _reduce_small(buf):
    rows = buf.shape[0] // N_SHARD
    hr = rows // 2

    def body(in_ref, out_ref, got1, part, got2, send, recv):
        x, y, c, chips = _place()
        me = 2 * x + y
        sibling = (x, y, 1 - c)

        def half(s, which):
            return in_ref.at[pl.ds(pl.multiple_of((2 * s + which) * hr, 8), hr), :]

        first = [pltpu.make_async_remote_copy(src_ref=half(s, 1 - c), dst_ref=got1.at[s], send_sem=send.at[s],
                                              recv_sem=recv.at[s], device_id=sibling, device_id_type=MESH)
                 for s in range(N_SHARD)]
        for cp in first:
            cp.start()
        for s in range(N_SHARD):
            first[s].wait()
            part[s] = half(s, c)[...] + got1[s]
        got2[me] = part[me]
        second = [pltpu.make_async_remote_copy(src_ref=part.at[2 * cx + cy], dst_ref=got2.at[me],
                                               send_sem=send.at[N_SHARD + j], recv_sem=recv.at[N_SHARD + j],
                                               device_id=(cx, cy, c), device_id_type=MESH)
                  for j, (cx, cy) in enumerate(chips)]
        for cp in second:
            cp.start()
        for cp in second:
            cp.wait()
        mine = ((got2[0] + got2[1]) + got2[2]) + got2[3]
        out_ref[pl.ds(pl.multiple_of(c * hr, 8), hr), :] = mine
        last = pltpu.make_async_remote_copy(
            src_ref=out_ref.at[pl.ds(pl.multiple_of(c * hr, 8), hr), :],
            dst_ref=out_ref.at[pl.ds(pl.multiple_of(c * hr, 8), hr), :],
            send_sem=send.at[2 * N_SHARD - 1], recv_sem=recv.at[2 * N_SHARD - 1], device_id=sibling, device_id_type=MESH)
        last.start()
        last.wait()

    vmem = pl.BlockSpec(memory_space=pltpu.VMEM)
    return pl.pallas_call(
        body, name="reduce_small", in_specs=[vmem], out_specs=vmem,
        out_shape=jax.ShapeDtypeStruct((rows, buf.shape[1]), buf.dtype),
        scratch_shapes=[pltpu.VMEM((N_SHARD, hr, buf.shape[1]), buf.dtype)] * 3
                       + [pltpu.SemaphoreType.DMA((2 * N_SHARD,))] * 2,
    )(buf)


def _reduce_scatter(bufs, wire, tag=""):
    parts = _chip_partials(bufs, wire, tag)
    return _finish_reduction(parts, _chip_exchange(parts), tag)


REPL_SIZES = (D_MODEL, D_MODEL, N_HEADS, D_CONV, D_CONV, D_CONV, D_MODEL, D_MODEL)
REPL_ROWS = 48


def _pack_repl(parts):
    flat = jnp.concatenate([p.reshape(-1).astype(_F32) for p in parts])
    return jnp.pad(flat, (0, REPL_ROWS * 128 - flat.shape[0])).reshape(REPL_ROWS, 128)


def _unpack_repl(packed, shapes):
    flat = packed.reshape(-1)
    out, off = [], 0
    for shape, size in zip(shapes, REPL_SIZES):
        out.append(flat[off:off + size].reshape(shape))
        off += size
    return out


W_IN_PIECES = (("q", 0, 512), ("k", 512, 1024), ("v", 1024, 1536), ("f", 1536, 1544), ("ga", 1544, 2056),
               ("u", 2056, 2568), ("ug", 2568, 3080), ("gc", 3080, 3592))


def _projection_operand(slots):
    w_in_t = slots[:, :W_IN_SHARD, :].reshape(D_IN, D_MODEL)
    zrow = jnp.zeros((128 - N_HEADS, D_MODEL), w_in_t.dtype)
    return jnp.concatenate([w_in_t[:OFF_F], w_in_t[OFF_F + N_HEADS:], w_in_t[OFF_F:OFF_F + N_HEADS], zrow], axis=0)


def _gradient_slots(pieces):
    dw = jnp.concatenate([pieces[name][:b - a] for name, a, b in W_IN_PIECES], axis=0)
    dw = jnp.pad(dw.reshape(N_SHARD, W_IN_SHARD, D_MODEL), ((0, 0), (0, W_IN_SLOT - W_IN_SHARD), (0, 0)))
    return dw.reshape(N_SHARD * W_IN_SLOT, D_MODEL)


def _local_grads(x, target, meta, ln_in_g, ln_in_b, w_in_slots, b_f, conv_w, conv_b, ln_conv_g, ln_conv_b,
                 w_pw, w_out, ln_out_g, ln_out_b, chip_partials=None):
    vec = lambda a: a.reshape(1, -1).astype(_F32)
    head = jnp.concatenate([jnp.zeros((ROW_PAD, D_MODEL), _F32), meta], axis=0)
    wpt = _projection_operand(w_in_slots)
    bf = jnp.pad(vec(b_f), ((0, 0), (0, 128 - N_HEADS)))
    cw = jnp.pad(conv_w, ((0, 32 - CONV_WIDTH), (0, 0)))

    h, q, k, v, ga, u, ug, gc, f, vt = _ln_proj(x, head, vec(ln_in_g), vec(ln_in_b), wpt)
    ccol = _decay_cumsum(f, bf)
    o, lse = _attn_fwd(q, k, vt, ccol)
    z, pw = _conv_fwd(u, ug, cw, vec(conv_b), vec(ln_conv_g), vec(ln_conv_b), w_pw)
    dr, do, dga, dpw, dgc, delta, dw_out, dg_out, db_out, loss = _out_loss_bwd(
        o, ga, pw, gc, h, target, w_out, vec(ln_out_g), vec(ln_out_b))
    dq, dk, dv, dck, dcq = _attn_bwd(q, k, v, do, lse, delta, ccol)
    df, dbf = _decay_bwd(dck, dcq, f, bf)
    dz, dw_pw, dg_conv, db_conv, dconv_b = _conv_bwd_rows(dpw, z, vec(ln_conv_g), vec(ln_conv_b), w_pw)
    du, dug, dcw = _conv_bwd_taps(dz, u, ug, cw)
    pieces = (dq, dk, dv, dga, du, dug, dgc)
    dwq, dwk, dwv, dwga = _proj_bwd_w(h, pieces[:4], "proj_bwd_w_attn")
    dwu, dwug, dwgc, dwf = _proj_bwd_w(h, pieces[4:] + (df,), "proj_bwd_w_conv")
    dw_in_t = _gradient_slots(dict(q=dwq, k=dwk, v=dwv, f=dwf, ga=dwga, u=dwu, ug=dwug, gc=dwgc))
    parts = chip_partials([dw_in_t, dw_pw, dw_out]) if chip_partials else []
    grad_x, dmeta, dg_in, db_in, *landed = _proj_bwd_x(pieces, df, wpt, dr, x, head, vec(ln_in_g), parts)
    grads = dict(meta=dmeta, ln_in_g=dg_in, ln_in_b=db_in, w_in_t=dw_in_t, b_f=dbf[:, :N_HEADS],
                 conv_w=dcw[:CONV_WIDTH], conv_b=dconv_b, ln_conv_g=dg_conv, ln_conv_b=db_conv,
                 w_pw=dw_pw, w_out=dw_out, ln_out_g=dg_out, ln_out_b=db_out, reduced=(parts, landed))
    return loss, grad_x, grads


def kernel(x, meta, ln_in_g, ln_in_b, w_in, b_f, conv_w, conv_b, ln_conv_g, ln_conv_b, w_pw, w_out, ln_out_g, ln_out_b, loss_target, m_meta, m_ln_in_g, m_ln_in_b, m_w_in, m_b_f, m_conv_w, m_conv_b, m_ln_conv_g, m_ln_conv_b, m_w_pw, m_w_out, m_ln_out_g, m_ln_out_b, v_meta, v_ln_in_g, v_ln_in_b, v_w_in, v_b_f, v_conv_w, v_conv_b, v_ln_conv_g, v_ln_conv_b, v_w_pw, v_w_out, v_ln_out_g, v_ln_out_b):
    conv_w2, w_pw2, w_out2 = conv_w[0], w_pw[0], w_out[0]
    w_t = jnp.transpose(w_in[0])

    small = jnp.concatenate([jnp.pad(conv_w2, ((0, 1), (0, 0))), meta.reshape(32, 128)], axis=0)
    shards = [jnp.pad(w_t.astype(_MM), ((0, W_IN_SLOT - W_IN_SHARD), (0, 0))), w_pw2.astype(_MM), w_out2.astype(_MM), small]
    g_in, g_pw, g_out, g_small = (_own_slot(g, s) for g, s in zip(_all_gather(shards), shards))
    w_pw_full = g_pw.reshape(D_CONV, D_CONV)
    w_out_full = g_out.reshape(D_MODEL, D_MODEL)
    conv_w_full = jnp.transpose(g_small[:, :CONV_WIDTH, :], (1, 0, 2)).reshape(CONV_WIDTH, D_CONV)
    meta_full = jnp.transpose(g_small[:, 32:64, :].reshape(N_SHARD, N_META, 256), (1, 0, 2)).reshape(N_META, D_MODEL)

    loss, grad_x, gr = _local_grads(x[0], loss_target[0], meta_full, ln_in_g, ln_in_b, g_in, b_f, conv_w_full,
                                    conv_b, ln_conv_g, ln_conv_b, w_pw_full, w_out_full, ln_out_g, ln_out_b,
                                    chip_partials=lambda big: _chip_partials([b.astype(_MM) for b in big], [_MM] * 3))
    grad_x = grad_x[None]
    gw_slot, gw_pw, gw_out = _finish_reduction(*gr["reduced"])

    repl_names = ("ln_in_g", "ln_in_b", "b_f", "conv_b", "ln_conv_g", "ln_conv_b", "ln_out_g", "ln_out_b")
    repl = _pack_repl([gr[n] for n in repl_names] + [loss])
    gcw = jnp.transpose(jnp.pad(gr["conv_w"], ((0, 1), (0, 0))).reshape(32, N_SHARD, 128), (1, 0, 2))
    gmeta = jnp.transpose(gr["meta"].reshape(N_META, N_SHARD, 256), (1, 0, 2)).reshape(N_SHARD, 32, 128)
    small_g = jnp.concatenate([gcw, gmeta, jnp.broadcast_to(repl[None], (N_SHARD, REPL_ROWS, 128))], axis=1)
    small_rows = small_g.shape[1]
    gsmall = _reduce_small(small_g.reshape(N_SHARD * small_rows, 128))
    g_conv_w = gsmall[:CONV_WIDTH]
    g_meta = gsmall[32:64].reshape(N_META, 256)
    g_repl = gsmall[64:64 + REPL_ROWS]

    repl_w = (ln_in_g, ln_in_b, b_f, conv_b, ln_conv_g, ln_conv_b, ln_out_g, ln_out_b)
    repl_m = (m_ln_in_g, m_ln_in_b, m_b_f, m_conv_b, m_ln_conv_g, m_ln_conv_b, m_ln_out_g, m_ln_out_b)
    repl_v = (v_ln_in_g, v_ln_in_b, v_b_f, v_conv_b, v_ln_conv_g, v_ln_conv_b, v_ln_out_g, v_ln_out_b)
    zero = jnp.zeros((1,), _F32)
    res_t = _adamw_tiled(w_t, gw_slot, jnp.transpose(m_w_in[0]), jnp.transpose(v_w_in[0]), 256, "adamw_w_in")
    gw_in, d_in, nm_in, nv_in = (jnp.transpose(a) for a in res_t)
    ws = [w_pw2, w_out2, conv_w2, meta, _pack_repl(repl_w + (zero,))]
    gs = [gw_pw, gw_out, g_conv_w, g_meta, g_repl]
    ms = [m_w_pw[0], m_w_out[0], m_conv_w[0], m_meta, _pack_repl(repl_m + (zero,))]
    vs = [v_w_pw[0], v_w_out[0], v_conv_w[0], v_meta, _pack_repl(repl_v + (zero + 1.0,))]
    ds, nms, nvs = _adamw_small(ws, gs, ms, vs)

    shapes = [w.shape for w in repl_w]
    total_loss = g_repl.reshape(-1)[sum(REPL_SIZES)]
    names = ("meta", "ln_in_g", "ln_in_b", "w_in", "b_f", "conv_w", "conv_b", "ln_conv_g", "ln_conv_b",
             "w_pw", "w_out", "ln_out_g", "ln_out_b")

    def assemble(t_in, smalls):
        pw_, out_, cw_, meta_, packed = smalls
        r = dict(zip(repl_names, _unpack_repl(packed, shapes)))
        r.update(meta=meta_, w_in=t_in[None], conv_w=cw_[None], w_pw=pw_[None], w_out=out_[None])
        return [r[n] for n in names]

    return (total_loss, grad_x, *assemble(gw_in, gs), *assemble(d_in, ds), *assemble(nm_in, nms),
            *assemble(nv_in, nvs))
```

```python
import functools

import jax
import jax.numpy as jnp
from jax import lax
from jax.experimental import pallas as pl
from jax.experimental.pallas import tpu as pltpu

D_MODEL = 1024
N_META = 16
D_ATTN = 512
D_CONV = 512
N_HEADS = 8
HEAD_DIM = 64
N_PAIRS = 4
CONV_WIDTH = 31
LN_EPS = 1e-5
ALPHA = 2.0 ** 0.25
ATTN_SCALE = 0.125
D_IN = 3592
OFF_F = 3 * D_ATTN
D_IN_PAD = 3712
N_SHARD = 4
W_IN_SHARD = D_IN // N_SHARD
W_IN_SLOT = 928

ROW_PAD = 112
ROW_X = ROW_PAD + N_META
HALO = 32
MASKED = 1e30

ADAM_LR = 0.001
ADAM_B1 = 0.9
ADAM_B2 = 0.999
ADAM_EPS = 1e-08
ADAM_WD = 0.01
ADAM_STEP = 10

VMEM_LIMIT = 56 * 1024 * 1024

_MM = jnp.bfloat16
_F32 = jnp.float32
MESH = pl.DeviceIdType.MESH


def _params(sem=None):
    return pltpu.CompilerParams(dimension_semantics=sem, vmem_limit_bytes=VMEM_LIMIT)


def _row_tile(n_rows):
    for t in (384, 256, 128):
        if n_rows % t == 0:
            return t
    raise ValueError(f"padded sequence length {n_rows} is not a multiple of 128")


def _sigmoid(x):
    return 1.0 / (1.0 + jnp.exp(-x))


def _ln_stats(x):
    mu = jnp.mean(x, axis=-1, keepdims=True)
    xc = x - mu
    var = jnp.mean(xc * xc, axis=-1, keepdims=True)
    rstd = lax.rsqrt(var + LN_EPS)
    return xc * rstd, rstd


def _ln_bwd(dy, xhat, rstd, g):
    dxh = dy * g
    m1 = jnp.mean(dxh, axis=-1, keepdims=True)
    m2 = jnp.mean(dxh * xhat, axis=-1, keepdims=True)
    return rstd * (dxh - m1 - xhat * m2)


def _dot(a, b):
    return jnp.dot(a, b, preferred_element_type=_F32)


def _dot_t0(a, b):
    return lax.dot_general(a, b, (((0,), (0,)), ((), ())), preferred_element_type=_F32)


def _dot_t1(a, b):
    return lax.dot_general(a, b, (((1,), (1,)), ((), ())), preferred_element_type=_F32)


def _dot_exact(a, b):
    return jnp.dot(a, b, preferred_element_type=_F32, precision=lax.Precision.HIGHEST)


def _full(shape):
    return pl.BlockSpec(shape, lambda *_: (0,) * len(shape))


def _x_specs(t):
    n = t // ROW_X
    return [pl.BlockSpec((ROW_X, D_MODEL), lambda i, s=s: (jnp.maximum(i * n + s - 1, 0), 0)) for s in range(n)]


def _x_tile(i, refs, head_ref=None):
    first = refs[0][...]
    if head_ref is not None:
        first = jnp.where(i == 0, head_ref[...], first)
    return jnp.concatenate([first] + [r[...] for r in refs[1:]], axis=0)


def _ln_proj(x, head, g, b, wpt, gather=()):
    lp = ROW_X + x.shape[0]
    t = _row_tile(lp)
    nt = lp // t
    nx = t // ROW_X
    ng = len(gather)

    def body(*refs):
        x_refs, (head_ref, g_ref, b_ref, wt_ref) = refs[:nx], refs[nx:nx + 4]
        shard_refs, refs = refs[nx + 4:nx + 4 + ng], refs[nx + 4 + ng:]
        h_ref, q_ref, k_ref, v_ref, ga_ref, u_ref, ug_ref, gc_ref, f_ref, vt_ref = refs[:10]
        gathered_refs, comm_sems = refs[10:10 + ng], refs[10 + ng:]
        i = pl.program_id(0)
        if ng:
            start, relay, finish = _gather_phases(shard_refs, gathered_refs, *comm_sems)
            pl.when(i == 0)(start)
            pl.when(i == nt // 2)(relay)
        xhat, _ = _ln_stats(_x_tile(i, x_refs, head_ref))
        rows = i * t + lax.broadcasted_iota(jnp.int32, (t, 1), 0)
        h = jnp.where(rows >= ROW_PAD, xhat * g_ref[...] + b_ref[...], 0.0)
        h_ref[...] = h
        hb = h.astype(_MM)
        q_ref[...] = (_dot_t1(hb, wt_ref[0:512, :]) * ATTN_SCALE).astype(q_ref.dtype)
        k_ref[...] = _dot_t1(hb, wt_ref[512:1024, :]).astype(k_ref.dtype)
        v = _dot_t1(hb, wt_ref[1024:1536, :])
        v_ref[...] = v.astype(v_ref.dtype)
        vt_ref[...] = v.T.astype(vt_ref.dtype)
        ga_ref[...] = _dot_t1(hb, wt_ref[1536:2048, :])
        u_ref[...] = _dot_t1(hb, wt_ref[2048:2560, :])
        ug_ref[...] = _dot_t1(hb, wt_ref[2560:3072, :])
        gc_ref[...] = _dot_t1(hb, wt_ref[3072:3584, :])
        f_ref[...] = _dot_t1(hb, wt_ref[3584:3712, :])
        if ng:
            pl.when(i == nt - 1)(finish)

    row = lambda w: pl.BlockSpec((t, w), lambda i: (i, 0))
    f32 = lambda w: jax.ShapeDtypeStruct((lp, w), _F32)
    mm = lambda w: jax.ShapeDtypeStruct((lp, w), _MM)
    any_spec = pl.BlockSpec(memory_space=pl.ANY)
    return pl.pallas_call(
        body, name="ln_proj", grid=(nt,),
        in_specs=_x_specs(t) + [_full((ROW_X, D_MODEL)), _full((1, D_MODEL)), _full((1, D_MODEL)),
                                _full((D_IN_PAD, D_MODEL))] + [any_spec] * ng,
        out_specs=[row(D_MODEL)] + [row(512)] * 7 + [row(128), pl.BlockSpec((512, t), lambda i: (0, i))]
                  + [any_spec] * ng,
        out_shape=[f32(D_MODEL), mm(512), mm(512), mm(512), f32(512), f32(512), f32(512), f32(512), f32(128),
                   jax.ShapeDtypeStruct((512, lp), _MM)]
                  + [jax.ShapeDtypeStruct((N_SHARD,) + s.shape, s.dtype) for s in gather],
        scratch_shapes=[pltpu.SemaphoreType.DMA((6 * ng,))] * (2 if ng else 0),
        compiler_params=_params(("arbitrary",)),
    )(*([x] * nx), head, g, b, wpt, *gather)


def _log_sigmoid(z):
    return jnp.minimum(z, 0.0) - jnp.log(1.0 + jnp.exp(-jnp.abs(z)))


def _decay_cumsum(f, bf):
    lp = f.shape[0]
    t = _row_tile(lp)

    def body(f_ref, bf_ref, ccol_ref):
        r = lax.broadcasted_iota(jnp.int32, (t, t), 0)
        c = lax.broadcasted_iota(jnp.int32, (t, t), 1)
        tril = (c <= r).astype(_F32)

        def step(bi, carry):
            rows = pl.ds(pl.multiple_of(bi * t, 128), t)
            lf = _log_sigmoid(f_ref[rows, :] + bf_ref[...])
            cb = _dot_exact(tril, lf) + carry
            for p in range(N_PAIRS):
                ccol_ref[p, rows, :] = cb if p == 0 else pltpu.roll(cb, 128 - 2 * p, axis=1)
            return cb[t - 1:t, :]

        lax.fori_loop(0, lp // t, step, jnp.zeros((1, 128), _F32))

    return pl.pallas_call(
        body, name="decay_cumsum",
        out_shape=jax.ShapeDtypeStruct((N_PAIRS, lp, 128), _F32),
        compiler_params=_params(),
    )(f, bf)


def _split3(x):
    hi = x.astype(_MM)
    r1 = x - hi.astype(_F32)
    mid = r1.astype(_MM)
    return hi, mid, (r1 - mid.astype(_F32)).astype(_MM)


def _augment(x, hd, first, second, transposed=False):
    lane = lax.broadcasted_iota(jnp.int32, (1, 128), 1)
    base = HEAD_DIM * (1 - hd)
    out = jnp.where(lane // HEAD_DIM == hd, x, jnp.zeros_like(x))
    for slot, val in enumerate((first, second)):
        lo = base + 3 * slot
        if val is None:
            continue
        if isinstance(val, float):
            out = jnp.where((lane >= lo) & (lane < lo + 3), jnp.asarray(val, x.dtype), out)
        else:
            for n, piece in enumerate(val):
                out = jnp.where(lane == lo + n, piece[:, hd:hd + 1], out)
    return _transpose_mm(out) if transposed else out


def _transpose_mm(x):
    return x.astype(_F32).T.astype(_MM)


def _diag_valid(i, t):
    sub = lax.broadcasted_iota(jnp.int32, (t, t), 0)
    return (sub <= lax.broadcasted_iota(jnp.int32, (t, t), 1)) & (i * t + sub >= ROW_PAD)


def _key_pieces(c, first_row):
    pos = first_row + lax.broadcasted_iota(jnp.int32, (c.shape[0], 1), 0)
    return _split3(jnp.where(pos >= ROW_PAD, -c, -MASKED))


def _key_operand(k, pieces, hd, transposed=False):
    return _augment(k, hd, pieces, 1.0, transposed)


def _attn_fwd(q, k, vt, ccol):
    lp = q.shape[0]
    t = _row_tile(lp)
    nb = lp // t

    def body(q_ref, k_ref, vt_ref, ccol_ref, o_ref, lse_ref, ka_ref):
        i = pl.program_id(1)

        @pl.when(i == 0)
        def _():
            for n in range(nb):
                rows = slice(n * t, (n + 1) * t)
                pieces = _key_pieces(ccol_ref[0, rows, :], n * t)
                for hd in range(2):
                    ka_ref[hd, rows, :] = _key_operand(k_ref[rows, :], pieces, hd)

        qats = [_augment(q_ref[...], hd, 1.0, None, transposed=True) for hd in range(2)]

        def scores(j):
            keys = pl.ds(pl.multiple_of(j * t, 128), t)
            return tuple(_dot(ka_ref[hd, keys, :], qats[hd]) for hd in range(2))

        def update(j, sts, states, diag):
            vt = vt_ref[:, pl.ds(pl.multiple_of(j * t, 128), t)]
            new = []
            for st, (m, l, acc) in zip(sts, states):
                if diag:
                    st = jnp.where(_diag_valid(i, t), st, -MASKED)
                m_new = jnp.maximum(m, jnp.max(st, axis=0, keepdims=True))
                a = jnp.exp(m - m_new)
                pt = jnp.exp(st - m_new)
                l = a * l + jnp.sum(pt, axis=0, keepdims=True)
                new.append((m_new, l, a * acc + _dot(vt, pt.astype(_MM))))
            return tuple(new)

        def step(j, carry):
            sts, states = carry
            return scores(j + 1), update(j, sts, states, False)

        def step2(n, carry):
            sts, states = carry
            nxt, last = scores(2 * n + 1), scores(2 * n + 2)
            return last, update(2 * n + 1, nxt, update(2 * n, sts, states, False), False)

        init = (jnp.full((1, t), -MASKED, _F32), jnp.zeros((1, t), _F32), jnp.zeros((128, t), _F32))
        carry = lax.fori_loop(0, i // 2, step2, (scores(0), (init, init)))
        sts, states = lax.fori_loop(0, i % 2, lambda _, c: step(i - 1, c), carry)
        (m0, l0, acc0), (m1, l1, acc1) = update(i, sts, states, True)
        row = lax.broadcasted_iota(jnp.int32, (128, t), 0)
        o_ref[...] = jnp.where(row < HEAD_DIM, acc0 / l0, acc1 / l1).T
        lse_ref[0] = jnp.where(row == 0, m0 + jnp.log(l0), jnp.where(row == 1, m1 + jnp.log(l1), 0.0)).T

    return pl.pallas_call(
        body, name="attn_fwd", grid=(N_PAIRS, nb),
        in_specs=[pl.BlockSpec((t, 128), lambda p, i: (i, p)),
                  pl.BlockSpec((lp, 128), lambda p, i: (0, p)),
                  pl.BlockSpec((128, lp), lambda p, i: (p, 0)),
                  pl.BlockSpec((1, lp, 128), lambda p, i: (p, 0, 0))],
        out_specs=[pl.BlockSpec((t, 128), lambda p, i: (i, p)),
                   pl.BlockSpec((1, t, 128), lambda p, i: (p, i, 0))],
        out_shape=[jax.ShapeDtypeStruct((lp, D_ATTN), _F32), jax.ShapeDtypeStruct((N_PAIRS, lp, 128), _F32)],
        scratch_shapes=[pltpu.VMEM((2, lp, 128), _MM)],
        compiler_params=_params(("arbitrary", "arbitrary")),
    )(q, k, vt, ccol)


CONV_CHUNK = 32


def _shifted_windows(win, sh, rows):
    sh[0] = win[...]
    for b in range(1, 8):
        sh[b, 0:rows - 8, :] = win[b:b + rows - 8, :]


def _tap_rows(sh, offsets, r0):
    out = {}
    for b in range(8):
        ds = [d for d in offsets if d % 8 == b]
        if ds:
            lo, hi = min(ds) // 8, max(ds) // 8
            rows = sh[b, pl.ds(pl.multiple_of(r0 + 8 * lo, 8), CONV_CHUNK + 8 * (hi - lo)), :]
            for d in ds:
                out[d] = rows[8 * (d // 8 - lo):8 * (d // 8 - lo) + CONV_CHUNK, :]
    return out


def _conv_fwd(u, ug, cw, cb, g, b, wpw):
    lp = u.shape[0]
    t = _row_tile(lp)

    def body(u_ref, ug_ref, cw_ref, cb_ref, g_ref, b_ref, wpw_ref, z_ref, pw_ref, win, sh):
        i = pl.program_id(0)

        @pl.when(i == 0)
        def _():
            win[0:HALO, :] = jnp.zeros((HALO, D_CONV), _F32)

        win[HALO:HALO + t, :] = u_ref[...] * _sigmoid(ug_ref[...])
        _shifted_windows(win, sh, t + HALO)

        def chunk(c, carry):
            r0 = c * CONV_CHUNK
            acc = jnp.broadcast_to(cb_ref[...], (CONV_CHUNK, D_CONV))
            first = HALO - (CONV_WIDTH - 1)
            taps = _tap_rows(sh, range(first, first + CONV_WIDTH), r0)
            for j in range(CONV_WIDTH):
                acc = acc + cw_ref[j:j + 1, :] * taps[first + j]
            z_ref[pl.ds(pl.multiple_of(r0, 8), CONV_CHUNK), :] = acc
            return carry

        lax.fori_loop(0, t // CONV_CHUNK, chunk, 0)
        xhat, _ = _ln_stats(z_ref[...])
        zn = xhat * g_ref[...] + b_ref[...]
        a = zn * _sigmoid(zn)
        pw_ref[...] = _dot(a.astype(_MM), wpw_ref[...])
        win[0:HALO, :] = win[t:t + HALO, :]

    row = pl.BlockSpec((t, D_CONV), lambda i: (i, 0))
    vec = _full((1, D_CONV))
    return pl.pallas_call(
        body, name="conv_fwd", grid=(lp // t,),
        in_specs=[row, row, _full((32, D_CONV)), vec, vec, vec, _full((D_CONV, D_CONV))],
        out_specs=[row, row],
        out_shape=[jax.ShapeDtypeStruct((lp, D_CONV), _F32)] * 2,
        scratch_shapes=[pltpu.VMEM((t + HALO, D_CONV), _F32), pltpu.VMEM((8, t + HALO, D_CONV), _F32)],
        compiler_params=_params(("arbitrary",)),
    )(u, ug, cw, cb, g, b, wpw)


def _out_loss_bwd(o, ga, pw, gc, h, target, w_out, g, b):
    lp = o.shape[0]
    t = _row_tile(lp)
    nx = t // ROW_X

    def body(*refs):
        o_ref, ga_ref, pw_ref, gc_ref, h_ref = refs[:5]
        tgt_refs, (w_ref, g_ref, b_ref) = refs[5:5 + nx], refs[5 + nx:8 + nx]
        dr_ref, do_ref, dga_ref, dpw_ref, dgc_ref, delta_ref, dw_ref, dg_ref, db_ref, loss_ref = refs[8 + nx:]
        i = pl.program_id(0)

        @pl.when(i == 0)
        def _():
            dw_ref[...] = jnp.zeros_like(dw_ref)
            dg_ref[...] = jnp.zeros_like(dg_ref)
            db_ref[...] = jnp.zeros_like(db_ref)
            loss_ref[...] = jnp.zeros_like(loss_ref)

        o, ga, pw, gc = o_ref[...], ga_ref[...], pw_ref[...], gc_ref[...]
        sa, sc = _sigmoid(ga), _sigmoid(gc)
        silu_a, silu_c = ga * sa, gc * sc
        ycat = jnp.concatenate([o * silu_a, pw * silu_c], axis=-1).astype(_MM)
        r = ALPHA * h_ref[...] + _dot(ycat, w_ref[...])
        xhat, rstd = _ln_stats(r)
        out = xhat * g_ref[...] + b_ref[...]
        rows = i * t + lax.broadcasted_iota(jnp.int32, (t, 1), 0)
        diff = jnp.where(rows >= ROW_X, out - _x_tile(i, tgt_refs), 0.0)
        loss_ref[...] += (0.5 / D_MODEL) * jnp.sum(diff * diff, keepdims=True)
        dout = diff * (1.0 / D_MODEL)
        dg_ref[...] += jnp.sum(dout * xhat, axis=0, keepdims=True)
        db_ref[...] += jnp.sum(dout, axis=0, keepdims=True)
        dr = _ln_bwd(dout, xhat, rstd, g_ref[...])
        dr_ref[...] = dr
        drb = dr.astype(_MM)
        dw_ref[...] += _dot_t0(ycat, drb)
        dy = _dot_t1(drb, w_ref[...])
        dya, dyc = dy[:, :D_ATTN], dy[:, D_ATTN:]
        do = dya * silu_a
        do_ref[...] = do.astype(do_ref.dtype)
        dga_ref[...] = (dya * o * (sa * (1.0 + ga * (1.0 - sa)))).astype(dga_ref.dtype)
        dpw_ref[...] = (dyc * silu_c).astype(dpw_ref.dtype)
        dgc_ref[...] = (dyc * pw * (sc * (1.0 + gc * (1.0 - sc)))).astype(dgc_ref.dtype)
        sel = (lax.broadcasted_iota(jnp.int32, (128, 128), 0) // HEAD_DIM
               == lax.broadcasted_iota(jnp.int32, (128, 128), 1)).astype(_F32)
        prod = do * o
        for p in range(N_PAIRS):
            delta_ref[p] = _dot_exact(prod[:, 128 * p:128 * (p + 1)], sel)

    row = lambda w: pl.BlockSpec((t, w), lambda i: (i, 0))
    return pl.pallas_call(
        body, name="out_loss_bwd", grid=(lp // t,),
        in_specs=[row(512)] * 4 + [row(D_MODEL)] + _x_specs(t) + [_full((D_MODEL, D_MODEL))] + [_full((1, D_MODEL))] * 2,
        out_specs=[row(D_MODEL)] + [row(512)] * 4 + [pl.BlockSpec((N_PAIRS, t, 128), lambda i: (0, i, 0)),
                   _full((D_MODEL, D_MODEL)), _full((1, D_MODEL)), _full((1, D_MODEL)), _full((1, 1))],
        out_shape=[jax.ShapeDtypeStruct((lp, D_MODEL), _F32)] + [jax.ShapeDtypeStruct((lp, 512), _MM)] * 4
                  + [jax.ShapeDtypeStruct((N_PAIRS, lp, 128), _F32), jax.ShapeDtypeStruct((D_MODEL, D_MODEL), _F32),
                     jax.ShapeDtypeStruct((1, D_MODEL), _F32), jax.ShapeDtypeStruct((1, D_MODEL), _F32),
                     jax.ShapeDtypeStruct((1, 1), _F32)],
        compiler_params=_params(("arbitrary",)),
    )(o, ga, pw, gc, h, *([target] * nx), w_out, g, b)


BWD_WIDTH = 2


def _attn_bwd(q, k, v, do, lse, delta, ccol):
    lp = q.shape[0]
    t = _row_tile(lp)
    nb = lp // t

    def body(q_ref, k_ref, v_ref, do_ref, lse_ref, delta_ref, ccol_ref,
             dq_ref, dk_ref, dv_ref, dck_ref, dcq_ref,
             qa_ref, qat_ref, doa_ref, doat_ref, dqt_ref):
        j = pl.program_id(1)

        @pl.when(j == 0)
        def _():
            dqt_ref[...] = jnp.zeros_like(dqt_ref)
            for n in range(nb):
                rows = slice(n * t, (n + 1) * t)
                lse_pieces, delta_pieces = _split3(-lse_ref[0, rows, :]), _split3(-delta_ref[0, rows, :])
                for hd in range(2):
                    qa = _augment(q_ref[rows, :], hd, 1.0, lse_pieces)
                    doa = _augment(do_ref[rows, :], hd, delta_pieces, None)
                    qa_ref[hd, rows, :] = qa
                    qat_ref[hd, :, rows] = _transpose_mm(qa)
                    doa_ref[hd, rows, :] = doa
                    doat_ref[hd, :, rows] = _transpose_mm(doa)

        pieces = _key_pieces(ccol_ref[0], j * t)
        kas = [_key_operand(k_ref[...], pieces, hd) for hd in range(2)]
        kats = [_transpose_mm(ka) for ka in kas]
        vas = [_augment(v_ref[...], hd, 1.0, None) for hd in range(2)]

        def block(i, carry, diag, width=1):
            rows = pl.ds(pl.multiple_of(i * t, 128), width * t)
            new = []
            for hd in range(2):
                dk, dv = carry[hd]
                st = _dot(kas[hd], qat_ref[hd, :, rows])
                if diag:
                    st = jnp.where(_diag_valid(j, t), st, -MASKED)
                pt = jnp.exp(st)
                dsb = (pt * _dot(vas[hd], doat_ref[hd, :, rows])).astype(_MM)
                dv = dv + _dot(pt.astype(_MM), doa_ref[hd, rows, :])
                dk = dk + _dot(dsb, qa_ref[hd, rows, :])
                dqt_ref[hd, :, rows] += _dot(kats[hd], dsb)
                new.append((dk, dv))
            return tuple(new)

        zero = jnp.zeros((t, 128), _F32)
        carry = block(j, ((zero, zero), (zero, zero)), True)
        rest = nb - 1 - j
        if nb > BWD_WIDTH:
            carry = lax.fori_loop(0, rest // BWD_WIDTH,
                                  lambda n, c: block(j + 1 + BWD_WIDTH * n, c, False, BWD_WIDTH), carry)
            rest = rest % BWD_WIDTH
        (dk0, dv0), (dk1, dv1) = lax.fori_loop(nb - rest, nb, functools.partial(block, diag=False), carry)
        lane = lax.broadcasted_iota(jnp.int32, (1, 128), 1)
        dk_ref[...] = jnp.where(lane < HEAD_DIM, dk0, dk1).astype(dk_ref.dtype)
        dv_ref[...] = jnp.where(lane < HEAD_DIM, dv0, dv1).astype(dv_ref.dtype)
        dck_ref[0] = jnp.where(lane == 0, dk0[:, HEAD_DIM:HEAD_DIM + 1], jnp.where(lane == 1, dk1[:, 0:1], 0.0))

        @pl.when(j == nb - 1)
        def _():
            row = lax.broadcasted_iota(jnp.int32, (128, t), 0)
            sub = lax.broadcasted_iota(jnp.int32, (8, t), 0)
            for n in range(nb):
                cols = slice(n * t, (n + 1) * t)
                t0, t1 = dqt_ref[0, :, cols], dqt_ref[1, :, cols]
                dq_ref[cols, :] = jnp.where(row < HEAD_DIM, t0, t1).T * ATTN_SCALE
                dcq_ref[0, :, cols] = jnp.where(sub == 0, t0[HEAD_DIM + 3:HEAD_DIM + 4, :],
                                                jnp.where(sub == 1, t1[3:4, :], 0.0))

    blk = pl.BlockSpec((t, 128), lambda p, j: (j, p))
    res = pl.BlockSpec((lp, 128), lambda p, j: (0, p))
    stat = pl.BlockSpec((1, lp, 128), lambda p, j: (p, 0, 0))
    stat_blk = pl.BlockSpec((1, t, 128), lambda p, j: (p, j, 0))
    return pl.pallas_call(
        body, name="attn_bwd", grid=(N_PAIRS, nb),
        in_specs=[res, blk, blk, res, stat, stat, stat_blk],
        out_specs=[res, blk, blk, stat_blk, pl.BlockSpec((1, 8, lp), lambda p, j: (p, 0, 0))],
        out_shape=[jax.ShapeDtypeStruct((lp, D_ATTN), _F32), jax.ShapeDtypeStruct((lp, D_ATTN), _MM),
                   jax.ShapeDtypeStruct((lp, D_ATTN), _MM), jax.ShapeDtypeStruct((N_PAIRS, lp, 128), _F32),
                   jax.ShapeDtypeStruct((N_PAIRS, 8, lp), _F32)],
        scratch_shapes=[pltpu.VMEM((2, lp, 128), _MM), pltpu.VMEM((2, 128, lp), _MM),
                        pltpu.VMEM((2, lp, 128), _MM), pltpu.VMEM((2, 128, lp), _MM),
                        pltpu.VMEM((2, 128, lp), _F32)],
        compiler_params=_params(("arbitrary", "arbitrary")),
    )(q, k, v, do, lse, delta, ccol)


def _decay_bwd(dck, dcq, f, bf):
    lp = f.shape[0]
    nb = lp // 128

    def body(dck_ref, dcq_ref, f_ref, bf_ref, df_ref, dbf_ref):
        r = lax.broadcasted_iota(jnp.int32, (128, 128), 0)
        c = lax.broadcasted_iota(jnp.int32, (128, 128), 1)
        triu = (c >= r).astype(_F32)

        def step(n, carry):
            tail, dbf = carry
            bi = nb - 1 - n
            rows = pl.ds(pl.multiple_of(bi * 128, 128), 128)
            cols = jnp.zeros((128, 128), _F32)
            for p in range(N_PAIRS):
                dcq = jnp.concatenate([dcq_ref[p, :, rows], jnp.zeros((120, 128), _F32)], axis=0).T
                dcp = dcq - dck_ref[p, rows, :]
                cols = cols + (dcp if p == 0 else pltpu.roll(dcp, 2 * p, axis=1))
            lane = lax.broadcasted_iota(jnp.int32, (128, 128), 1)
            cols = jnp.where(lane < N_HEADS, cols, 0.0)
            rev = _dot_exact(triu, cols) + tail
            pos = bi * 128 + lax.broadcasted_iota(jnp.int32, (128, 1), 0)
            df = jnp.where(pos >= ROW_PAD, rev * (1.0 - _sigmoid(f_ref[rows, :] + bf_ref[...])), 0.0)
            df_ref[rows, :] = df.astype(df_ref.dtype)
            return rev[0:1, :], dbf + jnp.sum(df, axis=0, keepdims=True)

        zero = jnp.zeros((1, 128), _F32)
        _, dbf = lax.fori_loop(0, nb, step, (zero, zero), unroll=True)
        dbf_ref[...] = dbf

    return pl.pallas_call(
        body, name="decay_bwd",
        out_shape=[jax.ShapeDtypeStruct((lp, 128), _MM), jax.ShapeDtypeStruct((1, 128), _F32)],
        compiler_params=_params(),
    )(dck, dcq, f, bf)


def _conv_bwd_rows(dpw, z, g, b, wpw):
    lp = z.shape[0]
    t = _row_tile(lp)

    def body(dpw_ref, z_ref, g_ref, b_ref, w_ref, dz_ref, dw_ref, dg_ref, db_ref, dcb_ref):
        i = pl.program_id(0)

        @pl.when(i == 0)
        def _():
            dw_ref[...] = jnp.zeros_like(dw_ref)
            dg_ref[...] = jnp.zeros_like(dg_ref)
            db_ref[...] = jnp.zeros_like(db_ref)
            dcb_ref[...] = jnp.zeros_like(dcb_ref)

        xhat, rstd = _ln_stats(z_ref[...])
        zn = xhat * g_ref[...] + b_ref[...]
        sg = _sigmoid(zn)
        dpw = dpw_ref[...]
        dw_ref[...] += _dot_t0((zn * sg).astype(_MM), dpw)
        dzn = _dot_t1(dpw, w_ref[...]) * (sg * (1.0 + zn * (1.0 - sg)))
        dg_ref[...] += jnp.sum(dzn * xhat, axis=0, keepdims=True)
        db_ref[...] += jnp.sum(dzn, axis=0, keepdims=True)
        dz = _ln_bwd(dzn, xhat, rstd, g_ref[...])
        dz_ref[...] = dz
        dcb_ref[...] += jnp.sum(dz, axis=0, keepdims=True)

    row = pl.BlockSpec((t, D_CONV), lambda i: (i, 0))
    vec = _full((1, D_CONV))
    vshape = jax.ShapeDtypeStruct((1, D_CONV), _F32)
    return pl.pallas_call(
        body, name="conv_bwd_rows", grid=(lp // t,),
        in_specs=[row, row, vec, vec, _full((D_CONV, D_CONV))],
        out_specs=[row, _full((D_CONV, D_CONV)), vec, vec, vec],
        out_shape=[jax.ShapeDtypeStruct((lp, D_CONV), _F32), jax.ShapeDtypeStruct((D_CONV, D_CONV), _F32),
                   vshape, vshape, vshape],
        compiler_params=_params(("arbitrary",)),
    )(dpw, z, g, b, wpw)


def _conv_bwd_taps(dz, u, ug, cw):
    lp = u.shape[0]
    t = _row_tile(lp)
    nt = lp // t

    def body(dz_ref, u_ref, ug_ref, cw_ref, du_ref, dug_ref, dcw_ref, win, sh, wacc):
        i = pl.program_id(0)

        @pl.when(i == 0)
        def _():
            win[t:t + HALO, :] = jnp.zeros((HALO, D_CONV), _F32)
            wacc[...] = jnp.zeros_like(wacc)

        win[0:t, :] = dz_ref[...]
        _shifted_windows(win, sh, t + HALO)

        def chunk(c, carry):
            r0 = c * CONV_CHUNK
            rows = pl.ds(pl.multiple_of(r0, 8), CONV_CHUNK)
            u = u_ref[rows, :]
            sg = _sigmoid(ug_ref[rows, :])
            hh = u * sg
            dhh = jnp.zeros((CONV_CHUNK, D_CONV), _F32)
            for j in range(CONV_WIDTH):
                d = CONV_WIDTH - 1 - j
                s = _tap_rows(sh, [d], r0)[d]
                dhh = dhh + cw_ref[j:j + 1, :] * s
                prod = hh * s
                wacc[j] += sum(prod[8 * n:8 * (n + 1), :] for n in range(CONV_CHUNK // 8))
            du_ref[rows, :] = (dhh * sg).astype(du_ref.dtype)
            dug_ref[rows, :] = (dhh * u * sg * (1.0 - sg)).astype(dug_ref.dtype)
            return carry

        lax.fori_loop(0, t // CONV_CHUNK, chunk, 0)
        win[t:t + HALO, :] = win[0:HALO, :]

        @pl.when(i == nt - 1)
        def _():
            dcw_ref[...] = jnp.sum(wacc[...], axis=1)

    row = pl.BlockSpec((t, D_CONV), lambda i: (nt - 1 - i, 0))
    return pl.pallas_call(
        body, name="conv_bwd_taps", grid=(nt,),
        in_specs=[row, row, row, _full((32, D_CONV))],
        out_specs=[row, row, _full((32, D_CONV))],
        out_shape=[jax.ShapeDtypeStruct((lp, D_CONV), _MM)] * 2 + [jax.ShapeDtypeStruct((32, D_CONV), _F32)],
        scratch_shapes=[pltpu.VMEM((t + HALO, D_CONV), _F32), pltpu.VMEM((8, t + HALO, D_CONV), _F32),
                        pltpu.VMEM((32, 8, D_CONV), _F32)],
        compiler_params=_params(("arbitrary",)),
    )(dz, u, ug, cw)


def _proj_bwd_x(pieces, df, wpt, dr, x, head, g, parts=()):
    seq = x.shape[0]
    lp = ROW_X + seq
    t = _row_tile(lp)
    nt = lp // t
    nx = t // ROW_X
    n_parts = len(parts)

    def body(*refs):
        dq_ref, dk_ref, dv_ref, dga_ref, du_ref, dug_ref, dgc_ref, df_ref, wt_ref, dr_ref = refs[:10]
        x_refs, (head_ref, g_ref) = refs[10:10 + nx], refs[10 + nx:12 + nx]
        part_refs, refs = refs[12 + nx:12 + nx + n_parts], refs[12 + nx + n_parts:]
        gx_ref, dmeta_ref, dg_ref, db_ref = refs[:4]
        land_refs, (buf, sem), comm_sems = refs[4:4 + n_parts], refs[4 + n_parts:6 + n_parts], refs[6 + n_parts:]
        i = pl.program_id(0)

        @pl.when(i == 0)
        def _():
            dg_ref[...] = jnp.zeros_like(dg_ref)
            db_ref[...] = jnp.zeros_like(db_ref)
            if n_parts:
                for cp in _chip_exchange_copies(part_refs, land_refs, *comm_sems):
                    cp.start()

        dh = ALPHA * dr_ref[...]
        for n, ref in enumerate((dq_ref, dk_ref, dv_ref, dga_ref, du_ref, dug_ref, dgc_ref)):
            dh = dh + _dot(ref[...].astype(_MM), wt_ref[512 * n:512 * (n + 1), :])
        dh = dh + _dot(df_ref[...], wt_ref[3584:3712, :])
        rows = i * t + lax.broadcasted_iota(jnp.int32, (t, 1), 0)
        dh = jnp.where(rows >= ROW_PAD, dh, 0.0)
        xhat, rstd = _ln_stats(_x_tile(i, x_refs, head_ref))
        dg_ref[...] += jnp.sum(dh * xhat, axis=0, keepdims=True)
        db_ref[...] += jnp.sum(dh, axis=0, keepdims=True)
        dx = _ln_bwd(dh, xhat, rstd, g_ref[...])

        def first_copy():
            return pltpu.make_async_copy(buf.at[0, pl.ds(ROW_X, t - ROW_X), :], gx_ref.at[pl.ds(0, t - ROW_X), :], sem)

        def tile_copy(n):
            return pltpu.make_async_copy(buf.at[n % 2], gx_ref.at[pl.ds(pl.multiple_of(n * t - ROW_X, ROW_X), t), :], sem)

        pl.when(i == 1)(lambda: first_copy().wait())
        pl.when(i > 1)(lambda: tile_copy(i - 1).wait())
        buf[i % 2] = dx

        @pl.when(i == 0)
        def _():
            dmeta_ref[...] = dx[ROW_PAD:ROW_X, :]
            first_copy().start()

        pl.when(i > 0)(lambda: tile_copy(i).start())
        if nt == 1:
            first_copy().wait()
        else:
            pl.when(i == nt - 1)(lambda: tile_copy(i).wait())

        @pl.when(i == nt - 1)
        def _():
            if n_parts:
                for cp in _chip_exchange_copies(part_refs, land_refs, *comm_sems):
                    cp.wait()

    row = lambda w: pl.BlockSpec((t, w), lambda i: (i, 0))
    vec = _full((1, D_MODEL))
    any_spec = pl.BlockSpec(memory_space=pl.ANY)
    return pl.pallas_call(
        body, name="proj_bwd_x", grid=(nt,),
        in_specs=[row(512)] * 7 + [row(128), _full((D_IN_PAD, D_MODEL)), row(D_MODEL)] + _x_specs(t)
                 + [_full((ROW_X, D_MODEL)), vec] + [any_spec] * n_parts,
        out_specs=[any_spec, _full((N_META, D_MODEL)), vec, vec] + [any_spec] * n_parts,
        out_shape=[jax.ShapeDtypeStruct((seq, D_MODEL), _F32), jax.ShapeDtypeStruct((N_META, D_MODEL), _F32),
                   jax.ShapeDtypeStruct((1, D_MODEL), _F32), jax.ShapeDtypeStruct((1, D_MODEL), _F32)]
                  + [jax.ShapeDtypeStruct(p.shape, p.dtype) for p in parts],
        scratch_shapes=[pltpu.VMEM((2, t, D_MODEL), _F32), pltpu.SemaphoreType.DMA(())]
                       + [pltpu.SemaphoreType.DMA((3 * n_parts,))] * (2 if n_parts else 0),
        compiler_params=_params(("arbitrary",)),
    )(*pieces, df, wpt, dr, *([x] * nx), head, g, *parts)


def _proj_bwd_w(h, pieces, name):
    lp = h.shape[0]
    t = _row_tile(lp)
    n = len(pieces)
    widths = [p.shape[1] for p in pieces]

    def body(*refs):
        h_ref, d_refs, w_refs = refs[0], refs[1:1 + n], refs[1 + n:]
        i = pl.program_id(0)

        @pl.when(i == 0)
        def _():
            for w_ref in w_refs:
                w_ref[...] = jnp.zeros_like(w_ref)

        hb = h_ref[...].astype(_MM)
        for d_ref, w_ref in zip(d_refs, w_refs):
            w_ref[...] += _dot_t0(d_ref[...].astype(_MM), hb)

    row = lambda w: pl.BlockSpec((t, w), lambda i: (i, 0))
    return pl.pallas_call(
        body, name=name, grid=(lp // t,),
        in_specs=[row(D_MODEL)] + [row(w) for w in widths],
        out_specs=[_full((w, D_MODEL)) for w in widths],
        out_shape=[jax.ShapeDtypeStruct((w, D_MODEL), _F32) for w in widths],
        compiler_params=_params(("arbitrary",)),
    )(h, *pieces)


def _adamw_math(w, g, m, v):
    m = ADAM_B1 * m + (1.0 - ADAM_B1) * g
    v = ADAM_B2 * v + (1.0 - ADAM_B2) * (g * g)
    m_hat = m / (1.0 - ADAM_B1 ** ADAM_STEP)
    v_hat = v / (1.0 - ADAM_B2 ** ADAM_STEP)
    delta = -ADAM_LR * (m_hat / (jnp.sqrt(v_hat) + ADAM_EPS) + ADAM_WD * w)
    return delta, m, v


def _adamw_tiled(w, g, m, v, rows, name):
    r, c = w.shape

    def body(w_ref, g_ref, m_ref, v_ref, go_ref, d_ref, nm_ref, nv_ref):
        g = g_ref[...]
        go_ref[...] = g
        d_ref[...], nm_ref[...], nv_ref[...] = _adamw_math(w_ref[...], g, m_ref[...], v_ref[...])

    spec = pl.BlockSpec((rows, c), lambda i: (i, 0))
    return pl.pallas_call(
        body, name=name, grid=(pl.cdiv(r, rows),), in_specs=[spec] * 4, out_specs=[spec] * 4,
        out_shape=[jax.ShapeDtypeStruct((r, c), _F32)] * 4,
        compiler_params=_params(("arbitrary",)),
    )(w, g, m, v)


def _adamw_small(ws, gs, ms, vs):
    n = len(ws)

    def body(*refs):
        ins, outs = refs[:4 * n], refs[4 * n:]
        for k in range(n):
            d, m, v = _adamw_math(ins[k][...], ins[n + k][...], ins[2 * n + k][...], ins[3 * n + k][...])
            outs[k][...], outs[n + k][...], outs[2 * n + k][...] = d, m, v

    shapes = [jax.ShapeDtypeStruct(w.shape, _F32) for w in ws]
    res = pl.pallas_call(body, name="adamw_small", out_shape=shapes * 3, compiler_params=_params())(*ws, *gs, *ms, *vs)
    return res[:n], res[n:2 * n], res[2 * n:]


def _place():
    x, y, c = lax.axis_index("x"), lax.axis_index("y"), lax.axis_index("c")
    chips = [(1 - x, y), (x, 1 - y), (1 - x, 1 - y)]
    return x, y, c, chips


def _half(ref, slot, c, rows):
    hr = rows // 2
    return ref.at[slot, pl.ds(pl.multiple_of(c * hr, 8), hr), :]


def _gather_phases(ins, outs, send, recv):
    n = len(ins)
    x, y, c, chips = _place()
    me = 2 * x + y
    sibling = (x, y, 1 - c)

    def copy(k, t, slot, half, to, src=None):
        rows = ins[t].shape[0]
        dst = _half(outs[t], slot, half, rows)
        return pltpu.make_async_remote_copy(
            src_ref=dst if src is None else src, dst_ref=dst,
            send_sem=send.at[k], recv_sem=recv.at[k], device_id=to, device_id_type=MESH)

    def first():
        cps = []
        for t in range(n):
            hr = ins[t].shape[0] // 2
            mine = ins[t].at[pl.ds(pl.multiple_of(c * hr, 8), hr), :]
            cps += [copy(6 * t + j, t, me, c, (*chip, c), src=mine) for j, chip in enumerate(chips)]
        return cps

    def passed():
        return [copy(6 * t + 3 + j, t, 2 * cx + cy, c, sibling) for t in range(n) for j, (cx, cy) in enumerate(chips)]

    def start():
        for cp in first():
            cp.start()

    def relay():
        fwd = passed()
        for t in range(n):
            for j, (cx, cy) in enumerate(chips):
                copy(6 * t + j, t, 2 * cx + cy, c, sibling).wait_recv()
                fwd[3 * t + j].start()

    def finish():
        for t in range(n):
            for j, (cx, cy) in enumerate(chips):
                copy(6 * t + 3 + j, t, 2 * cx + cy, 1 - c, sibling).wait_recv()
        for cp in first() + passed():
            cp.wait_send()

    return start, relay, finish


def _all_gather(shards):
    n = len(shards)

    def body(*refs):
        start, relay, finish = _gather_phases(refs[:n], refs[n:2 * n], *refs[2 * n:])
        start()
        relay()
        finish()

    any_spec = pl.BlockSpec(memory_space=pl.ANY)
    return pl.pallas_call(
        body, name="all_gather",
        in_specs=[any_spec] * n, out_specs=[any_spec] * n,
        out_shape=[jax.ShapeDtypeStruct((N_SHARD,) + s.shape, s.dtype) for s in shards],
        scratch_shapes=[pltpu.SemaphoreType.DMA((6 * n,)), pltpu.SemaphoreType.DMA((6 * n,))],
    )(*shards)


def _own_slot(gathered, own):
    me = 2 * lax.axis_index("x") + lax.axis_index("y")
    slot = lax.broadcasted_iota(jnp.int32, (N_SHARD,) + (1,) * own.ndim, 0)
    return jnp.where(slot == me, own[None], gathered)


def _sibling_exchange(bufs, tag=""):
    n = len(bufs)

    def body(*refs):
        ins, got = refs[:n], refs[n:2 * n]
        send, recv = refs[2 * n:]
        x, y, c, _ = _place()
        cps = []
        for t in range(n):
            hr = ins[t].shape[0] // (2 * N_SHARD)
            for s in range(N_SHARD):
                k = N_SHARD * t + s
                cps.append(pltpu.make_async_remote_copy(
                    src_ref=ins[t].at[pl.ds(pl.multiple_of((2 * s + 1 - c) * hr, 32 // ins[t].dtype.itemsize), hr), :],
                    dst_ref=got[t].at[s],
                    send_sem=send.at[k], recv_sem=recv.at[k], device_id=(x, y, 1 - c), device_id_type=MESH))
        for cp in cps:
            cp.start()
        for cp in cps:
            cp.wait()

    any_spec = pl.BlockSpec(memory_space=pl.ANY)
    return pl.pallas_call(
        body, name="sibling_exchange" + tag,
        in_specs=[any_spec] * n, out_specs=[any_spec] * n,
        out_shape=[jax.ShapeDtypeStruct((N_SHARD, b.shape[0] // (2 * N_SHARD), b.shape[1]), b.dtype) for b in bufs],
        scratch_shapes=[pltpu.SemaphoreType.DMA((N_SHARD * n,))] * 2,
    )(*bufs)


def _half_sum(bufs, got, wire, c, tag=""):
    n = len(bufs)

    def body(c_ref, *refs):
        ins, gots, outs = refs[:n], refs[n:2 * n], refs[2 * n:]
        for t in range(n):
            outs[t][0] = (ins[t][...].astype(_F32) + gots[t][0].astype(_F32)).astype(outs[t].dtype)

    slot = lambda g: pl.BlockSpec((1,) + tuple(g.shape[1:]), lambda s, c_ref: (s, 0, 0))
    grid_spec = pltpu.PrefetchScalarGridSpec(
        num_scalar_prefetch=1, grid=(N_SHARD,),
        in_specs=[pl.BlockSpec(tuple(g.shape[1:]), lambda s, c_ref: (2 * s + c_ref[0], 0)) for g in got]
                 + [slot(g) for g in got],
        out_specs=[slot(g) for g in got])
    return pl.pallas_call(
        body, name="half_sum" + tag, grid_spec=grid_spec,
        out_shape=[jax.ShapeDtypeStruct(g.shape, d) for g, d in zip(got, wire)],
        compiler_params=_params(("arbitrary",)),
    )(c, *bufs, *got)


def _chip_exchange_copies(ins, outs, send, recv):
    x, y, c, chips = _place()
    me = 2 * x + y
    return [pltpu.make_async_remote_copy(
        src_ref=ins[t].at[2 * cx + cy], dst_ref=outs[t].at[me], send_sem=send.at[3 * t + j], recv_sem=recv.at[3 * t + j],
        device_id=(cx, cy, c), device_id_type=MESH) for t in range(len(ins)) for j, (cx, cy) in enumerate(chips)]


def _chip_exchange(parts):
    n = len(parts)

    def body(*refs):
        cps = _chip_exchange_copies(refs[:n], refs[n:2 * n], *refs[2 * n:])
        for cp in cps:
            cp.start()
        for cp in cps:
            cp.wait()

    any_spec = pl.BlockSpec(memory_space=pl.ANY)
    return pl.pallas_call(
        body, name="chip_exchange",
        in_specs=[any_spec] * n, out_specs=[any_spec] * n,
        out_shape=[jax.ShapeDtypeStruct(p.shape, p.dtype) for p in parts],
        scratch_shapes=[pltpu.SemaphoreType.DMA((3 * n,))] * 2,
    )(*parts)


def _chip_sum(parts, got, me, tag=""):
    n = len(got)

    def body(me_ref, *refs):
        owns, gots, outs = refs[:n], refs[n:2 * n], refs[2 * n:]
        s = pl.program_id(0)
        for t in range(n):
            term = jnp.where(s == me_ref[0], owns[t][0], gots[t][0]).astype(_F32)

            @pl.when(s == 0)
            def _():
                outs[t][...] = term

            @pl.when(s > 0)
            def _():
                outs[t][...] += term

    blk = lambda g: (1,) + tuple(g.shape[1:])
    grid_spec = pltpu.PrefetchScalarGridSpec(
        num_scalar_prefetch=1, grid=(N_SHARD,),
        in_specs=[pl.BlockSpec(blk(g), lambda s, me_ref: (me_ref[0], 0, 0)) for g in got]
                 + [pl.BlockSpec(blk(g), lambda s, me_ref: (jnp.where(s == me_ref[0], (s + 1) % N_SHARD, s), 0, 0))
                    for g in got],
        out_specs=[pl.BlockSpec(tuple(g.shape[1:]), lambda s, me_ref: (0, 0)) for g in got])
    return pl.pallas_call(
        body, name="chip_sum" + tag, grid_spec=grid_spec,
        out_shape=[jax.ShapeDtypeStruct(g.shape[1:], _F32) for g in got],
        compiler_params=_params(("arbitrary",)),
    )(me, *parts, *got)


def _sibling_share(halves, tag=""):
    n = len(halves)

    def body(*refs):
        ins, outs = refs[:n], refs[n:2 * n]
        send, recv = refs[2 * n:]
        x, y, c, _ = _place()
        cps = [pltpu.make_async_remote_copy(
            src_ref=ins[t], dst_ref=outs[t], send_sem=send.at[t], recv_sem=recv.at[t],
            device_id=(x, y, 1 - c), device_id_type=MESH) for t in range(n)]
        for cp in cps:
            cp.start()
        for cp in cps:
            cp.wait()

    any_spec = pl.BlockSpec(memory_space=pl.ANY)
    return pl.pallas_call(
        body, name="sibling_share" + tag,
        in_specs=[any_spec] * n, out_specs=[any_spec] * n,
        out_shape=[jax.ShapeDtypeStruct(h.shape, h.dtype) for h in halves],
        scratch_shapes=[pltpu.SemaphoreType.DMA((n,))] * 2,
    )(*halves)


def _as_operand(index):
    return jnp.asarray(index, jnp.int32).reshape(1)


def _chip_partials(bufs, wire, tag=""):
    return _half_sum(bufs, _sibling_exchange(bufs, tag), wire, _as_operand(lax.axis_index("c")), tag)


def _finish_reduction(parts, landed, tag=""):
    c = lax.axis_index("c")
    half = _chip_sum(parts, landed, _as_operand(2 * lax.axis_index("x") + lax.axis_index("y")), tag)
    other = _sibling_share(half, tag)
    return [jnp.concatenate([jnp.where(c == 0, h, o), jnp.where(c == 0, o, h)], axis=0) for h, o in zip(half, other)]


def _reduce_small(buf):
    rows = buf.shape[0] // N_SHARD
    hr = rows // 2

    def body(in_ref, out_ref, got1, part, got2, send, recv):
        x, y, c, chips = _place()
        me = 2 * x + y
        sibling = (x, y, 1 - c)

        def half(s, which):
            return in_ref.at[pl.ds(pl.multiple_of((2 * s + which) * hr, 8), hr), :]

        first = [pltpu.make_async_remote_copy(src_ref=half(s, 1 - c), dst_ref=got1.at[s], send_sem=send.at[s],
                                              recv_sem=recv.at[s], device_id=sibling, device_id_type=MESH)
                 for s in range(N_SHARD)]
        for cp in first:
            cp.start()
        for s in range(N_SHARD):
            first[s].wait()
            part[s] = half(s, c)[...] + got1[s]
        got2[me] = part[me]
        second = [pltpu.make_async_remote_copy(src_ref=part.at[2 * cx + cy], dst_ref=got2.at[me],
                                               send_sem=send.at[N_SHARD + j], recv_sem=recv.at[N_SHARD + j],
                                               device_id=(cx, cy, c), device_id_type=MESH)
                  for j, (cx, cy) in enumerate(chips)]
        for cp in second:
            cp.start()
        for cp in second:
            cp.wait()
        mine = ((got2[0] + got2[1]) + got2[2]) + got2[3]
        out_ref[pl.ds(pl.multiple_of(c * hr, 8), hr), :] = mine
        last = pltpu.make_async_remote_copy(
            src_ref=out_ref.at[pl.ds(pl.multiple_of(c * hr, 8), hr), :],
            dst_ref=out_ref.at[pl.ds(pl.multiple_of(c * hr, 8), hr), :],
            send_sem=send.at[2 * N_SHARD - 1], recv_sem=recv.at[2 * N_SHARD - 1], device_id=sibling, device_id_type=MESH)
        last.start()
        last.wait()

    vmem = pl.BlockSpec(memory_space=pltpu.VMEM)
    return pl.pallas_call(
        body, name="reduce_small", in_specs=[vmem], out_specs=vmem,
        out_shape=jax.ShapeDtypeStruct((rows, buf.shape[1]), buf.dtype),
        scratch_shapes=[pltpu.VMEM((N_SHARD, hr, buf.shape[1]), buf.dtype)] * 3
                       + [pltpu.SemaphoreType.DMA((2 * N_SHARD,))] * 2,
    )(buf)


def _reduce_scatter(bufs, wire, tag=""):
    parts = _chip_partials(bufs, wire, tag)
    return _finish_reduction(parts, _chip_exchange(parts), tag)


REPL_SIZES = (D_MODEL, D_MODEL, N_HEADS, D_CONV, D_CONV, D_CONV, D_MODEL, D_MODEL)
REPL_ROWS = 48


def _pack_repl(parts):
    flat = jnp.concatenate([p.reshape(-1).astype(_F32) for p in parts])
    return jnp.pad(flat, (0, REPL_ROWS * 128 - flat.shape[0])).reshape(REPL_ROWS, 128)


def _unpack_repl(packed, shapes):
    flat = packed.reshape(-1)
    out, off = [], 0
    for shape, size in zip(shapes, REPL_SIZES):
        out.append(flat[off:off + size].reshape(shape))
        off += size
    return out


W_IN_PIECES = (("q", 0, 512), ("k", 512, 1024), ("v", 1024, 1536), ("f", 1536, 1544), ("ga", 1544, 2056),
               ("u", 2056, 2568), ("ug", 2568, 3080), ("gc", 3080, 3592))


def _projection_operand(slots):
    w_in_t = slots[:, :W_IN_SHARD, :].reshape(D_IN, D_MODEL)
    zrow = jnp.zeros((128 - N_HEADS, D_MODEL), w_in_t.dtype)
    return jnp.concatenate([w_in_t[:OFF_F], w_in_t[OFF_F + N_HEADS:], w_in_t[OFF_F:OFF_F + N_HEADS], zrow], axis=0)


def _gradient_slots(pieces):
    dw = jnp.concatenate([pieces[name][:b - a] for name, a, b in W_IN_PIECES], axis=0)
    dw = jnp.pad(dw.reshape(N_SHARD, W_IN_SHARD, D_MODEL), ((0, 0), (0, W_IN_SLOT - W_IN_SHARD), (0, 0)))
    return dw.reshape(N_SHARD * W_IN_SLOT, D_MODEL)


def _local_grads(x, target, meta, ln_in_g, ln_in_b, w_in_slots, b_f, conv_w, conv_b, ln_conv_g, ln_conv_b,
                 w_pw, w_out, ln_out_g, ln_out_b, chip_partials=None, late_shards=None):
    vec = lambda a: a.reshape(1, -1).astype(_F32)
    head = jnp.concatenate([jnp.zeros((ROW_PAD, D_MODEL), _F32), meta], axis=0)
    wpt = _projection_operand(w_in_slots)
    bf = jnp.pad(vec(b_f), ((0, 0), (0, 128 - N_HEADS)))
    cw = jnp.pad(conv_w, ((0, 32 - CONV_WIDTH), (0, 0)))

    h, q, k, v, ga, u, ug, gc, f, vt, *late = _ln_proj(x, head, vec(ln_in_g), vec(ln_in_b), wpt, late_shards or ())
    if late_shards:
        w_pw, w_out = (_own_slot(gathered, own).reshape(N_SHARD * own.shape[0], own.shape[1])
                       for gathered, own in zip(late, late_shards))
    ccol = _decay_cumsum(f, bf)
    o, lse = _attn_fwd(q, k, vt, ccol)
    z, pw = _conv_fwd(u, ug, cw, vec(conv_b), vec(ln_conv_g), vec(ln_conv_b), w_pw)
    dr, do, dga, dpw, dgc, delta, dw_out, dg_out, db_out, loss = _out_loss_bwd(
        o, ga, pw, gc, h, target, w_out, vec(ln_out_g), vec(ln_out_b))
    dq, dk, dv, dck, dcq = _attn_bwd(q, k, v, do, lse, delta, ccol)
    df, dbf = _decay_bwd(dck, dcq, f, bf)
    dz, dw_pw, dg_conv, db_conv, dconv_b = _conv_bwd_rows(dpw, z, vec(ln_conv_g), vec(ln_conv_b), w_pw)
    du, dug, dcw = _conv_bwd_taps(dz, u, ug, cw)
    pieces = (dq, dk, dv, dga, du, dug, dgc)
    dwq, dwk, dwv, dwga = _proj_bwd_w(h, pieces[:4], "proj_bwd_w_attn")
    dwu, dwug, dwgc, dwf = _proj_bwd_w(h, pieces[4:] + (df,), "proj_bwd_w_conv")
    dw_in_t = _gradient_slots(dict(q=dwq, k=dwk, v=dwv, f=dwf, ga=dwga, u=dwu, ug=dwug, gc=dwgc))
    parts = chip_partials([dw_in_t, dw_pw, dw_out]) if chip_partials else []
    grad_x, dmeta, dg_in, db_in, *landed = _proj_bwd_x(pieces, df, wpt, dr, x, head, vec(ln_in_g), parts)
    grads = dict(meta=dmeta, ln_in_g=dg_in, ln_in_b=db_in, w_in_t=dw_in_t, b_f=dbf[:, :N_HEADS],
                 conv_w=dcw[:CONV_WIDTH], conv_b=dconv_b, ln_conv_g=dg_conv, ln_conv_b=db_conv,
                 w_pw=dw_pw, w_out=dw_out, ln_out_g=dg_out, ln_out_b=db_out, reduced=(parts, landed))
    return loss, grad_x, grads


def kernel(x, meta, ln_in_g, ln_in_b, w_in, b_f, conv_w, conv_b, ln_conv_g, ln_conv_b, w_pw, w_out, ln_out_g, ln_out_b, loss_target, m_meta, m_ln_in_g, m_ln_in_b, m_w_in, m_b_f, m_conv_w, m_conv_b, m_ln_conv_g, m_ln_conv_b, m_w_pw, m_w_out, m_ln_out_g, m_ln_out_b, v_meta, v_ln_in_g, v_ln_in_b, v_w_in, v_b_f, v_conv_w, v_conv_b, v_ln_conv_g, v_ln_conv_b, v_w_pw, v_w_out, v_ln_out_g, v_ln_out_b):
    conv_w2, w_pw2, w_out2 = conv_w[0], w_pw[0], w_out[0]
    w_t = jnp.transpose(w_in[0])

    small = jnp.concatenate([jnp.pad(conv_w2, ((0, 1), (0, 0))), meta.reshape(32, 128)], axis=0)
    shards = [jnp.pad(w_t.astype(_MM), ((0, W_IN_SLOT - W_IN_SHARD), (0, 0))), small]
    g_in, g_small = (_own_slot(g, s) for g, s in zip(_all_gather(shards), shards))
    conv_w_full = jnp.transpose(g_small[:, :CONV_WIDTH, :], (1, 0, 2)).reshape(CONV_WIDTH, D_CONV)
    meta_full = jnp.transpose(g_small[:, 32:64, :].reshape(N_SHARD, N_META, 256), (1, 0, 2)).reshape(N_META, D_MODEL)

    loss, grad_x, gr = _local_grads(x[0], loss_target[0], meta_full, ln_in_g, ln_in_b, g_in, b_f, conv_w_full,
                                    conv_b, ln_conv_g, ln_conv_b, None, None, ln_out_g, ln_out_b,
                                    chip_partials=lambda big: _chip_partials([b.astype(_MM) for b in big], [_MM] * 3),
                                    late_shards=[w_pw2.astype(_MM), w_out2.astype(_MM)])
    grad_x = grad_x[None]
    gw_slot, gw_pw, gw_out = _finish_reduction(*gr["reduced"])

    repl_names = ("ln_in_g", "ln_in_b", "b_f", "conv_b", "ln_conv_g", "ln_conv_b", "ln_out_g", "ln_out_b")
    repl = _pack_repl([gr[n] for n in repl_names] + [loss])
    gcw = jnp.transpose(jnp.pad(gr["conv_w"], ((0, 1), (0, 0))).reshape(32, N_SHARD, 128), (1, 0, 2))
    gmeta = jnp.transpose(gr["meta"].reshape(N_META, N_SHARD, 256), (1, 0, 2)).reshape(N_SHARD, 32, 128)
    small_g = jnp.concatenate([gcw, gmeta, jnp.broadcast_to(repl[None], (N_SHARD, REPL_ROWS, 128))], axis=1)
    small_rows = small_g.shape[1]
    gsmall = _reduce_small(small_g.reshape(N_SHARD * small_rows, 128))
    g_conv_w = gsmall[:CONV_WIDTH]
    g_meta = gsmall[32:64].reshape(N_META, 256)
    g_repl = gsmall[64:64 + REPL_ROWS]

    repl_w = (ln_in_g, ln_in_b, b_f, conv_b, ln_conv_g, ln_conv_b, ln_out_g, ln_out_b)
    repl_m = (m_ln_in_g, m_ln_in_b, m_b_f, m_conv_b, m_ln_conv_g, m_ln_conv_b, m_ln_out_g, m_ln_out_b)
    repl_v = (v_ln_in_g, v_ln_in_b, v_b_f, v_conv_b, v_ln_conv_g, v_ln_conv_b, v_ln_out_g, v_ln_out_b)
    zero = jnp.zeros((1,), _F32)
    res_t = _adamw_tiled(w_t, gw_slot, jnp.transpose(m_w_in[0]), jnp.transpose(v_w_in[0]), 256, "adamw_w_in")
    gw_in, d_in, nm_in, nv_in = (jnp.transpose(a) for a in res_t)
    ws = [w_pw2, w_out2, conv_w2, meta, _pack_repl(repl_w + (zero,))]
    gs = [gw_pw, gw_out, g_conv_w, g_meta, g_repl]
    ms = [m_w_pw[0], m_w_out[0], m_conv_w[0], m_meta, _pack_repl(repl_m + (zero,))]
    vs = [v_w_pw[0], v_w_out[0], v_conv_w[0], v_meta, _pack_repl(repl_v + (zero + 1.0,))]
    ds, nms, nvs = _adamw_small(ws, gs, ms, vs)

    shapes = [w.shape for w in repl_w]
    total_loss = g_repl.reshape(-1)[sum(REPL_SIZES)]
    names = ("meta", "ln_in_g", "ln_in_b", "w_in", "b_f", "conv_w", "conv_b", "ln_conv_g", "ln_conv_b",
             "w_pw", "w_out", "ln_out_g", "ln_out_b")

    def assemble(t_in, smalls):
        pw_, out_, cw_, meta_, packed = smalls
        r = dict(zip(repl_names, _unpack_repl(packed, shapes)))
        r.update(meta=meta_, w_in=t_in[None], conv_w=cw_[None], w_pw=pw_[None], w_out=out_[None])
        return [r[n] for n in names]

    return (total_loss, grad_x, *assemble(gw_in, gs), *assemble(d_in, ds), *assemble(nm_in, nms),
            *assemble(nv_in, nvs))
```

```python
import functools

import jax
import jax.numpy as jnp
from jax import lax
from jax.experimental import pallas as pl
from jax.experimental.pallas import tpu as pltpu

D_MODEL = 1024
N_META = 16
D_ATTN = 512
D_CONV = 512
N_HEADS = 8
HEAD_DIM = 64
N_PAIRS = 4
CONV_WIDTH = 31
LN_EPS = 1e-5
ALPHA = 2.0 ** 0.25
ATTN_SCALE = 0.125
D_IN = 3592
OFF_F = 3 * D_ATTN
D_IN_PAD = 3712
N_SHARD = 4
W_IN_SHARD = D_IN // N_SHARD
W_IN_SLOT = 928

ROW_PAD = 112
ROW_X = ROW_PAD + N_META
HALO = 32
MASKED = 1e30

ADAM_LR = 0.001
ADAM_B1 = 0.9
ADAM_B2 = 0.999
ADAM_EPS = 1e-08
ADAM_WD = 0.01
ADAM_STEP = 10

VMEM_LIMIT = 56 * 1024 * 1024

_MM = jnp.bfloat16
_F32 = jnp.float32
MESH = pl.DeviceIdType.MESH


def _params(sem=None):
    return pltpu.CompilerParams(dimension_semantics=sem, vmem_limit_bytes=VMEM_LIMIT)


def _row_tile(n_rows):
    for t in (384, 256, 128):
        if n_rows % t == 0:
            return t
    raise ValueError(f"padded sequence length {n_rows} is not a multiple of 128")


def _sigmoid(x):
    return 1.0 / (1.0 + jnp.exp(-x))


def _ln_stats(x):
    mu = jnp.mean(x, axis=-1, keepdims=True)
    xc = x - mu
    var = jnp.mean(xc * xc, axis=-1, keepdims=True)
    rstd = lax.rsqrt(var + LN_EPS)
    return xc * rstd, rstd


def _ln_bwd(dy, xhat, rstd, g):
    dxh = dy * g
    m1 = jnp.mean(dxh, axis=-1, keepdims=True)
    m2 = jnp.mean(dxh * xhat, axis=-1, keepdims=True)
    return rstd * (dxh - m1 - xhat * m2)


def _dot(a, b):
    return jnp.dot(a, b, preferred_element_type=_F32)


def _dot_t0(a, b):
    return lax.dot_general(a, b, (((0,), (0,)), ((), ())), preferred_element_type=_F32)


def _dot_t1(a, b):
    return lax.dot_general(a, b, (((1,), (1,)), ((), ())), preferred_element_type=_F32)


def _dot_exact(a, b):
    return jnp.dot(a, b, preferred_element_type=_F32, precision=lax.Precision.HIGHEST)


def _full(shape):
    return pl.BlockSpec(shape, lambda *_: (0,) * len(shape))


def _x_specs(t):
    n = t // ROW_X
    return [pl.BlockSpec((ROW_X, D_MODEL), lambda i, s=s: (jnp.maximum(i * n + s - 1, 0), 0)) for s in range(n)]


def _x_tile(i, refs, head_ref=None):
    first = refs[0][...]
    if head_ref is not None:
        first = jnp.where(i == 0, head_ref[...], first)
    return jnp.concatenate([first] + [r[...] for r in refs[1:]], axis=0)


def _ln_proj(x, head, g, b, wpt, gather=()):
    lp = ROW_X + x.shape[0]
    t = _row_tile(lp)
    nt = lp // t
    nx = t // ROW_X
    ng = len(gather)

    def body(*refs):
        x_refs, (head_ref, g_ref, b_ref, wt_ref) = refs[:nx], refs[nx:nx + 4]
        shard_refs, refs = refs[nx + 4:nx + 4 + ng], refs[nx + 4 + ng:]
        h_ref, q_ref, k_ref, v_ref, ga_ref, u_ref, ug_ref, gc_ref, f_ref, vt_ref = refs[:10]
        gathered_refs, comm_sems = refs[10:10 + ng], refs[10 + ng:]
        i = pl.program_id(0)
        if ng:
            start, relay, finish = _gather_phases(shard_refs, gathered_refs, *comm_sems)
            pl.when(i == 0)(start)
            pl.when(i == nt // 2)(relay)
        xhat, _ = _ln_stats(_x_tile(i, x_refs, head_ref))
        rows = i * t + lax.broadcasted_iota(jnp.int32, (t, 1), 0)
        h = jnp.where(rows >= ROW_PAD, xhat * g_ref[...] + b_ref[...], 0.0)
        h_ref[...] = h
        hb = h.astype(_MM)
        q_ref[...] = (_dot_t1(hb, wt_ref[0:512, :]) * ATTN_SCALE).astype(q_ref.dtype)
        k_ref[...] = _dot_t1(hb, wt_ref[512:1024, :]).astype(k_ref.dtype)
        v = _dot_t1(hb, wt_ref[1024:1536, :])
        v_ref[...] = v.astype(v_ref.dtype)
        vt_ref[...] = v.T.astype(vt_ref.dtype)
        ga_ref[...] = _dot_t1(hb, wt_ref[1536:2048, :])
        u_ref[...] = _dot_t1(hb, wt_ref[2048:2560, :])
        ug_ref[...] = _dot_t1(hb, wt_ref[2560:3072, :])
        gc_ref[...] = _dot_t1(hb, wt_ref[3072:3584, :])
        f_ref[...] = _dot_t1(hb, wt_ref[3584:3712, :])
        if ng:
            pl.when(i == nt - 1)(finish)

    row = lambda w: pl.BlockSpec((t, w), lambda i: (i, 0))
    f32 = lambda w: jax.ShapeDtypeStruct((lp, w), _F32)
    mm = lambda w: jax.ShapeDtypeStruct((lp, w), _MM)
    any_spec = pl.BlockSpec(memory_space=pl.ANY)
    return pl.pallas_call(
        body, name="ln_proj", grid=(nt,),
        in_specs=_x_specs(t) + [_full((ROW_X, D_MODEL)), _full((1, D_MODEL)), _full((1, D_MODEL)),
                                _full((D_IN_PAD, D_MODEL))] + [any_spec] * ng,
        out_specs=[row(D_MODEL)] + [row(512)] * 7 + [row(128), pl.BlockSpec((512, t), lambda i: (0, i))]
                  + [any_spec] * ng,
        out_shape=[f32(D_MODEL), mm(512), mm(512), mm(512), f32(512), f32(512), f32(512), f32(512), f32(128),
                   jax.ShapeDtypeStruct((512, lp), _MM)]
                  + [jax.ShapeDtypeStruct((N_SHARD,) + s.shape, s.dtype) for s in gather],
        scratch_shapes=[pltpu.SemaphoreType.DMA((6 * ng,))] * (2 if ng else 0),
        compiler_params=_params(("arbitrary",)),
    )(*([x] * nx), head, g, b, wpt, *gather)


def _log_sigmoid(z):
    return jnp.minimum(z, 0.0) - jnp.log(1.0 + jnp.exp(-jnp.abs(z)))


def _decay_cumsum(f, bf):
    lp = f.shape[0]
    t = _row_tile(lp)

    def body(f_ref, bf_ref, ccol_ref):
        r = lax.broadcasted_iota(jnp.int32, (t, t), 0)
        c = lax.broadcasted_iota(jnp.int32, (t, t), 1)
        tril = (c <= r).astype(_F32)

        def step(bi, carry):
            rows = pl.ds(pl.multiple_of(bi * t, 128), t)
            lf = _log_sigmoid(f_ref[rows, :] + bf_ref[...])
            cb = _dot_exact(tril, lf) + carry
            for p in range(N_PAIRS):
                ccol_ref[p, rows, :] = cb if p == 0 else pltpu.roll(cb, 128 - 2 * p, axis=1)
            return cb[t - 1:t, :]

        lax.fori_loop(0, lp // t, step, jnp.zeros((1, 128), _F32))

    return pl.pallas_call(
        body, name="decay_cumsum",
        out_shape=jax.ShapeDtypeStruct((N_PAIRS, lp, 128), _F32),
        compiler_params=_params(),
    )(f, bf)


def _split3(x):
    hi = x.astype(_MM)
    r1 = x - hi.astype(_F32)
    mid = r1.astype(_MM)
    return hi, mid, (r1 - mid.astype(_F32)).astype(_MM)


def _augment(x, hd, first, second, transposed=False):
    lane = lax.broadcasted_iota(jnp.int32, (1, 128), 1)
    base = HEAD_DIM * (1 - hd)
    out = jnp.where(lane // HEAD_DIM == hd, x, jnp.zeros_like(x))
    for slot, val in enumerate((first, second)):
        lo = base + 3 * slot
        if val is None:
            continue
        if isinstance(val, float):
            out = jnp.where((lane >= lo) & (lane < lo + 3), jnp.asarray(val, x.dtype), out)
        else:
            for n, piece in enumerate(val):
                out = jnp.where(lane == lo + n, piece[:, hd:hd + 1], out)
    return _transpose_mm(out) if transposed else out


def _transpose_mm(x):
    return x.astype(_F32).T.astype(_MM)


def _diag_valid(i, t):
    sub = lax.broadcasted_iota(jnp.int32, (t, t), 0)
    return (sub <= lax.broadcasted_iota(jnp.int32, (t, t), 1)) & (i * t + sub >= ROW_PAD)


def _key_pieces(c, first_row):
    pos = first_row + lax.broadcasted_iota(jnp.int32, (c.shape[0], 1), 0)
    return _split3(jnp.where(pos >= ROW_PAD, -c, -MASKED))


def _key_operand(k, pieces, hd, transposed=False):
    return _augment(k, hd, pieces, 1.0, transposed)


def _attn_fwd(q, k, vt, ccol):
    lp = q.shape[0]
    t = _row_tile(lp)
    nb = lp // t

    def body(q_ref, k_ref, vt_ref, ccol_ref, o_ref, lse_ref, ka_ref):
        i = pl.program_id(1)

        @pl.when(i == 0)
        def _():
            for n in range(nb):
                rows = slice(n * t, (n + 1) * t)
                pieces = _key_pieces(ccol_ref[0, rows, :], n * t)
                for hd in range(2):
                    ka_ref[hd, rows, :] = _key_operand(k_ref[rows, :], pieces, hd)

        qats = [_augment(q_ref[...], hd, 1.0, None, transposed=True) for hd in range(2)]

        def scores(j):
            keys = pl.ds(pl.multiple_of(j * t, 128), t)
            return tuple(_dot(ka_ref[hd, keys, :], qats[hd]) for hd in range(2))

        def update(j, sts, states, diag):
            vt = vt_ref[:, pl.ds(pl.multiple_of(j * t, 128), t)]
            new = []
            for st, (m, l, acc) in zip(sts, states):
                if diag:
                    st = jnp.where(_diag_valid(i, t), st, -MASKED)
                m_new = jnp.maximum(m, jnp.max(st, axis=0, keepdims=True))
                a = jnp.exp(m - m_new)
                pt = jnp.exp(st - m_new)
                l = a * l + jnp.sum(pt, axis=0, keepdims=True)
                new.append((m_new, l, a * acc + _dot(vt, pt.astype(_MM))))
            return tuple(new)

        def step(j, carry):
            sts, states = carry
            return scores(j + 1), update(j, sts, states, False)

        def step2(n, carry):
            sts, states = carry
            nxt, last = scores(2 * n + 1), scores(2 * n + 2)
            return last, update(2 * n + 1, nxt, update(2 * n, sts, states, False), False)

        init = (jnp.full((1, t), -MASKED, _F32), jnp.zeros((1, t), _F32), jnp.zeros((128, t), _F32))
        carry = lax.fori_loop(0, i // 2, step2, (scores(0), (init, init)))
        sts, states = lax.fori_loop(0, i % 2, lambda _, c: step(i - 1, c), carry)
        (m0, l0, acc0), (m1, l1, acc1) = update(i, sts, states, True)
        row = lax.broadcasted_iota(jnp.int32, (128, t), 0)
        o_ref[...] = jnp.where(row < HEAD_DIM, acc0 / l0, acc1 / l1).T
        lse_ref[0] = jnp.where(row == 0, m0 + jnp.log(l0), jnp.where(row == 1, m1 + jnp.log(l1), 0.0)).T

    return pl.pallas_call(
        body, name="attn_fwd", grid=(N_PAIRS, nb),
        in_specs=[pl.BlockSpec((t, 128), lambda p, i: (i, p)),
                  pl.BlockSpec((lp, 128), lambda p, i: (0, p)),
                  pl.BlockSpec((128, lp), lambda p, i: (p, 0)),
                  pl.BlockSpec((1, lp, 128), lambda p, i: (p, 0, 0))],
        out_specs=[pl.BlockSpec((t, 128), lambda p, i: (i, p)),
                   pl.BlockSpec((1, t, 128), lambda p, i: (p, i, 0))],
        out_shape=[jax.ShapeDtypeStruct((lp, D_ATTN), _F32), jax.ShapeDtypeStruct((N_PAIRS, lp, 128), _F32)],
        scratch_shapes=[pltpu.VMEM((2, lp, 128), _MM)],
        compiler_params=_params(("arbitrary", "arbitrary")),
    )(q, k, vt, ccol)


CONV_CHUNK = 32


def _shifted_windows(win, sh, rows):
    sh[0] = win[...]
    for b in range(1, 8):
        sh[b, 0:rows - 8, :] = win[b:b + rows - 8, :]


def _tap_rows(sh, offsets, r0):
    out = {}
    for b in range(8):
        ds = [d for d in offsets if d % 8 == b]
        if ds:
            lo, hi = min(ds) // 8, max(ds) // 8
            rows = sh[b, pl.ds(pl.multiple_of(r0 + 8 * lo, 8), CONV_CHUNK + 8 * (hi - lo)), :]
            for d in ds:
                out[d] = rows[8 * (d // 8 - lo):8 * (d // 8 - lo) + CONV_CHUNK, :]
    return out


def _conv_fwd(u, ug, cw, cb, g, b, wpw):
    lp = u.shape[0]
    t = _row_tile(lp)

    def body(u_ref, ug_ref, cw_ref, cb_ref, g_ref, b_ref, wpw_ref, z_ref, pw_ref, win, sh):
        i = pl.program_id(0)

        @pl.when(i == 0)
        def _():
            win[0:HALO, :] = jnp.zeros((HALO, D_CONV), _F32)

        win[HALO:HALO + t, :] = u_ref[...] * _sigmoid(ug_ref[...])
        _shifted_windows(win, sh, t + HALO)

        def chunk(c, carry):
            r0 = c * CONV_CHUNK
            acc = jnp.broadcast_to(cb_ref[...], (CONV_CHUNK, D_CONV))
            first = HALO - (CONV_WIDTH - 1)
            taps = _tap_rows(sh, range(first, first + CONV_WIDTH), r0)
            for j in range(CONV_WIDTH):
                acc = acc + cw_ref[j:j + 1, :] * taps[first + j]
            z_ref[pl.ds(pl.multiple_of(r0, 8), CONV_CHUNK), :] = acc
            return carry

        lax.fori_loop(0, t // CONV_CHUNK, chunk, 0)
        xhat, _ = _ln_stats(z_ref[...])
        zn = xhat * g_ref[...] + b_ref[...]
        a = zn * _sigmoid(zn)
        pw_ref[...] = _dot(a.astype(_MM), wpw_ref[...])
        win[0:HALO, :] = win[t:t + HALO, :]

    row = pl.BlockSpec((t, D_CONV), lambda i: (i, 0))
    vec = _full((1, D_CONV))
    return pl.pallas_call(
        body, name="conv_fwd", grid=(lp // t,),
        in_specs=[row, row, _full((32, D_CONV)), vec, vec, vec, _full((D_CONV, D_CONV))],
        out_specs=[row, row],
        out_shape=[jax.ShapeDtypeStruct((lp, D_CONV), _F32)] * 2,
        scratch_shapes=[pltpu.VMEM((t + HALO, D_CONV), _F32), pltpu.VMEM((8, t + HALO, D_CONV), _F32)],
        compiler_params=_params(("arbitrary",)),
    )(u, ug, cw, cb, g, b, wpw)


def _out_loss_bwd(o, ga, pw, gc, h, target, w_out, g, b):
    lp = o.shape[0]
    t = _row_tile(lp)
    nx = t // ROW_X

    def body(*refs):
        o_ref, ga_ref, pw_ref, gc_ref, h_ref = refs[:5]
        tgt_refs, (w_ref, g_ref, b_ref) = refs[5:5 + nx], refs[5 + nx:8 + nx]
        dr_ref, do_ref, dga_ref, dpw_ref, dgc_ref, delta_ref, dw_ref, dg_ref, db_ref, loss_ref = refs[8 + nx:]
        i = pl.program_id(0)

        @pl.when(i == 0)
        def _():
            dw_ref[...] = jnp.zeros_like(dw_ref)
            dg_ref[...] = jnp.zeros_like(dg_ref)
            db_ref[...] = jnp.zeros_like(db_ref)
            loss_ref[...] = jnp.zeros_like(loss_ref)

        o, ga, pw, gc = o_ref[...], ga_ref[...], pw_ref[...], gc_ref[...]
        sa, sc = _sigmoid(ga), _sigmoid(gc)
        silu_a, silu_c = ga * sa, gc * sc
        ycat = jnp.concatenate([o * silu_a, pw * silu_c], axis=-1).astype(_MM)
        r = ALPHA * h_ref[...] + _dot(ycat, w_ref[...])
        xhat, rstd = _ln_stats(r)
        out = xhat * g_ref[...] + b_ref[...]
        rows = i * t + lax.broadcasted_iota(jnp.int32, (t, 1), 0)
        diff = jnp.where(rows >= ROW_X, out - _x_tile(i, tgt_refs), 0.0)
        loss_ref[...] += (0.5 / D_MODEL) * jnp.sum(diff * diff, keepdims=True)
        dout = diff * (1.0 / D_MODEL)
        dg_ref[...] += jnp.sum(dout * xhat, axis=0, keepdims=True)
        db_ref[...] += jnp.sum(dout, axis=0, keepdims=True)
        dr = _ln_bwd(dout, xhat, rstd, g_ref[...])
        dr_ref[...] = dr
        drb = dr.astype(_MM)
        dw_ref[...] += _dot_t0(ycat, drb)
        dy = _dot_t1(drb, w_ref[...])
        dya, dyc = dy[:, :D_ATTN], dy[:, D_ATTN:]
        do = dya * silu_a
        do_ref[...] = do.astype(do_ref.dtype)
        dga_ref[...] = (dya * o * (sa * (1.0 + ga * (1.0 - sa)))).astype(dga_ref.dtype)
        dpw_ref[...] = (dyc * silu_c).astype(dpw_ref.dtype)
        dgc_ref[...] = (dyc * pw * (sc * (1.0 + gc * (1.0 - sc)))).astype(dgc_ref.dtype)
        sel = (lax.broadcasted_iota(jnp.int32, (128, 128), 0) // HEAD_DIM
               == lax.broadcasted_iota(jnp.int32, (128, 128), 1)).astype(_F32)
        prod = do * o
        for p in range(N_PAIRS):
            delta_ref[p] = _dot_exact(prod[:, 128 * p:128 * (p + 1)], sel)

    row = lambda w: pl.BlockSpec((t, w), lambda i: (i, 0))
    return pl.pallas_call(
        body, name="out_loss_bwd", grid=(lp // t,),
        in_specs=[row(512)] * 4 + [row(D_MODEL)] + _x_specs(t) + [_full((D_MODEL, D_MODEL))] + [_full((1, D_MODEL))] * 2,
        out_specs=[row(D_MODEL)] + [row(512)] * 4 + [pl.BlockSpec((N_PAIRS, t, 128), lambda i: (0, i, 0)),
                   _full((D_MODEL, D_MODEL)), _full((1, D_MODEL)), _full((1, D_MODEL)), _full((1, 1))],
        out_shape=[jax.ShapeDtypeStruct((lp, D_MODEL), _F32)] + [jax.ShapeDtypeStruct((lp, 512), _MM)] * 4
                  + [jax.ShapeDtypeStruct((N_PAIRS, lp, 128), _F32), jax.ShapeDtypeStruct((D_MODEL, D_MODEL), _F32),
                     jax.ShapeDtypeStruct((1, D_MODEL), _F32), jax.ShapeDtypeStruct((1, D_MODEL), _F32),
                     jax.ShapeDtypeStruct((1, 1), _F32)],
        compiler_params=_params(("arbitrary",)),
    )(o, ga, pw, gc, h, *([target] * nx), w_out, g, b)


BWD_WIDTH = 2


def _attn_bwd(q, k, v, do, lse, delta, ccol):
    lp = q.shape[0]
    t = _row_tile(lp)
    nb = lp // t

    def body(q_ref, k_ref, v_ref, do_ref, lse_ref, delta_ref, ccol_ref,
             dq_ref, dk_ref, dv_ref, dck_ref, dcq_ref,
             qa_ref, qat_ref, doa_ref, doat_ref, dqt_ref):
        j = pl.program_id(1)

        @pl.when(j == 0)
        def _():
            dqt_ref[...] = jnp.zeros_like(dqt_ref)
            for n in range(nb):
                rows = slice(n * t, (n + 1) * t)
                lse_pieces, delta_pieces = _split3(-lse_ref[0, rows, :]), _split3(-delta_ref[0, rows, :])
                for hd in range(2):
                    qa = _augment(q_ref[rows, :], hd, 1.0, lse_pieces)
                    doa = _augment(do_ref[rows, :], hd, delta_pieces, None)
                    qa_ref[hd, rows, :] = qa
                    qat_ref[hd, :, rows] = _transpose_mm(qa)
                    doa_ref[hd, rows, :] = doa
                    doat_ref[hd, :, rows] = _transpose_mm(doa)

        pieces = _key_pieces(ccol_ref[0], j * t)
        kas = [_key_operand(k_ref[...], pieces, hd) for hd in range(2)]
        kats = [_transpose_mm(ka) for ka in kas]
        vas = [_augment(v_ref[...], hd, 1.0, None) for hd in range(2)]

        def block(i, carry, diag, width=1):
            rows = pl.ds(pl.multiple_of(i * t, 128), width * t)
            new = []
            for hd in range(2):
                dk, dv = carry[hd]
                st = _dot(kas[hd], qat_ref[hd, :, rows])
                if diag:
                    st = jnp.where(_diag_valid(j, t), st, -MASKED)
                pt = jnp.exp(st)
                dsb = (pt * _dot(vas[hd], doat_ref[hd, :, rows])).astype(_MM)
                dv = dv + _dot(pt.astype(_MM), doa_ref[hd, rows, :])
                dk = dk + _dot(dsb, qa_ref[hd, rows, :])
                dqt_ref[hd, :, rows] += _dot(kats[hd], dsb)
                new.append((dk, dv))
            return tuple(new)

        zero = jnp.zeros((t, 128), _F32)
        carry = block(j, ((zero, zero), (zero, zero)), True)
        rest = nb - 1 - j
        if nb > BWD_WIDTH:
            carry = lax.fori_loop(0, rest // BWD_WIDTH,
                                  lambda n, c: block(j + 1 + BWD_WIDTH * n, c, False, BWD_WIDTH), carry)
            rest = rest % BWD_WIDTH
        (dk0, dv0), (dk1, dv1) = lax.fori_loop(nb - rest, nb, functools.partial(block, diag=False), carry)
        lane = lax.broadcasted_iota(jnp.int32, (1, 128), 1)
        dk_ref[...] = jnp.where(lane < HEAD_DIM, dk0, dk1).astype(dk_ref.dtype)
        dv_ref[...] = jnp.where(lane < HEAD_DIM, dv0, dv1).astype(dv_ref.dtype)
        dck_ref[0] = jnp.where(lane == 0, dk0[:, HEAD_DIM:HEAD_DIM + 1], jnp.where(lane == 1, dk1[:, 0:1], 0.0))

        @pl.when(j == nb - 1)
        def _():
            row = lax.broadcasted_iota(jnp.int32, (128, t), 0)
            sub = lax.broadcasted_iota(jnp.int32, (8, t), 0)
            for n in range(nb):
                cols = slice(n * t, (n + 1) * t)
                t0, t1 = dqt_ref[0, :, cols], dqt_ref[1, :, cols]
                dq_ref[cols, :] = jnp.where(row < HEAD_DIM, t0, t1).T * ATTN_SCALE
                dcq_ref[0, :, cols] = jnp.where(sub == 0, t0[HEAD_DIM + 3:HEAD_DIM + 4, :],
                                                jnp.where(sub == 1, t1[3:4, :], 0.0))

    blk = pl.BlockSpec((t, 128), lambda p, j: (j, p))
    res = pl.BlockSpec((lp, 128), lambda p, j: (0, p))
    stat = pl.BlockSpec((1, lp, 128), lambda p, j: (p, 0, 0))
    stat_blk = pl.BlockSpec((1, t, 128), lambda p, j: (p, j, 0))
    return pl.pallas_call(
        body, name="attn_bwd", grid=(N_PAIRS, nb),
        in_specs=[res, blk, blk, res, stat, stat, stat_blk],
        out_specs=[res, blk, blk, stat_blk, pl.BlockSpec((1, 8, lp), lambda p, j: (p, 0, 0))],
        out_shape=[jax.ShapeDtypeStruct((lp, D_ATTN), _F32), jax.ShapeDtypeStruct((lp, D_ATTN), _MM),
                   jax.ShapeDtypeStruct((lp, D_ATTN), _MM), jax.ShapeDtypeStruct((N_PAIRS, lp, 128), _F32),
                   jax.ShapeDtypeStruct((N_PAIRS, 8, lp), _F32)],
        scratch_shapes=[pltpu.VMEM((2, lp, 128), _MM), pltpu.VMEM((2, 128, lp), _MM),
                        pltpu.VMEM((2, lp, 128), _MM), pltpu.VMEM((2, 128, lp), _MM),
                        pltpu.VMEM((2, 128, lp), _F32)],
        compiler_params=_params(("arbitrary", "arbitrary")),
    )(q, k, v, do, lse, delta, ccol)


def _decay_bwd(dck, dcq, f, bf):
    lp = f.shape[0]
    nb = lp // 128

    def body(dck_ref, dcq_ref, f_ref, bf_ref, df_ref, dbf_ref):
        r = lax.broadcasted_iota(jnp.int32, (128, 128), 0)
        c = lax.broadcasted_iota(jnp.int32, (128, 128), 1)
        triu = (c >= r).astype(_F32)

        def step(n, carry):
            tail, dbf = carry
            bi = nb - 1 - n
            rows = pl.ds(pl.multiple_of(bi * 128, 128), 128)
            cols = jnp.zeros((128, 128), _F32)
            for p in range(N_PAIRS):
                dcq = jnp.concatenate([dcq_ref[p, :, rows], jnp.zeros((120, 128), _F32)], axis=0).T
                dcp = dcq - dck_ref[p, rows, :]
                cols = cols + (dcp if p == 0 else pltpu.roll(dcp, 2 * p, axis=1))
            lane = lax.broadcasted_iota(jnp.int32, (128, 128), 1)
            cols = jnp.where(lane < N_HEADS, cols, 0.0)
            rev = _dot_exact(triu, cols) + tail
            pos = bi * 128 + lax.broadcasted_iota(jnp.int32, (128, 1), 0)
            df = jnp.where(pos >= ROW_PAD, rev * (1.0 - _sigmoid(f_ref[rows, :] + bf_ref[...])), 0.0)
            df_ref[rows, :] = df.astype(df_ref.dtype)
            return rev[0:1, :], dbf + jnp.sum(df, axis=0, keepdims=True)

        zero = jnp.zeros((1, 128), _F32)
        _, dbf = lax.fori_loop(0, nb, step, (zero, zero), unroll=True)
        dbf_ref[...] = dbf

    return pl.pallas_call(
        body, name="decay_bwd",
        out_shape=[jax.ShapeDtypeStruct((lp, 128), _MM), jax.ShapeDtypeStruct((1, 128), _F32)],
        compiler_params=_params(),
    )(dck, dcq, f, bf)


def _conv_bwd_rows(dpw, z, g, b, wpw):
    lp = z.shape[0]
    t = _row_tile(lp)

    def body(dpw_ref, z_ref, g_ref, b_ref, w_ref, dz_ref, dw_ref, dg_ref, db_ref, dcb_ref):
        i = pl.program_id(0)

        @pl.when(i == 0)
        def _():
            dw_ref[...] = jnp.zeros_like(dw_ref)
            dg_ref[...] = jnp.zeros_like(dg_ref)
            db_ref[...] = jnp.zeros_like(db_ref)
            dcb_ref[...] = jnp.zeros_like(dcb_ref)

        xhat, rstd = _ln_stats(z_ref[...])
        zn = xhat * g_ref[...] + b_ref[...]
        sg = _sigmoid(zn)
        dpw = dpw_ref[...]
        dw_ref[...] += _dot_t0((zn * sg).astype(_MM), dpw)
        dzn = _dot_t1(dpw, w_ref[...]) * (sg * (1.0 + zn * (1.0 - sg)))
        dg_ref[...] += jnp.sum(dzn * xhat, axis=0, keepdims=True)
        db_ref[...] += jnp.sum(dzn, axis=0, keepdims=True)
        dz = _ln_bwd(dzn, xhat, rstd, g_ref[...])
        dz_ref[...] = dz
        dcb_ref[...] += jnp.sum(dz, axis=0, keepdims=True)

    row = pl.BlockSpec((t, D_CONV), lambda i: (i, 0))
    vec = _full((1, D_CONV))
    vshape = jax.ShapeDtypeStruct((1, D_CONV), _F32)
    return pl.pallas_call(
        body, name="conv_bwd_rows", grid=(lp // t,),
        in_specs=[row, row, vec, vec, _full((D_CONV, D_CONV))],
        out_specs=[row, _full((D_CONV, D_CONV)), vec, vec, vec],
        out_shape=[jax.ShapeDtypeStruct((lp, D_CONV), _F32), jax.ShapeDtypeStruct((D_CONV, D_CONV), _F32),
                   vshape, vshape, vshape],
        compiler_params=_params(("arbitrary",)),
    )(dpw, z, g, b, wpw)


def _conv_bwd_taps(dz, u, ug, cw):
    lp = u.shape[0]
    t = _row_tile(lp)
    nt = lp // t

    def body(dz_ref, u_ref, ug_ref, cw_ref, du_ref, dug_ref, dcw_ref, win, sh, wacc):
        i = pl.program_id(0)

        @pl.when(i == 0)
        def _():
            win[t:t + HALO, :] = jnp.zeros((HALO, D_CONV), _F32)
            wacc[...] = jnp.zeros_like(wacc)

        win[0:t, :] = dz_ref[...]
        _shifted_windows(win, sh, t + HALO)

        def chunk(c, carry):
            r0 = c * CONV_CHUNK
            rows = pl.ds(pl.multiple_of(r0, 8), CONV_CHUNK)
            u = u_ref[rows, :]
            sg = _sigmoid(ug_ref[rows, :])
            hh = u * sg
            dhh = jnp.zeros((CONV_CHUNK, D_CONV), _F32)
            for j in range(CONV_WIDTH):
                d = CONV_WIDTH - 1 - j
                s = _tap_rows(sh, [d], r0)[d]
                dhh = dhh + cw_ref[j:j + 1, :] * s
                prod = hh * s
                wacc[j] += sum(prod[8 * n:8 * (n + 1), :] for n in range(CONV_CHUNK // 8))
            du_ref[rows, :] = (dhh * sg).astype(du_ref.dtype)
            dug_ref[rows, :] = (dhh * u * sg * (1.0 - sg)).astype(dug_ref.dtype)
            return carry

        lax.fori_loop(0, t // CONV_CHUNK, chunk, 0)
        win[t:t + HALO, :] = win[0:HALO, :]

        @pl.when(i == nt - 1)
        def _():
            dcw_ref[...] = jnp.sum(wacc[...], axis=1)

    row = pl.BlockSpec((t, D_CONV), lambda i: (nt - 1 - i, 0))
    return pl.pallas_call(
        body, name="conv_bwd_taps", grid=(nt,),
        in_specs=[row, row, row, _full((32, D_CONV))],
        out_specs=[row, row, _full((32, D_CONV))],
        out_shape=[jax.ShapeDtypeStruct((lp, D_CONV), _MM)] * 2 + [jax.ShapeDtypeStruct((32, D_CONV), _F32)],
        scratch_shapes=[pltpu.VMEM((t + HALO, D_CONV), _F32), pltpu.VMEM((8, t + HALO, D_CONV), _F32),
                        pltpu.VMEM((32, 8, D_CONV), _F32)],
        compiler_params=_params(("arbitrary",)),
    )(dz, u, ug, cw)


def _proj_bwd_x(pieces, df, wpt, dr, x, head, g, parts=()):
    seq = x.shape[0]
    lp = ROW_X + seq
    t = _row_tile(lp)
    nt = lp // t
    nx = t // ROW_X
    n_parts = len(parts)

    def body(*refs):
        dq_ref, dk_ref, dv_ref, dga_ref, du_ref, dug_ref, dgc_ref, df_ref, wt_ref, dr_ref = refs[:10]
        x_refs, (head_ref, g_ref) = refs[10:10 + nx], refs[10 + nx:12 + nx]
        part_refs, refs = refs[12 + nx:12 + nx + n_parts], refs[12 + nx + n_parts:]
        gx_ref, dmeta_ref, dg_ref, db_ref = refs[:4]
        land_refs, (buf, sem), comm_sems = refs[4:4 + n_parts], refs[4 + n_parts:6 + n_parts], refs[6 + n_parts:]
        i = pl.program_id(0)

        @pl.when(i == 0)
        def _():
            dg_ref[...] = jnp.zeros_like(dg_ref)
            db_ref[...] = jnp.zeros_like(db_ref)
            if n_parts:
                for cp in _chip_exchange_copies(part_refs, land_refs, *comm_sems):
                    cp.start()

        dh = ALPHA * dr_ref[...]
        for n, ref in enumerate((dq_ref, dk_ref, dv_ref, dga_ref, du_ref, dug_ref, dgc_ref)):
            dh = dh + _dot(ref[...].astype(_MM), wt_ref[512 * n:512 * (n + 1), :])
        dh = dh + _dot(df_ref[...], wt_ref[3584:3712, :])
        rows = i * t + lax.broadcasted_iota(jnp.int32, (t, 1), 0)
        dh = jnp.where(rows >= ROW_PAD, dh, 0.0)
        xhat, rstd = _ln_stats(_x_tile(i, x_refs, head_ref))
        dg_ref[...] += jnp.sum(dh * xhat, axis=0, keepdims=True)
        db_ref[...] += jnp.sum(dh, axis=0, keepdims=True)
        dx = _ln_bwd(dh, xhat, rstd, g_ref[...])

        def first_copy():
            return pltpu.make_async_copy(buf.at[0, pl.ds(ROW_X, t - ROW_X), :], gx_ref.at[pl.ds(0, t - ROW_X), :], sem)

        def tile_copy(n):
            return pltpu.make_async_copy(buf.at[n % 2], gx_ref.at[pl.ds(pl.multiple_of(n * t - ROW_X, ROW_X), t), :], sem)

        pl.when(i == 1)(lambda: first_copy().wait())
        pl.when(i > 1)(lambda: tile_copy(i - 1).wait())
        buf[i % 2] = dx

        @pl.when(i == 0)
        def _():
            dmeta_ref[...] = dx[ROW_PAD:ROW_X, :]
            first_copy().start()

        pl.when(i > 0)(lambda: tile_copy(i).start())
        if nt == 1:
            first_copy().wait()
        else:
            pl.when(i == nt - 1)(lambda: tile_copy(i).wait())

        @pl.when(i == nt - 1)
        def _():
            if n_parts:
                for cp in _chip_exchange_copies(part_refs, land_refs, *comm_sems):
                    cp.wait()

    row = lambda w: pl.BlockSpec((t, w), lambda i: (i, 0))
    vec = _full((1, D_MODEL))
    any_spec = pl.BlockSpec(memory_space=pl.ANY)
    return pl.pallas_call(
        body, name="proj_bwd_x", grid=(nt,),
        in_specs=[row(512)] * 7 + [row(128), _full((D_IN_PAD, D_MODEL)), row(D_MODEL)] + _x_specs(t)
                 + [_full((ROW_X, D_MODEL)), vec] + [any_spec] * n_parts,
        out_specs=[any_spec, _full((N_META, D_MODEL)), vec, vec] + [any_spec] * n_parts,
        out_shape=[jax.ShapeDtypeStruct((seq, D_MODEL), _F32), jax.ShapeDtypeStruct((N_META, D_MODEL), _F32),
                   jax.ShapeDtypeStruct((1, D_MODEL), _F32), jax.ShapeDtypeStruct((1, D_MODEL), _F32)]
                  + [jax.ShapeDtypeStruct(p.shape, p.dtype) for p in parts],
        scratch_shapes=[pltpu.VMEM((2, t, D_MODEL), _F32), pltpu.SemaphoreType.DMA(())]
                       + [pltpu.SemaphoreType.DMA((3 * n_parts,))] * (2 if n_parts else 0),
        compiler_params=_params(("arbitrary",)),
    )(*pieces, df, wpt, dr, *([x] * nx), head, g, *parts)


def _proj_bwd_w(h, pieces, name):
    lp = h.shape[0]
    t = _row_tile(lp)
    n = len(pieces)
    widths = [p.shape[1] for p in pieces]

    def body(*refs):
        h_ref, d_refs, w_refs = refs[0], refs[1:1 + n], refs[1 + n:]
        i = pl.program_id(0)

        @pl.when(i == 0)
        def _():
            for w_ref in w_refs:
                w_ref[...] = jnp.zeros_like(w_ref)

        hb = h_ref[...].astype(_MM)
        for d_ref, w_ref in zip(d_refs, w_refs):
            w_ref[...] += _dot_t0(d_ref[...].astype(_MM), hb)

    row = lambda w: pl.BlockSpec((t, w), lambda i: (i, 0))
    return pl.pallas_call(
        body, name=name, grid=(lp // t,),
        in_specs=[row(D_MODEL)] + [row(w) for w in widths],
        out_specs=[_full((w, D_MODEL)) for w in widths],
        out_shape=[jax.ShapeDtypeStruct((w, D_MODEL), _F32) for w in widths],
        compiler_params=_params(("arbitrary",)),
    )(h, *pieces)


def _adamw_math(w, g, m, v):
    m = ADAM_B1 * m + (1.0 - ADAM_B1) * g
    v = ADAM_B2 * v + (1.0 - ADAM_B2) * (g * g)
    m_hat = m / (1.0 - ADAM_B1 ** ADAM_STEP)
    v_hat = v / (1.0 - ADAM_B2 ** ADAM_STEP)
    delta = -ADAM_LR * (m_hat / (jnp.sqrt(v_hat) + ADAM_EPS) + ADAM_WD * w)
    return delta, m, v


def _adamw_tiled(w, g, m, v, rows, name):
    r, c = w.shape

    def body(w_ref, g_ref, m_ref, v_ref, go_ref, d_ref, nm_ref, nv_ref):
        g = g_ref[...]
        go_ref[...] = g
        d_ref[...], nm_ref[...], nv_ref[...] = _adamw_math(w_ref[...], g, m_ref[...], v_ref[...])

    spec = pl.BlockSpec((rows, c), lambda i: (i, 0))
    return pl.pallas_call(
        body, name=name, grid=(pl.cdiv(r, rows),), in_specs=[spec] * 4, out_specs=[spec] * 4,
        out_shape=[jax.ShapeDtypeStruct((r, c), _F32)] * 4,
        compiler_params=_params(("arbitrary",)),
    )(w, g, m, v)


def _adamw_small(ws, gs, ms, vs):
    n = len(ws)

    def body(*refs):
        ins, outs = refs[:4 * n], refs[4 * n:]
        for k in range(n):
            d, m, v = _adamw_math(ins[k][...], ins[n + k][...], ins[2 * n + k][...], ins[3 * n + k][...])
            outs[k][...], outs[n + k][...], outs[2 * n + k][...] = d, m, v

    shapes = [jax.ShapeDtypeStruct(w.shape, _F32) for w in ws]
    res = pl.pallas_call(body, name="adamw_small", out_shape=shapes * 3, compiler_params=_params())(*ws, *gs, *ms, *vs)
    return res[:n], res[n:2 * n], res[2 * n:]


def _place():
    x, y, c = lax.axis_index("x"), lax.axis_index("y"), lax.axis_index("c")
    chips = [(1 - x, y), (x, 1 - y), (1 - x, 1 - y)]
    return x, y, c, chips


def _half(ref, slot, c, rows):
    hr = rows // 2
    return ref.at[slot, pl.ds(pl.multiple_of(c * hr, 8), hr), :]


def _gather_phases(ins, outs, send, recv):
    n = len(ins)
    x, y, c, chips = _place()
    me = 2 * x + y
    sibling = (x, y, 1 - c)

    def copy(k, t, slot, half, to, src=None):
        rows = ins[t].shape[0]
        dst = _half(outs[t], slot, half, rows)
        return pltpu.make_async_remote_copy(
            src_ref=dst if src is None else src, dst_ref=dst,
            send_sem=send.at[k], recv_sem=recv.at[k], device_id=to, device_id_type=MESH)

    def first():
        cps = []
        for t in range(n):
            hr = ins[t].shape[0] // 2
            mine = ins[t].at[pl.ds(pl.multiple_of(c * hr, 8), hr), :]
            cps += [copy(6 * t + j, t, me, c, (*chip, c), src=mine) for j, chip in enumerate(chips)]
        return cps

    def passed():
        return [copy(6 * t + 3 + j, t, 2 * cx + cy, c, sibling) for t in range(n) for j, (cx, cy) in enumerate(chips)]

    def start():
        for cp in first():
            cp.start()

    def relay():
        fwd = passed()
        for t in range(n):
            for j, (cx, cy) in enumerate(chips):
                copy(6 * t + j, t, 2 * cx + cy, c, sibling).wait_recv()
                fwd[3 * t + j].start()

    def finish():
        for t in range(n):
            for j, (cx, cy) in enumerate(chips):
                copy(6 * t + 3 + j, t, 2 * cx + cy, 1 - c, sibling).wait_recv()
        for cp in first() + passed():
            cp.wait_send()

    return start, relay, finish


def _all_gather(shards):
    n = len(shards)

    def body(*refs):
        start, relay, finish = _gather_phases(refs[:n], refs[n:2 * n], *refs[2 * n:])
        start()
        relay()
        finish()

    any_spec = pl.BlockSpec(memory_space=pl.ANY)
    return pl.pallas_call(
        body, name="all_gather",
        in_specs=[any_spec] * n, out_specs=[any_spec] * n,
        out_shape=[jax.ShapeDtypeStruct((N_SHARD,) + s.shape, s.dtype) for s in shards],
        scratch_shapes=[pltpu.SemaphoreType.DMA((6 * n,)), pltpu.SemaphoreType.DMA((6 * n,))],
    )(*shards)


def _own_slot(gathered, own):
    me = 2 * lax.axis_index("x") + lax.axis_index("y")
    slot = lax.broadcasted_iota(jnp.int32, (N_SHARD,) + (1,) * own.ndim, 0)
    return jnp.where(slot == me, own[None], gathered)


def _sibling_exchange(bufs):
    n = len(bufs)

    def body(*refs):
        ins, got = refs[:n], refs[n:2 * n]
        send, recv = refs[2 * n:]
        x, y, c, _ = _place()
        cps = []
        for t in range(n):
            hr = ins[t].shape[0] // (2 * N_SHARD)
            for s in range(N_SHARD):
                k = N_SHARD * t + s
                cps.append(pltpu.make_async_remote_copy(
                    src_ref=ins[t].at[pl.ds(pl.multiple_of((2 * s + 1 - c) * hr, 32 // ins[t].dtype.itemsize), hr), :],
                    dst_ref=got[t].at[s],
                    send_sem=send.at[k], recv_sem=recv.at[k], device_id=(x, y, 1 - c), device_id_type=MESH))
        for cp in cps:
            cp.start()
        for cp in cps:
            cp.wait()

    any_spec = pl.BlockSpec(memory_space=pl.ANY)
    return pl.pallas_call(
        body, name="sibling_exchange",
        in_specs=[any_spec] * n, out_specs=[any_spec] * n,
        out_shape=[jax.ShapeDtypeStruct((N_SHARD, b.shape[0] // (2 * N_SHARD), b.shape[1]), b.dtype) for b in bufs],
        scratch_shapes=[pltpu.SemaphoreType.DMA((N_SHARD * n,))] * 2,
    )(*bufs)


def _half_sum(bufs, got, wire, c):
    n = len(bufs)

    def body(c_ref, *refs):
        ins, gots, outs = refs[:n], refs[n:2 * n], refs[2 * n:]
        for t in range(n):
            outs[t][0] = (ins[t][...].astype(_F32) + gots[t][0].astype(_F32)).astype(outs[t].dtype)

    slot = lambda g: pl.BlockSpec((1,) + tuple(g.shape[1:]), lambda s, c_ref: (s, 0, 0))
    grid_spec = pltpu.PrefetchScalarGridSpec(
        num_scalar_prefetch=1, grid=(N_SHARD,),
        in_specs=[pl.BlockSpec(tuple(g.shape[1:]), lambda s, c_ref: (2 * s + c_ref[0], 0)) for g in got]
                 + [slot(g) for g in got],
        out_specs=[slot(g) for g in got])
    return pl.pallas_call(
        body, name="half_sum", grid_spec=grid_spec,
        out_shape=[jax.ShapeDtypeStruct(g.shape, d) for g, d in zip(got, wire)],
        compiler_params=_params(("arbitrary",)),
    )(c, *bufs, *got)


def _chip_exchange_copies(ins, outs, send, recv):
    x, y, c, chips = _place()
    me = 2 * x + y
    return [pltpu.make_async_remote_copy(
        src_ref=ins[t].at[2 * cx + cy], dst_ref=outs[t].at[me], send_sem=send.at[3 * t + j], recv_sem=recv.at[3 * t + j],
        device_id=(cx, cy, c), device_id_type=MESH) for t in range(len(ins)) for j, (cx, cy) in enumerate(chips)]


def _chip_sum_share(parts, got, me):
    n = len(got)

    def body(me_ref, *refs):
        owns, gots, outs = refs[:n], refs[n:2 * n], refs[2 * n:3 * n]
        accs, (local, send, recv) = refs[3 * n:4 * n], refs[4 * n:]
        s = pl.program_id(0)
        x, y, c, _ = _place()
        for t in range(n):
            term = jnp.where(s == me_ref[0], owns[t][0], gots[t][0]).astype(_F32)

            @pl.when(s == 0)
            def _():
                accs[t][...] = term

            @pl.when(s > 0)
            def _():
                accs[t][...] += term

        @pl.when(s == N_SHARD - 1)
        def _():
            cps = []
            for t in range(n):
                hr = accs[t].shape[0]
                mine = outs[t].at[pl.ds(pl.multiple_of(c * hr, 8), hr), :]
                cps.append(pltpu.make_async_copy(accs[t], mine, local.at[t]))
                cps.append(pltpu.make_async_remote_copy(
                    src_ref=accs[t], dst_ref=mine, send_sem=send.at[t], recv_sem=recv.at[t],
                    device_id=(x, y, 1 - c), device_id_type=MESH))
            for cp in cps:
                cp.start()
            for cp in cps:
                cp.wait()

    blk = lambda g: (1,) + tuple(g.shape[1:])
    grid_spec = pltpu.PrefetchScalarGridSpec(
        num_scalar_prefetch=1, grid=(N_SHARD,),
        in_specs=[pl.BlockSpec(blk(g), lambda s, me_ref: (me_ref[0], 0, 0)) for g in got]
                 + [pl.BlockSpec(blk(g), lambda s, me_ref: (jnp.where(s == me_ref[0], (s + 1) % N_SHARD, s), 0, 0))
                    for g in got],
        out_specs=[pl.BlockSpec(memory_space=pl.ANY)] * n,
        scratch_shapes=[pltpu.VMEM(tuple(g.shape[1:]), _F32) for g in got] + [pltpu.SemaphoreType.DMA((n,))] * 3)
    return pl.pallas_call(
        body, name="chip_sum_share", grid_spec=grid_spec,
        out_shape=[jax.ShapeDtypeStruct((2 * g.shape[1], g.shape[2]), _F32) for g in got],
        compiler_params=_params(("arbitrary",)),
    )(me, *parts, *got)


def _as_operand(index):
    return jnp.asarray(index, jnp.int32).reshape(1)


def _chip_partials(bufs, wire):
    return _half_sum(bufs, _sibling_exchange(bufs), wire, _as_operand(lax.axis_index("c")))


def _finish_reduction(parts, landed):
    return _chip_sum_share(parts, landed, _as_operand(2 * lax.axis_index("x") + lax.axis_index("y")))


def _reduce_small(buf):
    rows = buf.shape[0] // N_SHARD
    hr = rows // 2

    def body(in_ref, out_ref, got1, part, got2, send, recv):
        x, y, c, chips = _place()
        me = 2 * x + y
        sibling = (x, y, 1 - c)

        def half(s, which):
            return in_ref.at[pl.ds(pl.multiple_of((2 * s + which) * hr, 8), hr), :]

        first = [pltpu.make_async_remote_copy(src_ref=half(s, 1 - c), dst_ref=got1.at[s], send_sem=send.at[s],
                                              recv_sem=recv.at[s], device_id=sibling, device_id_type=MESH)
                 for s in range(N_SHARD)]
        for cp in first:
            cp.start()
        for s in range(N_SHARD):
            first[s].wait()
            part[s] = half(s, c)[...] + got1[s]
        got2[me] = part[me]
        second = [pltpu.make_async_remote_copy(src_ref=part.at[2 * cx + cy], dst_ref=got2.at[me],
                                               send_sem=send.at[N_SHARD + j], recv_sem=recv.at[N_SHARD + j],
                                               device_id=(cx, cy, c), device_id_type=MESH)
                  for j, (cx, cy) in enumerate(chips)]
        for cp in second:
            cp.start()
        for cp in second:
            cp.wait()
        mine = ((got2[0] + got2[1]) + got2[2]) + got2[3]
        out_ref[pl.ds(pl.multiple_of(c * hr, 8), hr), :] = mine
        last = pltpu.make_async_remote_copy(
            src_ref=out_ref.at[pl.ds(pl.multiple_of(c * hr, 8), hr), :],
            dst_ref=out_ref.at[pl.ds(pl.multiple_of(c * hr, 8), hr), :],
            send_sem=send.at[2 * N_SHARD - 1], recv_sem=recv.at[2 * N_SHARD - 1], device_id=sibling, device_id_type=MESH)
        last.start()
        last.wait()

    vmem = pl.BlockSpec(memory_space=pltpu.VMEM)
    return pl.pallas_call(
        body, name="reduce_small", in_specs=[vmem], out_specs=vmem,
        out_shape=jax.ShapeDtypeStruct((rows, buf.shape[1]), buf.dtype),
        scratch_shapes=[pltpu.VMEM((N_SHARD, hr, buf.shape[1]), buf.dtype)] * 3
                       + [pltpu.SemaphoreType.DMA((2 * N_SHARD,))] * 2,
    )(buf)


REPL_SIZES = (D_MODEL, D_MODEL, N_HEADS, D_CONV, D_CONV, D_CONV, D_MODEL, D_MODEL)
REPL_ROWS = 48


def _pack_repl(parts):
    flat = jnp.concatenate([p.reshape(-1).astype(_F32) for p in parts])
    return jnp.pad(flat, (0, REPL_ROWS * 128 - flat.shape[0])).reshape(REPL_ROWS, 128)


def _unpack_repl(packed, shapes):
    flat = packed.reshape(-1)
    out, off = [], 0
    for shape, size in zip(shapes, REPL_SIZES):
        out.append(flat[off:off + size].reshape(shape))
        off += size
    return out


W_IN_PIECES = (("q", 0, 512), ("k", 512, 1024), ("v", 1024, 1536), ("f", 1536, 1544), ("ga", 1544, 2056),
               ("u", 2056, 2568), ("ug", 2568, 3080), ("gc", 3080, 3592))


def _projection_operand(slots):
    w_in_t = slots[:, :W_IN_SHARD, :].reshape(D_IN, D_MODEL)
    zrow = jnp.zeros((128 - N_HEADS, D_MODEL), w_in_t.dtype)
    return jnp.concatenate([w_in_t[:OFF_F], w_in_t[OFF_F + N_HEADS:], w_in_t[OFF_F:OFF_F + N_HEADS], zrow], axis=0)


def _gradient_slots(pieces):
    dw = jnp.concatenate([pieces[name][:b - a] for name, a, b in W_IN_PIECES], axis=0)
    dw = jnp.pad(dw.reshape(N_SHARD, W_IN_SHARD, D_MODEL), ((0, 0), (0, W_IN_SLOT - W_IN_SHARD), (0, 0)))
    return dw.reshape(N_SHARD * W_IN_SLOT, D_MODEL)


def _local_grads(x, target, meta, ln_in_g, ln_in_b, w_in_slots, b_f, conv_w, conv_b, ln_conv_g, ln_conv_b,
                 w_pw, w_out, ln_out_g, ln_out_b, chip_partials=None, late_shards=None):
    vec = lambda a: a.reshape(1, -1).astype(_F32)
    head = jnp.concatenate([jnp.zeros((ROW_PAD, D_MODEL), _F32), meta], axis=0)
    wpt = _projection_operand(w_in_slots)
    bf = jnp.pad(vec(b_f), ((0, 0), (0, 128 - N_HEADS)))
    cw = jnp.pad(conv_w, ((0, 32 - CONV_WIDTH), (0, 0)))

    h, q, k, v, ga, u, ug, gc, f, vt, *late = _ln_proj(x, head, vec(ln_in_g), vec(ln_in_b), wpt, late_shards or ())
    if late_shards:
        w_pw, w_out = (_own_slot(gathered, own).reshape(N_SHARD * own.shape[0], own.shape[1])
                       for gathered, own in zip(late, late_shards))
    ccol = _decay_cumsum(f, bf)
    o, lse = _attn_fwd(q, k, vt, ccol)
    z, pw = _conv_fwd(u, ug, cw, vec(conv_b), vec(ln_conv_g), vec(ln_conv_b), w_pw)
    dr, do, dga, dpw, dgc, delta, dw_out, dg_out, db_out, loss = _out_loss_bwd(
        o, ga, pw, gc, h, target, w_out, vec(ln_out_g), vec(ln_out_b))
    dq, dk, dv, dck, dcq = _attn_bwd(q, k, v, do, lse, delta, ccol)
    df, dbf = _decay_bwd(dck, dcq, f, bf)
    dz, dw_pw, dg_conv, db_conv, dconv_b = _conv_bwd_rows(dpw, z, vec(ln_conv_g), vec(ln_conv_b), w_pw)
    du, dug, dcw = _conv_bwd_taps(dz, u, ug, cw)
    pieces = (dq, dk, dv, dga, du, dug, dgc)
    dwq, dwk, dwv, dwga = _proj_bwd_w(h, pieces[:4], "proj_bwd_w_attn")
    dwu, dwug, dwgc, dwf = _proj_bwd_w(h, pieces[4:] + (df,), "proj_bwd_w_conv")
    dw_in_t = _gradient_slots(dict(q=dwq, k=dwk, v=dwv, f=dwf, ga=dwga, u=dwu, ug=dwug, gc=dwgc))
    parts = chip_partials([dw_in_t, dw_pw, dw_out]) if chip_partials else []
    grad_x, dmeta, dg_in, db_in, *landed = _proj_bwd_x(pieces, df, wpt, dr, x, head, vec(ln_in_g), parts)
    grads = dict(meta=dmeta, ln_in_g=dg_in, ln_in_b=db_in, w_in_t=dw_in_t, b_f=dbf[:, :N_HEADS],
                 conv_w=dcw[:CONV_WIDTH], conv_b=dconv_b, ln_conv_g=dg_conv, ln_conv_b=db_conv,
                 w_pw=dw_pw, w_out=dw_out, ln_out_g=dg_out, ln_out_b=db_out, reduced=(parts, landed))
    return loss, grad_x, grads


def kernel(x, meta, ln_in_g, ln_in_b, w_in, b_f, conv_w, conv_b, ln_conv_g, ln_conv_b, w_pw, w_out, ln_out_g, ln_out_b, loss_target, m_meta, m_ln_in_g, m_ln_in_b, m_w_in, m_b_f, m_conv_w, m_conv_b, m_ln_conv_g, m_ln_conv_b, m_w_pw, m_w_out, m_ln_out_g, m_ln_out_b, v_meta, v_ln_in_g, v_ln_in_b, v_w_in, v_b_f, v_conv_w, v_conv_b, v_ln_conv_g, v_ln_conv_b, v_w_pw, v_w_out, v_ln_out_g, v_ln_out_b):
    conv_w2, w_pw2, w_out2 = conv_w[0], w_pw[0], w_out[0]
    w_t = jnp.transpose(w_in[0])

    small = jnp.concatenate([jnp.pad(conv_w2, ((0, 1), (0, 0))), meta.reshape(32, 128)], axis=0)
    shards = [jnp.pad(w_t.astype(_MM), ((0, W_IN_SLOT - W_IN_SHARD), (0, 0))), small]
    g_in, g_small = (_own_slot(g, s) for g, s in zip(_all_gather(shards), shards))
    conv_w_full = jnp.transpose(g_small[:, :CONV_WIDTH, :], (1, 0, 2)).reshape(CONV_WIDTH, D_CONV)
    meta_full = jnp.transpose(g_small[:, 32:64, :].reshape(N_SHARD, N_META, 256), (1, 0, 2)).reshape(N_META, D_MODEL)

    loss, grad_x, gr = _local_grads(x[0], loss_target[0], meta_full, ln_in_g, ln_in_b, g_in, b_f, conv_w_full,
                                    conv_b, ln_conv_g, ln_conv_b, None, None, ln_out_g, ln_out_b,
                                    chip_partials=lambda big: _chip_partials([b.astype(_MM) for b in big], [_MM] * 3),
                                    late_shards=[w_pw2.astype(_MM), w_out2.astype(_MM)])
    grad_x = grad_x[None]
    gw_slot, gw_pw, gw_out = _finish_reduction(*gr["reduced"])

    repl_names = ("ln_in_g", "ln_in_b", "b_f", "conv_b", "ln_conv_g", "ln_conv_b", "ln_out_g", "ln_out_b")
    repl = _pack_repl([gr[n] for n in repl_names] + [loss])
    gcw = jnp.transpose(jnp.pad(gr["conv_w"], ((0, 1), (0, 0))).reshape(32, N_SHARD, 128), (1, 0, 2))
    gmeta = jnp.transpose(gr["meta"].reshape(N_META, N_SHARD, 256), (1, 0, 2)).reshape(N_SHARD, 32, 128)
    small_g = jnp.concatenate([gcw, gmeta, jnp.broadcast_to(repl[None], (N_SHARD, REPL_ROWS, 128))], axis=1)
    small_rows = small_g.shape[1]
    gsmall = _reduce_small(small_g.reshape(N_SHARD * small_rows, 128))
    g_conv_w = gsmall[:CONV_WIDTH]
    g_meta = gsmall[32:64].reshape(N_META, 256)
    g_repl = gsmall[64:64 + REPL_ROWS]

    repl_w = (ln_in_g, ln_in_b, b_f, conv_b, ln_conv_g, ln_conv_b, ln_out_g, ln_out_b)
    repl_m = (m_ln_in_g, m_ln_in_b, m_b_f, m_conv_b, m_ln_conv_g, m_ln_conv_b, m_ln_out_g, m_ln_out_b)
    repl_v = (v_ln_in_g, v_ln_in_b, v_b_f, v_conv_b, v_ln_conv_g, v_ln_conv_b, v_ln_out_g, v_ln_out_b)
    zero = jnp.zeros((1,), _F32)
    res_t = _adamw_tiled(w_t, gw_slot, jnp.transpose(m_w_in[0]), jnp.transpose(v_w_in[0]), 256, "adamw_w_in")
    gw_in, d_in, nm_in, nv_in = (jnp.transpose(a) for a in res_t)
    ws = [w_pw2, w_out2, conv_w2, meta, _pack_repl(repl_w + (zero,))]
    gs = [gw_pw, gw_out, g_conv_w, g_meta, g_repl]
    ms = [m_w_pw[0], m_w_out[0], m_conv_w[0], m_meta, _pack_repl(repl_m + (zero,))]
    vs = [v_w_pw[0], v_w_out[0], v_conv_w[0], v_meta, _pack_repl(repl_v + (zero + 1.0,))]
    ds, nms, nvs = _adamw_small(ws, gs, ms, vs)

    shapes = [w.shape for w in repl_w]
    total_loss = g_repl.reshape(-1)[sum(REPL_SIZES)]
    names = ("meta", "ln_in_g", "ln_in_b", "w_in", "b_f", "conv_w", "conv_b", "ln_conv_g", "ln_conv_b",
             "w_pw", "w_out", "ln_out_g", "ln_out_b")

    def assemble(t_in, smalls):
        pw_, out_, cw_, meta_, packed = smalls
        r = dict(zip(repl_names, _unpack_repl(packed, shapes)))
        r.update(meta=meta_, w_in=t_in[None], conv_w=cw_[None], w_pw=pw_[None], w_out=out_[None])
        return [r[n] for n in names]

    return (total_loss, grad_x, *assemble(gw_in, gs), *assemble(d_in, ds), *assemble(nm_in, nms),
            *assemble(nv_in, nvs))
```

```python
import functools

import jax
import jax.numpy as jnp
from jax import lax
from jax.experimental import pallas as pl
from jax.experimental.pallas import tpu as pltpu

D_MODEL = 1024
N_META = 16
D_ATTN = 512
D_CONV = 512
N_HEADS = 8
HEAD_DIM = 64
N_PAIRS = 4
CONV_WIDTH = 31
LN_EPS = 1e-5
ALPHA = 2.0 ** 0.25
ATTN_SCALE = 0.125
D_IN = 3592
OFF_F = 3 * D_ATTN
D_IN_PAD = 3712
N_SHARD = 4
W_IN_SHARD = D_IN // N_SHARD
W_IN_SLOT = 928

ROW_PAD = 112
ROW_X = ROW_PAD + N_META
HALO = 32
MASKED = 1e30

ADAM_LR = 0.001
ADAM_B1 = 0.9
ADAM_B2 = 0.999
ADAM_EPS = 1e-08
ADAM_WD = 0.01
ADAM_STEP = 10

VMEM_LIMIT = 56 * 1024 * 1024

_MM = jnp.bfloat16
_F32 = jnp.float32
MESH = pl.DeviceIdType.MESH


def _params(sem=None):
    return pltpu.CompilerParams(dimension_semantics=sem, vmem_limit_bytes=VMEM_LIMIT)


def _row_tile(n_rows):
    for t in (384, 256, 128):
        if n_rows % t == 0:
            return t
    raise ValueError(f"padded sequence length {n_rows} is not a multiple of 128")


def _sigmoid(x):
    return 1.0 / (1.0 + jnp.exp(-x))


def _ln_stats(x):
    mu = jnp.mean(x, axis=-1, keepdims=True)
    xc = x - mu
    var = jnp.mean(xc * xc, axis=-1, keepdims=True)
    rstd = lax.rsqrt(var + LN_EPS)
    return xc * rstd, rstd


def _ln_bwd(dy, xhat, rstd, g):
    dxh = dy * g
    m1 = jnp.mean(dxh, axis=-1, keepdims=True)
    m2 = jnp.mean(dxh * xhat, axis=-1, keepdims=True)
    return rstd * (dxh - m1 - xhat * m2)


def _dot(a, b):
    return jnp.dot(a, b, preferred_element_type=_F32)


def _dot_t0(a, b):
    return lax.dot_general(a, b, (((0,), (0,)), ((), ())), preferred_element_type=_F32)


def _dot_t1(a, b):
    return lax.dot_general(a, b, (((1,), (1,)), ((), ())), preferred_element_type=_F32)


def _dot_exact(a, b):
    return jnp.dot(a, b, preferred_element_type=_F32, precision=lax.Precision.HIGHEST)


def _full(shape):
    return pl.BlockSpec(shape, lambda *_: (0,) * len(shape))


def _x_specs(t):
    n = t // ROW_X
    return [pl.BlockSpec((ROW_X, D_MODEL), lambda i, s=s: (jnp.maximum(i * n + s - 1, 0), 0)) for s in range(n)]


def _x_tile(i, refs, head_ref=None):
    first = refs[0][...]
    if head_ref is not None:
        first = jnp.where(i == 0, head_ref[...], first)
    return jnp.concatenate([first] + [r[...] for r in refs[1:]], axis=0)


def _ln_proj(x, head, g, b, wpt, gather=()):
    lp = ROW_X + x.shape[0]
    t = _row_tile(lp)
    nt = lp // t
    nx = t // ROW_X
    ng = len(gather)

    def body(*refs):
        x_refs, (head_ref, g_ref, b_ref, wt_ref) = refs[:nx], refs[nx:nx + 4]
        shard_refs, refs = refs[nx + 4:nx + 4 + ng], refs[nx + 4 + ng:]
        h_ref, q_ref, k_ref, v_ref, ga_ref, u_ref, ug_ref, gc_ref, f_ref, vt_ref = refs[:10]
        gathered_refs, comm_sems = refs[10:10 + ng], refs[10 + ng:]
        i = pl.program_id(0)
        if ng:
            start, relay, finish = _gather_phases(shard_refs, gathered_refs, *comm_sems)
            pl.when(i == 0)(start)
            pl.when(i == nt // 2)(relay)
        xhat, _ = _ln_stats(_x_tile(i, x_refs, head_ref))
        rows = i * t + lax.broadcasted_iota(jnp.int32, (t, 1), 0)
        h = jnp.where(rows >= ROW_PAD, xhat * g_ref[...] + b_ref[...], 0.0)
        h_ref[...] = h
        hb = h.astype(_MM)
        q_ref[...] = (_dot_t1(hb, wt_ref[0:512, :]) * ATTN_SCALE).astype(q_ref.dtype)
        k_ref[...] = _dot_t1(hb, wt_ref[512:1024, :]).astype(k_ref.dtype)
        v = _dot_t1(hb, wt_ref[1024:1536, :])
        v_ref[...] = v.astype(v_ref.dtype)
        vt_ref[...] = v.T.astype(vt_ref.dtype)
        ga_ref[...] = _dot_t1(hb, wt_ref[1536:2048, :])
        u_ref[...] = _dot_t1(hb, wt_ref[2048:2560, :])
        ug_ref[...] = _dot_t1(hb, wt_ref[2560:3072, :])
        gc_ref[...] = _dot_t1(hb, wt_ref[3072:3584, :])
        f_ref[...] = _dot_t1(hb, wt_ref[3584:3712, :])
        if ng:
            pl.when(i == nt - 1)(finish)

    row = lambda w: pl.BlockSpec((t, w), lambda i: (i, 0))
    f32 = lambda w: jax.ShapeDtypeStruct((lp, w), _F32)
    mm = lambda w: jax.ShapeDtypeStruct((lp, w), _MM)
    any_spec = pl.BlockSpec(memory_space=pl.ANY)
    return pl.pallas_call(
        body, name="ln_proj", grid=(nt,),
        in_specs=_x_specs(t) + [_full((ROW_X, D_MODEL)), _full((1, D_MODEL)), _full((1, D_MODEL)),
                                _full((D_IN_PAD, D_MODEL))] + [any_spec] * ng,
        out_specs=[row(D_MODEL)] + [row(512)] * 7 + [row(128), pl.BlockSpec((512, t), lambda i: (0, i))]
                  + [any_spec] * ng,
        out_shape=[f32(D_MODEL), mm(512), mm(512), mm(512), f32(512), f32(512), f32(512), f32(512), f32(128),
                   jax.ShapeDtypeStruct((512, lp), _MM)]
                  + [jax.ShapeDtypeStruct((N_SHARD,) + s.shape, s.dtype) for s in gather],
        scratch_shapes=[pltpu.SemaphoreType.DMA((6 * ng,))] * (2 if ng else 0),
        compiler_params=_params(("arbitrary",)),
    )(*([x] * nx), head, g, b, wpt, *gather)


def _log_sigmoid(z):
    return jnp.minimum(z, 0.0) - jnp.log(1.0 + jnp.exp(-jnp.abs(z)))


def _decay_cumsum(f, bf):
    lp = f.shape[0]
    t = _row_tile(lp)

    def body(f_ref, bf_ref, ccol_ref):
        r = lax.broadcasted_iota(jnp.int32, (t, t), 0)
        c = lax.broadcasted_iota(jnp.int32, (t, t), 1)
        tril = (c <= r).astype(_F32)

        def step(bi, carry):
            rows = pl.ds(pl.multiple_of(bi * t, 128), t)
            lf = _log_sigmoid(f_ref[rows, :] + bf_ref[...])
            cb = _dot_exact(tril, lf) + carry
            for p in range(N_PAIRS):
                ccol_ref[p, rows, :] = cb if p == 0 else pltpu.roll(cb, 128 - 2 * p, axis=1)
            return cb[t - 1:t, :]

        lax.fori_loop(0, lp // t, step, jnp.zeros((1, 128), _F32))

    return pl.pallas_call(
        body, name="decay_cumsum",
        out_shape=jax.ShapeDtypeStruct((N_PAIRS, lp, 128), _F32),
        compiler_params=_params(),
    )(f, bf)


def _split3(x):
    hi = x.astype(_MM)
    r1 = x - hi.astype(_F32)
    mid = r1.astype(_MM)
    return hi, mid, (r1 - mid.astype(_F32)).astype(_MM)


def _augment(x, hd, first, second, transposed=False):
    lane = lax.broadcasted_iota(jnp.int32, (1, 128), 1)
    base = HEAD_DIM * (1 - hd)
    out = jnp.where(lane // HEAD_DIM == hd, x, jnp.zeros_like(x))
    for slot, val in enumerate((first, second)):
        lo = base + 3 * slot
        if val is None:
            continue
        if isinstance(val, float):
            out = jnp.where((lane >= lo) & (lane < lo + 3), jnp.asarray(val, x.dtype), out)
        else:
            for n, piece in enumerate(val):
                out = jnp.where(lane == lo + n, piece[:, hd:hd + 1], out)
    return _transpose_mm(out) if transposed else out


def _transpose_mm(x):
    return x.astype(_F32).T.astype(_MM)


def _diag_valid(i, t):
    sub = lax.broadcasted_iota(jnp.int32, (t, t), 0)
    return (sub <= lax.broadcasted_iota(jnp.int32, (t, t), 1)) & (i * t + sub >= ROW_PAD)


def _key_pieces(c, first_row):
    pos = first_row + lax.broadcasted_iota(jnp.int32, (c.shape[0], 1), 0)
    return _split3(jnp.where(pos >= ROW_PAD, -c, -MASKED))


def _key_operand(k, pieces, hd, transposed=False):
    return _augment(k, hd, pieces, 1.0, transposed)


def _attn_fwd(q, k, vt, ccol):
    lp = q.shape[0]
    t = _row_tile(lp)
    nb = lp // t

    def body(q_ref, k_ref, vt_ref, ccol_ref, o_ref, lse_ref, ka_ref):
        i = pl.program_id(1)

        @pl.when(i == 0)
        def _():
            for n in range(nb):
                rows = slice(n * t, (n + 1) * t)
                pieces = _key_pieces(ccol_ref[0, rows, :], n * t)
                for hd in range(2):
                    ka_ref[hd, rows, :] = _key_operand(k_ref[rows, :], pieces, hd)

        qats = [_augment(q_ref[...], hd, 1.0, None, transposed=True) for hd in range(2)]

        def scores(j):
            keys = pl.ds(pl.multiple_of(j * t, 128), t)
            return tuple(_dot(ka_ref[hd, keys, :], qats[hd]) for hd in range(2))

        def update(j, sts, states, diag):
            vt = vt_ref[:, pl.ds(pl.multiple_of(j * t, 128), t)]
            new = []
            for st, (m, l, acc) in zip(sts, states):
                if diag:
                    st = jnp.where(_diag_valid(i, t), st, -MASKED)
                m_new = jnp.maximum(m, jnp.max(st, axis=0, keepdims=True))
                a = jnp.exp(m - m_new)
                pt = jnp.exp(st - m_new)
                l = a * l + jnp.sum(pt, axis=0, keepdims=True)
                new.append((m_new, l, a * acc + _dot(vt, pt.astype(_MM))))
            return tuple(new)

        def step(j, carry):
            sts, states = carry
            return scores(j + 1), update(j, sts, states, False)

        def step2(n, carry):
            sts, states = carry
            nxt, last = scores(2 * n + 1), scores(2 * n + 2)
            return last, update(2 * n + 1, nxt, update(2 * n, sts, states, False), False)

        init = (jnp.full((1, t), -MASKED, _F32), jnp.zeros((1, t), _F32), jnp.zeros((128, t), _F32))
        carry = lax.fori_loop(0, i // 2, step2, (scores(0), (init, init)))
        sts, states = lax.fori_loop(0, i % 2, lambda _, c: step(i - 1, c), carry)
        (m0, l0, acc0), (m1, l1, acc1) = update(i, sts, states, True)
        row = lax.broadcasted_iota(jnp.int32, (128, t), 0)
        o_ref[...] = jnp.where(row < HEAD_DIM, acc0 / l0, acc1 / l1).T
        lse_ref[0] = jnp.where(row == 0, m0 + jnp.log(l0), jnp.where(row == 1, m1 + jnp.log(l1), 0.0)).T

    return pl.pallas_call(
        body, name="attn_fwd", grid=(N_PAIRS, nb),
        in_specs=[pl.BlockSpec((t, 128), lambda p, i: (i, p)),
                  pl.BlockSpec((lp, 128), lambda p, i: (0, p)),
                  pl.BlockSpec((128, lp), lambda p, i: (p, 0)),
                  pl.BlockSpec((1, lp, 128), lambda p, i: (p, 0, 0))],
        out_specs=[pl.BlockSpec((t, 128), lambda p, i: (i, p)),
                   pl.BlockSpec((1, t, 128), lambda p, i: (p, i, 0))],
        out_shape=[jax.ShapeDtypeStruct((lp, D_ATTN), _F32), jax.ShapeDtypeStruct((N_PAIRS, lp, 128), _F32)],
        scratch_shapes=[pltpu.VMEM((2, lp, 128), _MM)],
        compiler_params=_params(("arbitrary", "arbitrary")),
    )(q, k, vt, ccol)


CONV_CHUNK = 32


def _shifted_windows(win, sh, rows):
    sh[0] = win[...]
    for b in range(1, 8):
        sh[b, 0:rows - 8, :] = win[b:b + rows - 8, :]


def _tap_rows(sh, offsets, r0):
    out = {}
    for b in range(8):
        ds = [d for d in offsets if d % 8 == b]
        if ds:
            lo, hi = min(ds) // 8, max(ds) // 8
            rows = sh[b, pl.ds(pl.multiple_of(r0 + 8 * lo, 8), CONV_CHUNK + 8 * (hi - lo)), :]
            for d in ds:
                out[d] = rows[8 * (d // 8 - lo):8 * (d // 8 - lo) + CONV_CHUNK, :]
    return out


def _conv_fwd(u, ug, cw, cb, g, b, wpw):
    lp = u.shape[0]
    t = _row_tile(lp)

    def body(u_ref, ug_ref, cw_ref, cb_ref, g_ref, b_ref, wpw_ref, z_ref, pw_ref, win, sh):
        i = pl.program_id(0)

        @pl.when(i == 0)
        def _():
            win[0:HALO, :] = jnp.zeros((HALO, D_CONV), _F32)

        win[HALO:HALO + t, :] = u_ref[...] * _sigmoid(ug_ref[...])
        _shifted_windows(win, sh, t + HALO)

        def chunk(c, carry):
            r0 = c * CONV_CHUNK
            acc = jnp.broadcast_to(cb_ref[...], (CONV_CHUNK, D_CONV))
            first = HALO - (CONV_WIDTH - 1)
            taps = _tap_rows(sh, range(first, first + CONV_WIDTH), r0)
            for j in range(CONV_WIDTH):
                acc = acc + cw_ref[j:j + 1, :] * taps[first + j]
            z_ref[pl.ds(pl.multiple_of(r0, 8), CONV_CHUNK), :] = acc
            return carry

        lax.fori_loop(0, t // CONV_CHUNK, chunk, 0)
        xhat, _ = _ln_stats(z_ref[...])
        zn = xhat * g_ref[...] + b_ref[...]
        a = zn * _sigmoid(zn)
        pw_ref[...] = _dot(a.astype(_MM), wpw_ref[...])
        win[0:HALO, :] = win[t:t + HALO, :]

    row = pl.BlockSpec((t, D_CONV), lambda i: (i, 0))
    vec = _full((1, D_CONV))
    return pl.pallas_call(
        body, name="conv_fwd", grid=(lp // t,),
        in_specs=[row, row, _full((32, D_CONV)), vec, vec, vec, _full((D_CONV, D_CONV))],
        out_specs=[row, row],
        out_shape=[jax.ShapeDtypeStruct((lp, D_CONV), _F32)] * 2,
        scratch_shapes=[pltpu.VMEM((t + HALO, D_CONV), _F32), pltpu.VMEM((8, t + HALO, D_CONV), _F32)],
        compiler_params=_params(("arbitrary",)),
    )(u, ug, cw, cb, g, b, wpw)


def _out_loss_bwd(o, ga, pw, gc, h, target, w_out, g, b):
    lp = o.shape[0]
    t = _row_tile(lp)
    nx = t // ROW_X

    def body(*refs):
        o_ref, ga_ref, pw_ref, gc_ref, h_ref = refs[:5]
        tgt_refs, (w_ref, g_ref, b_ref) = refs[5:5 + nx], refs[5 + nx:8 + nx]
        dr_ref, do_ref, dga_ref, dpw_ref, dgc_ref, delta_ref, dw_ref, dg_ref, db_ref, loss_ref = refs[8 + nx:]
        i = pl.program_id(0)

        @pl.when(i == 0)
        def _():
            dw_ref[...] = jnp.zeros_like(dw_ref)
            dg_ref[...] = jnp.zeros_like(dg_ref)
            db_ref[...] = jnp.zeros_like(db_ref)
            loss_ref[...] = jnp.zeros_like(loss_ref)

        o, ga, pw, gc = o_ref[...], ga_ref[...], pw_ref[...], gc_ref[...]
        sa, sc = _sigmoid(ga), _sigmoid(gc)
        silu_a, silu_c = ga * sa, gc * sc
        ycat = jnp.concatenate([o * silu_a, pw * silu_c], axis=-1).astype(_MM)
        r = ALPHA * h_ref[...] + _dot(ycat, w_ref[...])
        xhat, rstd = _ln_stats(r)
        out = xhat * g_ref[...] + b_ref[...]
        rows = i * t + lax.broadcasted_iota(jnp.int32, (t, 1), 0)
        diff = jnp.where(rows >= ROW_X, out - _x_tile(i, tgt_refs), 0.0)
        loss_ref[...] += (0.5 / D_MODEL) * jnp.sum(diff * diff, keepdims=True)
        dout = diff * (1.0 / D_MODEL)
        dg_ref[...] += jnp.sum(dout * xhat, axis=0, keepdims=True)
        db_ref[...] += jnp.sum(dout, axis=0, keepdims=True)
        dr = _ln_bwd(dout, xhat, rstd, g_ref[...])
        dr_ref[...] = dr
        drb = dr.astype(_MM)
        dw_ref[...] += _dot_t0(ycat, drb)
        dy = _dot_t1(drb, w_ref[...])
        dya, dyc = dy[:, :D_ATTN], dy[:, D_ATTN:]
        do = dya * silu_a
        do_ref[...] = do.astype(do_ref.dtype)
        dga_ref[...] = (dya * o * (sa * (1.0 + ga * (1.0 - sa)))).astype(dga_ref.dtype)
        dpw_ref[...] = (dyc * silu_c).astype(dpw_ref.dtype)
        dgc_ref[...] = (dyc * pw * (sc * (1.0 + gc * (1.0 - sc)))).astype(dgc_ref.dtype)
        sel = (lax.broadcasted_iota(jnp.int32, (128, 128), 0) // HEAD_DIM
               == lax.broadcasted_iota(jnp.int32, (128, 128), 1)).astype(_F32)
        prod = do * o
        for p in range(N_PAIRS):
            delta_ref[p] = _dot_exact(prod[:, 128 * p:128 * (p + 1)], sel)

    row = lambda w: pl.BlockSpec((t, w), lambda i: (i, 0))
    return pl.pallas_call(
        body, name="out_loss_bwd", grid=(lp // t,),
        in_specs=[row(512)] * 4 + [row(D_MODEL)] + _x_specs(t) + [_full((D_MODEL, D_MODEL))] + [_full((1, D_MODEL))] * 2,
        out_specs=[row(D_MODEL)] + [row(512)] * 4 + [pl.BlockSpec((N_PAIRS, t, 128), lambda i: (0, i, 0)),
                   _full((D_MODEL, D_MODEL)), _full((1, D_MODEL)), _full((1, D_MODEL)), _full((1, 1))],
        out_shape=[jax.ShapeDtypeStruct((lp, D_MODEL), _F32)] + [jax.ShapeDtypeStruct((lp, 512), _MM)] * 4
                  + [jax.ShapeDtypeStruct((N_PAIRS, lp, 128), _F32), jax.ShapeDtypeStruct((D_MODEL, D_MODEL), _F32),
                     jax.ShapeDtypeStruct((1, D_MODEL), _F32), jax.ShapeDtypeStruct((1, D_MODEL), _F32),
                     jax.ShapeDtypeStruct((1, 1), _F32)],
        compiler_params=_params(("arbitrary",)),
    )(o, ga, pw, gc, h, *([target] * nx), w_out, g, b)


BWD_WIDTH = 2


def _attn_bwd(q, k, v, do, lse, delta, ccol):
    lp = q.shape[0]
    t = _row_tile(lp)
    nb = lp // t

    def body(q_ref, k_ref, v_ref, do_ref, lse_ref, delta_ref, ccol_ref,
             dq_ref, dk_ref, dv_ref, dck_ref, dcq_ref,
             qa_ref, qat_ref, doa_ref, doat_ref, dqt_ref):
        j = pl.program_id(1)

        @pl.when(j == 0)
        def _():
            dqt_ref[...] = jnp.zeros_like(dqt_ref)
            for n in range(nb):
                rows = slice(n * t, (n + 1) * t)
                lse_pieces, delta_pieces = _split3(-lse_ref[0, rows, :]), _split3(-delta_ref[0, rows, :])
                for hd in range(2):
                    qa = _augment(q_ref[rows, :], hd, 1.0, lse_pieces)
                    doa = _augment(do_ref[rows, :], hd, delta_pieces, None)
                    qa_ref[hd, rows, :] = qa
                    qat_ref[hd, :, rows] = _transpose_mm(qa)
                    doa_ref[hd, rows, :] = doa
                    doat_ref[hd, :, rows] = _transpose_mm(doa)

        pieces = _key_pieces(ccol_ref[0], j * t)
        kas = [_key_operand(k_ref[...], pieces, hd) for hd in range(2)]
        kats = [_transpose_mm(ka) for ka in kas]
        vas = [_augment(v_ref[...], hd, 1.0, None) for hd in range(2)]

        def block(i, carry, diag, width=1):
            rows = pl.ds(pl.multiple_of(i * t, 128), width * t)
            new = []
            for hd in range(2):
                dk, dv = carry[hd]
                st = _dot(kas[hd], qat_ref[hd, :, rows])
                if diag:
                    st = jnp.where(_diag_valid(j, t), st, -MASKED)
                pt = jnp.exp(st)
                dsb = (pt * _dot(vas[hd], doat_ref[hd, :, rows])).astype(_MM)
                dv = dv + _dot(pt.astype(_MM), doa_ref[hd, rows, :])
                dk = dk + _dot(dsb, qa_ref[hd, rows, :])
                dqt_ref[hd, :, rows] += _dot(kats[hd], dsb)
                new.append((dk, dv))
            return tuple(new)

        zero = jnp.zeros((t, 128), _F32)
        carry = block(j, ((zero, zero), (zero, zero)), True)
        rest = nb - 1 - j
        if nb > BWD_WIDTH:
            carry = lax.fori_loop(0, rest // BWD_WIDTH,
                                  lambda n, c: block(j + 1 + BWD_WIDTH * n, c, False, BWD_WIDTH), carry)
            rest = rest % BWD_WIDTH
        (dk0, dv0), (dk1, dv1) = lax.fori_loop(nb - rest, nb, functools.partial(block, diag=False), carry)
        lane = lax.broadcasted_iota(jnp.int32, (1, 128), 1)
        dk_ref[...] = jnp.where(lane < HEAD_DIM, dk0, dk1).astype(dk_ref.dtype)
        dv_ref[...] = jnp.where(lane < HEAD_DIM, dv0, dv1).astype(dv_ref.dtype)
        dck_ref[0] = jnp.where(lane == 0, dk0[:, HEAD_DIM:HEAD_DIM + 1], jnp.where(lane == 1, dk1[:, 0:1], 0.0))

        @pl.when(j == nb - 1)
        def _():
            row = lax.broadcasted_iota(jnp.int32, (128, t), 0)
            sub = lax.broadcasted_iota(jnp.int32, (8, t), 0)
            for n in range(nb):
                cols = slice(n * t, (n + 1) * t)
                t0, t1 = dqt_ref[0, :, cols], dqt_ref[1, :, cols]
                dq_ref[cols, :] = jnp.where(row < HEAD_DIM, t0, t1).T * ATTN_SCALE
                dcq_ref[0, :, cols] = jnp.where(sub == 0, t0[HEAD_DIM + 3:HEAD_DIM + 4, :],
                                                jnp.where(sub == 1, t1[3:4, :], 0.0))

    blk = pl.BlockSpec((t, 128), lambda p, j: (j, p))
    res = pl.BlockSpec((lp, 128), lambda p, j: (0, p))
    stat = pl.BlockSpec((1, lp, 128), lambda p, j: (p, 0, 0))
    stat_blk = pl.BlockSpec((1, t, 128), lambda p, j: (p, j, 0))
    return pl.pallas_call(
        body, name="attn_bwd", grid=(N_PAIRS, nb),
        in_specs=[res, blk, blk, res, stat, stat, stat_blk],
        out_specs=[res, blk, blk, stat_blk, pl.BlockSpec((1, 8, lp), lambda p, j: (p, 0, 0))],
        out_shape=[jax.ShapeDtypeStruct((lp, D_ATTN), _F32), jax.ShapeDtypeStruct((lp, D_ATTN), _MM),
                   jax.ShapeDtypeStruct((lp, D_ATTN), _MM), jax.ShapeDtypeStruct((N_PAIRS, lp, 128), _F32),
                   jax.ShapeDtypeStruct((N_PAIRS, 8, lp), _F32)],
        scratch_shapes=[pltpu.VMEM((2, lp, 128), _MM), pltpu.VMEM((2, 128, lp), _MM),
                        pltpu.VMEM((2, lp, 128), _MM), pltpu.VMEM((2, 128, lp), _MM),
                        pltpu.VMEM((2, 128, lp), _F32)],
        compiler_params=_params(("arbitrary", "arbitrary")),
    )(q, k, v, do, lse, delta, ccol)


def _decay_bwd(dck, dcq, f, bf):
    lp = f.shape[0]
    nb = lp // 128

    def body(dck_ref, dcq_ref, f_ref, bf_ref, df_ref, dbf_ref):
        r = lax.broadcasted_iota(jnp.int32, (128, 128), 0)
        c = lax.broadcasted_iota(jnp.int32, (128, 128), 1)
        triu = (c >= r).astype(_F32)

        def step(n, carry):
            tail, dbf = carry
            bi = nb - 1 - n
            rows = pl.ds(pl.multiple_of(bi * 128, 128), 128)
            cols = jnp.zeros((128, 128), _F32)
            for p in range(N_PAIRS):
                dcq = jnp.concatenate([dcq_ref[p, :, rows], jnp.zeros((120, 128), _F32)], axis=0).T
                dcp = dcq - dck_ref[p, rows, :]
                cols = cols + (dcp if p == 0 else pltpu.roll(dcp, 2 * p, axis=1))
            lane = lax.broadcasted_iota(jnp.int32, (128, 128), 1)
            cols = jnp.where(lane < N_HEADS, cols, 0.0)
            rev = _dot_exact(triu, cols) + tail
            pos = bi * 128 + lax.broadcasted_iota(jnp.int32, (128, 1), 0)
            df = jnp.where(pos >= ROW_PAD, rev * (1.0 - _sigmoid(f_ref[rows, :] + bf_ref[...])), 0.0)
            df_ref[rows, :] = df.astype(df_ref.dtype)
            return rev[0:1, :], dbf + jnp.sum(df, axis=0, keepdims=True)

        zero = jnp.zeros((1, 128), _F32)
        _, dbf = lax.fori_loop(0, nb, step, (zero, zero), unroll=True)
        dbf_ref[...] = dbf

    return pl.pallas_call(
        body, name="decay_bwd",
        out_shape=[jax.ShapeDtypeStruct((lp, 128), _MM), jax.ShapeDtypeStruct((1, 128), _F32)],
        compiler_params=_params(),
    )(dck, dcq, f, bf)


def _conv_bwd_rows(dpw, z, g, b, wpw):
    lp = z.shape[0]
    t = _row_tile(lp)

    def body(dpw_ref, z_ref, g_ref, b_ref, w_ref, dz_ref, dw_ref, dg_ref, db_ref, dcb_ref):
        i = pl.program_id(0)

        @pl.when(i == 0)
        def _():
            dw_ref[...] = jnp.zeros_like(dw_ref)
            dg_ref[...] = jnp.zeros_like(dg_ref)
            db_ref[...] = jnp.zeros_like(db_ref)
            dcb_ref[...] = jnp.zeros_like(dcb_ref)

        xhat, rstd = _ln_stats(z_ref[...])
        zn = xhat * g_ref[...] + b_ref[...]
        sg = _sigmoid(zn)
        dpw = dpw_ref[...]
        dw_ref[...] += _dot_t0((zn * sg).astype(_MM), dpw)
        dzn = _dot_t1(dpw, w_ref[...]) * (sg * (1.0 + zn * (1.0 - sg)))
        dg_ref[...] += jnp.sum(dzn * xhat, axis=0, keepdims=True)
        db_ref[...] += jnp.sum(dzn, axis=0, keepdims=True)
        dz = _ln_bwd(dzn, xhat, rstd, g_ref[...])
        dz_ref[...] = dz
        dcb_ref[...] += jnp.sum(dz, axis=0, keepdims=True)

    row = pl.BlockSpec((t, D_CONV), lambda i: (i, 0))
    vec = _full((1, D_CONV))
    vshape = jax.ShapeDtypeStruct((1, D_CONV), _F32)
    return pl.pallas_call(
        body, name="conv_bwd_rows", grid=(lp // t,),
        in_specs=[row, row, vec, vec, _full((D_CONV, D_CONV))],
        out_specs=[row, _full((D_CONV, D_CONV)), vec, vec, vec],
        out_shape=[jax.ShapeDtypeStruct((lp, D_CONV), _F32), jax.ShapeDtypeStruct((D_CONV, D_CONV), _F32),
                   vshape, vshape, vshape],
        compiler_params=_params(("arbitrary",)),
    )(dpw, z, g, b, wpw)


def _conv_bwd_taps(dz, u, ug, cw):
    lp = u.shape[0]
    t = _row_tile(lp)
    nt = lp // t

    def body(dz_ref, u_ref, ug_ref, cw_ref, du_ref, dug_ref, dcw_ref, win, sh, wacc):
        i = pl.program_id(0)

        @pl.when(i == 0)
        def _():
            win[t:t + HALO, :] = jnp.zeros((HALO, D_CONV), _F32)
            wacc[...] = jnp.zeros_like(wacc)

        win[0:t, :] = dz_ref[...]
        _shifted_windows(win, sh, t + HALO)

        def chunk(c, carry):
            r0 = c * CONV_CHUNK
            rows = pl.ds(pl.multiple_of(r0, 8), CONV_CHUNK)
            u = u_ref[rows, :]
            sg = _sigmoid(ug_ref[rows, :])
            hh = u * sg
            dhh = jnp.zeros((CONV_CHUNK, D_CONV), _F32)
            for j in range(CONV_WIDTH):
                d = CONV_WIDTH - 1 - j
                s = _tap_rows(sh, [d], r0)[d]
                dhh = dhh + cw_ref[j:j + 1, :] * s
                prod = hh * s
                wacc[j] += sum(prod[8 * n:8 * (n + 1), :] for n in range(CONV_CHUNK // 8))
            du_ref[rows, :] = (dhh * sg).astype(du_ref.dtype)
            dug_ref[rows, :] = (dhh * u * sg * (1.0 - sg)).astype(dug_ref.dtype)
            return carry

        lax.fori_loop(0, t // CONV_CHUNK, chunk, 0)
        win[t:t + HALO, :] = win[0:HALO, :]

        @pl.when(i == nt - 1)
        def _():
            dcw_ref[...] = jnp.sum(wacc[...], axis=1)

    row = pl.BlockSpec((t, D_CONV), lambda i: (nt - 1 - i, 0))
    return pl.pallas_call(
        body, name="conv_bwd_taps", grid=(nt,),
        in_specs=[row, row, row, _full((32, D_CONV))],
        out_specs=[row, row, _full((32, D_CONV))],
        out_shape=[jax.ShapeDtypeStruct((lp, D_CONV), _MM)] * 2 + [jax.ShapeDtypeStruct((32, D_CONV), _F32)],
        scratch_shapes=[pltpu.VMEM((t + HALO, D_CONV), _F32), pltpu.VMEM((8, t + HALO, D_CONV), _F32),
                        pltpu.VMEM((32, 8, D_CONV), _F32)],
        compiler_params=_params(("arbitrary",)),
    )(dz, u, ug, cw)


def _proj_bwd_x(pieces, df, wpt, dr, x, head, g, parts=()):
    seq = x.shape[0]
    lp = ROW_X + seq
    t = _row_tile(lp)
    nt = lp // t
    nx = t // ROW_X
    n_parts = len(parts)

    def body(*refs):
        dq_ref, dk_ref, dv_ref, dga_ref, du_ref, dug_ref, dgc_ref, df_ref, wt_ref, dr_ref = refs[:10]
        x_refs, (head_ref, g_ref) = refs[10:10 + nx], refs[10 + nx:12 + nx]
        part_refs, refs = refs[12 + nx:12 + nx + n_parts], refs[12 + nx + n_parts:]
        gx_ref, dmeta_ref, dg_ref, db_ref = refs[:4]
        land_refs, (buf, sem), comm_sems = refs[4:4 + n_parts], refs[4 + n_parts:6 + n_parts], refs[6 + n_parts:]
        i = pl.program_id(0)

        @pl.when(i == 0)
        def _():
            dg_ref[...] = jnp.zeros_like(dg_ref)
            db_ref[...] = jnp.zeros_like(db_ref)
            if n_parts:
                for cp in _chip_exchange_copies(part_refs, land_refs, *comm_sems):
                    cp.start()

        dh = ALPHA * dr_ref[...]
        for n, ref in enumerate((dq_ref, dk_ref, dv_ref, dga_ref, du_ref, dug_ref, dgc_ref)):
            dh = dh + _dot(ref[...].astype(_MM), wt_ref[512 * n:512 * (n + 1), :])
        dh = dh + _dot(df_ref[...], wt_ref[3584:3712, :])
        rows = i * t + lax.broadcasted_iota(jnp.int32, (t, 1), 0)
        dh = jnp.where(rows >= ROW_PAD, dh, 0.0)
        xhat, rstd = _ln_stats(_x_tile(i, x_refs, head_ref))
        dg_ref[...] += jnp.sum(dh * xhat, axis=0, keepdims=True)
        db_ref[...] += jnp.sum(dh, axis=0, keepdims=True)
        dx = _ln_bwd(dh, xhat, rstd, g_ref[...])

        def first_copy():
            return pltpu.make_async_copy(buf.at[0, pl.ds(ROW_X, t - ROW_X), :], gx_ref.at[pl.ds(0, t - ROW_X), :], sem)

        def tile_copy(n):
            return pltpu.make_async_copy(buf.at[n % 2], gx_ref.at[pl.ds(pl.multiple_of(n * t - ROW_X, ROW_X), t), :], sem)

        pl.when(i == 1)(lambda: first_copy().wait())
        pl.when(i > 1)(lambda: tile_copy(i - 1).wait())
        buf[i % 2] = dx

        @pl.when(i == 0)
        def _():
            dmeta_ref[...] = dx[ROW_PAD:ROW_X, :]
            first_copy().start()

        pl.when(i > 0)(lambda: tile_copy(i).start())
        if nt == 1:
            first_copy().wait()
        else:
            pl.when(i == nt - 1)(lambda: tile_copy(i).wait())

        @pl.when(i == nt - 1)
        def _():
            if n_parts:
                for cp in _chip_exchange_copies(part_refs, land_refs, *comm_sems):
                    cp.wait()

    row = lambda w: pl.BlockSpec((t, w), lambda i: (i, 0))
    vec = _full((1, D_MODEL))
    any_spec = pl.BlockSpec(memory_space=pl.ANY)
    return pl.pallas_call(
        body, name="proj_bwd_x", grid=(nt,),
        in_specs=[row(512)] * 7 + [row(128), _full((D_IN_PAD, D_MODEL)), row(D_MODEL)] + _x_specs(t)
                 + [_full((ROW_X, D_MODEL)), vec] + [any_spec] * n_parts,
        out_specs=[any_spec, _full((N_META, D_MODEL)), vec, vec] + [any_spec] * n_parts,
        out_shape=[jax.ShapeDtypeStruct((seq, D_MODEL), _F32), jax.ShapeDtypeStruct((N_META, D_MODEL), _F32),
                   jax.ShapeDtypeStruct((1, D_MODEL), _F32), jax.ShapeDtypeStruct((1, D_MODEL), _F32)]
                  + [jax.ShapeDtypeStruct(p.shape, p.dtype) for p in parts],
        scratch_shapes=[pltpu.VMEM((2, t, D_MODEL), _F32), pltpu.SemaphoreType.DMA(())]
                       + [pltpu.SemaphoreType.DMA((3 * n_parts,))] * (2 if n_parts else 0),
        compiler_params=_params(("arbitrary",)),
    )(*pieces, df, wpt, dr, *([x] * nx), head, g, *parts)


def _proj_bwd_w(h, pieces, name):
    lp = h.shape[0]
    t = _row_tile(lp)
    nt = lp // t
    n = len(pieces)
    widths = [p.shape[1] for p in pieces]

    def body(*refs):
        h_ref, d_refs, w_refs, accs = refs[0], refs[1:1 + n], refs[1 + n:1 + 2 * n], refs[1 + 2 * n:]
        i = pl.program_id(0)

        @pl.when(i == 0)
        def _():
            for acc in accs:
                acc[...] = jnp.zeros_like(acc)

        hb = h_ref[...].astype(_MM)
        for d_ref, acc in zip(d_refs, accs):
            acc[...] += _dot_t0(d_ref[...].astype(_MM), hb)

        @pl.when(i == nt - 1)
        def _():
            for w_ref, acc in zip(w_refs, accs):
                w_ref[...] = acc[...].astype(w_ref.dtype)

    row = lambda w: pl.BlockSpec((t, w), lambda i: (i, 0))
    return pl.pallas_call(
        body, name=name, grid=(nt,),
        in_specs=[row(D_MODEL)] + [row(w) for w in widths],
        out_specs=[_full((w, D_MODEL)) for w in widths],
        out_shape=[jax.ShapeDtypeStruct((w, D_MODEL), _MM) for w in widths],
        scratch_shapes=[pltpu.VMEM((w, D_MODEL), _F32) for w in widths],
        compiler_params=_params(("arbitrary",)),
    )(h, *pieces)


def _adamw_math(w, g, m, v):
    m = ADAM_B1 * m + (1.0 - ADAM_B1) * g
    v = ADAM_B2 * v + (1.0 - ADAM_B2) * (g * g)
    m_hat = m / (1.0 - ADAM_B1 ** ADAM_STEP)
    v_hat = v / (1.0 - ADAM_B2 ** ADAM_STEP)
    delta = -ADAM_LR * (m_hat / (jnp.sqrt(v_hat) + ADAM_EPS) + ADAM_WD * w)
    return delta, m, v


def _adamw_tiled(w, g, m, v, rows, name):
    r, c = w.shape

    def body(w_ref, g_ref, m_ref, v_ref, go_ref, d_ref, nm_ref, nv_ref):
        g = g_ref[...]
        go_ref[...] = g
        d_ref[...], nm_ref[...], nv_ref[...] = _adamw_math(w_ref[...], g, m_ref[...], v_ref[...])

    spec = pl.BlockSpec((rows, c), lambda i: (i, 0))
    return pl.pallas_call(
        body, name=name, grid=(pl.cdiv(r, rows),), in_specs=[spec] * 4, out_specs=[spec] * 4,
        out_shape=[jax.ShapeDtypeStruct((r, c), _F32)] * 4,
        compiler_params=_params(("arbitrary",)),
    )(w, g, m, v)


def _adamw_small(ws, gs, ms, vs):
    n = len(ws)

    def body(*refs):
        ins, outs = refs[:4 * n], refs[4 * n:]
        for k in range(n):
            d, m, v = _adamw_math(ins[k][...], ins[n + k][...], ins[2 * n + k][...], ins[3 * n + k][...])
            outs[k][...], outs[n + k][...], outs[2 * n + k][...] = d, m, v

    shapes = [jax.ShapeDtypeStruct(w.shape, _F32) for w in ws]
    res = pl.pallas_call(body, name="adamw_small", out_shape=shapes * 3, compiler_params=_params())(*ws, *gs, *ms, *vs)
    return res[:n], res[n:2 * n], res[2 * n:]


def _place():
    x, y, c = lax.axis_index("x"), lax.axis_index("y"), lax.axis_index("c")
    chips = [(1 - x, y), (x, 1 - y), (1 - x, 1 - y)]
    return x, y, c, chips


def _half(ref, slot, c, rows):
    hr = rows // 2
    return ref.at[slot, pl.ds(pl.multiple_of(c * hr, 8), hr), :]


def _gather_phases(ins, outs, send, recv):
    n = len(ins)
    x, y, c, chips = _place()
    me = 2 * x + y
    sibling = (x, y, 1 - c)

    def copy(k, t, slot, half, to, src=None):
        rows = ins[t].shape[0]
        dst = _half(outs[t], slot, half, rows)
        return pltpu.make_async_remote_copy(
            src_ref=dst if src is None else src, dst_ref=dst,
            send_sem=send.at[k], recv_sem=recv.at[k], device_id=to, device_id_type=MESH)

    def first():
        cps = []
        for t in range(n):
            hr = ins[t].shape[0] // 2
            mine = ins[t].at[pl.ds(pl.multiple_of(c * hr, 8), hr), :]
            cps += [copy(6 * t + j, t, me, c, (*chip, c), src=mine) for j, chip in enumerate(chips)]
        return cps

    def passed():
        return [copy(6 * t + 3 + j, t, 2 * cx + cy, c, sibling) for t in range(n) for j, (cx, cy) in enumerate(chips)]

    def start():
        for cp in first():
            cp.start()

    def relay():
        fwd = passed()
        for t in range(n):
            for j, (cx, cy) in enumerate(chips):
                copy(6 * t + j, t, 2 * cx + cy, c, sibling).wait_recv()
                fwd[3 * t + j].start()

    def finish():
        for t in range(n):
            for j, (cx, cy) in enumerate(chips):
                copy(6 * t + 3 + j, t, 2 * cx + cy, 1 - c, sibling).wait_recv()
        for cp in first() + passed():
            cp.wait_send()

    return start, relay, finish


def _all_gather(shards):
    n = len(shards)

    def body(*refs):
        start, relay, finish = _gather_phases(refs[:n], refs[n:2 * n], *refs[2 * n:])
        start()
        relay()
        finish()

    any_spec = pl.BlockSpec(memory_space=pl.ANY)
    return pl.pallas_call(
        body, name="all_gather",
        in_specs=[any_spec] * n, out_specs=[any_spec] * n,
        out_shape=[jax.ShapeDtypeStruct((N_SHARD,) + s.shape, s.dtype) for s in shards],
        scratch_shapes=[pltpu.SemaphoreType.DMA((6 * n,)), pltpu.SemaphoreType.DMA((6 * n,))],
    )(*shards)


def _own_slot(gathered, own):
    me = 2 * lax.axis_index("x") + lax.axis_index("y")
    slot = lax.broadcasted_iota(jnp.int32, (N_SHARD,) + (1,) * own.ndim, 0)
    return jnp.where(slot == me, own[None], gathered)


def _sibling_exchange(bufs):
    n = len(bufs)

    def body(*refs):
        ins, got = refs[:n], refs[n:2 * n]
        send, recv = refs[2 * n:]
        x, y, c, _ = _place()
        cps = []
        for t in range(n):
            hr = ins[t].shape[0] // (2 * N_SHARD)
            for s in range(N_SHARD):
                k = N_SHARD * t + s
                cps.append(pltpu.make_async_remote_copy(
                    src_ref=ins[t].at[pl.ds(pl.multiple_of((2 * s + 1 - c) * hr, 32 // ins[t].dtype.itemsize), hr), :],
                    dst_ref=got[t].at[s],
                    send_sem=send.at[k], recv_sem=recv.at[k], device_id=(x, y, 1 - c), device_id_type=MESH))
        for cp in cps:
            cp.start()
        for cp in cps:
            cp.wait()

    any_spec = pl.BlockSpec(memory_space=pl.ANY)
    return pl.pallas_call(
        body, name="sibling_exchange",
        in_specs=[any_spec] * n, out_specs=[any_spec] * n,
        out_shape=[jax.ShapeDtypeStruct((N_SHARD, b.shape[0] // (2 * N_SHARD), b.shape[1]), b.dtype) for b in bufs],
        scratch_shapes=[pltpu.SemaphoreType.DMA((N_SHARD * n,))] * 2,
    )(*bufs)


def _half_sum(bufs, got, wire, c):
    n = len(bufs)

    def body(c_ref, *refs):
        ins, gots, outs = refs[:n], refs[n:2 * n], refs[2 * n:]
        for t in range(n):
            outs[t][0] = (ins[t][...].astype(_F32) + gots[t][0].astype(_F32)).astype(outs[t].dtype)

    slot = lambda g: pl.BlockSpec((1,) + tuple(g.shape[1:]), lambda s, c_ref: (s, 0, 0))
    grid_spec = pltpu.PrefetchScalarGridSpec(
        num_scalar_prefetch=1, grid=(N_SHARD,),
        in_specs=[pl.BlockSpec(tuple(g.shape[1:]), lambda s, c_ref: (2 * s + c_ref[0], 0)) for g in got]
                 + [slot(g) for g in got],
        out_specs=[slot(g) for g in got])
    return pl.pallas_call(
        body, name="half_sum", grid_spec=grid_spec,
        out_shape=[jax.ShapeDtypeStruct(g.shape, d) for g, d in zip(got, wire)],
        compiler_params=_params(("arbitrary",)),
    )(c, *bufs, *got)


def _chip_exchange_copies(ins, outs, send, recv):
    x, y, c, chips = _place()
    me = 2 * x + y
    return [pltpu.make_async_remote_copy(
        src_ref=ins[t].at[2 * cx + cy], dst_ref=outs[t].at[me], send_sem=send.at[3 * t + j], recv_sem=recv.at[3 * t + j],
        device_id=(cx, cy, c), device_id_type=MESH) for t in range(len(ins)) for j, (cx, cy) in enumerate(chips)]


def _chip_sum_share(parts, got, me):
    n = len(got)

    def body(me_ref, *refs):
        owns, gots, outs = refs[:n], refs[n:2 * n], refs[2 * n:3 * n]
        accs, (local, send, recv) = refs[3 * n:4 * n], refs[4 * n:]
        s = pl.program_id(0)
        x, y, c, _ = _place()
        for t in range(n):
            term = jnp.where(s == me_ref[0], owns[t][0], gots[t][0]).astype(_F32)

            @pl.when(s == 0)
            def _():
                accs[t][...] = term

            @pl.when(s > 0)
            def _():
                accs[t][...] += term

        @pl.when(s == N_SHARD - 1)
        def _():
            cps = []
            for t in range(n):
                hr = accs[t].shape[0]
                mine = outs[t].at[pl.ds(pl.multiple_of(c * hr, 8), hr), :]
                cps.append(pltpu.make_async_copy(accs[t], mine, local.at[t]))
                cps.append(pltpu.make_async_remote_copy(
                    src_ref=accs[t], dst_ref=mine, send_sem=send.at[t], recv_sem=recv.at[t],
                    device_id=(x, y, 1 - c), device_id_type=MESH))
            for cp in cps:
                cp.start()
            for cp in cps:
                cp.wait()

    blk = lambda g: (1,) + tuple(g.shape[1:])
    grid_spec = pltpu.PrefetchScalarGridSpec(
        num_scalar_prefetch=1, grid=(N_SHARD,),
        in_specs=[pl.BlockSpec(blk(g), lambda s, me_ref: (me_ref[0], 0, 0)) for g in got]
                 + [pl.BlockSpec(blk(g), lambda s, me_ref: (jnp.where(s == me_ref[0], (s + 1) % N_SHARD, s), 0, 0))
                    for g in got],
        out_specs=[pl.BlockSpec(memory_space=pl.ANY)] * n,
        scratch_shapes=[pltpu.VMEM(tuple(g.shape[1:]), _F32) for g in got] + [pltpu.SemaphoreType.DMA((n,))] * 3)
    return pl.pallas_call(
        body, name="chip_sum_share", grid_spec=grid_spec,
        out_shape=[jax.ShapeDtypeStruct((2 * g.shape[1], g.shape[2]), _F32) for g in got],
        compiler_params=_params(("arbitrary",)),
    )(me, *parts, *got)


def _as_operand(index):
    return jnp.asarray(index, jnp.int32).reshape(1)


def _chip_partials(bufs, wire):
    return _half_sum(bufs, _sibling_exchange(bufs), wire, _as_operand(lax.axis_index("c")))


def _finish_reduction(parts, landed):
    return _chip_sum_share(parts, landed, _as_operand(2 * lax.axis_index("x") + lax.axis_index("y")))


def _reduce_small(buf):
    rows = buf.shape[0] // N_SHARD
    hr = rows // 2

    def body(in_ref, out_ref, got1, part, got2, send, recv):
        x, y, c, chips = _place()
        me = 2 * x + y
        sibling = (x, y, 1 - c)

        def half(s, which):
            return in_ref.at[pl.ds(pl.multiple_of((2 * s + which) * hr, 8), hr), :]

        first = [pltpu.make_async_remote_copy(src_ref=half(s, 1 - c), dst_ref=got1.at[s], send_sem=send.at[s],
                                              recv_sem=recv.at[s], device_id=sibling, device_id_type=MESH)
                 for s in range(N_SHARD)]
        for cp in first:
            cp.start()
        for s in range(N_SHARD):
            first[s].wait()
            part[s] = half(s, c)[...] + got1[s]
        got2[me] = part[me]
        second = [pltpu.make_async_remote_copy(src_ref=part.at[2 * cx + cy], dst_ref=got2.at[me],
                                               send_sem=send.at[N_SHARD + j], recv_sem=recv.at[N_SHARD + j],
                                               device_id=(cx, cy, c), device_id_type=MESH)
                  for j, (cx, cy) in enumerate(chips)]
        for cp in second:
            cp.start()
        for cp in second:
            cp.wait()
        mine = ((got2[0] + got2[1]) + got2[2]) + got2[3]
        out_ref[pl.ds(pl.multiple_of(c * hr, 8), hr), :] = mine
        last = pltpu.make_async_remote_copy(
            src_ref=out_ref.at[pl.ds(pl.multiple_of(c * hr, 8), hr), :],
            dst_ref=out_ref.at[pl.ds(pl.multiple_of(c * hr, 8), hr), :],
            send_sem=send.at[2 * N_SHARD - 1], recv_sem=recv.at[2 * N_SHARD - 1], device_id=sibling, device_id_type=MESH)
        last.start()
        last.wait()

    vmem = pl.BlockSpec(memory_space=pltpu.VMEM)
    return pl.pallas_call(
        body, name="reduce_small", in_specs=[vmem], out_specs=vmem,
        out_shape=jax.ShapeDtypeStruct((rows, buf.shape[1]), buf.dtype),
        scratch_shapes=[pltpu.VMEM((N_SHARD, hr, buf.shape[1]), buf.dtype)] * 3
                       + [pltpu.SemaphoreType.DMA((2 * N_SHARD,))] * 2,
    )(buf)


REPL_SIZES = (D_MODEL, D_MODEL, N_HEADS, D_CONV, D_CONV, D_CONV, D_MODEL, D_MODEL)
REPL_ROWS = 48


def _pack_repl(parts):
    flat = jnp.concatenate([p.reshape(-1).astype(_F32) for p in parts])
    return jnp.pad(flat, (0, REPL_ROWS * 128 - flat.shape[0])).reshape(REPL_ROWS, 128)


def _unpack_repl(packed, shapes):
    flat = packed.reshape(-1)
    out, off = [], 0
    for shape, size in zip(shapes, REPL_SIZES):
        out.append(flat[off:off + size].reshape(shape))
        off += size
    return out


W_IN_PIECES = (("q", 0, 512), ("k", 512, 1024), ("v", 1024, 1536), ("f", 1536, 1544), ("ga", 1544, 2056),
               ("u", 2056, 2568), ("ug", 2568, 3080), ("gc", 3080, 3592))


def _projection_operand(slots):
    w_in_t = slots[:, :W_IN_SHARD, :].reshape(D_IN, D_MODEL)
    zrow = jnp.zeros((128 - N_HEADS, D_MODEL), w_in_t.dtype)
    return jnp.concatenate([w_in_t[:OFF_F], w_in_t[OFF_F + N_HEADS:], w_in_t[OFF_F:OFF_F + N_HEADS], zrow], axis=0)


def _gradient_slots(pieces):
    dw = jnp.concatenate([pieces[name][:b - a] for name, a, b in W_IN_PIECES], axis=0)
    dw = jnp.pad(dw.reshape(N_SHARD, W_IN_SHARD, D_MODEL), ((0, 0), (0, W_IN_SLOT - W_IN_SHARD), (0, 0)))
    return dw.reshape(N_SHARD * W_IN_SLOT, D_MODEL)


def _local_grads(x, target, meta, ln_in_g, ln_in_b, w_in_slots, b_f, conv_w, conv_b, ln_conv_g, ln_conv_b,
                 w_pw, w_out, ln_out_g, ln_out_b, chip_partials=None, late_shards=None):
    vec = lambda a: a.reshape(1, -1).astype(_F32)
    head = jnp.concatenate([jnp.zeros((ROW_PAD, D_MODEL), _F32), meta], axis=0)
    wpt = _projection_operand(w_in_slots)
    bf = jnp.pad(vec(b_f), ((0, 0), (0, 128 - N_HEADS)))
    cw = jnp.pad(conv_w, ((0, 32 - CONV_WIDTH), (0, 0)))

    h, q, k, v, ga, u, ug, gc, f, vt, *late = _ln_proj(x, head, vec(ln_in_g), vec(ln_in_b), wpt, late_shards or ())
    if late_shards:
        w_pw, w_out = (_own_slot(gathered, own).reshape(N_SHARD * own.shape[0], own.shape[1])
                       for gathered, own in zip(late, late_shards))
    ccol = _decay_cumsum(f, bf)
    o, lse = _attn_fwd(q, k, vt, ccol)
    z, pw = _conv_fwd(u, ug, cw, vec(conv_b), vec(ln_conv_g), vec(ln_conv_b), w_pw)
    dr, do, dga, dpw, dgc, delta, dw_out, dg_out, db_out, loss = _out_loss_bwd(
        o, ga, pw, gc, h, target, w_out, vec(ln_out_g), vec(ln_out_b))
    dq, dk, dv, dck, dcq = _attn_bwd(q, k, v, do, lse, delta, ccol)
    df, dbf = _decay_bwd(dck, dcq, f, bf)
    dz, dw_pw, dg_conv, db_conv, dconv_b = _conv_bwd_rows(dpw, z, vec(ln_conv_g), vec(ln_conv_b), w_pw)
    du, dug, dcw = _conv_bwd_taps(dz, u, ug, cw)
    pieces = (dq, dk, dv, dga, du, dug, dgc)
    dwq, dwk, dwv, dwga = _proj_bwd_w(h, pieces[:4], "proj_bwd_w_attn")
    dwu, dwug, dwgc, dwf = _proj_bwd_w(h, pieces[4:] + (df,), "proj_bwd_w_conv")
    dw_in_t = _gradient_slots(dict(q=dwq, k=dwk, v=dwv, f=dwf, ga=dwga, u=dwu, ug=dwug, gc=dwgc))
    parts = chip_partials([dw_in_t, dw_pw, dw_out]) if chip_partials else []
    grad_x, dmeta, dg_in, db_in, *landed = _proj_bwd_x(pieces, df, wpt, dr, x, head, vec(ln_in_g), parts)
    grads = dict(meta=dmeta, ln_in_g=dg_in, ln_in_b=db_in, w_in_t=dw_in_t, b_f=dbf[:, :N_HEADS],
                 conv_w=dcw[:CONV_WIDTH], conv_b=dconv_b, ln_conv_g=dg_conv, ln_conv_b=db_conv,
                 w_pw=dw_pw, w_out=dw_out, ln_out_g=dg_out, ln_out_b=db_out, reduced=(parts, landed))
    return loss, grad_x, grads


def kernel(x, meta, ln_in_g, ln_in_b, w_in, b_f, conv_w, conv_b, ln_conv_g, ln_conv_b, w_pw, w_out, ln_out_g, ln_out_b, loss_target, m_meta, m_ln_in_g, m_ln_in_b, m_w_in, m_b_f, m_conv_w, m_conv_b, m_ln_conv_g, m_ln_conv_b, m_w_pw, m_w_out, m_ln_out_g, m_ln_out_b, v_meta, v_ln_in_g, v_ln_in_b, v_w_in, v_b_f, v_conv_w, v_conv_b, v_ln_conv_g, v_ln_conv_b, v_w_pw, v_w_out, v_ln_out_g, v_ln_out_b):
    conv_w2, w_pw2, w_out2 = conv_w[0], w_pw[0], w_out[0]
    w_t = jnp.transpose(w_in[0])

    small = jnp.concatenate([jnp.pad(conv_w2, ((0, 1), (0, 0))), meta.reshape(32, 128)], axis=0)
    shards = [jnp.pad(w_t.astype(_MM), ((0, W_IN_SLOT - W_IN_SHARD), (0, 0))), small]
    g_in, g_small = (_own_slot(g, s) for g, s in zip(_all_gather(shards), shards))
    conv_w_full = jnp.transpose(g_small[:, :CONV_WIDTH, :], (1, 0, 2)).reshape(CONV_WIDTH, D_CONV)
    meta_full = jnp.transpose(g_small[:, 32:64, :].reshape(N_SHARD, N_META, 256), (1, 0, 2)).reshape(N_META, D_MODEL)

    loss, grad_x, gr = _local_grads(x[0], loss_target[0], meta_full, ln_in_g, ln_in_b, g_in, b_f, conv_w_full,
                                    conv_b, ln_conv_g, ln_conv_b, None, None, ln_out_g, ln_out_b,
                                    chip_partials=lambda big: _chip_partials([b.astype(_MM) for b in big], [_MM] * 3),
                                    late_shards=[w_pw2.astype(_MM), w_out2.astype(_MM)])
    grad_x = grad_x[None]
    gw_slot, gw_pw, gw_out = _finish_reduction(*gr["reduced"])

    repl_names = ("ln_in_g", "ln_in_b", "b_f", "conv_b", "ln_conv_g", "ln_conv_b", "ln_out_g", "ln_out_b")
    repl = _pack_repl([gr[n] for n in repl_names] + [loss])
    gcw = jnp.transpose(jnp.pad(gr["conv_w"], ((0, 1), (0, 0))).reshape(32, N_SHARD, 128), (1, 0, 2))
    gmeta = jnp.transpose(gr["meta"].reshape(N_META, N_SHARD, 256), (1, 0, 2)).reshape(N_SHARD, 32, 128)
    small_g = jnp.concatenate([gcw, gmeta, jnp.broadcast_to(repl[None], (N_SHARD, REPL_ROWS, 128))], axis=1)
    small_rows = small_g.shape[1]
    gsmall = _reduce_small(small_g.reshape(N_SHARD * small_rows, 128))
    g_conv_w = gsmall[:CONV_WIDTH]
    g_meta = gsmall[32:64].reshape(N_META, 256)
    g_repl = gsmall[64:64 + REPL_ROWS]

    repl_w = (ln_in_g, ln_in_b, b_f, conv_b, ln_conv_g, ln_conv_b, ln_out_g, ln_out_b)
    repl_m = (m_ln_in_g, m_ln_in_b, m_b_f, m_conv_b, m_ln_conv_g, m_ln_conv_b, m_ln_out_g, m_ln_out_b)
    repl_v = (v_ln_in_g, v_ln_in_b, v_b_f, v_conv_b, v_ln_conv_g, v_ln_conv_b, v_ln_out_g, v_ln_out_b)
    zero = jnp.zeros((1,), _F32)
    res_t = _adamw_tiled(w_t, gw_slot, jnp.transpose(m_w_in[0]), jnp.transpose(v_w_in[0]), 256, "adamw_w_in")
    gw_in, d_in, nm_in, nv_in = (jnp.transpose(a) for a in res_t)
    ws = [w_pw2, w_out2, conv_w2, meta, _pack_repl(repl_w + (zero,))]
    gs = [gw_pw, gw_out, g_conv_w, g_meta, g_repl]
    ms = [m_w_pw[0], m_w_out[0], m_conv_w[0], m_meta, _pack_repl(repl_m + (zero,))]
    vs = [v_w_pw[0], v_w_out[0], v_conv_w[0], v_meta, _pack_repl(repl_v + (zero + 1.0,))]
    ds, nms, nvs = _adamw_small(ws, gs, ms, vs)

    shapes = [w.shape for w in repl_w]
    total_loss = g_repl.reshape(-1)[sum(REPL_SIZES)]
    names = ("meta", "ln_in_g", "ln_in_b", "w_in", "b_f", "conv_w", "conv_b", "ln_conv_g", "ln_conv_b",
             "w_pw", "w_out", "ln_out_g", "ln_out_b")

    def assemble(t_in, smalls):
        pw_, out_, cw_, meta_, packed = smalls
        r = dict(zip(repl_names, _unpack_repl(packed, shapes)))
        r.update(meta=meta_, w_in=t_in[None], conv_w=cw_[None], w_pw=pw_[None], w_out=out_[None])
        return [r[n] for n in names]

    return (total_loss, grad_x, *assemble(gw_in, gs), *assemble(d_in, ds), *assemble(nm_in, nms),
            *assemble(nv_in, nvs))
```

```python
import functools

import jax
import jax.numpy as jnp
from jax import lax
from jax.experimental import pallas as pl
from jax.experimental.pallas import tpu as pltpu

D_MODEL = 1024
N_META = 16
D_ATTN = 512
D_CONV = 512
N_HEADS = 8
HEAD_DIM = 64
N_PAIRS = 4
CONV_WIDTH = 31
LN_EPS = 1e-5
ALPHA = 2.0 ** 0.25
ATTN_SCALE = 0.125
D_IN = 3592
OFF_F = 3 * D_ATTN
D_IN_PAD = 3712
N_SHARD = 4
W_IN_SHARD = D_IN // N_SHARD
W_IN_SLOT = 928

ROW_PAD = 112
ROW_X = ROW_PAD + N_META
HALO = 32
MASKED = 1e30

ADAM_LR = 0.001
ADAM_B1 = 0.9
ADAM_B2 = 0.999
ADAM_EPS = 1e-08
ADAM_WD = 0.01
ADAM_STEP = 10

VMEM_LIMIT = 56 * 1024 * 1024

_MM = jnp.bfloat16
_F32 = jnp.float32
MESH = pl.DeviceIdType.MESH


def _params(sem=None):
    return pltpu.CompilerParams(dimension_semantics=sem, vmem_limit_bytes=VMEM_LIMIT)


def _row_tile(n_rows):
    for t in (384, 256, 128):
        if n_rows % t == 0:
            return t
    raise ValueError(f"padded sequence length {n_rows} is not a multiple of 128")


def _sigmoid(x):
    return 1.0 / (1.0 + jnp.exp(-x))


def _ln_stats(x):
    mu = jnp.mean(x, axis=-1, keepdims=True)
    xc = x - mu
    var = jnp.mean(xc * xc, axis=-1, keepdims=True)
    rstd = lax.rsqrt(var + LN_EPS)
    return xc * rstd, rstd


def _ln_bwd(dy, xhat, rstd, g):
    dxh = dy * g
    m1 = jnp.mean(dxh, axis=-1, keepdims=True)
    m2 = jnp.mean(dxh * xhat, axis=-1, keepdims=True)
    return rstd * (dxh - m1 - xhat * m2)


def _dot(a, b):
    return jnp.dot(a, b, preferred_element_type=_F32)


def _dot_t0(a, b):
    return lax.dot_general(a, b, (((0,), (0,)), ((), ())), preferred_element_type=_F32)


def _dot_t1(a, b):
    return lax.dot_general(a, b, (((1,), (1,)), ((), ())), preferred_element_type=_F32)


def _dot_exact(a, b):
    return jnp.dot(a, b, preferred_element_type=_F32, precision=lax.Precision.HIGHEST)


def _full(shape):
    return pl.BlockSpec(shape, lambda *_: (0,) * len(shape))


def _x_specs(t):
    n = t // ROW_X
    return [pl.BlockSpec((ROW_X, D_MODEL), lambda i, s=s: (jnp.maximum(i * n + s - 1, 0), 0)) for s in range(n)]


def _x_tile(i, refs, head_ref=None):
    first = refs[0][...]
    if head_ref is not None:
        first = jnp.where(i == 0, head_ref[...], first)
    return jnp.concatenate([first] + [r[...] for r in refs[1:]], axis=0)


def _ln_proj(x, head, g, b, wpt, gather=()):
    lp = ROW_X + x.shape[0]
    t = _row_tile(lp)
    nt = lp // t
    nx = t // ROW_X
    ng = len(gather)

    def body(*refs):
        x_refs, (head_ref, g_ref, b_ref, wt_ref) = refs[:nx], refs[nx:nx + 4]
        shard_refs, refs = refs[nx + 4:nx + 4 + ng], refs[nx + 4 + ng:]
        h_ref, q_ref, k_ref, v_ref, ga_ref, u_ref, ug_ref, gc_ref, f_ref, vt_ref = refs[:10]
        gathered_refs, comm_sems = refs[10:10 + ng], refs[10 + ng:]
        i = pl.program_id(0)
        if ng:
            start, relay, finish = _gather_phases(shard_refs, gathered_refs, *comm_sems)
            pl.when(i == 0)(start)
            pl.when(i == nt // 2)(relay)
        xhat, _ = _ln_stats(_x_tile(i, x_refs, head_ref))
        rows = i * t + lax.broadcasted_iota(jnp.int32, (t, 1), 0)
        h = jnp.where(rows >= ROW_PAD, xhat * g_ref[...] + b_ref[...], 0.0)
        h_ref[...] = h
        hb = h.astype(_MM)
        q_ref[...] = (_dot_t1(hb, wt_ref[0:512, :]) * ATTN_SCALE).astype(q_ref.dtype)
        k_ref[...] = _dot_t1(hb, wt_ref[512:1024, :]).astype(k_ref.dtype)
        v = _dot_t1(hb, wt_ref[1024:1536, :])
        v_ref[...] = v.astype(v_ref.dtype)
        vt_ref[...] = v.T.astype(vt_ref.dtype)
        ga_ref[...] = _dot_t1(hb, wt_ref[1536:2048, :])
        u_ref[...] = _dot_t1(hb, wt_ref[2048:2560, :])
        ug_ref[...] = _dot_t1(hb, wt_ref[2560:3072, :])
        gc_ref[...] = _dot_t1(hb, wt_ref[3072:3584, :])
        f_ref[...] = _dot_t1(hb, wt_ref[3584:3712, :])
        if ng:
            pl.when(i == nt - 1)(finish)

    row = lambda w: pl.BlockSpec((t, w), lambda i: (i, 0))
    f32 = lambda w: jax.ShapeDtypeStruct((lp, w), _F32)
    mm = lambda w: jax.ShapeDtypeStruct((lp, w), _MM)
    any_spec = pl.BlockSpec(memory_space=pl.ANY)
    return pl.pallas_call(
        body, name="ln_proj", grid=(nt,),
        in_specs=_x_specs(t) + [_full((ROW_X, D_MODEL)), _full((1, D_MODEL)), _full((1, D_MODEL)),
                                _full((D_IN_PAD, D_MODEL))] + [any_spec] * ng,
        out_specs=[row(D_MODEL)] + [row(512)] * 7 + [row(128), pl.BlockSpec((512, t), lambda i: (0, i))]
                  + [any_spec] * ng,
        out_shape=[f32(D_MODEL), mm(512), mm(512), mm(512), f32(512), f32(512), f32(512), f32(512), f32(128),
                   jax.ShapeDtypeStruct((512, lp), _MM)]
                  + [jax.ShapeDtypeStruct((N_SHARD,) + s.shape, s.dtype) for s in gather],
        scratch_shapes=[pltpu.SemaphoreType.DMA((6 * ng,))] * (2 if ng else 0),
        compiler_params=_params(("arbitrary",)),
    )(*([x] * nx), head, g, b, wpt, *gather)


def _log_sigmoid(z):
    return jnp.minimum(z, 0.0) - jnp.log(1.0 + jnp.exp(-jnp.abs(z)))


def _decay_cumsum(f, bf):
    lp = f.shape[0]
    t = _row_tile(lp)

    def body(f_ref, bf_ref, ccol_ref):
        r = lax.broadcasted_iota(jnp.int32, (t, t), 0)
        c = lax.broadcasted_iota(jnp.int32, (t, t), 1)
        tril = (c <= r).astype(_F32)

        def step(bi, carry):
            rows = pl.ds(pl.multiple_of(bi * t, 128), t)
            lf = _log_sigmoid(f_ref[rows, :] + bf_ref[...])
            cb = _dot_exact(tril, lf) + carry
            for p in range(N_PAIRS):
                ccol_ref[p, rows, :] = cb if p == 0 else pltpu.roll(cb, 128 - 2 * p, axis=1)
            return cb[t - 1:t, :]

        lax.fori_loop(0, lp // t, step, jnp.zeros((1, 128), _F32))

    return pl.pallas_call(
        body, name="decay_cumsum",
        out_shape=jax.ShapeDtypeStruct((N_PAIRS, lp, 128), _F32),
        compiler_params=_params(),
    )(f, bf)


def _split3(x):
    hi = x.astype(_MM)
    r1 = x - hi.astype(_F32)
    mid = r1.astype(_MM)
    return hi, mid, (r1 - mid.astype(_F32)).astype(_MM)


def _augment(x, hd, first, second, transposed=False):
    lane = lax.broadcasted_iota(jnp.int32, (1, 128), 1)
    base = HEAD_DIM * (1 - hd)
    out = jnp.where(lane // HEAD_DIM == hd, x, jnp.zeros_like(x))
    for slot, val in enumerate((first, second)):
        lo = base + 3 * slot
        if val is None:
            continue
        if isinstance(val, float):
            out = jnp.where((lane >= lo) & (lane < lo + 3), jnp.asarray(val, x.dtype), out)
        else:
            for n, piece in enumerate(val):
                out = jnp.where(lane == lo + n, piece[:, hd:hd + 1], out)
    return _transpose_mm(out) if transposed else out


def _transpose_mm(x):
    return x.astype(_F32).T.astype(_MM)


def _diag_valid(i, t):
    sub = lax.broadcasted_iota(jnp.int32, (t, t), 0)
    return (sub <= lax.broadcasted_iota(jnp.int32, (t, t), 1)) & (i * t + sub >= ROW_PAD)


def _key_pieces(c, first_row):
    pos = first_row + lax.broadcasted_iota(jnp.int32, (c.shape[0], 1), 0)
    return _split3(jnp.where(pos >= ROW_PAD, -c, -MASKED))


def _key_operand(k, pieces, hd, transposed=False):
    return _augment(k, hd, pieces, 1.0, transposed)


def _attn_fwd(q, k, vt, ccol):
    lp = q.shape[0]
    t = _row_tile(lp)
    nb = lp // t

    def body(q_ref, k_ref, vt_ref, ccol_ref, o_ref, lse_ref, ka_ref):
        i = pl.program_id(1)

        @pl.when(i == 0)
        def _():
            for n in range(nb):
                rows = slice(n * t, (n + 1) * t)
                pieces = _key_pieces(ccol_ref[0, rows, :], n * t)
                for hd in range(2):
                    ka_ref[hd, rows, :] = _key_operand(k_ref[rows, :], pieces, hd)

        qats = [_augment(q_ref[...], hd, 1.0, None, transposed=True) for hd in range(2)]

        def scores(j):
            keys = pl.ds(pl.multiple_of(j * t, 128), t)
            return tuple(_dot(ka_ref[hd, keys, :], qats[hd]) for hd in range(2))

        def update(j, sts, states, diag):
            vt = vt_ref[:, pl.ds(pl.multiple_of(j * t, 128), t)]
            new = []
            for st, (m, l, acc) in zip(sts, states):
                if diag:
                    st = jnp.where(_diag_valid(i, t), st, -MASKED)
                m_new = jnp.maximum(m, jnp.max(st, axis=0, keepdims=True))
                a = jnp.exp(m - m_new)
                pt = jnp.exp(st - m_new)
                l = a * l + jnp.sum(pt, axis=0, keepdims=True)
                new.append((m_new, l, a * acc + _dot(vt, pt.astype(_MM))))
            return tuple(new)

        def step(j, carry):
            sts, states = carry
            return scores(j + 1), update(j, sts, states, False)

        def step2(n, carry):
            sts, states = carry
            nxt, last = scores(2 * n + 1), scores(2 * n + 2)
            return last, update(2 * n + 1, nxt, update(2 * n, sts, states, False), False)

        init = (jnp.full((1, t), -MASKED, _F32), jnp.zeros((1, t), _F32), jnp.zeros((128, t), _F32))
        def pair(n, states):
            first, second = scores(2 * n), scores(2 * n + 1)
            return update(2 * n + 1, second, update(2 * n, first, states, False), False)

        states = lax.fori_loop(0, i // 2, pair, (init, init))
        states = lax.fori_loop(0, i % 2, lambda _, s: update(i - 1, scores(i - 1), s, False), states)
        (m0, l0, acc0), (m1, l1, acc1) = update(i, scores(i), states, True)
        row = lax.broadcasted_iota(jnp.int32, (128, t), 0)
        o_ref[...] = jnp.where(row < HEAD_DIM, acc0 / l0, acc1 / l1).T
        lse_ref[0] = jnp.where(row == 0, m0 + jnp.log(l0), jnp.where(row == 1, m1 + jnp.log(l1), 0.0)).T

    return pl.pallas_call(
        body, name="attn_fwd", grid=(N_PAIRS, nb),
        in_specs=[pl.BlockSpec((t, 128), lambda p, i: (i, p)),
                  pl.BlockSpec((lp, 128), lambda p, i: (0, p)),
                  pl.BlockSpec((128, lp), lambda p, i: (p, 0)),
                  pl.BlockSpec((1, lp, 128), lambda p, i: (p, 0, 0))],
        out_specs=[pl.BlockSpec((t, 128), lambda p, i: (i, p)),
                   pl.BlockSpec((1, t, 128), lambda p, i: (p, i, 0))],
        out_shape=[jax.ShapeDtypeStruct((lp, D_ATTN), _F32), jax.ShapeDtypeStruct((N_PAIRS, lp, 128), _F32)],
        scratch_shapes=[pltpu.VMEM((2, lp, 128), _MM)],
        compiler_params=_params(("arbitrary", "arbitrary")),
    )(q, k, vt, ccol)


CONV_CHUNK = 32


def _shifted_windows(win, sh, rows):
    sh[0] = win[...]
    for b in range(1, 8):
        sh[b, 0:rows - 8, :] = win[b:b + rows - 8, :]


def _tap_rows(sh, offsets, r0):
    out = {}
    for b in range(8):
        ds = [d for d in offsets if d % 8 == b]
        if ds:
            lo, hi = min(ds) // 8, max(ds) // 8
            rows = sh[b, pl.ds(pl.multiple_of(r0 + 8 * lo, 8), CONV_CHUNK + 8 * (hi - lo)), :]
            for d in ds:
                out[d] = rows[8 * (d // 8 - lo):8 * (d // 8 - lo) + CONV_CHUNK, :]
    return out


def _conv_fwd(u, ug, cw, cb, g, b, wpw):
    lp = u.shape[0]
    t = _row_tile(lp)

    def body(u_ref, ug_ref, cw_ref, cb_ref, g_ref, b_ref, wpw_ref, z_ref, pw_ref, win, sh):
        i = pl.program_id(0)

        @pl.when(i == 0)
        def _():
            win[0:HALO, :] = jnp.zeros((HALO, D_CONV), _F32)

        win[HALO:HALO + t, :] = u_ref[...] * _sigmoid(ug_ref[...])
        _shifted_windows(win, sh, t + HALO)

        def chunk(c, carry):
            r0 = c * CONV_CHUNK
            acc = jnp.broadcast_to(cb_ref[...], (CONV_CHUNK, D_CONV))
            first = HALO - (CONV_WIDTH - 1)
            taps = _tap_rows(sh, range(first, first + CONV_WIDTH), r0)
            for j in range(CONV_WIDTH):
                acc = acc + cw_ref[j:j + 1, :] * taps[first + j]
            z_ref[pl.ds(pl.multiple_of(r0, 8), CONV_CHUNK), :] = acc
            return carry

        lax.fori_loop(0, t // CONV_CHUNK, chunk, 0)
        xhat, _ = _ln_stats(z_ref[...])
        zn = xhat * g_ref[...] + b_ref[...]
        a = zn * _sigmoid(zn)
        pw_ref[...] = _dot(a.astype(_MM), wpw_ref[...])
        win[0:HALO, :] = win[t:t + HALO, :]

    row = pl.BlockSpec((t, D_CONV), lambda i: (i, 0))
    vec = _full((1, D_CONV))
    return pl.pallas_call(
        body, name="conv_fwd", grid=(lp // t,),
        in_specs=[row, row, _full((32, D_CONV)), vec, vec, vec, _full((D_CONV, D_CONV))],
        out_specs=[row, row],
        out_shape=[jax.ShapeDtypeStruct((lp, D_CONV), _F32)] * 2,
        scratch_shapes=[pltpu.VMEM((t + HALO, D_CONV), _F32), pltpu.VMEM((8, t + HALO, D_CONV), _F32)],
        compiler_params=_params(("arbitrary",)),
    )(u, ug, cw, cb, g, b, wpw)


def _out_loss_bwd(o, ga, pw, gc, h, target, w_out, g, b):
    lp = o.shape[0]
    t = _row_tile(lp)
    nx = t // ROW_X

    def body(*refs):
        o_ref, ga_ref, pw_ref, gc_ref, h_ref = refs[:5]
        tgt_refs, (w_ref, g_ref, b_ref) = refs[5:5 + nx], refs[5 + nx:8 + nx]
        dr_ref, do_ref, dga_ref, dpw_ref, dgc_ref, delta_ref, dw_ref, dg_ref, db_ref, loss_ref = refs[8 + nx:]
        i = pl.program_id(0)

        @pl.when(i == 0)
        def _():
            dw_ref[...] = jnp.zeros_like(dw_ref)
            dg_ref[...] = jnp.zeros_like(dg_ref)
            db_ref[...] = jnp.zeros_like(db_ref)
            loss_ref[...] = jnp.zeros_like(loss_ref)

        o, ga, pw, gc = o_ref[...], ga_ref[...], pw_ref[...], gc_ref[...]
        sa, sc = _sigmoid(ga), _sigmoid(gc)
        silu_a, silu_c = ga * sa, gc * sc
        ycat = jnp.concatenate([o * silu_a, pw * silu_c], axis=-1).astype(_MM)
        r = ALPHA * h_ref[...] + _dot(ycat, w_ref[...])
        xhat, rstd = _ln_stats(r)
        out = xhat * g_ref[...] + b_ref[...]
        rows = i * t + lax.broadcasted_iota(jnp.int32, (t, 1), 0)
        diff = jnp.where(rows >= ROW_X, out - _x_tile(i, tgt_refs), 0.0)
        loss_ref[...] += (0.5 / D_MODEL) * jnp.sum(diff * diff, keepdims=True)
        dout = diff * (1.0 / D_MODEL)
        dg_ref[...] += jnp.sum(dout * xhat, axis=0, keepdims=True)
        db_ref[...] += jnp.sum(dout, axis=0, keepdims=True)
        dr = _ln_bwd(dout, xhat, rstd, g_ref[...])
        dr_ref[...] = dr
        drb = dr.astype(_MM)
        dw_ref[...] += _dot_t0(ycat, drb)
        dy = _dot_t1(drb, w_ref[...])
        dya, dyc = dy[:, :D_ATTN], dy[:, D_ATTN:]
        do = dya * silu_a
        do_ref[...] = do.astype(do_ref.dtype)
        dga_ref[...] = (dya * o * (sa * (1.0 + ga * (1.0 - sa)))).astype(dga_ref.dtype)
        dpw_ref[...] = (dyc * silu_c).astype(dpw_ref.dtype)
        dgc_ref[...] = (dyc * pw * (sc * (1.0 + gc * (1.0 - sc)))).astype(dgc_ref.dtype)
        sel = (lax.broadcasted_iota(jnp.int32, (128, 128), 0) // HEAD_DIM
               == lax.broadcasted_iota(jnp.int32, (128, 128), 1)).astype(_F32)
        prod = do * o
        for p in range(N_PAIRS):
            delta_ref[p] = _dot_exact(prod[:, 128 * p:128 * (p + 1)], sel)

    row = lambda w: pl.BlockSpec((t, w), lambda i: (i, 0))
    return pl.pallas_call(
        body, name="out_loss_bwd", grid=(lp // t,),
        in_specs=[row(512)] * 4 + [row(D_MODEL)] + _x_specs(t) + [_full((D_MODEL, D_MODEL))] + [_full((1, D_MODEL))] * 2,
        out_specs=[row(D_MODEL)] + [row(512)] * 4 + [pl.BlockSpec((N_PAIRS, t, 128), lambda i: (0, i, 0)),
                   _full((D_MODEL, D_MODEL)), _full((1, D_MODEL)), _full((1, D_MODEL)), _full((1, 1))],
        out_shape=[jax.ShapeDtypeStruct((lp, D_MODEL), _F32)] + [jax.ShapeDtypeStruct((lp, 512), _MM)] * 4
                  + [jax.ShapeDtypeStruct((N_PAIRS, lp, 128), _F32), jax.ShapeDtypeStruct((D_MODEL, D_MODEL), _F32),
                     jax.ShapeDtypeStruct((1, D_MODEL), _F32), jax.ShapeDtypeStruct((1, D_MODEL), _F32),
                     jax.ShapeDtypeStruct((1, 1), _F32)],
        compiler_params=_params(("arbitrary",)),
    )(o, ga, pw, gc, h, *([target] * nx), w_out, g, b)


BWD_WIDTH = 2


def _attn_bwd(q, k, v, do, lse, delta, ccol):
    lp = q.shape[0]
    t = _row_tile(lp)
    nb = lp // t

    def body(q_ref, k_ref, v_ref, do_ref, lse_ref, delta_ref, ccol_ref,
             dq_ref, dk_ref, dv_ref, dck_ref, dcq_ref,
             qa_ref, qat_ref, doa_ref, doat_ref, dqt_ref):
        j = pl.program_id(1)

        @pl.when(j == 0)
        def _():
            dqt_ref[...] = jnp.zeros_like(dqt_ref)
            for n in range(nb):
                rows = slice(n * t, (n + 1) * t)
                lse_pieces, delta_pieces = _split3(-lse_ref[0, rows, :]), _split3(-delta_ref[0, rows, :])
                for hd in range(2):
                    qa = _augment(q_ref[rows, :], hd, 1.0, lse_pieces)
                    doa = _augment(do_ref[rows, :], hd, delta_pieces, None)
                    qa_ref[hd, rows, :] = qa
                    qat_ref[hd, :, rows] = _transpose_mm(qa)
                    doa_ref[hd, rows, :] = doa
                    doat_ref[hd, :, rows] = _transpose_mm(doa)

        pieces = _key_pieces(ccol_ref[0], j * t)
        kas = [_key_operand(k_ref[...], pieces, hd) for hd in range(2)]
        kats = [_transpose_mm(ka) for ka in kas]
        vas = [_augment(v_ref[...], hd, 1.0, None) for hd in range(2)]

        def block(i, carry, diag, width=1):
            rows = pl.ds(pl.multiple_of(i * t, 128), width * t)
            new = []
            for hd in range(2):
                dk, dv = carry[hd]
                st = _dot(kas[hd], qat_ref[hd, :, rows])
                if diag:
                    st = jnp.where(_diag_valid(j, t), st, -MASKED)
                pt = jnp.exp(st)
                dsb = (pt * _dot(vas[hd], doat_ref[hd, :, rows])).astype(_MM)
                dv = dv + _dot(pt.astype(_MM), doa_ref[hd, rows, :])
                dk = dk + _dot(dsb, qa_ref[hd, rows, :])
                dqt_ref[hd, :, rows] += _dot(kats[hd], dsb)
                new.append((dk, dv))
            return tuple(new)

        zero = jnp.zeros((t, 128), _F32)
        carry = block(j, ((zero, zero), (zero, zero)), True)
        rest = nb - 1 - j
        if nb > BWD_WIDTH:
            carry = lax.fori_loop(0, rest // BWD_WIDTH,
                                  lambda n, c: block(j + 1 + BWD_WIDTH * n, c, False, BWD_WIDTH), carry)
            rest = rest % BWD_WIDTH
        (dk0, dv0), (dk1, dv1) = lax.fori_loop(nb - rest, nb, functools.partial(block, diag=False), carry)
        lane = lax.broadcasted_iota(jnp.int32, (1, 128), 1)
        dk_ref[...] = jnp.where(lane < HEAD_DIM, dk0, dk1).astype(dk_ref.dtype)
        dv_ref[...] = jnp.where(lane < HEAD_DIM, dv0, dv1).astype(dv_ref.dtype)
        dck_ref[0] = jnp.where(lane == 0, dk0[:, HEAD_DIM:HEAD_DIM + 1], jnp.where(lane == 1, dk1[:, 0:1], 0.0))

        @pl.when(j == nb - 1)
        def _():
            row = lax.broadcasted_iota(jnp.int32, (128, t), 0)
            sub = lax.broadcasted_iota(jnp.int32, (8, t), 0)
            for n in range(nb):
                cols = slice(n * t, (n + 1) * t)
                t0, t1 = dqt_ref[0, :, cols], dqt_ref[1, :, cols]
                dq_ref[cols, :] = jnp.where(row < HEAD_DIM, t0, t1).T * ATTN_SCALE
                dcq_ref[0, :, cols] = jnp.where(sub == 0, t0[HEAD_DIM + 3:HEAD_DIM + 4, :],
                                                jnp.where(sub == 1, t1[3:4, :], 0.0))

    blk = pl.BlockSpec((t, 128), lambda p, j: (j, p))
    res = pl.BlockSpec((lp, 128), lambda p, j: (0, p))
    stat = pl.BlockSpec((1, lp, 128), lambda p, j: (p, 0, 0))
    stat_blk = pl.BlockSpec((1, t, 128), lambda p, j: (p, j, 0))
    return pl.pallas_call(
        body, name="attn_bwd", grid=(N_PAIRS, nb),
        in_specs=[res, blk, blk, res, stat, stat, stat_blk],
        out_specs=[res, blk, blk, stat_blk, pl.BlockSpec((1, 8, lp), lambda p, j: (p, 0, 0))],
        out_shape=[jax.ShapeDtypeStruct((lp, D_ATTN), _F32), jax.ShapeDtypeStruct((lp, D_ATTN), _MM),
                   jax.ShapeDtypeStruct((lp, D_ATTN), _MM), jax.ShapeDtypeStruct((N_PAIRS, lp, 128), _F32),
                   jax.ShapeDtypeStruct((N_PAIRS, 8, lp), _F32)],
        scratch_shapes=[pltpu.VMEM((2, lp, 128), _MM), pltpu.VMEM((2, 128, lp), _MM),
                        pltpu.VMEM((2, lp, 128), _MM), pltpu.VMEM((2, 128, lp), _MM),
                        pltpu.VMEM((2, 128, lp), _F32)],
        compiler_params=_params(("arbitrary", "arbitrary")),
    )(q, k, v, do, lse, delta, ccol)


def _decay_bwd(dck, dcq, f, bf):
    lp = f.shape[0]
    nb = lp // 128

    def body(dck_ref, dcq_ref, f_ref, bf_ref, df_ref, dbf_ref):
        r = lax.broadcasted_iota(jnp.int32, (128, 128), 0)
        c = lax.broadcasted_iota(jnp.int32, (128, 128), 1)
        triu = (c >= r).astype(_F32)

        def step(n, carry):
            tail, dbf = carry
            bi = nb - 1 - n
            rows = pl.ds(pl.multiple_of(bi * 128, 128), 128)
            cols = jnp.zeros((128, 128), _F32)
            for p in range(N_PAIRS):
                dcq = jnp.concatenate([dcq_ref[p, :, rows], jnp.zeros((120, 128), _F32)], axis=0).T
                dcp = dcq - dck_ref[p, rows, :]
                cols = cols + (dcp if p == 0 else pltpu.roll(dcp, 2 * p, axis=1))
            lane = lax.broadcasted_iota(jnp.int32, (128, 128), 1)
            cols = jnp.where(lane < N_HEADS, cols, 0.0)
            rev = _dot_exact(triu, cols) + tail
            pos = bi * 128 + lax.broadcasted_iota(jnp.int32, (128, 1), 0)
            df = jnp.where(pos >= ROW_PAD, rev * (1.0 - _sigmoid(f_ref[rows, :] + bf_ref[...])), 0.0)
            df_ref[rows, :] = df.astype(df_ref.dtype)
            return rev[0:1, :], dbf + jnp.sum(df, axis=0, keepdims=True)

        zero = jnp.zeros((1, 128), _F32)
        _, dbf = lax.fori_loop(0, nb, step, (zero, zero), unroll=True)
        dbf_ref[...] = dbf

    return pl.pallas_call(
        body, name="decay_bwd",
        out_shape=[jax.ShapeDtypeStruct((lp, 128), _MM), jax.ShapeDtypeStruct((1, 128), _F32)],
        compiler_params=_params(),
    )(dck, dcq, f, bf)


def _conv_bwd_rows(dpw, z, g, b, wpw):
    lp = z.shape[0]
    t = _row_tile(lp)

    def body(dpw_ref, z_ref, g_ref, b_ref, w_ref, dz_ref, dw_ref, dg_ref, db_ref, dcb_ref):
        i = pl.program_id(0)

        @pl.when(i == 0)
        def _():
            dw_ref[...] = jnp.zeros_like(dw_ref)
            dg_ref[...] = jnp.zeros_like(dg_ref)
            db_ref[...] = jnp.zeros_like(db_ref)
            dcb_ref[...] = jnp.zeros_like(dcb_ref)

        xhat, rstd = _ln_stats(z_ref[...])
        zn = xhat * g_ref[...] + b_ref[...]
        sg = _sigmoid(zn)
        dpw = dpw_ref[...]
        dw_ref[...] += _dot_t0((zn * sg).astype(_MM), dpw)
        dzn = _dot_t1(dpw, w_ref[...]) * (sg * (1.0 + zn * (1.0 - sg)))
        dg_ref[...] += jnp.sum(dzn * xhat, axis=0, keepdims=True)
        db_ref[...] += jnp.sum(dzn, axis=0, keepdims=True)
        dz = _ln_bwd(dzn, xhat, rstd, g_ref[...])
        dz_ref[...] = dz
        dcb_ref[...] += jnp.sum(dz, axis=0, keepdims=True)

    row = pl.BlockSpec((t, D_CONV), lambda i: (i, 0))
    vec = _full((1, D_CONV))
    vshape = jax.ShapeDtypeStruct((1, D_CONV), _F32)
    return pl.pallas_call(
        body, name="conv_bwd_rows", grid=(lp // t,),
        in_specs=[row, row, vec, vec, _full((D_CONV, D_CONV))],
        out_specs=[row, _full((D_CONV, D_CONV)), vec, vec, vec],
        out_shape=[jax.ShapeDtypeStruct((lp, D_CONV), _F32), jax.ShapeDtypeStruct((D_CONV, D_CONV), _F32),
                   vshape, vshape, vshape],
        compiler_params=_params(("arbitrary",)),
    )(dpw, z, g, b, wpw)


def _conv_bwd_taps(dz, u, ug, cw):
    lp = u.shape[0]
    t = _row_tile(lp)
    nt = lp // t

    def body(dz_ref, u_ref, ug_ref, cw_ref, du_ref, dug_ref, dcw_ref, win, sh, wacc):
        i = pl.program_id(0)

        @pl.when(i == 0)
        def _():
            win[t:t + HALO, :] = jnp.zeros((HALO, D_CONV), _F32)
            wacc[...] = jnp.zeros_like(wacc)

        win[0:t, :] = dz_ref[...]
        _shifted_windows(win, sh, t + HALO)

        def chunk(c, carry):
            r0 = c * CONV_CHUNK
            rows = pl.ds(pl.multiple_of(r0, 8), CONV_CHUNK)
            u = u_ref[rows, :]
            sg = _sigmoid(ug_ref[rows, :])
            hh = u * sg
            dhh = jnp.zeros((CONV_CHUNK, D_CONV), _F32)
            for j in range(CONV_WIDTH):
                d = CONV_WIDTH - 1 - j
                s = _tap_rows(sh, [d], r0)[d]
                dhh = dhh + cw_ref[j:j + 1, :] * s
                prod = hh * s
                wacc[j] += sum(prod[8 * n:8 * (n + 1), :] for n in range(CONV_CHUNK // 8))
            du_ref[rows, :] = (dhh * sg).astype(du_ref.dtype)
            dug_ref[rows, :] = (dhh * u * sg * (1.0 - sg)).astype(dug_ref.dtype)
            return carry

        lax.fori_loop(0, t // CONV_CHUNK, chunk, 0)
        win[t:t + HALO, :] = win[0:HALO, :]

        @pl.when(i == nt - 1)
        def _():
            dcw_ref[...] = jnp.sum(wacc[...], axis=1)

    row = pl.BlockSpec((t, D_CONV), lambda i: (nt - 1 - i, 0))
    return pl.pallas_call(
        body, name="conv_bwd_taps", grid=(nt,),
        in_specs=[row, row, row, _full((32, D_CONV))],
        out_specs=[row, row, _full((32, D_CONV))],
        out_shape=[jax.ShapeDtypeStruct((lp, D_CONV), _MM)] * 2 + [jax.ShapeDtypeStruct((32, D_CONV), _F32)],
        scratch_shapes=[pltpu.VMEM((t + HALO, D_CONV), _F32), pltpu.VMEM((8, t + HALO, D_CONV), _F32),
                        pltpu.VMEM((32, 8, D_CONV), _F32)],
        compiler_params=_params(("arbitrary",)),
    )(dz, u, ug, cw)


def _proj_bwd_x(pieces, df, wpt, dr, x, head, g, parts=()):
    seq = x.shape[0]
    lp = ROW_X + seq
    t = _row_tile(lp)
    nt = lp // t
    nx = t // ROW_X
    n_parts = len(parts)

    def body(*refs):
        dq_ref, dk_ref, dv_ref, dga_ref, du_ref, dug_ref, dgc_ref, df_ref, wt_ref, dr_ref = refs[:10]
        x_refs, (head_ref, g_ref) = refs[10:10 + nx], refs[10 + nx:12 + nx]
        part_refs, refs = refs[12 + nx:12 + nx + n_parts], refs[12 + nx + n_parts:]
        gx_ref, dmeta_ref, dg_ref, db_ref = refs[:4]
        land_refs, (buf, sem), comm_sems = refs[4:4 + n_parts], refs[4 + n_parts:6 + n_parts], refs[6 + n_parts:]
        i = pl.program_id(0)

        @pl.when(i == 0)
        def _():
            dg_ref[...] = jnp.zeros_like(dg_ref)
            db_ref[...] = jnp.zeros_like(db_ref)
            if n_parts:
                for cp in _chip_exchange_copies(part_refs, land_refs, *comm_sems):
                    cp.start()

        dh = ALPHA * dr_ref[...]
        for n, ref in enumerate((dq_ref, dk_ref, dv_ref, dga_ref, du_ref, dug_ref, dgc_ref)):
            dh = dh + _dot(ref[...].astype(_MM), wt_ref[512 * n:512 * (n + 1), :])
        dh = dh + _dot(df_ref[...], wt_ref[3584:3712, :])
        rows = i * t + lax.broadcasted_iota(jnp.int32, (t, 1), 0)
        dh = jnp.where(rows >= ROW_PAD, dh, 0.0)
        xhat, rstd = _ln_stats(_x_tile(i, x_refs, head_ref))
        dg_ref[...] += jnp.sum(dh * xhat, axis=0, keepdims=True)
        db_ref[...] += jnp.sum(dh, axis=0, keepdims=True)
        dx = _ln_bwd(dh, xhat, rstd, g_ref[...])

        def first_copy():
            return pltpu.make_async_copy(buf.at[0, pl.ds(ROW_X, t - ROW_X), :], gx_ref.at[pl.ds(0, t - ROW_X), :], sem)

        def tile_copy(n):
            return pltpu.make_async_copy(buf.at[n % 2], gx_ref.at[pl.ds(pl.multiple_of(n * t - ROW_X, ROW_X), t), :], sem)

        pl.when(i == 1)(lambda: first_copy().wait())
        pl.when(i > 1)(lambda: tile_copy(i - 1).wait())
        buf[i % 2] = dx

        @pl.when(i == 0)
        def _():
            dmeta_ref[...] = dx[ROW_PAD:ROW_X, :]
            first_copy().start()

        pl.when(i > 0)(lambda: tile_copy(i).start())
        if nt == 1:
            first_copy().wait()
        else:
            pl.when(i == nt - 1)(lambda: tile_copy(i).wait())

        @pl.when(i == nt - 1)
        def _():
            if n_parts:
                for cp in _chip_exchange_copies(part_refs, land_refs, *comm_sems):
                    cp.wait()

    row = lambda w: pl.BlockSpec((t, w), lambda i: (i, 0))
    vec = _full((1, D_MODEL))
    any_spec = pl.BlockSpec(memory_space=pl.ANY)
    return pl.pallas_call(
        body, name="proj_bwd_x", grid=(nt,),
        in_specs=[row(512)] * 7 + [row(128), _full((D_IN_PAD, D_MODEL)), row(D_MODEL)] + _x_specs(t)
                 + [_full((ROW_X, D_MODEL)), vec] + [any_spec] * n_parts,
        out_specs=[any_spec, _full((N_META, D_MODEL)), vec, vec] + [any_spec] * n_parts,
        out_shape=[jax.ShapeDtypeStruct((seq, D_MODEL), _F32), jax.ShapeDtypeStruct((N_META, D_MODEL), _F32),
                   jax.ShapeDtypeStruct((1, D_MODEL), _F32), jax.ShapeDtypeStruct((1, D_MODEL), _F32)]
                  + [jax.ShapeDtypeStruct(p.shape, p.dtype) for p in parts],
        scratch_shapes=[pltpu.VMEM((2, t, D_MODEL), _F32), pltpu.SemaphoreType.DMA(())]
                       + [pltpu.SemaphoreType.DMA((3 * n_parts,))] * (2 if n_parts else 0),
        compiler_params=_params(("arbitrary",)),
    )(*pieces, df, wpt, dr, *([x] * nx), head, g, *parts)


def _proj_bwd_w(h, pieces, name):
    lp = h.shape[0]
    t = _row_tile(lp)
    n = len(pieces)
    widths = [p.shape[1] for p in pieces]

    def body(*refs):
        h_ref, d_refs, w_refs = refs[0], refs[1:1 + n], refs[1 + n:]
        i = pl.program_id(0)

        @pl.when(i == 0)
        def _():
            for w_ref in w_refs:
                w_ref[...] = jnp.zeros_like(w_ref)

        hb = h_ref[...].astype(_MM)
        for d_ref, w_ref in zip(d_refs, w_refs):
            w_ref[...] += _dot_t0(d_ref[...].astype(_MM), hb)

    row = lambda w: pl.BlockSpec((t, w), lambda i: (i, 0))
    return pl.pallas_call(
        body, name=name, grid=(lp // t,),
        in_specs=[row(D_MODEL)] + [row(w) for w in widths],
        out_specs=[_full((w, D_MODEL)) for w in widths],
        out_shape=[jax.ShapeDtypeStruct((w, D_MODEL), _F32) for w in widths],
        compiler_params=_params(("arbitrary",)),
    )(h, *pieces)


def _adamw_math(w, g, m, v):
    m = ADAM_B1 * m + (1.0 - ADAM_B1) * g
    v = ADAM_B2 * v + (1.0 - ADAM_B2) * (g * g)
    m_hat = m / (1.0 - ADAM_B1 ** ADAM_STEP)
    v_hat = v / (1.0 - ADAM_B2 ** ADAM_STEP)
    delta = -ADAM_LR * (m_hat / (jnp.sqrt(v_hat) + ADAM_EPS) + ADAM_WD * w)
    return delta, m, v


def _adamw_tiled(w, g, m, v, rows, name):
    r, c = w.shape

    def body(w_ref, g_ref, m_ref, v_ref, go_ref, d_ref, nm_ref, nv_ref):
        g = g_ref[...]
        go_ref[...] = g
        d_ref[...], nm_ref[...], nv_ref[...] = _adamw_math(w_ref[...], g, m_ref[...], v_ref[...])

    spec = pl.BlockSpec((rows, c), lambda i: (i, 0))
    return pl.pallas_call(
        body, name=name, grid=(pl.cdiv(r, rows),), in_specs=[spec] * 4, out_specs=[spec] * 4,
        out_shape=[jax.ShapeDtypeStruct((r, c), _F32)] * 4,
        compiler_params=_params(("arbitrary",)),
    )(w, g, m, v)


def _adamw_small(ws, gs, ms, vs):
    n = len(ws)

    def body(*refs):
        ins, outs = refs[:4 * n], refs[4 * n:]
        for k in range(n):
            d, m, v = _adamw_math(ins[k][...], ins[n + k][...], ins[2 * n + k][...], ins[3 * n + k][...])
            outs[k][...], outs[n + k][...], outs[2 * n + k][...] = d, m, v

    shapes = [jax.ShapeDtypeStruct(w.shape, _F32) for w in ws]
    res = pl.pallas_call(body, name="adamw_small", out_shape=shapes * 3, compiler_params=_params())(*ws, *gs, *ms, *vs)
    return res[:n], res[n:2 * n], res[2 * n:]


def _place():
    x, y, c = lax.axis_index("x"), lax.axis_index("y"), lax.axis_index("c")
    chips = [(1 - x, y), (x, 1 - y), (1 - x, 1 - y)]
    return x, y, c, chips


def _half(ref, slot, c, rows):
    hr = rows // 2
    return ref.at[slot, pl.ds(pl.multiple_of(c * hr, 8), hr), :]


def _gather_phases(ins, outs, send, recv):
    n = len(ins)
    x, y, c, chips = _place()
    me = 2 * x + y
    sibling = (x, y, 1 - c)

    def copy(k, t, slot, half, to, src=None):
        rows = ins[t].shape[0]
        dst = _half(outs[t], slot, half, rows)
        return pltpu.make_async_remote_copy(
            src_ref=dst if src is None else src, dst_ref=dst,
            send_sem=send.at[k], recv_sem=recv.at[k], device_id=to, device_id_type=MESH)

    def first():
        cps = []
        for t in range(n):
            hr = ins[t].shape[0] // 2
            mine = ins[t].at[pl.ds(pl.multiple_of(c * hr, 8), hr), :]
            cps += [copy(6 * t + j, t, me, c, (*chip, c), src=mine) for j, chip in enumerate(chips)]
        return cps

    def passed():
        return [copy(6 * t + 3 + j, t, 2 * cx + cy, c, sibling) for t in range(n) for j, (cx, cy) in enumerate(chips)]

    def start():
        for cp in first():
            cp.start()

    def relay():
        fwd = passed()
        for t in range(n):
            for j, (cx, cy) in enumerate(chips):
                copy(6 * t + j, t, 2 * cx + cy, c, sibling).wait_recv()
                fwd[3 * t + j].start()

    def finish():
        for t in range(n):
            for j, (cx, cy) in enumerate(chips):
                copy(6 * t + 3 + j, t, 2 * cx + cy, 1 - c, sibling).wait_recv()
        for cp in first() + passed():
            cp.wait_send()

    return start, relay, finish


def _all_gather(shards):
    n = len(shards)

    def body(*refs):
        start, relay, finish = _gather_phases(refs[:n], refs[n:2 * n], *refs[2 * n:])
        start()
        relay()
        finish()

    any_spec = pl.BlockSpec(memory_space=pl.ANY)
    return pl.pallas_call(
        body, name="all_gather",
        in_specs=[any_spec] * n, out_specs=[any_spec] * n,
        out_shape=[jax.ShapeDtypeStruct((N_SHARD,) + s.shape, s.dtype) for s in shards],
        scratch_shapes=[pltpu.SemaphoreType.DMA((6 * n,)), pltpu.SemaphoreType.DMA((6 * n,))],
    )(*shards)


def _own_slot(gathered, own):
    me = 2 * lax.axis_index("x") + lax.axis_index("y")
    slot = lax.broadcasted_iota(jnp.int32, (N_SHARD,) + (1,) * own.ndim, 0)
    return jnp.where(slot == me, own[None], gathered)


def _sibling_exchange(bufs):
    n = len(bufs)

    def body(*refs):
        ins, got = refs[:n], refs[n:2 * n]
        send, recv = refs[2 * n:]
        x, y, c, _ = _place()
        cps = []
        for t in range(n):
            hr = ins[t].shape[0] // (2 * N_SHARD)
            for s in range(N_SHARD):
                k = N_SHARD * t + s
                cps.append(pltpu.make_async_remote_copy(
                    src_ref=ins[t].at[pl.ds(pl.multiple_of((2 * s + 1 - c) * hr, 32 // ins[t].dtype.itemsize), hr), :],
                    dst_ref=got[t].at[s],
                    send_sem=send.at[k], recv_sem=recv.at[k], device_id=(x, y, 1 - c), device_id_type=MESH))
        for cp in cps:
            cp.start()
        for cp in cps:
            cp.wait()

    any_spec = pl.BlockSpec(memory_space=pl.ANY)
    return pl.pallas_call(
        body, name="sibling_exchange",
        in_specs=[any_spec] * n, out_specs=[any_spec] * n,
        out_shape=[jax.ShapeDtypeStruct((N_SHARD, b.shape[0] // (2 * N_SHARD), b.shape[1]), b.dtype) for b in bufs],
        scratch_shapes=[pltpu.SemaphoreType.DMA((N_SHARD * n,))] * 2,
    )(*bufs)


def _half_sum(bufs, got, wire, c):
    n = len(bufs)

    def body(c_ref, *refs):
        ins, gots, outs = refs[:n], refs[n:2 * n], refs[2 * n:]
        for t in range(n):
            outs[t][0] = (ins[t][...].astype(_F32) + gots[t][0].astype(_F32)).astype(outs[t].dtype)

    slot = lambda g: pl.BlockSpec((1,) + tuple(g.shape[1:]), lambda s, c_ref: (s, 0, 0))
    grid_spec = pltpu.PrefetchScalarGridSpec(
        num_scalar_prefetch=1, grid=(N_SHARD,),
        in_specs=[pl.BlockSpec(tuple(g.shape[1:]), lambda s, c_ref: (2 * s + c_ref[0], 0)) for g in got]
                 + [slot(g) for g in got],
        out_specs=[slot(g) for g in got])
    return pl.pallas_call(
        body, name="half_sum", grid_spec=grid_spec,
        out_shape=[jax.ShapeDtypeStruct(g.shape, d) for g, d in zip(got, wire)],
        compiler_params=_params(("arbitrary",)),
    )(c, *bufs, *got)


def _chip_exchange_copies(ins, outs, send, recv):
    x, y, c, chips = _place()
    me = 2 * x + y
    return [pltpu.make_async_remote_copy(
        src_ref=ins[t].at[2 * cx + cy], dst_ref=outs[t].at[me], send_sem=send.at[3 * t + j], recv_sem=recv.at[3 * t + j],
        device_id=(cx, cy, c), device_id_type=MESH) for t in range(len(ins)) for j, (cx, cy) in enumerate(chips)]


def _chip_sum_share(parts, got, me):
    n = len(got)

    def body(me_ref, *refs):
        owns, gots, outs = refs[:n], refs[n:2 * n], refs[2 * n:3 * n]
        accs, (local, send, recv) = refs[3 * n:4 * n], refs[4 * n:]
        s = pl.program_id(0)
        x, y, c, _ = _place()
        for t in range(n):
            term = jnp.where(s == me_ref[0], owns[t][0], gots[t][0]).astype(_F32)

            @pl.when(s == 0)
            def _():
                accs[t][...] = term

            @pl.when(s > 0)
            def _():
                accs[t][...] += term

        @pl.when(s == N_SHARD - 1)
        def _():
            cps = []
            for t in range(n):
                hr = accs[t].shape[0]
                mine = outs[t].at[pl.ds(pl.multiple_of(c * hr, 8), hr), :]
                cps.append(pltpu.make_async_copy(accs[t], mine, local.at[t]))
                cps.append(pltpu.make_async_remote_copy(
                    src_ref=accs[t], dst_ref=mine, send_sem=send.at[t], recv_sem=recv.at[t],
                    device_id=(x, y, 1 - c), device_id_type=MESH))
            for cp in cps:
                cp.start()
            for cp in cps:
                cp.wait()

    blk = lambda g: (1,) + tuple(g.shape[1:])
    grid_spec = pltpu.PrefetchScalarGridSpec(
        num_scalar_prefetch=1, grid=(N_SHARD,),
        in_specs=[pl.BlockSpec(blk(g), lambda s, me_ref: (me_ref[0], 0, 0)) for g in got]
                 + [pl.BlockSpec(blk(g), lambda s, me_ref: (jnp.where(s == me_ref[0], (s + 1) % N_SHARD, s), 0, 0))
                    for g in got],
        out_specs=[pl.BlockSpec(memory_space=pl.ANY)] * n,
        scratch_shapes=[pltpu.VMEM(tuple(g.shape[1:]), _F32) for g in got] + [pltpu.SemaphoreType.DMA((n,))] * 3)
    return pl.pallas_call(
        body, name="chip_sum_share", grid_spec=grid_spec,
        out_shape=[jax.ShapeDtypeStruct((2 * g.shape[1], g.shape[2]), _F32) for g in got],
        compiler_params=_params(("arbitrary",)),
    )(me, *parts, *got)


def _as_operand(index):
    return jnp.asarray(index, jnp.int32).reshape(1)


def _chip_partials(bufs, wire):
    return _half_sum(bufs, _sibling_exchange(bufs), wire, _as_operand(lax.axis_index("c")))


def _finish_reduction(parts, landed):
    return _chip_sum_share(parts, landed, _as_operand(2 * lax.axis_index("x") + lax.axis_index("y")))


def _reduce_small(buf):
    rows = buf.shape[0] // N_SHARD
    hr = rows // 2

    def body(in_ref, out_ref, got1, part, got2, send, recv):
        x, y, c, chips = _place()
        me = 2 * x + y
        sibling = (x, y, 1 - c)

        def half(s, which):
            return in_ref.at[pl.ds(pl.multiple_of((2 * s + which) * hr, 8), hr), :]

        first = [pltpu.make_async_remote_copy(src_ref=half(s, 1 - c), dst_ref=got1.at[s], send_sem=send.at[s],
                                              recv_sem=recv.at[s], device_id=sibling, device_id_type=MESH)
                 for s in range(N_SHARD)]
        for cp in first:
            cp.start()
        for s in range(N_SHARD):
            first[s].wait()
            part[s] = half(s, c)[...] + got1[s]
        got2[me] = part[me]
        second = [pltpu.make_async_remote_copy(src_ref=part.at[2 * cx + cy], dst_ref=got2.at[me],
                                               send_sem=send.at[N_SHARD + j], recv_sem=recv.at[N_SHARD + j],
                                               device_id=(cx, cy, c), device_id_type=MESH)
                  for j, (cx, cy) in enumerate(chips)]
        for cp in second:
            cp.start()
        for cp in second:
            cp.wait()
        mine = ((got2[0] + got2[1]) + got2[2]) + got2[3]
        out_ref[pl.ds(pl.multiple_of(c * hr, 8), hr), :] = mine
        last = pltpu.make_async_remote_copy(
            src_ref=out_ref.at[pl.ds(pl.multiple_of(c * hr, 8), hr), :],
            dst_ref=out_ref.at[pl.ds(pl.multiple_of(c * hr, 8), hr), :],
            send_sem=send.at[2 * N_SHARD - 1], recv_sem=recv.at[2 * N_SHARD - 1], device_id=sibling, device_id_type=MESH)
        last.start()
        last.wait()

    vmem = pl.BlockSpec(memory_space=pltpu.VMEM)
    return pl.pallas_call(
        body, name="reduce_small", in_specs=[vmem], out_specs=vmem,
        out_shape=jax.ShapeDtypeStruct((rows, buf.shape[1]), buf.dtype),
        scratch_shapes=[pltpu.VMEM((N_SHARD, hr, buf.shape[1]), buf.dtype)] * 3
                       + [pltpu.SemaphoreType.DMA((2 * N_SHARD,))] * 2,
    )(buf)


REPL_SIZES = (D_MODEL, D_MODEL, N_HEADS, D_CONV, D_CONV, D_CONV, D_MODEL, D_MODEL)
REPL_ROWS = 48


def _pack_repl(parts):
    flat = jnp.concatenate([p.reshape(-1).astype(_F32) for p in parts])
    return jnp.pad(flat, (0, REPL_ROWS * 128 - flat.shape[0])).reshape(REPL_ROWS, 128)


def _unpack_repl(packed, shapes):
    flat = packed.reshape(-1)
    out, off = [], 0
    for shape, size in zip(shapes, REPL_SIZES):
        out.append(flat[off:off + size].reshape(shape))
        off += size
    return out


W_IN_PIECES = (("q", 0, 512), ("k", 512, 1024), ("v", 1024, 1536), ("f", 1536, 1544), ("ga", 1544, 2056),
               ("u", 2056, 2568), ("ug", 2568, 3080), ("gc", 3080, 3592))


def _projection_operand(slots):
    w_in_t = slots[:, :W_IN_SHARD, :].reshape(D_IN, D_MODEL)
    zrow = jnp.zeros((128 - N_HEADS, D_MODEL), w_in_t.dtype)
    return jnp.concatenate([w_in_t[:OFF_F], w_in_t[OFF_F + N_HEADS:], w_in_t[OFF_F:OFF_F + N_HEADS], zrow], axis=0)


def _gradient_slots(pieces):
    dw = jnp.concatenate([pieces[name][:b - a] for name, a, b in W_IN_PIECES], axis=0)
    dw = jnp.pad(dw.reshape(N_SHARD, W_IN_SHARD, D_MODEL), ((0, 0), (0, W_IN_SLOT - W_IN_SHARD), (0, 0)))
    return dw.reshape(N_SHARD * W_IN_SLOT, D_MODEL)


def _local_grads(x, target, meta, ln_in_g, ln_in_b, w_in_slots, b_f, conv_w, conv_b, ln_conv_g, ln_conv_b,
                 w_pw, w_out, ln_out_g, ln_out_b, chip_partials=None, late_shards=None):
    vec = lambda a: a.reshape(1, -1).astype(_F32)
    head = jnp.concatenate([jnp.zeros((ROW_PAD, D_MODEL), _F32), meta], axis=0)
    wpt = _projection_operand(w_in_slots)
    bf = jnp.pad(vec(b_f), ((0, 0), (0, 128 - N_HEADS)))
    cw = jnp.pad(conv_w, ((0, 32 - CONV_WIDTH), (0, 0)))

    h, q, k, v, ga, u, ug, gc, f, vt, *late = _ln_proj(x, head, vec(ln_in_g), vec(ln_in_b), wpt, late_shards or ())
    if late_shards:
        w_pw, w_out = (_own_slot(gathered, own).reshape(N_SHARD * own.shape[0], own.shape[1])
                       for gathered, own in zip(late, late_shards))
    ccol = _decay_cumsum(f, bf)
    o, lse = _attn_fwd(q, k, vt, ccol)
    z, pw = _conv_fwd(u, ug, cw, vec(conv_b), vec(ln_conv_g), vec(ln_conv_b), w_pw)
    dr, do, dga, dpw, dgc, delta, dw_out, dg_out, db_out, loss = _out_loss_bwd(
        o, ga, pw, gc, h, target, w_out, vec(ln_out_g), vec(ln_out_b))
    dq, dk, dv, dck, dcq = _attn_bwd(q, k, v, do, lse, delta, ccol)
    df, dbf = _decay_bwd(dck, dcq, f, bf)
    dz, dw_pw, dg_conv, db_conv, dconv_b = _conv_bwd_rows(dpw, z, vec(ln_conv_g), vec(ln_conv_b), w_pw)
    du, dug, dcw = _conv_bwd_taps(dz, u, ug, cw)
    pieces = (dq, dk, dv, dga, du, dug, dgc)
    dwq, dwk, dwv, dwga = _proj_bwd_w(h, pieces[:4], "proj_bwd_w_attn")
    dwu, dwug, dwgc, dwf = _proj_bwd_w(h, pieces[4:] + (df,), "proj_bwd_w_conv")
    dw_in_t = _gradient_slots(dict(q=dwq, k=dwk, v=dwv, f=dwf, ga=dwga, u=dwu, ug=dwug, gc=dwgc))
    parts = chip_partials([dw_in_t, dw_pw, dw_out]) if chip_partials else []
    grad_x, dmeta, dg_in, db_in, *landed = _proj_bwd_x(pieces, df, wpt, dr, x, head, vec(ln_in_g), parts)
    grads = dict(meta=dmeta, ln_in_g=dg_in, ln_in_b=db_in, w_in_t=dw_in_t, b_f=dbf[:, :N_HEADS],
                 conv_w=dcw[:CONV_WIDTH], conv_b=dconv_b, ln_conv_g=dg_conv, ln_conv_b=db_conv,
                 w_pw=dw_pw, w_out=dw_out, ln_out_g=dg_out, ln_out_b=db_out, reduced=(parts, landed))
    return loss, grad_x, grads


def kernel(x, meta, ln_in_g, ln_in_b, w_in, b_f, conv_w, conv_b, ln_conv_g, ln_conv_b, w_pw, w_out, ln_out_g, ln_out_b, loss_target, m_meta, m_ln_in_g, m_ln_in_b, m_w_in, m_b_f, m_conv_w, m_conv_b, m_ln_conv_g, m_ln_conv_b, m_w_pw, m_w_out, m_ln_out_g, m_ln_out_b, v_meta, v_ln_in_g, v_ln_in_b, v_w_in, v_b_f, v_conv_w, v_conv_b, v_ln_conv_g, v_ln_conv_b, v_w_pw, v_w_out, v_ln_out_g, v_ln_out_b):
    conv_w2, w_pw2, w_out2 = conv_w[0], w_pw[0], w_out[0]
    w_t = jnp.transpose(w_in[0])

    small = jnp.concatenate([jnp.pad(conv_w2, ((0, 1), (0, 0))), meta.reshape(32, 128)], axis=0)
    shards = [jnp.pad(w_t.astype(_MM), ((0, W_IN_SLOT - W_IN_SHARD), (0, 0))), small]
    g_in, g_small = (_own_slot(g, s) for g, s in zip(_all_gather(shards), shards))
    conv_w_full = jnp.transpose(g_small[:, :CONV_WIDTH, :], (1, 0, 2)).reshape(CONV_WIDTH, D_CONV)
    meta_full = jnp.transpose(g_small[:, 32:64, :].reshape(N_SHARD, N_META, 256), (1, 0, 2)).reshape(N_META, D_MODEL)

    loss, grad_x, gr = _local_grads(x[0], loss_target[0], meta_full, ln_in_g, ln_in_b, g_in, b_f, conv_w_full,
                                    conv_b, ln_conv_g, ln_conv_b, None, None, ln_out_g, ln_out_b,
                                    chip_partials=lambda big: _chip_partials([b.astype(_MM) for b in big], [_MM] * 3),
                                    late_shards=[w_pw2.astype(_MM), w_out2.astype(_MM)])
    grad_x = grad_x[None]
    gw_slot, gw_pw, gw_out = _finish_reduction(*gr["reduced"])

    repl_names = ("ln_in_g", "ln_in_b", "b_f", "conv_b", "ln_conv_g", "ln_conv_b", "ln_out_g", "ln_out_b")
    repl = _pack_repl([gr[n] for n in repl_names] + [loss])
    gcw = jnp.transpose(jnp.pad(gr["conv_w"], ((0, 1), (0, 0))).reshape(32, N_SHARD, 128), (1, 0, 2))
    gmeta = jnp.transpose(gr["meta"].reshape(N_META, N_SHARD, 256), (1, 0, 2)).reshape(N_SHARD, 32, 128)
    small_g = jnp.concatenate([gcw, gmeta, jnp.broadcast_to(repl[None], (N_SHARD, REPL_ROWS, 128))], axis=1)
    small_rows = small_g.shape[1]
    gsmall = _reduce_small(small_g.reshape(N_SHARD * small_rows, 128))
    g_conv_w = gsmall[:CONV_WIDTH]
    g_meta = gsmall[32:64].reshape(N_META, 256)
    g_repl = gsmall[64:64 + REPL_ROWS]

    repl_w = (ln_in_g, ln_in_b, b_f, conv_b, ln_conv_g, ln_conv_b, ln_out_g, ln_out_b)
    repl_m = (m_ln_in_g, m_ln_in_b, m_b_f, m_conv_b, m_ln_conv_g, m_ln_conv_b, m_ln_out_g, m_ln_out_b)
    repl_v = (v_ln_in_g, v_ln_in_b, v_b_f, v_conv_b, v_ln_conv_g, v_ln_conv_b, v_ln_out_g, v_ln_out_b)
    zero = jnp.zeros((1,), _F32)
    res_t = _adamw_tiled(w_t, gw_slot, jnp.transpose(m_w_in[0]), jnp.transpose(v_w_in[0]), 256, "adamw_w_in")
    gw_in, d_in, nm_in, nv_in = (jnp.transpose(a) for a in res_t)
    ws = [w_pw2, w_out2, conv_w2, meta, _pack_repl(repl_w + (zero,))]
    gs = [gw_pw, gw_out, g_conv_w, g_meta, g_repl]
    ms = [m_w_pw[0], m_w_out[0], m_conv_w[0], m_meta, _pack_repl(repl_m + (zero,))]
    vs = [v_w_pw[0], v_w_out[0], v_conv_w[0], v_meta, _pack_repl(repl_v + (zero + 1.0,))]
    ds, nms, nvs = _adamw_small(ws, gs, ms, vs)

    shapes = [w.shape for w in repl_w]
    total_loss = g_repl.reshape(-1)[sum(REPL_SIZES)]
    names = ("meta", "ln_in_g", "ln_in_b", "w_in", "b_f", "conv_w", "conv_b", "ln_conv_g", "ln_conv_b",
             "w_pw", "w_out", "ln_out_g", "ln_out_b")

    def assemble(t_in, smalls):
        pw_, out_, cw_, meta_, packed = smalls
        r = dict(zip(repl_names, _unpack_repl(packed, shapes)))
        r.update(meta=meta_, w_in=t_in[None], conv_w=cw_[None], w_pw=pw_[None], w_out=out_[None])
        return [r[n] for n in names]

    return (total_loss, grad_x, *assemble(gw_in, gs), *assemble(d_in, ds), *assemble(nm_in, nms),
            *assemble(nv_in, nvs))
```
